```python
import math
import jax, jax.numpy as jnp
from jax import lax
import numpy as np

D_MODEL = 1024
BATCH = 16
SEQ = 4096
DEPTH = 4

MEM_LEN = 256
A_HEADS = 4
A_QK_DIM = 64
A_V_DIM = 2 * A_QK_DIM
B_HEADS = 4
B_HEAD_DIM = 64
DILATED_PATTERNS = ((128, 1), (512, 4), (2048, 16))
DILATED_BLK = 128
C_HEADS = 4
C_HEAD_DIM = 64
IDX_HEADS = 8
IDX_DIM = 64
TOPK_MAX = 256
IN_SPLITS = (
    A_HEADS * 2 * A_QK_DIM,
    A_HEADS * 2 * A_QK_DIM,
    A_HEADS * A_V_DIM,
    B_HEADS * B_HEAD_DIM,
    B_HEADS * B_HEAD_DIM,
    B_HEADS * B_HEAD_DIM,
    C_HEADS * C_HEAD_DIM,
    C_HEADS * C_HEAD_DIM,
    C_HEADS * C_HEAD_DIM,
    IDX_HEADS * IDX_DIM,
    IDX_DIM,
    IDX_HEADS,
)
N_IN = 3656
MIX_WIDTH = A_HEADS * A_V_DIM + B_HEADS * B_HEAD_DIM + C_HEADS * C_HEAD_DIM
REL_BUCKETS = 32
REL_MAX_DIST = 2048
N_BIAS_HEADS = A_HEADS + B_HEADS + C_HEADS
MEM_HEADS = 4
MEM_HEAD_DIM = D_MODEL // MEM_HEADS
D_FF = 2816
CONV_WIDTH = 3

Q_BLK = 128
EPS = 1e-6
NEG = -1e30

kernel_name = "hybrid_diff_dilated_dsa_trunk"


def rmsnorm(x, g):
    x32 = x.astype(jnp.float32)
    y = x32 * lax.rsqrt(jnp.mean(x32 * x32, axis=-1, keepdims=True) + EPS)
    return (y * g.astype(jnp.float32)).astype(x.dtype)


def rel_bucket(dist):
    n = jnp.maximum(dist, 0)
    max_exact = REL_BUCKETS // 2
    nf = jnp.maximum(n, 1).astype(jnp.float32)
    large = max_exact + (jnp.log(nf / max_exact) / math.log(REL_MAX_DIST / max_exact)
                         * (REL_BUCKETS - max_exact)).astype(jnp.int32)
    large = jnp.minimum(large, REL_BUCKETS - 1)
    return jnp.where(n < max_exact, n, large)


def diff_attention(q, k, v, lam, lam_init, subln_g, table):
    Bn, S, H, _, d = q.shape
    nb = S // Q_BLK
    scale = d ** -0.5
    kpos = jnp.arange(S, dtype=jnp.int32)
    v32 = v.astype(jnp.float32)

    def block(args):
        qb, start = args
        qpos = start + jnp.arange(Q_BLK, dtype=jnp.int32)
        s = jnp.einsum('bqhmd,bkhmd->bhmqk', qb, k).astype(jnp.float32) * scale
        dist = qpos[:, None] - kpos[None, :]
        bias = jnp.transpose(table[rel_bucket(dist)].astype(jnp.float32), (2, 0, 1))
        s = jnp.where(dist >= 0, s + bias[None, :, None], NEG)
        p = jax.nn.softmax(s, axis=-1)
        attn = p[:, :, 0] - lam * p[:, :, 1]
        return jnp.einsum('bhqk,bkhd->bqhd', attn, v32)

    qb = jnp.moveaxis(q.reshape(Bn, nb, Q_BLK, H, 2, d), 1, 0)
    starts = jnp.arange(nb, dtype=jnp.int32) * Q_BLK
    o = lax.map(block, (qb, starts))
    o = jnp.moveaxis(o, 0, 1).reshape(Bn, S, H, v.shape[-1])
    return rmsnorm(o, subln_g) * (1.0 - lam_init)


def dilated_branch(q, k, v, window, r, table, scale):
    Bn, S, H, d = q.shape
    n_back = window // r
    unit = r * DILATED_BLK
    S_pad = -(-S // unit) * unit
    nb = S_pad // unit
    pad = ((0, 0), (0, S_pad - S), (0, 0), (0, 0))
    qb = jnp.pad(q, pad).reshape(Bn, nb, DILATED_BLK, r, H, d)
    kb = jnp.pad(k, pad).reshape(Bn, nb, DILATED_BLK, r, H, d)
    vb = jnp.pad(v, pad).reshape(Bn, nb, DILATED_BLK, r, H, d)

    def with_prev(t):
        prev = jnp.pad(t, ((0, 0), (1, 0), (0, 0), (0, 0), (0, 0), (0, 0)))[:, :nb]
        return jnp.concatenate([prev, t], axis=2)

    kk = with_prev(kb)
    vv = with_prev(vb).astype(jnp.float32)
    s = jnp.einsum('bnqchd,bnkchd->bnchqk', qb, kk).astype(jnp.float32) * scale
    qi = jnp.arange(DILATED_BLK, dtype=jnp.int32)[:, None]
    ki = jnp.arange(2 * DILATED_BLK, dtype=jnp.int32)[None, :]
    dist_m = qi + DILATED_BLK - ki
    m_q = jnp.arange(nb, dtype=jnp.int32)[:, None, None] * DILATED_BLK + qi[None]
    mask = (dist_m >= 0) & (dist_m <= n_back) & (m_q - dist_m >= 0)
    bias = jnp.transpose(table[rel_bucket(dist_m * r)].astype(jnp.float32), (2, 0, 1))
    s = jnp.where(mask[None, :, None, None], s + bias, NEG)
    lse = jax.nn.logsumexp(s, axis=-1)
    p = jnp.exp(s - lse[..., None])
    o = jnp.einsum('bnchqk,bnkchd->bnqchd', p, vv).reshape(Bn, S_pad, H, d)[:, :S]
    lse = jnp.transpose(lse, (0, 1, 4, 2, 3)).reshape(Bn, S_pad, H)[:, :S]
    return o, lse


def dilated_attention(q, k, v, table):
    scale = q.shape[-1] ** -0.5
    outs, lses = [], []
    for window, r in DILATED_PATTERNS:
        o, lse = dilated_branch(q, k, v, window, r, table, scale)
        outs.append(o)
        lses.append(lse)
    wts = jax.nn.softmax(jnp.stack(lses, 0), axis=0)
    return jnp.einsum('pbsh,pbshd->bshd', wts, jnp.stack(outs, 0))


def dsa_attention(q, k, v, iq, ik, iw, table):
    Bn, S, H, d = q.shape
    nb = S // Q_BLK
    topk = min(TOPK_MAX, S // 4)
    scale = d ** -0.5
    kpos = jnp.arange(S, dtype=jnp.int32)
    iw = iw.astype(jnp.float32) * (IDX_HEADS ** -0.5)

    def block(args):
        qb, iqb, iwb, start = args
        qpos = start + jnp.arange(Q_BLK, dtype=jnp.int32)
        logits = jnp.einsum('bqhd,bkd->bqhk', iqb, ik).astype(jnp.float32) * (IDX_DIM ** -0.5)
        score = jnp.einsum('bqhk,bqh->bqk', jax.nn.relu(logits), iwb)
        score = jnp.where(kpos[None, None, :] <= qpos[None, :, None], score, NEG)
        _, idx = lax.top_k(score, topk)
        valid = idx <= qpos[None, :, None]
        flat = idx.reshape(Bn, Q_BLK * topk)
        ks = jax.vmap(lambda kb, ib: kb[ib])(k, flat).reshape(Bn, Q_BLK, topk, H, d)
        vs = jax.vmap(lambda vb, ib: vb[ib])(v, flat).reshape(Bn, Q_BLK, topk, H, d)
        s = jnp.einsum('bqhd,bqkhd->bhqk', qb, ks).astype(jnp.float32) * scale
        bias = jnp.transpose(table[rel_bucket(qpos[None, :, None] - idx)].astype(jnp.float32), (0, 3, 1, 2))
        s = jnp.where(valid[:, None], s + bias, NEG)
        p = jax.nn.softmax(s, axis=-1)
        return jnp.einsum('bhqk,bqkhd->bqhd', p, vs.astype(jnp.float32))

    qb = jnp.moveaxis(q.reshape(Bn, nb, Q_BLK, H, d), 1, 0)
    iqb = jnp.moveaxis(iq.reshape(Bn, nb, Q_BLK, IDX_HEADS, IDX_DIM), 1, 0)
    iwb = jnp.moveaxis(iw.reshape(Bn, nb, Q_BLK, IDX_HEADS), 1, 0)
    starts = jnp.arange(nb, dtype=jnp.int32) * Q_BLK
    o = lax.map(block, (qb, iqb, iwb, starts))
    return jnp.moveaxis(o, 0, 1).reshape(Bn, S, H, d)


def hybrid_mixer(h, w_in, lam_q1, lam_k1, lam_q2, lam_k2, subln_g, w_out, rel_bias, layer_idx):
    Bn, S, _ = h.shape
    proj = h @ w_in
    (aq, ak, av, bq, bk, bv, cq, ck, cv, iq, ik, iw) = jnp.split(
        proj, np.cumsum(IN_SPLITS)[:-1].tolist(), axis=-1)
    lam_init = 0.8 - 0.6 * math.exp(-0.3 * layer_idx)
    lam = (jnp.exp(jnp.sum(lam_q1.astype(jnp.float32) * lam_k1.astype(jnp.float32)))
           - jnp.exp(jnp.sum(lam_q2.astype(jnp.float32) * lam_k2.astype(jnp.float32))) + lam_init)
    o_a = diff_attention(aq.reshape(Bn, S, A_HEADS, 2, A_QK_DIM), ak.reshape(Bn, S, A_HEADS, 2, A_QK_DIM),
                         av.reshape(Bn, S, A_HEADS, A_V_DIM), lam, lam_init, subln_g,
                         rel_bias[:, :A_HEADS])
    o_b = dilated_attention(bq.reshape(Bn, S, B_HEADS, B_HEAD_DIM), bk.reshape(Bn, S, B_HEADS, B_HEAD_DIM),
                            bv.reshape(Bn, S, B_HEADS, B_HEAD_DIM),
                            rel_bias[:, A_HEADS:A_HEADS + B_HEADS])
    o_c = dsa_attention(cq.reshape(Bn, S, C_HEADS, C_HEAD_DIM), ck.reshape(Bn, S, C_HEADS, C_HEAD_DIM),
                        cv.reshape(Bn, S, C_HEADS, C_HEAD_DIM), iq.reshape(Bn, S, IDX_HEADS, IDX_DIM),
                        ik, iw, rel_bias[:, A_HEADS + B_HEADS:])
    o = jnp.concatenate([o_a.reshape(Bn, S, -1), o_b.reshape(Bn, S, -1), o_c.reshape(Bn, S, -1)],
                        axis=-1).astype(h.dtype)
    return o @ w_out


def memory_attention(h, mem_n, w_mq, w_mkv, w_mo):
    Bn, S, D = h.shape
    q = (h @ w_mq).reshape(Bn, S, MEM_HEADS, MEM_HEAD_DIM)
    k, v = jnp.split(mem_n @ w_mkv, 2, axis=-1)
    k = k.reshape(Bn, -1, MEM_HEADS, MEM_HEAD_DIM)
    v = v.reshape(Bn, -1, MEM_HEADS, MEM_HEAD_DIM)
    s = jnp.einsum('bqhd,bkhd->bhqk', q, k).astype(jnp.float32) * (MEM_HEAD_DIM ** -0.5)
    p = jax.nn.softmax(s, axis=-1)
    o = jnp.einsum('bhqk,bkhd->bqhd', p, v.astype(jnp.float32)).astype(h.dtype).reshape(Bn, S, D)
    return o @ w_mo


def conv_ffn(h, w_up, conv_w, conv_b, w_down):
    S = h.shape[1]
    u = h @ w_up
    u_p = jnp.pad(u, ((0, 0), (CONV_WIDTH - 1, 0), (0, 0)))
    c = conv_b
    for j in range(CONV_WIDTH):
        c = c + conv_w[j] * u_p[:, j:j + S]
    gate, val = jnp.split(c, 2, axis=-1)
    return (jax.nn.silu(gate) * val) @ w_down


def setup_inputs(seed: int = 0) -> dict:
    key = jax.random.key(seed)
    ks = jax.random.split(key, 24)

    def nrm(k, shape, scale):
        return jax.random.normal(k, shape, jnp.float32) * scale

    L, D, F = DEPTH, D_MODEL, D_FF
    return {
        "x": nrm(ks[0], (BATCH, SEQ, D), 1.0),
        "mem": nrm(ks[1], (BATCH, MEM_LEN, D), 1.0),
        "rel_bias": nrm(ks[2], (REL_BUCKETS, N_BIAS_HEADS), 0.2),
        "norm_mix": 1.0 + nrm(ks[3], (L, D), 0.1),
        "w_in": nrm(ks[4], (L, D, N_IN), D ** -0.5),
        "lam_q1": nrm(ks[5], (L, A_QK_DIM), 0.1),
        "lam_k1": nrm(ks[6], (L, A_QK_DIM), 0.1),
        "lam_q2": nrm(ks[7], (L, A_QK_DIM), 0.1),
        "lam_k2": nrm(ks[8], (L, A_QK_DIM), 0.1),
        "subln": 1.0 + nrm(ks[9], (L, A_V_DIM), 0.1),
        "w_out": nrm(ks[10], (L, MIX_WIDTH, D), MIX_WIDTH ** -0.5),
        "norm_mem": 1.0 + nrm(ks[11], (L, D), 0.1),
        "norm_memkv": 1.0 + nrm(ks[12], (L, D), 0.1),
        "w_mq": nrm(ks[13], (L, D, D), D ** -0.5),
        "w_mkv": nrm(ks[14], (L, D, 2 * D), D ** -0.5),
        "w_mo": nrm(ks[15], (L, D, D), D ** -0.5),
        "norm_ffn": 1.0 + nrm(ks[16], (L, D), 0.1),
        "w_up": nrm(ks[17], (L, D, 2 * F), D ** -0.5),
        "conv_w": nrm(ks[18], (L, CONV_WIDTH, 2 * F), CONV_WIDTH ** -0.5),
        "conv_b": nrm(ks[19], (L, 2 * F), 0.02),
        "w_down": nrm(ks[20], (L, F, D), F ** -0.5),
        "norm_final": 1.0 + nrm(ks[21], (D,), 0.1),
    }


def reference(x, mem, rel_bias, norm_mix, w_in, lam_q1, lam_k1, lam_q2, lam_k2, subln, w_out,
              norm_mem, norm_memkv, w_mq, w_mkv, w_mo, norm_ffn, w_up, conv_w, conv_b, w_down,
              norm_final):
    for l in range(DEPTH):
        h = rmsnorm(x, norm_mix[l])
        x = x + hybrid_mixer(h, w_in[l], lam_q1[l], lam_k1[l], lam_q2[l], lam_k2[l], subln[l],
                             w_out[l], rel_bias, l)
        h = rmsnorm(x, norm_mem[l])
        x = x + memory_attention(h, rmsnorm(mem, norm_memkv[l]), w_mq[l], w_mkv[l], w_mo[l])
        h = rmsnorm(x, norm_ffn[l])
        x = x + conv_ffn(h, w_up[l], conv_w[l], conv_b[l], w_down[l])
    return rmsnorm(x, norm_final)
```

```python
import functools
import math

import jax
import jax.numpy as jnp
from jax import lax
from jax.experimental import pallas as pl
from jax.experimental.pallas import tpu as pltpu

F32 = jnp.float32
BF16 = jnp.bfloat16

EPS = 1e-6
NEG = -1e30
LANES = 128
VMEM_LIMIT = 56 * 1024 * 1024

A_HEADS, A_QK, A_V = 4, 64, 128
B_HEADS, B_DIM = 4, 64
C_HEADS, C_DIM = 4, 64
IDX_HEADS, IDX_DIM = 8, 64
TOPK_MAX = 256
DILATED_PATTERNS = ((128, 1), (512, 4), (2048, 16))
DIL_BLK = 128
REL_BUCKETS, REL_MAX_DIST = 32, 2048
MEM_HEADS = 4
CONV_WIDTH = 3

N_MAIN = 2816
N_B = 768
N_I = 256


def _cparams(sem):
    return pltpu.CompilerParams(dimension_semantics=sem, vmem_limit_bytes=VMEM_LIMIT)


def _dot(a, b):
    return jnp.dot(a, b, preferred_element_type=F32)


def _dot_nt(a, b):
    return lax.dot_general(a, b, (((1,), (1,)), ((), ())), preferred_element_type=F32)


def _rms(x, g):
    return x * lax.rsqrt(jnp.mean(x * x, axis=-1, keepdims=True) + EPS) * g


def _rel_bucket(n):
    max_exact = REL_BUCKETS // 2
    nf = jnp.maximum(n, 1).astype(F32)
    large = max_exact + (jnp.log(nf / max_exact) / math.log(REL_MAX_DIST / max_exact)
                         * (REL_BUCKETS - max_exact)).astype(jnp.int32)
    large = jnp.minimum(large, REL_BUCKETS - 1)
    return jnp.where(n < max_exact, n, large)


def _bias_tiles_kernel(par_ref, tab_ref, o_ref, *, tq, tk):
    t = pl.program_id(0)
    head, off, mult = par_ref[t, 0], par_ref[t, 1], par_ref[t, 2]
    lo, hi = par_ref[t, 3], par_ref[t, 4]
    dist = off + lax.broadcasted_iota(jnp.int32, (tq, tk), 0) - lax.broadcasted_iota(jnp.int32, (tq, tk), 1)
    bucket = _rel_bucket(jnp.maximum(dist * mult, 0))
    val = jnp.zeros((tq, tk), F32)
    for b in range(REL_BUCKETS):
        val = jnp.where(bucket == b, tab_ref[b, head], val)
    o_ref[...] = jnp.where((dist >= lo) & (dist <= hi), val, NEG)


def _bias_tiles(params, table, tq, tk):
    n = params.shape[0]
    return pl.pallas_call(
        functools.partial(_bias_tiles_kernel, tq=tq, tk=tk),
        grid=(n,),
        in_specs=[pl.BlockSpec(memory_space=pltpu.SMEM), pl.BlockSpec(memory_space=pltpu.SMEM)],
        out_specs=pl.BlockSpec((None, tq, tk), lambda t: (t, 0, 0)),
        out_shape=jax.ShapeDtypeStruct((n, tq, tk), F32),
        compiler_params=_cparams(("arbitrary",)),
        name="bias_tiles",
    )(params, table)


def _inproj_kernel(x_ref, g_ref, w_ref, om_ref, ob_ref, oi_ref):
    h = _rms(x_ref[...], g_ref[...]).astype(BF16)
    step = 512
    for c in range(0, N_MAIN, step):
        e = min(c + step, N_MAIN)
        om_ref[:, c:e] = _dot(h, w_ref[:, c:e]).astype(BF16)
    ob_ref[...] = _dot(h, w_ref[:, N_MAIN:N_MAIN + N_B])
    oi_ref[...] = _dot(h, w_ref[:, N_MAIN + N_B:])


def _inproj(x2, g, w, tm):
    T, D = x2.shape
    n_all = N_MAIN + N_B + N_I
    return pl.pallas_call(
        _inproj_kernel,
        grid=(T // tm,),
        in_specs=[pl.BlockSpec((tm, D), lambda i: (i, 0)),
                  pl.BlockSpec((1, D), lambda i: (0, 0)),
                  pl.BlockSpec((D, n_all), lambda i: (0, 0))],
        out_specs=[pl.BlockSpec((tm, N_MAIN), lambda i: (i, 0)),
                   pl.BlockSpec((tm, N_B), lambda i: (i, 0)),
                   pl.BlockSpec((tm, N_I), lambda i: (i, 0))],
        out_shape=[jax.ShapeDtypeStruct((T, N_MAIN), BF16),
                   jax.ShapeDtypeStruct((T, N_B), F32),
                   jax.ShapeDtypeStruct((T, N_I), F32)],
        compiler_params=_cparams(("parallel",)),
        name="inproj",
    )(x2, g, w)


def _diff_attn_kernel(lam_ref, q_ref, k_ref, v_ref, bias_ref, g_ref, o_ref, m_sc, l_sc, acc_sc,
                      *, tq, lam_init):
    qi = pl.program_id(2)
    lv = lam_ref[...]
    lam = (jnp.exp(jnp.sum(lv[0:1] * lv[1:2], axis=-1, keepdims=True))
           - jnp.exp(jnp.sum(lv[2:3] * lv[3:4], axis=-1, keepdims=True)) + lam_init)
    lane = lax.broadcasted_iota(jnp.int32, (tq, LANES), 1)
    q = q_ref[...].astype(F32) * (A_QK ** -0.5)
    qm = (jnp.where(lane < A_QK, q, 0.0).astype(BF16), jnp.where(lane >= A_QK, q, 0.0).astype(BF16))
    m_sc[...] = jnp.full(m_sc.shape, NEG, F32)
    l_sc[...] = jnp.zeros(l_sc.shape, F32)
    acc_sc[...] = jnp.zeros(acc_sc.shape, F32)

    def body(j, carry):
        start = pl.multiple_of(j * tq, tq)
        k = k_ref[pl.ds(start, tq), :]
        v = v_ref[pl.ds(start, tq), :]
        bias = bias_ref[qi - j]
        for m in range(2):
            s = _dot_nt(qm[m], k) + bias
            m_prev = m_sc[m]
            m_new = jnp.maximum(m_prev, jnp.max(s, axis=-1, keepdims=True))
            alpha = jnp.exp(m_prev - m_new)
            p = jnp.exp(s - m_new)
            l_sc[m] = alpha * l_sc[m] + jnp.sum(p, axis=-1, keepdims=True)
            acc_sc[m] = alpha * acc_sc[m] + _dot(p.astype(BF16), v)
            m_sc[m] = m_new
        return carry

    lax.fori_loop(0, qi + 1, body, 0)
    o = acc_sc[0] / l_sc[0] - lam * (acc_sc[1] / l_sc[1])
    o_ref[...] = (_rms(o, g_ref[...]) * (1.0 - lam_init)).astype(o_ref.dtype)


def _diff_attn(proj, lamv, bias_a, subln_g, lam_init, tq):
    B, S, _ = proj.shape
    nd = bias_a.shape[1]
    return pl.pallas_call(
        functools.partial(_diff_attn_kernel, tq=tq, lam_init=lam_init),
        grid=(A_HEADS, B, S // tq),
        in_specs=[pl.BlockSpec((4, A_QK), lambda h, b, i: (0, 0)),
                  pl.BlockSpec((None, tq, LANES), lambda h, b, i: (b, i, h)),
                  pl.BlockSpec((None, S, LANES), lambda h, b, i: (b, 0, A_HEADS + h)),
                  pl.BlockSpec((None, S, LANES), lambda h, b, i: (b, 0, 2 * A_HEADS + h)),
                  pl.BlockSpec((None, nd, tq, tq), lambda h, b, i: (h, 0, 0, 0)),
                  pl.BlockSpec((1, A_V), lambda h, b, i: (0, 0))],
        out_specs=pl.BlockSpec((None, tq, LANES), lambda h, b, i: (b, i, h)),
        out_shape=jax.ShapeDtypeStruct((B, S, A_HEADS * A_V), BF16),
        scratch_shapes=[pltpu.VMEM((2, tq, 1), F32), pltpu.VMEM((2, tq, 1), F32),
                        pltpu.VMEM((2, tq, A_V), F32)],
        compiler_params=_cparams(("parallel", "parallel", "arbitrary")),
        name="diff_attn",
    )(lamv, proj, proj, proj, bias_a, subln_g)


def _dil_kernel(q_ref, k_ref, v_ref, bias_ref, o_ref, n_sc, m_sc, l_sc, *, seq, patterns):
    blk = DIL_BLK
    lane = lax.broadcasted_iota(jnp.int32, (blk, LANES), 1)
    first = lane < B_DIM

    def attend(base, r, with_prev, p_idx):
        q = q_ref[pl.ds(base, blk, stride=r), :] * (B_DIM ** -0.5)
        kc = k_ref[pl.ds(base, blk, stride=r), :].astype(BF16)
        vc = v_ref[pl.ds(base, blk, stride=r), :].astype(BF16)
        if with_prev:
            kp = k_ref[pl.ds(base - blk * r, blk, stride=r), :].astype(BF16)
            vp = v_ref[pl.ds(base - blk * r, blk, stride=r), :].astype(BF16)
        outs, ms, ls = [], [], []
        for hh in range(2):
            qh = jnp.where(first if hh == 0 else ~first, q, 0.0).astype(BF16)
            sc = _dot_nt(qh, kc) + bias_ref[p_idx, hh, 1]
            mx = jnp.max(sc, axis=-1, keepdims=True)
            if with_prev:
                sp = _dot_nt(qh, kp) + bias_ref[p_idx, hh, 0]
                mx = jnp.maximum(mx, jnp.max(sp, axis=-1, keepdims=True))
            pc = jnp.exp(sc - mx)
            l = jnp.sum(pc, axis=-1, keepdims=True)
            o = _dot(pc.astype(BF16), vc)
            if with_prev:
                pp = jnp.exp(sp - mx)
                l = l + jnp.sum(pp, axis=-1, keepdims=True)
                o = o + _dot(pp.astype(BF16), vp)
            outs.append(o)
            ms.append(jnp.broadcast_to(mx, (blk, LANES)))
            ls.append(jnp.broadcast_to(l, (blk, LANES)))
        return (jnp.where(first, outs[0], outs[1]), jnp.where(first, ms[0], ms[1]),
                jnp.where(first, ls[0], ls[1]))

    def merge(base, r, res, is_first):
        o, m, l = res
        rows = pl.ds(base, blk, stride=r)
        if is_first:
            n_sc[rows, :] = o
            m_sc[rows, :] = m
            l_sc[rows, :] = l
        else:
            m_old = m_sc[rows, :]
            m_new = jnp.maximum(m_old, m)
            a = jnp.exp(m_old - m_new)
            b = jnp.exp(m - m_new)
            n_sc[rows, :] = a * n_sc[rows, :] + b * o
            l_sc[rows, :] = a * l_sc[rows, :] + b * l
            m_sc[rows, :] = m_new

    for p_idx, (_, r) in enumerate(patterns):
        nb = seq // (r * blk)

        def head_blocks(c, carry, r=r, p_idx=p_idx):
            merge(c, r, attend(c, r, False, p_idx), p_idx == 0)
            return carry

        lax.fori_loop(0, r, head_blocks, 0)

        def later_blocks(idx, carry, r=r, p_idx=p_idx):
            n = idx // r + 1
            c = idx % r
            base = n * blk * r + c
            merge(base, r, attend(base, r, True, p_idx), p_idx == 0)
            return carry

        lax.fori_loop(0, r * (nb - 1), later_blocks, 0)

    o_ref[...] = (n_sc[...] / l_sc[...]).astype(o_ref.dtype)


def _dil_attn(proj_b, bias_b):
    B, S, _ = proj_b.shape
    npat = bias_b.shape[0]
    return pl.pallas_call(
        functools.partial(_dil_kernel, seq=S, patterns=DILATED_PATTERNS),
        grid=(B, B_HEADS // 2),
        in_specs=[pl.BlockSpec((None, S, LANES), lambda b, hp: (b, 0, hp)),
                  pl.BlockSpec((None, S, LANES), lambda b, hp: (b, 0, 2 + hp)),
                  pl.BlockSpec((None, S, LANES), lambda b, hp: (b, 0, 4 + hp)),
                  pl.BlockSpec((npat, 2, 2, DIL_BLK, DIL_BLK), lambda b, hp: (0, hp, 0, 0, 0))],
        out_specs=pl.BlockSpec((None, S, LANES), lambda b, hp: (b, 0, hp)),
        out_shape=jax.ShapeDtypeStruct((B, S, B_HEADS * B_DIM), BF16),
        scratch_shapes=[pltpu.VMEM((S, LANES), F32)] * 3,
        compiler_params=_cparams(("parallel", "arbitrary")),
        name="dilated_attn",
    )(proj_b, proj_b, proj_b, bias_b)


def _float_key(f):
    b = pltpu.bitcast(f, jnp.int32)
    return jnp.where(b >= 0, b, b ^ jnp.int32(0x7FFFFFFF))


def _key_float(k):
    return pltpu.bitcast(jnp.where(k >= 0, k, k ^ jnp.int32(0x7FFFFFFF)), F32)


def _dsa_kernel(iq_ref, ik_ref, iw_ref, cq_ref, ck_ref, cv_ref, bias_ref, o_ref,
                sc_ref, m_sc, l_sc, acc_sc, *, tq, ch, topk):
    qi = pl.program_id(1)
    q0 = qi * tq
    nch = (q0 + tq - 1) // ch + 1
    sub = ch // LANES
    rows = q0 + lax.broadcasted_iota(jnp.int32, (tq, 1), 0)
    lane = lax.broadcasted_iota(jnp.int32, (tq, LANES), 1)
    first = lane < IDX_DIM

    iw = iw_ref[...] * (IDX_HEADS ** -0.5)
    qs = []
    for h in range(IDX_HEADS):
        tile = iq_ref[:, (h // 2) * LANES:(h // 2 + 1) * LANES]
        qs.append(jnp.where(first if h % 2 == 0 else ~first, tile, jnp.zeros_like(tile)))
    qstack = jnp.concatenate(qs, axis=0)
    wcols = [iw[:, h:h + 1] for h in range(IDX_HEADS)]

    def score_body(j, carry):
        mx, mn = carry
        start = pl.multiple_of(j * ch, ch)
        kk = ik_ref[pl.ds(start, ch), :].astype(BF16)
        logits = _dot_nt(qstack, kk)
        sc = jnp.zeros((tq, ch), F32)
        for h in range(IDX_HEADS):
            sc = sc + jnp.maximum(logits[h * tq:(h + 1) * tq] * (IDX_DIM ** -0.5), 0.0) * wcols[h]
        cols = start + lax.broadcasted_iota(jnp.int32, (1, ch), 1)
        valid = cols <= rows
        mx = jnp.maximum(mx, jnp.max(jnp.where(valid, sc, NEG), axis=-1, keepdims=True))
        mn = jnp.minimum(mn, jnp.min(jnp.where(valid, sc, -NEG), axis=-1, keepdims=True))
        sc_ref[j] = jnp.where(valid, sc, NEG)
        return mx, mn

    mx, mn = lax.fori_loop(0, nch, score_body,
                           (jnp.full((tq, 1), NEG, F32), jnp.full((tq, 1), -NEG, F32)))

    def count_ge(thr):
        def body(j, acc):
            return acc + jnp.sum(jnp.where(sc_ref[j] >= thr, 1.0, 0.0), axis=-1, keepdims=True)
        return lax.fori_loop(0, nch, body, jnp.zeros((tq, 1), F32))

    kf = float(topk)
    n_valid = (rows + 1).astype(F32)
    lo0 = _float_key(mn)
    hi0 = jnp.where(n_valid > kf, _float_key(mx), lo0)

    def search_cond(st):
        lo, hi, _, it = st
        return jnp.logical_and(jnp.max(jnp.where(lo < hi, 1.0, 0.0)) > 0.0, it < 40)

    def search_body(st):
        lo, hi, c_lo, it = st
        mid = (lo >> 1) + (hi >> 1) + ((lo | hi) & 1)
        c = count_ge(_key_float(mid))
        active = lo < hi
        ge = jnp.logical_and(active, c >= kf)
        exact = jnp.logical_and(active, c == kf)
        lt = jnp.logical_and(active, c < kf)
        lo = jnp.where(ge, mid, lo)
        c_lo = jnp.where(ge, c, c_lo)
        hi = jnp.where(exact, mid, jnp.where(lt, mid - 1, hi))
        return lo, hi, c_lo, it + 1

    lo, _, c_lo, _ = lax.while_loop(search_cond, search_body, (lo0, hi0, n_valid, jnp.int32(0)))
    thr = _key_float(lo)

    tie_rows = jnp.logical_and(n_valid > kf, c_lo > kf)

    @pl.when(jnp.max(jnp.where(tie_rows, 1.0, 0.0)) > 0.0)
    def _():
        def gt_body(j, acc):
            return acc + jnp.sum(jnp.where(sc_ref[j] > thr, 1.0, 0.0), axis=-1, keepdims=True)
        need = kf - lax.fori_loop(0, nch, gt_body, jnp.zeros((tq, 1), F32))
        upper = (lax.broadcasted_iota(jnp.int32, (ch, ch), 0)
                 <= lax.broadcasted_iota(jnp.int32, (ch, ch), 1)).astype(BF16)

        def drop_body(j, seen):
            sc = sc_ref[j]
            eq = sc == thr
            eqf = jnp.where(eq, 1.0, 0.0)
            rank = _dot(eqf.astype(BF16), upper) + seen
            drop = jnp.logical_and(jnp.logical_and(eq, rank > need), tie_rows)
            sc_ref[j] = jnp.where(drop, NEG, sc)
            return seen + jnp.sum(eqf, axis=-1, keepdims=True)

        lax.fori_loop(0, nch, drop_body, jnp.zeros((tq, 1), F32))

    lane_c = lax.broadcasted_iota(jnp.int32, (tq, C_HEADS * C_DIM), 1)
    cq = cq_ref[...].astype(F32) * (C_DIM ** -0.5)
    qh = [jnp.where(lane_c // C_DIM == h, cq, 0.0).astype(BF16) for h in range(C_HEADS)]
    m_sc[...] = jnp.full(m_sc.shape, NEG, F32)
    l_sc[...] = jnp.zeros(l_sc.shape, F32)
    acc_sc[...] = jnp.zeros(acc_sc.shape, F32)
    qblk = q0 // LANES

    def attn_body(j, carry):
        start = pl.multiple_of(j * ch, ch)
        kc = ck_ref[pl.ds(start, ch), :]
        vc = cv_ref[pl.ds(start, ch), :]
        sel = sc_ref[j] >= thr
        for h in range(C_HEADS):
            bias = jnp.concatenate(
                [bias_ref[h, jnp.maximum(qblk - (j * sub + u), 0)] for u in range(sub)], axis=1)
            s = jnp.where(sel, _dot_nt(qh[h], kc) + bias, NEG)
            m_prev = m_sc[h]
            m_new = jnp.maximum(m_prev, jnp.max(s, axis=-1, keepdims=True))
            alpha = jnp.exp(m_prev - m_new)
            p = jnp.exp(s - m_new)
            l_sc[h] = alpha * l_sc[h] + jnp.sum(p, axis=-1, keepdims=True)
            acc_sc[h] = alpha * acc_sc[h] + _dot(p.astype(BF16), vc)
            m_sc[h] = m_new
        return carry

    lax.fori_loop(0, nch, attn_body, 0)
    out = jnp.zeros((tq, C_HEADS * C_DIM), F32)
    for h in range(C_HEADS):
        out = jnp.where(lane_c // C_DIM == h, acc_sc[h] / l_sc[h], out)
    o_ref[...] = out.astype(o_ref.dtype)


def _dsa_attn(proj, proj_i, bias_c, tq, ch):
    B, S, _ = proj.shape
    topk = min(TOPK_MAX, S // 4)
    nd = bias_c.shape[1]
    cw = C_HEADS * C_DIM
    return pl.pallas_call(
        functools.partial(_dsa_kernel, tq=tq, ch=ch, topk=topk),
        grid=(B, S // tq),
        in_specs=[pl.BlockSpec((None, tq, IDX_HEADS * IDX_DIM), lambda b, i: (b, i, 3)),
                  pl.BlockSpec((None, S, LANES), lambda b, i: (b, 0, 0)),
                  pl.BlockSpec((None, tq, LANES), lambda b, i: (b, i, 1)),
                  pl.BlockSpec((None, tq, cw), lambda b, i: (b, i, 8)),
                  pl.BlockSpec((None, S, cw), lambda b, i: (b, 0, 9)),
                  pl.BlockSpec((None, S, cw), lambda b, i: (b, 0, 10)),
                  pl.BlockSpec((C_HEADS, nd, LANES, LANES), lambda b, i: (0, 0, 0, 0))],
        out_specs=pl.BlockSpec((None, tq, cw), lambda b, i: (b, i, 0)),
        out_shape=jax.ShapeDtypeStruct((B, S, cw), BF16),
        scratch_shapes=[pltpu.VMEM((S // ch, tq, ch), F32),
                        pltpu.VMEM((C_HEADS, tq, 1), F32), pltpu.VMEM((C_HEADS, tq, 1), F32),
                        pltpu.VMEM((C_HEADS, tq, cw), F32)],
        compiler_params=_cparams(("parallel", "arbitrary")),
        name="dsa_attn",
    )(proj, proj_i, proj_i, proj, proj, proj, bias_c)


def _memkv_kernel(mem_ref, g_ref, w_ref, o_ref):
    h = _rms(mem_ref[...], g_ref[...]).astype(BF16)
    o_ref[...] = _dot(h, w_ref[...]).astype(o_ref.dtype)


def _memkv(mem, g, w):
    B, M, D = mem.shape
    L = w.shape[0]
    return pl.pallas_call(
        _memkv_kernel,
        grid=(L, B),
        in_specs=[pl.BlockSpec((None, M, D), lambda l, b: (b, 0, 0)),
                  pl.BlockSpec((None, 1, D), lambda l, b: (l, 0, 0)),
                  pl.BlockSpec((None, D, 2 * D), lambda l, b: (l, 0, 0))],
        out_specs=pl.BlockSpec((None, None, M, 2 * D), lambda l, b: (l, b, 0, 0)),
        out_shape=jax.ShapeDtypeStruct((L, B, M, 2 * D), BF16),
        compiler_params=_cparams(("parallel", "arbitrary")),
        name="mem_kv",
    )(mem, g, w)


def _outmem_kernel(x_ref, oa_ref, ob_ref, oc_ref, wo_ref, g_ref, wq_ref, kv_ref, wmo_ref, o_ref):
    D = x_ref.shape[-1]
    na, nb = oa_ref.shape[-1], ob_ref.shape[-1]
    x = (x_ref[...] + _dot(oa_ref[...], wo_ref[0:na]) + _dot(ob_ref[...], wo_ref[na:na + nb])
         + _dot(oc_ref[...], wo_ref[na + nb:]))
    h = _rms(x, g_ref[...]).astype(BF16)
    hd = D // MEM_HEADS
    q = (_dot(h, wq_ref[...]) * (hd ** -0.5)).astype(BF16)
    outs = []
    for hh in range(MEM_HEADS):
        k = kv_ref[:, hh * hd:(hh + 1) * hd]
        v = kv_ref[:, D + hh * hd:D + (hh + 1) * hd]
        s = _dot_nt(q[:, hh * hd:(hh + 1) * hd], k)
        p = jnp.exp(s - jnp.max(s, axis=-1, keepdims=True))
        p = p / jnp.sum(p, axis=-1, keepdims=True)
        outs.append(_dot(p.astype(BF16), v).astype(BF16))
    o = jnp.concatenate(outs, axis=-1)
    o_ref[...] = x + _dot(o, wmo_ref[...])


def _outmem(x, oa, ob, oc, wo, g, wq, kv, wmo, tm):
    B, S, D = x.shape
    M = kv.shape[1]
    const = lambda b, i: (0, 0)
    return pl.pallas_call(
        _outmem_kernel,
        grid=(B, S // tm),
        in_specs=[pl.BlockSpec((None, tm, D), lambda b, i: (b, i, 0)),
                  pl.BlockSpec((None, tm, oa.shape[-1]), lambda b, i: (b, i, 0)),
                  pl.BlockSpec((None, tm, ob.shape[-1]), lambda b, i: (b, i, 0)),
                  pl.BlockSpec((None, tm, oc.shape[-1]), lambda b, i: (b, i, 0)),
                  pl.BlockSpec(wo.shape, const),
                  pl.BlockSpec((1, D), const),
                  pl.BlockSpec(wq.shape, const),
                  pl.BlockSpec((None, M, 2 * D), lambda b, i: (b, 0, 0)),
                  pl.BlockSpec(wmo.shape, const)],
        out_specs=pl.BlockSpec((None, tm, D), lambda b, i: (b, i, 0)),
        out_shape=jax.ShapeDtypeStruct((B, S, D), F32),
        compiler_params=_cparams(("parallel", "arbitrary")),
        name="outproj_memattn",
    )(x, oa, ob, oc, wo, g, wq, kv, wmo)


HALO = 8


def _ffn_kernel(x_ref, xp_ref, g_ref, wg_ref, wv_ref, cwg_ref, cwv_ref, cbg_ref, cbv_ref, wd_ref,
                o_ref, *, tm, fc):
    i = pl.program_id(1)
    x = x_ref[...]
    g = g_ref[...]
    hp = _rms(xp_ref[...], g) * jnp.where(i > 0, 1.0, 0.0)
    h = jnp.concatenate([hp, _rms(x, g)], axis=0).astype(BF16)
    F = wd_ref.shape[0]

    def conv(u, cw_ref, cb_ref, c, e):
        out = cb_ref[:, c:e]
        for j in range(CONV_WIDTH):
            shift = CONV_WIDTH - 1 - j
            out = out + cw_ref[j:j + 1, c:e] * u[HALO - shift:HALO - shift + tm]
        return out

    acc = x
    for c in range(0, F, fc):
        e = min(c + fc, F)
        gate = conv(_dot(h, wg_ref[:, c:e]), cwg_ref, cbg_ref, c, e)
        val = conv(_dot(h, wv_ref[:, c:e]), cwv_ref, cbv_ref, c, e)
        act = (gate * jax.nn.sigmoid(gate) * val).astype(BF16)
        acc = acc + _dot(act, wd_ref[c:e, :])
    o_ref[...] = acc


def _ffn(x, g, wg, wv, cwg, cwv, cbg, cbv, wd, tm, fc):
    B, S, D = x.shape
    F = wd.shape[0]
    const = lambda b, i: (0, 0)
    hb = tm // HALO
    return pl.pallas_call(
        functools.partial(_ffn_kernel, tm=tm, fc=fc),
        grid=(B, S // tm),
        in_specs=[pl.BlockSpec((None, tm, D), lambda b, i: (b, i, 0)),
                  pl.BlockSpec((None, HALO, D), lambda b, i: (b, jnp.maximum(i * hb - 1, 0), 0)),
                  pl.BlockSpec((1, D), const),
                  pl.BlockSpec((D, F), const), pl.BlockSpec((D, F), const),
                  pl.BlockSpec((CONV_WIDTH, F), const), pl.BlockSpec((CONV_WIDTH, F), const),
                  pl.BlockSpec((1, F), const), pl.BlockSpec((1, F), const),
                  pl.BlockSpec((F, D), const)],
        out_specs=pl.BlockSpec((None, tm, D), lambda b, i: (b, i, 0)),
        out_shape=jax.ShapeDtypeStruct((B, S, D), F32),
        compiler_params=_cparams(("parallel", "arbitrary")),
        name="conv_ffn",
    )(x, x, g, wg, wv, cwg, cwv, cbg, cbv, wd)


def _final_norm_kernel(x_ref, g_ref, o_ref):
    o_ref[...] = _rms(x_ref[...], g_ref[...])


def _final_norm(x2, g, tm):
    T, D = x2.shape
    return pl.pallas_call(
        _final_norm_kernel,
        grid=(T // tm,),
        in_specs=[pl.BlockSpec((tm, D), lambda i: (i, 0)), pl.BlockSpec((1, D), lambda i: (0, 0))],
        out_specs=pl.BlockSpec((tm, D), lambda i: (i, 0)),
        out_shape=jax.ShapeDtypeStruct((T, D), F32),
        compiler_params=_cparams(("parallel",)),
        name="final_norm",
    )(x2, g)


def _tile_params(S, tq_a):
    big = 1 << 30
    pa = [(h, d * tq_a, 1, 0, big) for h in range(A_HEADS) for d in range(S // tq_a)]
    pb = []
    for _, r in DILATED_PATTERNS:
        for h in range(B_HEADS):
            pb.append((A_HEADS + h, DIL_BLK, r, 1, DIL_BLK))
            pb.append((A_HEADS + h, 0, r, 0, DIL_BLK))
    pc = [(A_HEADS + B_HEADS + h, d * LANES, 1, -big, big)
          for h in range(C_HEADS) for d in range(S // LANES)]
    to = lambda p: jnp.asarray(p, jnp.int32)
    return to(pa), to(pb), to(pc)


def _in_weights(w_in_l):
    sizes = (512, 512, 512, 256, 256, 256, 256, 256, 256, 512, 64, 8)
    offs = [0]
    for s in sizes:
        offs.append(offs[-1] + s)
    (aq, ak, av, bq, bk, bv, cq, ck, cv, iq, ik, iw) = [w_in_l[:, offs[i]:offs[i + 1]] for i in range(12)]
    pad = jnp.zeros((w_in_l.shape[0], N_I - 2 * IDX_DIM - IDX_HEADS), w_in_l.dtype)
    return jnp.concatenate([aq, ak, av, iq, cq, ck, cv, bq, bk, bv, ik, ik, iw, pad], axis=1).astype(BF16)


def _forward(x, mem, rel_bias, norm_mix, w_in, lam_q1, lam_k1, lam_q2, lam_k2, subln, w_out,
             norm_mem, norm_memkv, w_mq, w_mkv, w_mo, norm_ffn, w_up, conv_w, conv_b, w_down,
             norm_final, *, tq_a, tq_c, ch_c, tm_proj, tm_mem, tm_ffn, fc):
    B, S, D = x.shape
    L = w_in.shape[0]
    F = w_down.shape[1]
    pa, pb, pc = _tile_params(S, tq_a)
    bias_a = _bias_tiles(pa, rel_bias, tq_a, tq_a).reshape(A_HEADS, S // tq_a, tq_a, tq_a)
    bias_b = _bias_tiles(pb, rel_bias, DIL_BLK, DIL_BLK).reshape(
        len(DILATED_PATTERNS), B_HEADS, 2, DIL_BLK, DIL_BLK)
    bias_c = _bias_tiles(pc, rel_bias, LANES, LANES).reshape(C_HEADS, S // LANES, LANES, LANES)
    kv_all = _memkv(mem, norm_memkv.reshape(L, 1, D), w_mkv.astype(BF16))

    for l in range(L):
        lam_init = 0.8 - 0.6 * math.exp(-0.3 * l)
        proj, proj_b, proj_i = _inproj(x.reshape(B * S, D), norm_mix[l].reshape(1, D),
                                       _in_weights(w_in[l]), tm_proj)
        proj = proj.reshape(B, S, N_MAIN)
        lamv = jnp.stack([lam_q1[l], lam_k1[l], lam_q2[l], lam_k2[l]], axis=0)
        o_a = _diff_attn(proj, lamv, bias_a, subln[l].reshape(1, A_V), lam_init, tq_a)
        o_b = _dil_attn(proj_b.reshape(B, S, N_B), bias_b)
        o_c = _dsa_attn(proj, proj_i.reshape(B, S, N_I), bias_c, tq_c, ch_c)
        x = _outmem(x, o_a, o_b, o_c, w_out[l].astype(BF16), norm_mem[l].reshape(1, D),
                    w_mq[l].astype(BF16), kv_all[l], w_mo[l].astype(BF16), tm_mem)
        wu = w_up[l].astype(BF16)
        x = _ffn(x, norm_ffn[l].reshape(1, D), wu[:, :F], wu[:, F:], conv_w[l][:, :F], conv_w[l][:, F:],
                 conv_b[l][:F].reshape(1, F), conv_b[l][F:].reshape(1, F), w_down[l].astype(BF16),
                 tm_ffn, fc)
    return _final_norm(x.reshape(B * S, D), norm_final.reshape(1, D), tm_proj).reshape(B, S, D)


def kernel(x, mem, rel_bias, norm_mix, w_in, lam_q1, lam_k1, lam_q2, lam_k2, subln, w_out,
           norm_mem, norm_memkv, w_mq, w_mkv, w_mo, norm_ffn, w_up, conv_w, conv_b, w_down,
           norm_final):
    return _forward(x, mem, rel_bias, norm_mix, w_in, lam_q1, lam_k1, lam_q2, lam_k2, subln, w_out,
                    norm_mem, norm_memkv, w_mq, w_mkv, w_mo, norm_ffn, w_up, conv_w, conv_b, w_down,
                    norm_final, tq_a=256, tq_c=128, ch_c=512, tm_proj=512, tm_mem=512, tm_ffn=512,
                    fc=256)
```

```python
import functools
import math

import jax
import jax.numpy as jnp
from jax import lax
from jax.experimental import pallas as pl
from jax.experimental.pallas import tpu as pltpu

F32 = jnp.float32
BF16 = jnp.bfloat16

EPS = 1e-6
NEG = -1e30
LANES = 128
VMEM_LIMIT = 56 * 1024 * 1024

A_HEADS, A_QK, A_V = 4, 64, 128
B_HEADS, B_DIM = 4, 64
C_HEADS, C_DIM = 4, 64
IDX_HEADS, IDX_DIM = 8, 64
TOPK_MAX = 256
DILATED_PATTERNS = ((128, 1), (512, 4), (2048, 16))
DIL_BLK = 128
REL_BUCKETS, REL_MAX_DIST = 32, 2048
MEM_HEADS = 4
CONV_WIDTH = 3

N_MAIN = 2816
N_B = 768
N_I = 256


def _cparams(sem):
    return pltpu.CompilerParams(dimension_semantics=sem, vmem_limit_bytes=VMEM_LIMIT)


def _dot(a, b):
    return jnp.dot(a, b, preferred_element_type=F32)


def _dot_nt(a, b):
    return lax.dot_general(a, b, (((1,), (1,)), ((), ())), preferred_element_type=F32)


def _rms(x, g):
    return x * lax.rsqrt(jnp.mean(x * x, axis=-1, keepdims=True) + EPS) * g


def _rel_bucket(n):
    max_exact = REL_BUCKETS // 2
    nf = jnp.maximum(n, 1).astype(F32)
    large = max_exact + (jnp.log(nf / max_exact) / math.log(REL_MAX_DIST / max_exact)
                         * (REL_BUCKETS - max_exact)).astype(jnp.int32)
    large = jnp.minimum(large, REL_BUCKETS - 1)
    return jnp.where(n < max_exact, n, large)


def _bias_tiles_kernel(par_ref, tab_ref, o_ref, *, tq, tk):
    t = pl.program_id(0)
    head, off, mult = par_ref[t, 0], par_ref[t, 1], par_ref[t, 2]
    lo, hi, sgn = par_ref[t, 3], par_ref[t, 4], par_ref[t, 5]
    dist = off + sgn * (lax.broadcasted_iota(jnp.int32, (tq, tk), 0)
                        - lax.broadcasted_iota(jnp.int32, (tq, tk), 1))
    bucket = _rel_bucket(jnp.maximum(dist * mult, 0))
    val = jnp.zeros((tq, tk), F32)
    for b in range(REL_BUCKETS):
        val = jnp.where(bucket == b, tab_ref[b, head], val)
    o_ref[...] = jnp.where((dist >= lo) & (dist <= hi), val, NEG)


def _bias_tiles(params, table, tq, tk):
    n = params.shape[0]
    return pl.pallas_call(
        functools.partial(_bias_tiles_kernel, tq=tq, tk=tk),
        grid=(n,),
        in_specs=[pl.BlockSpec(memory_space=pltpu.SMEM), pl.BlockSpec(memory_space=pltpu.SMEM)],
        out_specs=pl.BlockSpec((None, tq, tk), lambda t: (t, 0, 0)),
        out_shape=jax.ShapeDtypeStruct((n, tq, tk), F32),
        compiler_params=_cparams(("arbitrary",)),
        name="bias_tiles",
    )(params, table)


def _inproj_kernel(x_ref, g_ref, w_ref, om_ref, ob_ref, oi_ref):
    h = _rms(x_ref[...], g_ref[...]).astype(BF16)
    step = 512
    for c in range(0, N_MAIN, step):
        e = min(c + step, N_MAIN)
        om_ref[:, c:e] = _dot(h, w_ref[:, c:e]).astype(BF16)
    ob_ref[...] = _dot(h, w_ref[:, N_MAIN:N_MAIN + N_B])
    oi_ref[...] = _dot(h, w_ref[:, N_MAIN + N_B:])


def _inproj(x2, g, w, tm):
    T, D = x2.shape
    n_all = N_MAIN + N_B + N_I
    return pl.pallas_call(
        _inproj_kernel,
        grid=(T // tm,),
        in_specs=[pl.BlockSpec((tm, D), lambda i: (i, 0)),
                  pl.BlockSpec((1, D), lambda i: (0, 0)),
                  pl.BlockSpec((D, n_all), lambda i: (0, 0))],
        out_specs=[pl.BlockSpec((tm, N_MAIN), lambda i: (i, 0)),
                   pl.BlockSpec((tm, N_B), lambda i: (i, 0)),
                   pl.BlockSpec((tm, N_I), lambda i: (i, 0))],
        out_shape=[jax.ShapeDtypeStruct((T, N_MAIN), BF16),
                   jax.ShapeDtypeStruct((T, N_B), F32),
                   jax.ShapeDtypeStruct((T, N_I), F32)],
        compiler_params=_cparams(("parallel",)),
        name="inproj",
    )(x2, g, w)


def _diff_attn_kernel(lam_ref, q_ref, k_ref, vt_ref, bias_ref, g_ref, o_ref, acc_sc,
                      *, tq, tk, lam_init):
    qi = pl.program_id(2)
    last = (qi * tq + tq - 1) // tk
    lv = lam_ref[...]
    lam = (jnp.exp(jnp.sum(lv[0:1] * lv[1:2], axis=-1, keepdims=True))
           - jnp.exp(jnp.sum(lv[2:3] * lv[3:4], axis=-1, keepdims=True)) + lam_init)
    lane = lax.broadcasted_iota(jnp.int32, (tq, LANES), 1)
    q = q_ref[...].astype(F32) * (A_QK ** -0.5)
    qcat = jnp.concatenate([jnp.where(lane < A_QK, q, 0.0), jnp.where(lane >= A_QK, q, 0.0)],
                           axis=0).astype(BF16)
    acc_sc[...] = jnp.zeros(acc_sc.shape, F32)

    def scores(j):
        k = k_ref[pl.ds(pl.multiple_of(j * tk, tk), tk), :]
        bias = bias_ref[jnp.minimum(qi - j * (tk // tq), bias_ref.shape[0] - 1)]
        return _dot_nt(k, qcat) + jnp.concatenate([bias, bias], axis=1)

    def body(j, carry):
        s, m_prev, l_prev = carry
        s_next = scores(jnp.minimum(j + 1, last))
        m_new = jnp.maximum(m_prev, jnp.max(s, axis=0, keepdims=True))
        alpha = jnp.exp(m_prev - m_new)
        p = jnp.exp(s - m_new)
        l_new = alpha * l_prev + jnp.sum(p, axis=0, keepdims=True)
        acc_sc[...] = alpha * acc_sc[...] + _dot(vt_ref[j], p.astype(BF16))
        return s_next, m_new, l_new

    neg = jnp.full((1, 2 * tq), NEG, F32)
    zero = jnp.zeros((1, 2 * tq), F32)
    _, _, l = lax.fori_loop(0, last + 1, body, (scores(0), neg, zero))
    inv = 1.0 / l
    o = acc_sc[:, :tq] * inv[:, :tq] - lam * (acc_sc[:, tq:] * inv[:, tq:])
    o = o * lax.rsqrt(jnp.mean(o * o, axis=0, keepdims=True) + EPS) * g_ref[...] * (1.0 - lam_init)
    o_ref[...] = o.T.astype(o_ref.dtype)


def _diff_attn(proj, vt, lamv, bias_a, subln_g, lam_init, tq):
    B, S, _ = proj.shape
    nd, tk = bias_a.shape[1], bias_a.shape[2]
    return pl.pallas_call(
        functools.partial(_diff_attn_kernel, tq=tq, tk=tk, lam_init=lam_init),
        grid=(A_HEADS, B, S // tq),
        in_specs=[pl.BlockSpec((4, A_QK), lambda h, b, i: (0, 0)),
                  pl.BlockSpec((None, tq, LANES), lambda h, b, i: (b, i, h)),
                  pl.BlockSpec((None, S, LANES), lambda h, b, i: (b, 0, A_HEADS + h)),
                  pl.BlockSpec((None, None, S // tk, A_V, tk), lambda h, b, i: (b, h, 0, 0, 0)),
                  pl.BlockSpec((None, nd, tk, tq), lambda h, b, i: (h, 0, 0, 0)),
                  pl.BlockSpec((A_V, 1), lambda h, b, i: (0, 0))],
        out_specs=pl.BlockSpec((None, tq, LANES), lambda h, b, i: (b, i, h)),
        out_shape=jax.ShapeDtypeStruct((B, S, A_HEADS * A_V), BF16),
        scratch_shapes=[pltpu.VMEM((A_V, 2 * tq), F32)],
        compiler_params=_cparams(("parallel", "parallel", "arbitrary")),
        name="diff_attn",
    )(lamv, proj, proj, vt, bias_a, subln_g)


def _dil_kernel(q_ref, k_ref, v_ref, bias_ref, o_ref, n_sc, m_sc, l_sc, *, seq, patterns):
    blk = DIL_BLK
    lane = lax.broadcasted_iota(jnp.int32, (blk, LANES), 1)
    first = lane < B_DIM

    def attend(base, r, with_prev, p_idx):
        q = q_ref[pl.ds(base, blk, stride=r), :] * (B_DIM ** -0.5)
        kc = k_ref[pl.ds(base, blk, stride=r), :].astype(BF16)
        vc = v_ref[pl.ds(base, blk, stride=r), :].astype(BF16)
        if with_prev:
            kp = k_ref[pl.ds(base - blk * r, blk, stride=r), :].astype(BF16)
            vp = v_ref[pl.ds(base - blk * r, blk, stride=r), :].astype(BF16)
        outs, ms, ls = [], [], []
        for hh in range(2):
            qh = jnp.where(first if hh == 0 else ~first, q, 0.0).astype(BF16)
            sc = _dot_nt(qh, kc) + bias_ref[p_idx, hh, 1]
            mx = jnp.max(sc, axis=-1, keepdims=True)
            if with_prev:
                sp = _dot_nt(qh, kp) + bias_ref[p_idx, hh, 0]
                mx = jnp.maximum(mx, jnp.max(sp, axis=-1, keepdims=True))
            pc = jnp.exp(sc - mx)
            l = jnp.sum(pc, axis=-1, keepdims=True)
            o = _dot(pc.astype(BF16), vc)
            if with_prev:
                pp = jnp.exp(sp - mx)
                l = l + jnp.sum(pp, axis=-1, keepdims=True)
                o = o + _dot(pp.astype(BF16), vp)
            outs.append(o)
            ms.append(jnp.broadcast_to(mx, (blk, LANES)))
            ls.append(jnp.broadcast_to(l, (blk, LANES)))
        return (jnp.where(first, outs[0], outs[1]), jnp.where(first, ms[0], ms[1]),
                jnp.where(first, ls[0], ls[1]))

    def merge(base, r, res, is_first):
        o, m, l = res
        rows = pl.ds(base, blk, stride=r)
        if is_first:
            n_sc[rows, :] = o
            m_sc[rows, :] = m
            l_sc[rows, :] = l
        else:
            m_old = m_sc[rows, :]
            m_new = jnp.maximum(m_old, m)
            a = jnp.exp(m_old - m_new)
            b = jnp.exp(m - m_new)
            n_sc[rows, :] = a * n_sc[rows, :] + b * o
            l_sc[rows, :] = a * l_sc[rows, :] + b * l
            m_sc[rows, :] = m_new

    for p_idx, (_, r) in enumerate(patterns):
        nb = seq // (r * blk)

        def head_blocks(c, carry, r=r, p_idx=p_idx):
            merge(c, r, attend(c, r, False, p_idx), p_idx == 0)
            return carry

        lax.fori_loop(0, r, head_blocks, 0)

        def later_blocks(idx, carry, r=r, p_idx=p_idx):
            n = idx // r + 1
            c = idx % r
            base = n * blk * r + c
            merge(base, r, attend(base, r, True, p_idx), p_idx == 0)
            return carry

        lax.fori_loop(0, r * (nb - 1), later_blocks, 0)

    o_ref[...] = (n_sc[...] / l_sc[...]).astype(o_ref.dtype)


def _dil_attn(proj_b, bias_b):
    B, S, _ = proj_b.shape
    npat = bias_b.shape[0]
    return pl.pallas_call(
        functools.partial(_dil_kernel, seq=S, patterns=DILATED_PATTERNS),
        grid=(B, B_HEADS // 2),
        in_specs=[pl.BlockSpec((None, S, LANES), lambda b, hp: (b, 0, hp)),
                  pl.BlockSpec((None, S, LANES), lambda b, hp: (b, 0, 2 + hp)),
                  pl.BlockSpec((None, S, LANES), lambda b, hp: (b, 0, 4 + hp)),
                  pl.BlockSpec((npat, 2, 2, DIL_BLK, DIL_BLK), lambda b, hp: (0, hp, 0, 0, 0))],
        out_specs=pl.BlockSpec((None, S, LANES), lambda b, hp: (b, 0, hp)),
        out_shape=jax.ShapeDtypeStruct((B, S, B_HEADS * B_DIM), BF16),
        scratch_shapes=[pltpu.VMEM((S, LANES), F32)] * 3,
        compiler_params=_cparams(("parallel", "arbitrary")),
        name="dilated_attn",
    )(proj_b, proj_b, proj_b, bias_b)


def _float_key(f):
    b = pltpu.bitcast(f, jnp.int32)
    return jnp.where(b >= 0, b, b ^ jnp.int32(0x7FFFFFFF))


def _key_float(k):
    return pltpu.bitcast(jnp.where(k >= 0, k, k ^ jnp.int32(0x7FFFFFFF)), F32)


def _dsa_kernel(iq_ref, ik_ref, iwt_ref, cq_ref, ck_ref, cvt_ref, bias_ref, o_ref, sc_ref, acc_sc,
                *, tq, ch, topk):
    qi = pl.program_id(1)
    q0 = qi * tq
    nch = (q0 + tq - 1) // ch + 1
    qidx = q0 + lax.broadcasted_iota(jnp.int32, (1, tq), 1)
    lane = lax.broadcasted_iota(jnp.int32, (tq, LANES), 1)
    first = lane < IDX_DIM

    w = iwt_ref[...] * (IDX_HEADS ** -0.5)
    qs = []
    for h in range(IDX_HEADS):
        tile = iq_ref[:, (h // 2) * LANES:(h // 2 + 1) * LANES].astype(F32) * (IDX_DIM ** -0.5)
        qs.append(jnp.where(first if h % 2 == 0 else ~first, tile, 0.0).astype(BF16))

    def score_body(j, carry):
        mx, mn = carry
        start = pl.multiple_of(j * ch, ch)
        kk = ik_ref[pl.ds(start, ch), :].astype(BF16)
        sc = jnp.zeros((ch, tq), F32)
        for h in range(IDX_HEADS):
            sc = sc + jnp.maximum(_dot_nt(kk, qs[h]), 0.0) * w[h:h + 1, :]
        kidx = start + lax.broadcasted_iota(jnp.int32, (ch, 1), 0)
        valid = kidx <= qidx
        mx = jnp.maximum(mx, jnp.max(jnp.where(valid, sc, NEG), axis=0, keepdims=True))
        mn = jnp.minimum(mn, jnp.min(jnp.where(valid, sc, -NEG), axis=0, keepdims=True))
        sc_ref[j] = jnp.where(valid, sc, NEG)
        return mx, mn

    mx, mn = lax.fori_loop(0, nch, score_body,
                           (jnp.full((1, tq), NEG, F32), jnp.full((1, tq), -NEG, F32)))

    def count_where(pred):
        def body(j, acc):
            hit = jnp.where(pred(sc_ref[j]), 1.0, 0.0)
            return acc + jnp.sum(hit.reshape(ch // 8, 8, tq), axis=0)
        acc = lax.fori_loop(0, nch, body, jnp.zeros((8, tq), F32))
        return jnp.sum(acc, axis=0, keepdims=True)

    def count_ge(thr):
        return count_where(lambda sc: sc >= thr)

    kf = float(topk)
    n_valid = (qidx + 1).astype(F32)
    lo0 = _float_key(mn)
    hi0 = jnp.where(n_valid > kf, _float_key(mx), lo0)

    def search_cond(st):
        lo, hi, _, it = st
        return jnp.logical_and(jnp.max(jnp.where(lo < hi, 1.0, 0.0)) > 0.0, it < 40)

    def search_body(st):
        lo, hi, c_lo, it = st
        mid = (lo >> 1) + (hi >> 1) + ((lo | hi) & 1)
        c = count_ge(_key_float(mid))
        active = lo < hi
        ge = jnp.logical_and(active, c >= kf)
        exact = jnp.logical_and(active, c == kf)
        lt = jnp.logical_and(active, c < kf)
        lo = jnp.where(ge, mid, lo)
        c_lo = jnp.where(ge, c, c_lo)
        hi = jnp.where(exact, mid, jnp.where(lt, mid - 1, hi))
        return lo, hi, c_lo, it + 1

    lo, _, c_lo, _ = lax.while_loop(search_cond, search_body, (lo0, hi0, n_valid, jnp.int32(0)))
    thr = _key_float(lo)

    tie_rows = jnp.logical_and(n_valid > kf, c_lo > kf)

    @pl.when(jnp.max(jnp.where(tie_rows, 1.0, 0.0)) > 0.0)
    def _():
        need = kf - count_where(lambda sc: sc > thr)
        lower = (lax.broadcasted_iota(jnp.int32, (ch, ch), 1)
                 <= lax.broadcasted_iota(jnp.int32, (ch, ch), 0)).astype(BF16)

        def drop_body(j, seen):
            sc = sc_ref[j]
            eq = sc == thr
            eqf = jnp.where(eq, 1.0, 0.0)
            rank = _dot(lower, eqf.astype(BF16)) + seen
            drop = jnp.logical_and(jnp.logical_and(eq, rank > need), tie_rows)
            sc_ref[j] = jnp.where(drop, NEG, sc)
            return seen + jnp.sum(eqf, axis=0, keepdims=True)

        lax.fori_loop(0, nch, drop_body, jnp.zeros((1, tq), F32))

    cw = C_HEADS * C_DIM
    lane_c = lax.broadcasted_iota(jnp.int32, (tq, cw), 1)
    cq = cq_ref[...].astype(F32) * (C_DIM ** -0.5)
    qh = [jnp.where(lane_c // C_DIM == h, cq, 0.0).astype(BF16) for h in range(C_HEADS)]
    acc_sc[...] = jnp.zeros(acc_sc.shape, F32)
    nd = bias_ref.shape[1]

    def attn_body(j, carry):
        start = pl.multiple_of(j * ch, ch)
        kc = ck_ref[pl.ds(start, ch), :]
        sel = sc_ref[j] >= thr
        out = []
        for h in range(C_HEADS):
            bias = jnp.concatenate([
                jnp.concatenate([
                    bias_ref[h, jnp.clip(q0 // LANES + c - (j * (ch // LANES) + u), 0, nd - 1)]
                    for c in range(tq // LANES)], axis=1)
                for u in range(ch // LANES)], axis=0)
            m_prev, l_prev = carry[2 * h], carry[2 * h + 1]
            s = jnp.where(sel, _dot_nt(kc, qh[h]) + bias, NEG)
            m_new = jnp.maximum(m_prev, jnp.max(s, axis=0, keepdims=True))
            alpha = jnp.exp(m_prev - m_new)
            p = jnp.exp(s - m_new)
            out += [m_new, alpha * l_prev + jnp.sum(p, axis=0, keepdims=True)]
            vt = cvt_ref[j, h * C_DIM:(h + 1) * C_DIM, :]
            acc_sc[h] = alpha * acc_sc[h] + _dot(vt, p.astype(BF16))
        return tuple(out)

    neg = jnp.full((1, tq), NEG, F32)
    zero = jnp.zeros((1, tq), F32)
    stats = lax.fori_loop(0, nch, attn_body, (neg, zero) * C_HEADS)
    o_t = jnp.concatenate([acc_sc[h] * (1.0 / stats[2 * h + 1]) for h in range(C_HEADS)], axis=0)
    o_ref[...] = o_t.T.astype(o_ref.dtype)


def _dsa_attn(proj, proj_i, iwt, cvt, bias_c, tq, ch):
    B, S, _ = proj.shape
    topk = min(TOPK_MAX, S // 4)
    nd = bias_c.shape[1]
    cw = C_HEADS * C_DIM
    return pl.pallas_call(
        functools.partial(_dsa_kernel, tq=tq, ch=ch, topk=topk),
        grid=(B, S // tq),
        in_specs=[pl.BlockSpec((None, tq, IDX_HEADS * IDX_DIM), lambda b, i: (b, i, 3)),
                  pl.BlockSpec((None, S, LANES), lambda b, i: (b, 0, 0)),
                  pl.BlockSpec((None, IDX_HEADS, tq), lambda b, i: (b, 0, i)),
                  pl.BlockSpec((None, tq, cw), lambda b, i: (b, i, 8)),
                  pl.BlockSpec((None, S, cw), lambda b, i: (b, 0, 9)),
                  pl.BlockSpec((None, S // ch, cw, ch), lambda b, i: (b, 0, 0, 0)),
                  pl.BlockSpec((C_HEADS, nd, LANES, LANES), lambda b, i: (0, 0, 0, 0))],
        out_specs=pl.BlockSpec((None, tq, cw), lambda b, i: (b, i, 0)),
        out_shape=jax.ShapeDtypeStruct((B, S, cw), BF16),
        scratch_shapes=[pltpu.VMEM((S // ch, ch, tq), F32),
                        pltpu.VMEM((C_HEADS, C_DIM, tq), F32)],
        compiler_params=_cparams(("parallel", "arbitrary")),
        name="dsa_attn",
    )(proj, proj_i, iwt, proj, proj, cvt, bias_c)


def _memkv_kernel(mem_ref, g_ref, w_ref, o_ref):
    h = _rms(mem_ref[...], g_ref[...]).astype(BF16)
    o_ref[...] = _dot(h, w_ref[...]).astype(o_ref.dtype)


def _memkv(mem, g, w):
    B, M, D = mem.shape
    L = w.shape[0]
    return pl.pallas_call(
        _memkv_kernel,
        grid=(L, B),
        in_specs=[pl.BlockSpec((None, M, D), lambda l, b: (b, 0, 0)),
                  pl.BlockSpec((None, 1, D), lambda l, b: (l, 0, 0)),
                  pl.BlockSpec((None, D, 2 * D), lambda l, b: (l, 0, 0))],
        out_specs=pl.BlockSpec((None, None, M, 2 * D), lambda l, b: (l, b, 0, 0)),
        out_shape=jax.ShapeDtypeStruct((L, B, M, 2 * D), BF16),
        compiler_params=_cparams(("parallel", "arbitrary")),
        name="mem_kv",
    )(mem, g, w)


def _outmem_kernel(x_ref, oa_ref, ob_ref, oc_ref, wo_ref, g_ref, wq_ref, kv_ref, wmo_ref, o_ref):
    D = x_ref.shape[-1]
    na, nb = oa_ref.shape[-1], ob_ref.shape[-1]
    x = (x_ref[...] + _dot(oa_ref[...], wo_ref[0:na]) + _dot(ob_ref[...], wo_ref[na:na + nb])
         + _dot(oc_ref[...], wo_ref[na + nb:]))
    h = _rms(x, g_ref[...]).astype(BF16)
    hd = D // MEM_HEADS
    q = (_dot(h, wq_ref[...]) * (hd ** -0.5)).astype(BF16)
    outs = []
    for hh in range(MEM_HEADS):
        k = kv_ref[:, hh * hd:(hh + 1) * hd]
        v = kv_ref[:, D + hh * hd:D + (hh + 1) * hd]
        s = _dot_nt(q[:, hh * hd:(hh + 1) * hd], k)
        p = jnp.exp(s - jnp.max(s, axis=-1, keepdims=True))
        p = p / jnp.sum(p, axis=-1, keepdims=True)
        outs.append(_dot(p.astype(BF16), v).astype(BF16))
    o = jnp.concatenate(outs, axis=-1)
    o_ref[...] = x + _dot(o, wmo_ref[...])


def _outmem(x, oa, ob, oc, wo, g, wq, kv, wmo, tm):
    B, S, D = x.shape
    M = kv.shape[1]
    const = lambda b, i: (0, 0)
    return pl.pallas_call(
        _outmem_kernel,
        grid=(B, S // tm),
        in_specs=[pl.BlockSpec((None, tm, D), lambda b, i: (b, i, 0)),
                  pl.BlockSpec((None, tm, oa.shape[-1]), lambda b, i: (b, i, 0)),
                  pl.BlockSpec((None, tm, ob.shape[-1]), lambda b, i: (b, i, 0)),
                  pl.BlockSpec((None, tm, oc.shape[-1]), lambda b, i: (b, i, 0)),
                  pl.BlockSpec(wo.shape, const),
                  pl.BlockSpec((1, D), const),
                  pl.BlockSpec(wq.shape, const),
                  pl.BlockSpec((None, M, 2 * D), lambda b, i: (b, 0, 0)),
                  pl.BlockSpec(wmo.shape, const)],
        out_specs=pl.BlockSpec((None, tm, D), lambda b, i: (b, i, 0)),
        out_shape=jax.ShapeDtypeStruct((B, S, D), F32),
        compiler_params=_cparams(("parallel", "arbitrary")),
        name="outproj_memattn",
    )(x, oa, ob, oc, wo, g, wq, kv, wmo)


HALO = 8


def _ffn_kernel(x_ref, xp_ref, g_ref, wg_ref, wv_ref, cwg_ref, cwv_ref, cbg_ref, cbv_ref, wd_ref,
                o_ref, *, tm, fc):
    i = pl.program_id(1)
    x = x_ref[...]
    g = g_ref[...]
    hp = _rms(xp_ref[...], g) * jnp.where(i > 0, 1.0, 0.0)
    h = jnp.concatenate([hp, _rms(x, g)], axis=0).astype(BF16)
    F = wd_ref.shape[0]

    def conv(u, cw_ref, cb_ref, c, e):
        out = cb_ref[:, c:e]
        for j in range(CONV_WIDTH):
            shift = CONV_WIDTH - 1 - j
            out = out + cw_ref[j:j + 1, c:e] * u[HALO - shift:HALO - shift + tm]
        return out

    acc = x
    for c in range(0, F, fc):
        e = min(c + fc, F)
        gate = conv(_dot(h, wg_ref[:, c:e]), cwg_ref, cbg_ref, c, e)
        val = conv(_dot(h, wv_ref[:, c:e]), cwv_ref, cbv_ref, c, e)
        act = (gate * jax.nn.sigmoid(gate) * val).astype(BF16)
        acc = acc + _dot(act, wd_ref[c:e, :])
    o_ref[...] = acc


def _ffn(x, g, wg, wv, cwg, cwv, cbg, cbv, wd, tm, fc):
    B, S, D = x.shape
    F = wd.shape[0]
    const = lambda b, i: (0, 0)
    hb = tm // HALO
    return pl.pallas_call(
        functools.partial(_ffn_kernel, tm=tm, fc=fc),
        grid=(B, S // tm),
        in_specs=[pl.BlockSpec((None, tm, D), lambda b, i: (b, i, 0)),
                  pl.BlockSpec((None, HALO, D), lambda b, i: (b, jnp.maximum(i * hb - 1, 0), 0)),
                  pl.BlockSpec((1, D), const),
                  pl.BlockSpec((D, F), const), pl.BlockSpec((D, F), const),
                  pl.BlockSpec((CONV_WIDTH, F), const), pl.BlockSpec((CONV_WIDTH, F), const),
                  pl.BlockSpec((1, F), const), pl.BlockSpec((1, F), const),
                  pl.BlockSpec((F, D), const)],
        out_specs=pl.BlockSpec((None, tm, D), lambda b, i: (b, i, 0)),
        out_shape=jax.ShapeDtypeStruct((B, S, D), F32),
        compiler_params=_cparams(("parallel", "arbitrary")),
        name="conv_ffn",
    )(x, x, g, wg, wv, cwg, cwv, cbg, cbv, wd)


def _final_norm_kernel(x_ref, g_ref, o_ref):
    o_ref[...] = _rms(x_ref[...], g_ref[...])


def _final_norm(x2, g, tm):
    T, D = x2.shape
    return pl.pallas_call(
        _final_norm_kernel,
        grid=(T // tm,),
        in_specs=[pl.BlockSpec((tm, D), lambda i: (i, 0)), pl.BlockSpec((1, D), lambda i: (0, 0))],
        out_specs=pl.BlockSpec((tm, D), lambda i: (i, 0)),
        out_shape=jax.ShapeDtypeStruct((T, D), F32),
        compiler_params=_cparams(("parallel",)),
        name="final_norm",
    )(x2, g)


def _tile_params(S, tq_a, tk_a):
    big = 1 << 30
    pa = [(h, d * tq_a, 1, 0, big, -1) for h in range(A_HEADS)
          for d in range(min(S // tq_a, (REL_MAX_DIST + tk_a) // tq_a + 1))]
    pb = []
    for _, r in DILATED_PATTERNS:
        for h in range(B_HEADS):
            pb.append((A_HEADS + h, DIL_BLK, r, 1, DIL_BLK, 1))
            pb.append((A_HEADS + h, 0, r, 0, DIL_BLK, 1))
    pc = [(A_HEADS + B_HEADS + h, d * LANES, 1, -big, big, -1)
          for h in range(C_HEADS) for d in range(_n_offsets(S, LANES))]
    to = lambda p: jnp.asarray(p, jnp.int32)
    return to(pa), to(pb), to(pc)


def _n_offsets(S, blk):
    return min(S // blk, REL_MAX_DIST // blk + 2)


def _in_weights(w_in_l):
    sizes = (512, 512, 512, 256, 256, 256, 256, 256, 256, 512, 64, 8)
    offs = [0]
    for s in sizes:
        offs.append(offs[-1] + s)
    (aq, ak, av, bq, bk, bv, cq, ck, cv, iq, ik, iw) = [w_in_l[:, offs[i]:offs[i + 1]] for i in range(12)]
    pad = jnp.zeros((w_in_l.shape[0], N_I - 2 * IDX_DIM - IDX_HEADS), w_in_l.dtype)
    return jnp.concatenate([aq, ak, av, iq, cq, ck, cv, bq, bk, bv, ik, ik, iw, pad], axis=1).astype(BF16)


def _forward(x, mem, rel_bias, norm_mix, w_in, lam_q1, lam_k1, lam_q2, lam_k2, subln, w_out,
             norm_mem, norm_memkv, w_mq, w_mkv, w_mo, norm_ffn, w_up, conv_w, conv_b, w_down,
             norm_final, *, tq_a, tk_a, tq_c, ch_c, tm_proj, tm_mem, tm_ffn, fc):
    B, S, D = x.shape
    L = w_in.shape[0]
    F = w_down.shape[1]
    pa, pb, pc = _tile_params(S, tq_a, tk_a)
    bias_a = _bias_tiles(pa, rel_bias, tk_a, tq_a).reshape(A_HEADS, -1, tk_a, tq_a)
    bias_b = _bias_tiles(pb, rel_bias, DIL_BLK, DIL_BLK).reshape(
        len(DILATED_PATTERNS), B_HEADS, 2, DIL_BLK, DIL_BLK)
    bias_c = _bias_tiles(pc, rel_bias, LANES, LANES).reshape(C_HEADS, -1, LANES, LANES)
    kv_all = _memkv(mem, norm_memkv.reshape(L, 1, D), w_mkv.astype(BF16))

    for l in range(L):
        lam_init = 0.8 - 0.6 * math.exp(-0.3 * l)
        proj, proj_b, proj_i = _inproj(x.reshape(B * S, D), norm_mix[l].reshape(1, D),
                                       _in_weights(w_in[l]), tm_proj)
        proj = proj.reshape(B, S, N_MAIN)
        lamv = jnp.stack([lam_q1[l], lam_k1[l], lam_q2[l], lam_k2[l]], axis=0)
        avt = proj[:, :, 2 * A_HEADS * LANES:3 * A_HEADS * LANES].reshape(
            B, S // tk_a, tk_a, A_HEADS, A_V).transpose(0, 3, 1, 4, 2)
        cw = C_HEADS * C_DIM
        cvt = proj[:, :, N_MAIN - cw:].reshape(B, S // ch_c, ch_c, cw).transpose(0, 1, 3, 2)
        proj_i = proj_i.reshape(B, S, N_I)
        iwt = proj_i[:, :, 2 * IDX_DIM:2 * IDX_DIM + IDX_HEADS].transpose(0, 2, 1)
        o_a = _diff_attn(proj, avt, lamv, bias_a, subln[l].reshape(A_V, 1), lam_init, tq_a)
        o_b = _dil_attn(proj_b.reshape(B, S, N_B), bias_b)
        o_c = _dsa_attn(proj, proj_i, iwt, cvt, bias_c, tq_c, ch_c)
        x = _outmem(x, o_a, o_b, o_c, w_out[l].astype(BF16), norm_mem[l].reshape(1, D),
                    w_mq[l].astype(BF16), kv_all[l], w_mo[l].astype(BF16), tm_mem)
        wu = w_up[l].astype(BF16)
        x = _ffn(x, norm_ffn[l].reshape(1, D), wu[:, :F], wu[:, F:], conv_w[l][:, :F], conv_w[l][:, F:],
                 conv_b[l][:F].reshape(1, F), conv_b[l][F:].reshape(1, F), w_down[l].astype(BF16),
                 tm_ffn, fc)
    return _final_norm(x.reshape(B * S, D), norm_final.reshape(1, D), tm_proj).reshape(B, S, D)


def kernel(x, mem, rel_bias, norm_mix, w_in, lam_q1, lam_k1, lam_q2, lam_k2, subln, w_out,
           norm_mem, norm_memkv, w_mq, w_mkv, w_mo, norm_ffn, w_up, conv_w, conv_b, w_down,
           norm_final):
    return _forward(x, mem, rel_bias, norm_mix, w_in, lam_q1, lam_k1, lam_q2, lam_k2, subln, w_out,
                    norm_mem, norm_memkv, w_mq, w_mkv, w_mo, norm_ffn, w_up, conv_w, conv_b, w_down,
                    norm_final, tq_a=256, tk_a=512, tq_c=256, ch_c=512, tm_proj=512, tm_mem=512, tm_ffn=512,
                    fc=256)
```

```python
import functools
import math

import jax
import jax.numpy as jnp
from jax import lax
from jax.experimental import pallas as pl
from jax.experimental.pallas import tpu as pltpu

F32 = jnp.float32
BF16 = jnp.bfloat16

EPS = 1e-6
NEG = -1e30
LANES = 128
VMEM_LIMIT = 56 * 1024 * 1024

A_HEADS, A_QK, A_V = 4, 64, 128
B_HEADS, B_DIM = 4, 64
C_HEADS, C_DIM = 4, 64
IDX_HEADS, IDX_DIM = 8, 64
TOPK_MAX = 256
DILATED_PATTERNS = ((128, 1), (512, 4), (2048, 16))
DIL_BLK = 128
REL_BUCKETS, REL_MAX_DIST = 32, 2048
MEM_HEADS = 4
CONV_WIDTH = 3

N_MAIN = 2816
N_B = 768
N_I = 256


def _cparams(sem):
    return pltpu.CompilerParams(dimension_semantics=sem, vmem_limit_bytes=VMEM_LIMIT)


def _dot(a, b):
    return jnp.dot(a, b, preferred_element_type=F32)


def _dot_nt(a, b):
    return lax.dot_general(a, b, (((1,), (1,)), ((), ())), preferred_element_type=F32)


def _rms(x, g):
    return x * lax.rsqrt(jnp.mean(x * x, axis=-1, keepdims=True) + EPS) * g


def _rel_bucket(n):
    max_exact = REL_BUCKETS // 2
    nf = jnp.maximum(n, 1).astype(F32)
    large = max_exact + (jnp.log(nf / max_exact) / math.log(REL_MAX_DIST / max_exact)
                         * (REL_BUCKETS - max_exact)).astype(jnp.int32)
    large = jnp.minimum(large, REL_BUCKETS - 1)
    return jnp.where(n < max_exact, n, large)


def _bias_tiles_kernel(par_ref, tab_ref, o_ref, *, tq, tk):
    t = pl.program_id(0)
    head, off, mult = par_ref[t, 0], par_ref[t, 1], par_ref[t, 2]
    lo, hi, sgn = par_ref[t, 3], par_ref[t, 4], par_ref[t, 5]
    dist = off + sgn * (lax.broadcasted_iota(jnp.int32, (tq, tk), 0)
                        - lax.broadcasted_iota(jnp.int32, (tq, tk), 1))
    bucket = _rel_bucket(jnp.maximum(dist * mult, 0))
    val = jnp.zeros((tq, tk), F32)
    for b in range(REL_BUCKETS):
        val = jnp.where(bucket == b, tab_ref[b, head], val)
    o_ref[...] = jnp.where((dist >= lo) & (dist <= hi), val, NEG)


def _bias_tiles(params, table, tq, tk):
    n = params.shape[0]
    return pl.pallas_call(
        functools.partial(_bias_tiles_kernel, tq=tq, tk=tk),
        grid=(n,),
        in_specs=[pl.BlockSpec(memory_space=pltpu.SMEM), pl.BlockSpec(memory_space=pltpu.SMEM)],
        out_specs=pl.BlockSpec((None, tq, tk), lambda t: (t, 0, 0)),
        out_shape=jax.ShapeDtypeStruct((n, tq, tk), F32),
        compiler_params=_cparams(("arbitrary",)),
        name="bias_tiles",
    )(params, table)


def _inproj_kernel(x_ref, g_ref, w_ref, om_ref, ob_ref, oi_ref):
    h = _rms(x_ref[...], g_ref[...]).astype(BF16)
    step = 512
    for c in range(0, N_MAIN, step):
        e = min(c + step, N_MAIN)
        om_ref[:, c:e] = _dot(h, w_ref[:, c:e]).astype(BF16)
    ob_ref[...] = _dot(h, w_ref[:, N_MAIN:N_MAIN + N_B])
    oi_ref[...] = _dot(h, w_ref[:, N_MAIN + N_B:])


def _inproj(x2, g, w, tm):
    T, D = x2.shape
    n_all = N_MAIN + N_B + N_I
    return pl.pallas_call(
        _inproj_kernel,
        grid=(T // tm,),
        in_specs=[pl.BlockSpec((tm, D), lambda i: (i, 0)),
                  pl.BlockSpec((1, D), lambda i: (0, 0)),
                  pl.BlockSpec((D, n_all), lambda i: (0, 0))],
        out_specs=[pl.BlockSpec((tm, N_MAIN), lambda i: (i, 0)),
                   pl.BlockSpec((tm, N_B), lambda i: (i, 0)),
                   pl.BlockSpec((tm, N_I), lambda i: (i, 0))],
        out_shape=[jax.ShapeDtypeStruct((T, N_MAIN), BF16),
                   jax.ShapeDtypeStruct((T, N_B), F32),
                   jax.ShapeDtypeStruct((T, N_I), F32)],
        compiler_params=_cparams(("parallel",)),
        name="inproj",
    )(x2, g, w)


def _diff_attn_kernel(lam_ref, q_ref, k_ref, vt_ref, bias_ref, g_ref, o_ref, acc_sc,
                      *, tq, tk, lam_init):
    qi = pl.program_id(2)
    last = (qi * tq + tq - 1) // tk
    lv = lam_ref[...]
    lam = (jnp.exp(jnp.sum(lv[0:1] * lv[1:2], axis=-1, keepdims=True))
           - jnp.exp(jnp.sum(lv[2:3] * lv[3:4], axis=-1, keepdims=True)) + lam_init)
    lane = lax.broadcasted_iota(jnp.int32, (tq, LANES), 1)
    q = q_ref[...].astype(F32) * (A_QK ** -0.5)
    qcat = jnp.concatenate([jnp.where(lane < A_QK, q, 0.0), jnp.where(lane >= A_QK, q, 0.0)],
                           axis=0).astype(BF16)
    acc_sc[...] = jnp.zeros(acc_sc.shape, F32)

    def scores(j):
        k = k_ref[pl.ds(pl.multiple_of(j * tk, tk), tk), :]
        bias = bias_ref[jnp.minimum(qi - j * (tk // tq), bias_ref.shape[0] - 1)]
        return _dot_nt(k, qcat) + jnp.concatenate([bias, bias], axis=1)

    def body(j, carry):
        s, m_prev, l_prev = carry
        s_next = scores(jnp.minimum(j + 1, last))
        m_new = jnp.maximum(m_prev, jnp.max(s, axis=0, keepdims=True))
        alpha = jnp.exp(m_prev - m_new)
        p = jnp.exp(s - m_new)
        l_new = alpha * l_prev + jnp.sum(p, axis=0, keepdims=True)
        acc_sc[...] = alpha * acc_sc[...] + _dot(vt_ref[j], p.astype(BF16))
        return s_next, m_new, l_new

    neg = jnp.full((1, 2 * tq), NEG, F32)
    zero = jnp.zeros((1, 2 * tq), F32)
    _, _, l = lax.fori_loop(0, last + 1, body, (scores(0), neg, zero))
    inv = 1.0 / l
    o = acc_sc[:, :tq] * inv[:, :tq] - lam * (acc_sc[:, tq:] * inv[:, tq:])
    o = o * lax.rsqrt(jnp.mean(o * o, axis=0, keepdims=True) + EPS) * g_ref[...] * (1.0 - lam_init)
    o_ref[...] = o.T.astype(o_ref.dtype)


def _diff_attn(proj, vt, lamv, bias_a, subln_g, lam_init, tq):
    B, S, _ = proj.shape
    nd, tk = bias_a.shape[1], bias_a.shape[2]
    return pl.pallas_call(
        functools.partial(_diff_attn_kernel, tq=tq, tk=tk, lam_init=lam_init),
        grid=(A_HEADS, B, S // tq),
        in_specs=[pl.BlockSpec((4, A_QK), lambda h, b, i: (0, 0)),
                  pl.BlockSpec((None, tq, LANES), lambda h, b, i: (b, i, h)),
                  pl.BlockSpec((None, S, LANES), lambda h, b, i: (b, 0, A_HEADS + h)),
                  pl.BlockSpec((None, None, S // tk, A_V, tk), lambda h, b, i: (b, h, 0, 0, 0)),
                  pl.BlockSpec((None, nd, tk, tq), lambda h, b, i: (h, 0, 0, 0)),
                  pl.BlockSpec((A_V, 1), lambda h, b, i: (0, 0))],
        out_specs=pl.BlockSpec((None, tq, LANES), lambda h, b, i: (b, i, h)),
        out_shape=jax.ShapeDtypeStruct((B, S, A_HEADS * A_V), BF16),
        scratch_shapes=[pltpu.VMEM((A_V, 2 * tq), F32)],
        compiler_params=_cparams(("parallel", "parallel", "arbitrary")),
        name="diff_attn",
    )(lamv, proj, proj, vt, bias_a, subln_g)


def _dil_kernel(q_ref, k_ref, v_ref, bias_ref, o_ref, n_sc, m_sc, l_sc, *, seq, patterns):
    blk = DIL_BLK
    lane = lax.broadcasted_iota(jnp.int32, (blk, LANES), 1)
    first = lane < B_DIM

    def attend(base, r, with_prev, p_idx):
        q = q_ref[pl.ds(base, blk, stride=r), :] * (B_DIM ** -0.5)
        kc = k_ref[pl.ds(base, blk, stride=r), :].astype(BF16)
        vc = v_ref[pl.ds(base, blk, stride=r), :].astype(BF16)
        if with_prev:
            kp = k_ref[pl.ds(base - blk * r, blk, stride=r), :].astype(BF16)
            vp = v_ref[pl.ds(base - blk * r, blk, stride=r), :].astype(BF16)
        outs, ms, ls = [], [], []
        for hh in range(2):
            qh = jnp.where(first if hh == 0 else ~first, q, 0.0).astype(BF16)
            sc = _dot_nt(qh, kc) + bias_ref[p_idx, hh, 1]
            mx = jnp.max(sc, axis=-1, keepdims=True)
            if with_prev:
                sp = _dot_nt(qh, kp) + bias_ref[p_idx, hh, 0]
                mx = jnp.maximum(mx, jnp.max(sp, axis=-1, keepdims=True))
            pc = jnp.exp(sc - mx)
            l = jnp.sum(pc, axis=-1, keepdims=True)
            o = _dot(pc.astype(BF16), vc)
            if with_prev:
                pp = jnp.exp(sp - mx)
                l = l + jnp.sum(pp, axis=-1, keepdims=True)
                o = o + _dot(pp.astype(BF16), vp)
            outs.append(o)
            ms.append(jnp.broadcast_to(mx, (blk, LANES)))
            ls.append(jnp.broadcast_to(l, (blk, LANES)))
        return (jnp.where(first, outs[0], outs[1]), jnp.where(first, ms[0], ms[1]),
                jnp.where(first, ls[0], ls[1]))

    def merge(base, r, res, is_first):
        o, m, l = res
        rows = pl.ds(base, blk, stride=r)
        if is_first:
            n_sc[rows, :] = o
            m_sc[rows, :] = m
            l_sc[rows, :] = l
        else:
            m_old = m_sc[rows, :]
            m_new = jnp.maximum(m_old, m)
            a = jnp.exp(m_old - m_new)
            b = jnp.exp(m - m_new)
            n_sc[rows, :] = a * n_sc[rows, :] + b * o
            l_sc[rows, :] = a * l_sc[rows, :] + b * l
            m_sc[rows, :] = m_new

    for p_idx, (_, r) in enumerate(patterns):
        nb = seq // (r * blk)

        def head_blocks(c, carry, r=r, p_idx=p_idx):
            merge(c, r, attend(c, r, False, p_idx), p_idx == 0)
            return carry

        lax.fori_loop(0, r, head_blocks, 0)

        def later_blocks(idx, carry, r=r, p_idx=p_idx):
            n = idx // r + 1
            c = idx % r
            base = n * blk * r + c
            merge(base, r, attend(base, r, True, p_idx), p_idx == 0)
            return carry

        lax.fori_loop(0, r * (nb - 1), later_blocks, 0)

    o_ref[...] = (n_sc[...] / l_sc[...]).astype(o_ref.dtype)


def _dil_attn(proj_b, bias_b):
    B, S, _ = proj_b.shape
    npat = bias_b.shape[0]
    return pl.pallas_call(
        functools.partial(_dil_kernel, seq=S, patterns=DILATED_PATTERNS),
        grid=(B, B_HEADS // 2),
        in_specs=[pl.BlockSpec((None, S, LANES), lambda b, hp: (b, 0, hp)),
                  pl.BlockSpec((None, S, LANES), lambda b, hp: (b, 0, 2 + hp)),
                  pl.BlockSpec((None, S, LANES), lambda b, hp: (b, 0, 4 + hp)),
                  pl.BlockSpec((npat, 2, 2, DIL_BLK, DIL_BLK), lambda b, hp: (0, hp, 0, 0, 0))],
        out_specs=pl.BlockSpec((None, S, LANES), lambda b, hp: (b, 0, hp)),
        out_shape=jax.ShapeDtypeStruct((B, S, B_HEADS * B_DIM), BF16),
        scratch_shapes=[pltpu.VMEM((S, LANES), F32)] * 3,
        compiler_params=_cparams(("parallel", "arbitrary")),
        name="dilated_attn",
    )(proj_b, proj_b, proj_b, bias_b)


def _float_key(f):
    b = pltpu.bitcast(f, jnp.int32)
    return jnp.where(b >= 0, b, b ^ jnp.int32(0x7FFFFFFF))


def _key_float(k):
    return pltpu.bitcast(jnp.where(k >= 0, k, k ^ jnp.int32(0x7FFFFFFF)), F32)


def _dsa_kernel(iq_ref, ik_ref, iwt_ref, cq_ref, ck_ref, cvt_ref, bias_ref, o_ref, sc_ref, acc_sc,
                *, tq, ch, ca, topk):
    qi = pl.program_id(1)
    q0 = qi * tq
    nch = (q0 + tq - 1) // ch + 1
    qidx = q0 + lax.broadcasted_iota(jnp.int32, (1, tq), 1)
    lane = lax.broadcasted_iota(jnp.int32, (tq, LANES), 1)
    first = lane < IDX_DIM

    w = iwt_ref[...] * (IDX_HEADS ** -0.5)
    qs = []
    for h in range(IDX_HEADS):
        tile = iq_ref[:, (h // 2) * LANES:(h // 2 + 1) * LANES].astype(F32) * (IDX_DIM ** -0.5)
        qs.append(jnp.where(first if h % 2 == 0 else ~first, tile, 0.0).astype(BF16))

    def score_body(j, carry):
        mx, mn = carry
        start = pl.multiple_of(j * ch, ch)
        kk = ik_ref[pl.ds(start, ch), :].astype(BF16)
        sc = jnp.zeros((ch, tq), F32)
        for h in range(IDX_HEADS):
            sc = sc + jnp.maximum(_dot_nt(kk, qs[h]), 0.0) * w[h:h + 1, :]
        kidx = start + lax.broadcasted_iota(jnp.int32, (ch, 1), 0)
        valid = kidx <= qidx
        mx = jnp.maximum(mx, jnp.max(jnp.where(valid, sc, NEG), axis=0, keepdims=True))
        mn = jnp.minimum(mn, jnp.min(jnp.where(valid, sc, -NEG), axis=0, keepdims=True))
        sc_ref[j] = jnp.where(valid, sc, NEG)
        return mx, mn

    mx, mn = lax.fori_loop(0, nch, score_body,
                           (jnp.full((1, tq), NEG, F32), jnp.full((1, tq), -NEG, F32)))

    def count_where(pred):
        rows = 32
        def body(j, acc):
            hit = jnp.where(pred(sc_ref[j]), 1.0, 0.0)
            return acc + jnp.sum(hit.reshape(ch // rows, rows, tq), axis=0)
        acc = lax.fori_loop(0, nch, body, jnp.zeros((rows, tq), F32))
        return jnp.sum(acc, axis=0, keepdims=True)

    def count_ge(thr):
        return count_where(lambda sc: sc >= thr)

    kf = float(topk)
    n_valid = (qidx + 1).astype(F32)
    lo0 = _float_key(mn)
    hi0 = jnp.where(n_valid > kf, _float_key(mx), lo0)

    def search_cond(st):
        lo, hi, _, it = st
        return jnp.logical_and(jnp.max(jnp.where(lo < hi, 1.0, 0.0)) > 0.0, it < 40)

    def search_body(st):
        lo, hi, c_lo, it = st
        mid = (lo >> 1) + (hi >> 1) + ((lo | hi) & 1)
        c = count_ge(_key_float(mid))
        active = lo < hi
        ge = jnp.logical_and(active, c >= kf)
        exact = jnp.logical_and(active, c == kf)
        lt = jnp.logical_and(active, c < kf)
        lo = jnp.where(ge, mid, lo)
        c_lo = jnp.where(ge, c, c_lo)
        hi = jnp.where(exact, mid, jnp.where(lt, mid - 1, hi))
        return lo, hi, c_lo, it + 1

    lo, _, c_lo, _ = lax.while_loop(search_cond, search_body, (lo0, hi0, n_valid, jnp.int32(0)))
    thr = _key_float(lo)

    tie_rows = jnp.logical_and(n_valid > kf, c_lo > kf)

    @pl.when(jnp.max(jnp.where(tie_rows, 1.0, 0.0)) > 0.0)
    def _():
        need = kf - count_where(lambda sc: sc > thr)
        lower = (lax.broadcasted_iota(jnp.int32, (ch, ch), 1)
                 <= lax.broadcasted_iota(jnp.int32, (ch, ch), 0)).astype(BF16)

        def drop_body(j, seen):
            sc = sc_ref[j]
            eq = sc == thr
            eqf = jnp.where(eq, 1.0, 0.0)
            rank = _dot(lower, eqf.astype(BF16)) + seen
            drop = jnp.logical_and(jnp.logical_and(eq, rank > need), tie_rows)
            sc_ref[j] = jnp.where(drop, NEG, sc)
            return seen + jnp.sum(eqf, axis=0, keepdims=True)

        lax.fori_loop(0, nch, drop_body, jnp.zeros((1, tq), F32))

    cw = C_HEADS * C_DIM
    lane_c = lax.broadcasted_iota(jnp.int32, (tq, cw), 1)
    cq = cq_ref[...].astype(F32) * (C_DIM ** -0.5)
    qcat = jnp.concatenate([jnp.where(lane_c // C_DIM == h, cq, 0.0) for h in range(C_HEADS)],
                           axis=0).astype(BF16)
    acc_sc[...] = jnp.zeros(acc_sc.shape, F32)
    nd = bias_ref.shape[1]
    last = (q0 + tq - 1) // ca

    def scores(j):
        start = pl.multiple_of(j * ca, ca)
        kc = ck_ref[pl.ds(start, ca), :]
        sub = pl.multiple_of((j % (ch // ca)) * ca, ca)
        sel = sc_ref[j // (ch // ca), pl.ds(sub, ca), :] >= thr
        bias = jnp.concatenate([
            jnp.concatenate([
                bias_ref[h, jnp.clip(q0 // LANES + c - (j * (ca // LANES) + u), 0, nd - 1)]
                for h in range(C_HEADS) for c in range(tq // LANES)], axis=1)
            for u in range(ca // LANES)], axis=0)
        return jnp.where(jnp.concatenate([sel] * C_HEADS, axis=1), _dot_nt(kc, qcat) + bias, NEG)

    def attn_body(j, carry):
        s, m_prev, l_prev = carry
        s_next = scores(jnp.minimum(j + 1, last))
        m_new = jnp.maximum(m_prev, jnp.max(s, axis=0, keepdims=True))
        alpha = jnp.exp(m_prev - m_new)
        p = jnp.exp(s - m_new)
        l_new = alpha * l_prev + jnp.sum(p, axis=0, keepdims=True)
        pb = p.astype(BF16)
        for h in range(C_HEADS):
            cols = slice(h * tq, (h + 1) * tq)
            vt = cvt_ref[j, h * C_DIM:(h + 1) * C_DIM, :]
            acc_sc[h] = alpha[:, cols] * acc_sc[h] + _dot(vt, pb[:, cols])
        return s_next, m_new, l_new

    neg = jnp.full((1, C_HEADS * tq), NEG, F32)
    zero = jnp.zeros((1, C_HEADS * tq), F32)
    _, _, l = lax.fori_loop(0, last + 1, attn_body, (scores(0), neg, zero))
    inv = 1.0 / l
    o_t = jnp.concatenate([acc_sc[h] * inv[:, h * tq:(h + 1) * tq] for h in range(C_HEADS)], axis=0)
    o_ref[...] = o_t.T.astype(o_ref.dtype)


def _dsa_attn(proj, proj_i, iwt, cvt, bias_c, tq, ch):
    B, S, _ = proj.shape
    ca = cvt.shape[-1]
    topk = min(TOPK_MAX, S // 4)
    nd = bias_c.shape[1]
    cw = C_HEADS * C_DIM
    return pl.pallas_call(
        functools.partial(_dsa_kernel, tq=tq, ch=ch, ca=ca, topk=topk),
        grid=(B, S // tq),
        in_specs=[pl.BlockSpec((None, tq, IDX_HEADS * IDX_DIM), lambda b, i: (b, i, 3)),
                  pl.BlockSpec((None, S, LANES), lambda b, i: (b, 0, 0)),
                  pl.BlockSpec((None, IDX_HEADS, tq), lambda b, i: (b, 0, i)),
                  pl.BlockSpec((None, tq, cw), lambda b, i: (b, i, 8)),
                  pl.BlockSpec((None, S, cw), lambda b, i: (b, 0, 9)),
                  pl.BlockSpec((None, S // ca, cw, ca), lambda b, i: (b, 0, 0, 0)),
                  pl.BlockSpec((C_HEADS, nd, LANES, LANES), lambda b, i: (0, 0, 0, 0))],
        out_specs=pl.BlockSpec((None, tq, cw), lambda b, i: (b, i, 0)),
        out_shape=jax.ShapeDtypeStruct((B, S, cw), BF16),
        scratch_shapes=[pltpu.VMEM((S // ch, ch, tq), F32),
                        pltpu.VMEM((C_HEADS, C_DIM, tq), F32)],
        compiler_params=_cparams(("parallel", "arbitrary")),
        name="dsa_attn",
    )(proj, proj_i, iwt, proj, proj, cvt, bias_c)


def _memkv_kernel(mem_ref, g_ref, w_ref, o_ref):
    h = _rms(mem_ref[...], g_ref[...]).astype(BF16)
    o_ref[...] = _dot(h, w_ref[...]).astype(o_ref.dtype)


def _memkv(mem, g, w):
    B, M, D = mem.shape
    L = w.shape[0]
    return pl.pallas_call(
        _memkv_kernel,
        grid=(L, B),
        in_specs=[pl.BlockSpec((None, M, D), lambda l, b: (b, 0, 0)),
                  pl.BlockSpec((None, 1, D), lambda l, b: (l, 0, 0)),
                  pl.BlockSpec((None, D, 2 * D), lambda l, b: (l, 0, 0))],
        out_specs=pl.BlockSpec((None, None, M, 2 * D), lambda l, b: (l, b, 0, 0)),
        out_shape=jax.ShapeDtypeStruct((L, B, M, 2 * D), BF16),
        compiler_params=_cparams(("parallel", "arbitrary")),
        name="mem_kv",
    )(mem, g, w)


def _outmem_kernel(x_ref, oa_ref, ob_ref, oc_ref, wo_ref, g_ref, wq_ref, kv_ref, wmo_ref, o_ref):
    D = x_ref.shape[-1]
    na, nb = oa_ref.shape[-1], ob_ref.shape[-1]
    x = (x_ref[...] + _dot(oa_ref[...], wo_ref[0:na]) + _dot(ob_ref[...], wo_ref[na:na + nb])
         + _dot(oc_ref[...], wo_ref[na + nb:]))
    h = _rms(x, g_ref[...]).astype(BF16)
    hd = D // MEM_HEADS
    q = (_dot(h, wq_ref[...]) * (hd ** -0.5)).astype(BF16)
    outs = []
    for hh in range(MEM_HEADS):
        k = kv_ref[:, hh * hd:(hh + 1) * hd]
        v = kv_ref[:, D + hh * hd:D + (hh + 1) * hd]
        s = _dot_nt(q[:, hh * hd:(hh + 1) * hd], k)
        p = jnp.exp(s - jnp.max(s, axis=-1, keepdims=True))
        p = p / jnp.sum(p, axis=-1, keepdims=True)
        outs.append(_dot(p.astype(BF16), v).astype(BF16))
    o = jnp.concatenate(outs, axis=-1)
    o_ref[...] = x + _dot(o, wmo_ref[...])


def _outmem(x, oa, ob, oc, wo, g, wq, kv, wmo, tm):
    B, S, D = x.shape
    M = kv.shape[1]
    const = lambda b, i: (0, 0)
    return pl.pallas_call(
        _outmem_kernel,
        grid=(B, S // tm),
        in_specs=[pl.BlockSpec((None, tm, D), lambda b, i: (b, i, 0)),
                  pl.BlockSpec((None, tm, oa.shape[-1]), lambda b, i: (b, i, 0)),
                  pl.BlockSpec((None, tm, ob.shape[-1]), lambda b, i: (b, i, 0)),
                  pl.BlockSpec((None, tm, oc.shape[-1]), lambda b, i: (b, i, 0)),
                  pl.BlockSpec(wo.shape, const),
                  pl.BlockSpec((1, D), const),
                  pl.BlockSpec(wq.shape, const),
                  pl.BlockSpec((None, M, 2 * D), lambda b, i: (b, 0, 0)),
                  pl.BlockSpec(wmo.shape, const)],
        out_specs=pl.BlockSpec((None, tm, D), lambda b, i: (b, i, 0)),
        out_shape=jax.ShapeDtypeStruct((B, S, D), F32),
        compiler_params=_cparams(("parallel", "arbitrary")),
        name="outproj_memattn",
    )(x, oa, ob, oc, wo, g, wq, kv, wmo)


HALO = 8


def _ffn_kernel(x_ref, xp_ref, g_ref, wg_ref, wv_ref, cwg_ref, cwv_ref, cbg_ref, cbv_ref, wd_ref,
                o_ref, *, tm, fc):
    i = pl.program_id(1)
    x = x_ref[...]
    g = g_ref[...]
    hp = _rms(xp_ref[...], g) * jnp.where(i > 0, 1.0, 0.0)
    h = jnp.concatenate([hp, _rms(x, g)], axis=0).astype(BF16)
    F = wd_ref.shape[0]

    def conv(u, cw_ref, cb_ref, c, e):
        out = cb_ref[:, c:e]
        for j in range(CONV_WIDTH):
            shift = CONV_WIDTH - 1 - j
            out = out + cw_ref[j:j + 1, c:e] * u[HALO - shift:HALO - shift + tm]
        return out

    acc = x
    for c in range(0, F, fc):
        e = min(c + fc, F)
        gate = conv(_dot(h, wg_ref[:, c:e]), cwg_ref, cbg_ref, c, e)
        val = conv(_dot(h, wv_ref[:, c:e]), cwv_ref, cbv_ref, c, e)
        act = (gate * jax.nn.sigmoid(gate) * val).astype(BF16)
        acc = acc + _dot(act, wd_ref[c:e, :])
    o_ref[...] = acc


def _ffn(x, g, wg, wv, cwg, cwv, cbg, cbv, wd, tm, fc):
    B, S, D = x.shape
    F = wd.shape[0]
    const = lambda b, i: (0, 0)
    hb = tm // HALO
    return pl.pallas_call(
        functools.partial(_ffn_kernel, tm=tm, fc=fc),
        grid=(B, S // tm),
        in_specs=[pl.BlockSpec((None, tm, D), lambda b, i: (b, i, 0)),
                  pl.BlockSpec((None, HALO, D), lambda b, i: (b, jnp.maximum(i * hb - 1, 0), 0)),
                  pl.BlockSpec((1, D), const),
                  pl.BlockSpec((D, F), const), pl.BlockSpec((D, F), const),
                  pl.BlockSpec((CONV_WIDTH, F), const), pl.BlockSpec((CONV_WIDTH, F), const),
                  pl.BlockSpec((1, F), const), pl.BlockSpec((1, F), const),
                  pl.BlockSpec((F, D), const)],
        out_specs=pl.BlockSpec((None, tm, D), lambda b, i: (b, i, 0)),
        out_shape=jax.ShapeDtypeStruct((B, S, D), F32),
        compiler_params=_cparams(("parallel", "arbitrary")),
        name="conv_ffn",
    )(x, x, g, wg, wv, cwg, cwv, cbg, cbv, wd)


def _final_norm_kernel(x_ref, g_ref, o_ref):
    o_ref[...] = _rms(x_ref[...], g_ref[...])


def _final_norm(x2, g, tm):
    T, D = x2.shape
    return pl.pallas_call(
        _final_norm_kernel,
        grid=(T // tm,),
        in_specs=[pl.BlockSpec((tm, D), lambda i: (i, 0)), pl.BlockSpec((1, D), lambda i: (0, 0))],
        out_specs=pl.BlockSpec((tm, D), lambda i: (i, 0)),
        out_shape=jax.ShapeDtypeStruct((T, D), F32),
        compiler_params=_cparams(("parallel",)),
        name="final_norm",
    )(x2, g)


def _tile_params(S, tq_a, tk_a):
    big = 1 << 30
    pa = [(h, d * tq_a, 1, 0, big, -1) for h in range(A_HEADS)
          for d in range(min(S // tq_a, (REL_MAX_DIST + tk_a) // tq_a + 1))]
    pb = []
    for _, r in DILATED_PATTERNS:
        for h in range(B_HEADS):
            pb.append((A_HEADS + h, DIL_BLK, r, 1, DIL_BLK, 1))
            pb.append((A_HEADS + h, 0, r, 0, DIL_BLK, 1))
    pc = [(A_HEADS + B_HEADS + h, d * LANES, 1, -big, big, -1)
          for h in range(C_HEADS) for d in range(_n_offsets(S, LANES))]
    to = lambda p: jnp.asarray(p, jnp.int32)
    return to(pa), to(pb), to(pc)


def _n_offsets(S, blk):
    return min(S // blk, REL_MAX_DIST // blk + 2)


def _in_weights(w_in_l):
    sizes = (512, 512, 512, 256, 256, 256, 256, 256, 256, 512, 64, 8)
    offs = [0]
    for s in sizes:
        offs.append(offs[-1] + s)
    (aq, ak, av, bq, bk, bv, cq, ck, cv, iq, ik, iw) = [w_in_l[:, offs[i]:offs[i + 1]] for i in range(12)]
    pad = jnp.zeros((w_in_l.shape[0], N_I - 2 * IDX_DIM - IDX_HEADS), w_in_l.dtype)
    return jnp.concatenate([aq, ak, av, iq, cq, ck, cv, bq, bk, bv, ik, ik, iw, pad], axis=1).astype(BF16)


def _forward(x, mem, rel_bias, norm_mix, w_in, lam_q1, lam_k1, lam_q2, lam_k2, subln, w_out,
             norm_mem, norm_memkv, w_mq, w_mkv, w_mo, norm_ffn, w_up, conv_w, conv_b, w_down,
             norm_final, *, tq_a, tk_a, tq_c, ch_c, ca_c, tm_proj, tm_mem, tm_ffn, fc):
    B, S, D = x.shape
    L = w_in.shape[0]
    F = w_down.shape[1]
    pa, pb, pc = _tile_params(S, tq_a, tk_a)
    bias_a = _bias_tiles(pa, rel_bias, tk_a, tq_a).reshape(A_HEADS, -1, tk_a, tq_a)
    bias_b = _bias_tiles(pb, rel_bias, DIL_BLK, DIL_BLK).reshape(
        len(DILATED_PATTERNS), B_HEADS, 2, DIL_BLK, DIL_BLK)
    bias_c = _bias_tiles(pc, rel_bias, LANES, LANES).reshape(C_HEADS, -1, LANES, LANES)
    kv_all = _memkv(mem, norm_memkv.reshape(L, 1, D), w_mkv.astype(BF16))

    for l in range(L):
        lam_init = 0.8 - 0.6 * math.exp(-0.3 * l)
        proj, proj_b, proj_i = _inproj(x.reshape(B * S, D), norm_mix[l].reshape(1, D),
                                       _in_weights(w_in[l]), tm_proj)
        proj = proj.reshape(B, S, N_MAIN)
        lamv = jnp.stack([lam_q1[l], lam_k1[l], lam_q2[l], lam_k2[l]], axis=0)
        avt = proj[:, :, 2 * A_HEADS * LANES:3 * A_HEADS * LANES].reshape(
            B, S // tk_a, tk_a, A_HEADS, A_V).transpose(0, 3, 1, 4, 2)
        cw = C_HEADS * C_DIM
        cvt = proj[:, :, N_MAIN - cw:].reshape(B, S // ca_c, ca_c, cw).transpose(0, 1, 3, 2)
        proj_i = proj_i.reshape(B, S, N_I)
        iwt = proj_i[:, :, 2 * IDX_DIM:2 * IDX_DIM + IDX_HEADS].transpose(0, 2, 1)
        o_a = _diff_attn(proj, avt, lamv, bias_a, subln[l].reshape(A_V, 1), lam_init, tq_a)
        o_b = _dil_attn(proj_b.reshape(B, S, N_B), bias_b)
        o_c = _dsa_attn(proj, proj_i, iwt, cvt, bias_c, tq_c, ch_c)
        x = _outmem(x, o_a, o_b, o_c, w_out[l].astype(BF16), norm_mem[l].reshape(1, D),
                    w_mq[l].astype(BF16), kv_all[l], w_mo[l].astype(BF16), tm_mem)
        wu = w_up[l].astype(BF16)
        x = _ffn(x, norm_ffn[l].reshape(1, D), wu[:, :F], wu[:, F:], conv_w[l][:, :F], conv_w[l][:, F:],
                 conv_b[l][:F].reshape(1, F), conv_b[l][F:].reshape(1, F), w_down[l].astype(BF16),
                 tm_ffn, fc)
    return _final_norm(x.reshape(B * S, D), norm_final.reshape(1, D), tm_proj).reshape(B, S, D)


def kernel(x, mem, rel_bias, norm_mix, w_in, lam_q1, lam_k1, lam_q2, lam_k2, subln, w_out,
           norm_mem, norm_memkv, w_mq, w_mkv, w_mo, norm_ffn, w_up, conv_w, conv_b, w_down,
           norm_final):
    return _forward(x, mem, rel_bias, norm_mix, w_in, lam_q1, lam_k1, lam_q2, lam_k2, subln, w_out,
                    norm_mem, norm_memkv, w_mq, w_mkv, w_mo, norm_ffn, w_up, conv_w, conv_b, w_down,
                    norm_final, tq_a=256, tk_a=512, tq_c=256, ch_c=512, ca_c=256, tm_proj=512, tm_mem=512, tm_ffn=512,
                    fc=256)
```

```python
import functools
import math

import jax
import jax.numpy as jnp
from jax import lax
from jax.experimental import pallas as pl
from jax.experimental.pallas import tpu as pltpu

F32 = jnp.float32
BF16 = jnp.bfloat16

EPS = 1e-6
NEG = -1e30
LOG2E = math.log2(math.e)
LANES = 128
VMEM_LIMIT = 56 * 1024 * 1024

A_HEADS, A_QK, A_V = 4, 64, 128
B_HEADS, B_DIM = 4, 64
C_HEADS, C_DIM = 4, 64
IDX_HEADS, IDX_DIM = 8, 64
TOPK_MAX = 256
DILATED_PATTERNS = ((128, 1), (512, 4), (2048, 16))
DIL_BLK = 128
REL_BUCKETS, REL_MAX_DIST = 32, 2048
MEM_HEADS = 4
CONV_WIDTH = 3

N_MAIN = 2816
N_B = 768
N_I = 256


def _cparams(sem):
    return pltpu.CompilerParams(dimension_semantics=sem, vmem_limit_bytes=VMEM_LIMIT)


def _dot(a, b):
    return jnp.dot(a, b, preferred_element_type=F32)


def _dot_nt(a, b):
    return lax.dot_general(a, b, (((1,), (1,)), ((), ())), preferred_element_type=F32)


def _rms(x, g):
    return x * lax.rsqrt(jnp.mean(x * x, axis=-1, keepdims=True) + EPS) * g


def _rel_bucket(n):
    max_exact = REL_BUCKETS // 2
    nf = jnp.maximum(n, 1).astype(F32)
    large = max_exact + (jnp.log(nf / max_exact) / math.log(REL_MAX_DIST / max_exact)
                         * (REL_BUCKETS - max_exact)).astype(jnp.int32)
    large = jnp.minimum(large, REL_BUCKETS - 1)
    return jnp.where(n < max_exact, n, large)


def _bias_tiles_kernel(par_ref, tab_ref, o_ref, *, tq, tk, scale):
    t = pl.program_id(0)
    head, off, mult = par_ref[t, 0], par_ref[t, 1], par_ref[t, 2]
    lo, hi, sgn = par_ref[t, 3], par_ref[t, 4], par_ref[t, 5]
    dist = off + sgn * (lax.broadcasted_iota(jnp.int32, (tq, tk), 0)
                        - lax.broadcasted_iota(jnp.int32, (tq, tk), 1))
    bucket = _rel_bucket(jnp.maximum(dist * mult, 0))
    val = jnp.zeros((tq, tk), F32)
    for b in range(REL_BUCKETS):
        val = jnp.where(bucket == b, tab_ref[b, head] * scale, val)
    o_ref[...] = jnp.where((dist >= lo) & (dist <= hi), val, NEG)


def _bias_tiles(params, table, tq, tk, scale=1.0):
    n = params.shape[0]
    return pl.pallas_call(
        functools.partial(_bias_tiles_kernel, tq=tq, tk=tk, scale=scale),
        grid=(n,),
        in_specs=[pl.BlockSpec(memory_space=pltpu.SMEM), pl.BlockSpec(memory_space=pltpu.SMEM)],
        out_specs=pl.BlockSpec((None, tq, tk), lambda t: (t, 0, 0)),
        out_shape=jax.ShapeDtypeStruct((n, tq, tk), F32),
        compiler_params=_cparams(("arbitrary",)),
        name="bias_tiles",
    )(params, table)


def _inproj_kernel(x_ref, g_ref, w_ref, om_ref, ob_ref, oi_ref):
    h = _rms(x_ref[...], g_ref[...]).astype(BF16)
    step = 512
    for c in range(0, N_MAIN, step):
        e = min(c + step, N_MAIN)
        om_ref[:, c:e] = _dot(h, w_ref[:, c:e]).astype(BF16)
    ob_ref[...] = _dot(h, w_ref[:, N_MAIN:N_MAIN + N_B])
    oi_ref[...] = _dot(h, w_ref[:, N_MAIN + N_B:])


def _inproj(x2, g, w, tm):
    T, D = x2.shape
    n_all = N_MAIN + N_B + N_I
    return pl.pallas_call(
        _inproj_kernel,
        grid=(T // tm,),
        in_specs=[pl.BlockSpec((tm, D), lambda i: (i, 0)),
                  pl.BlockSpec((1, D), lambda i: (0, 0)),
                  pl.BlockSpec((D, n_all), lambda i: (0, 0))],
        out_specs=[pl.BlockSpec((tm, N_MAIN), lambda i: (i, 0)),
                   pl.BlockSpec((tm, N_B), lambda i: (i, 0)),
                   pl.BlockSpec((tm, N_I), lambda i: (i, 0))],
        out_shape=[jax.ShapeDtypeStruct((T, N_MAIN), BF16),
                   jax.ShapeDtypeStruct((T, N_B), F32),
                   jax.ShapeDtypeStruct((T, N_I), F32)],
        compiler_params=_cparams(("parallel",)),
        name="inproj",
    )(x2, g, w)


def _diff_attn_kernel(lam_ref, q_ref, k_ref, vt_ref, bias_ref, g_ref, o_ref, s_sc, acc_sc,
                      *, tq, tk, d_min, d_const, lam_init):
    qi = pl.program_id(2)
    gran = min(tq, tk)
    last = (qi * tq + tq - 1) // tk
    masked_tile = d_const - d_min + 1
    lv = lam_ref[...]
    lam = (jnp.exp(jnp.sum(lv[0:1] * lv[1:2], axis=-1, keepdims=True))
           - jnp.exp(jnp.sum(lv[2:3] * lv[3:4], axis=-1, keepdims=True)) + lam_init)
    lane = lax.broadcasted_iota(jnp.int32, (tq, LANES), 1)
    q = q_ref[...].astype(F32) * (A_QK ** -0.5 * LOG2E)
    qcat = jnp.concatenate([jnp.where(lane < A_QK, q, 0.0), jnp.where(lane >= A_QK, q, 0.0)],
                           axis=0).astype(BF16)
    acc_sc[...] = jnp.zeros(acc_sc.shape, F32)

    def scores_into(slot, j):
        jc = jnp.minimum(j, last)
        k = k_ref[pl.ds(pl.multiple_of(jc * tk, tk), tk), :]
        d = qi * (tq // gran) - jc * (tk // gran)
        bias = bias_ref[jnp.where(j <= last, jnp.minimum(d, d_const) - d_min, masked_tile)]
        s_sc[slot] = _dot_nt(k, qcat) + jnp.concatenate([bias, bias], axis=1)

    def absorb(slot, j, carry):
        m_prev, l_prev = carry
        m_new = jnp.maximum(m_prev, jnp.max(s_sc[slot], axis=0, keepdims=True))
        alpha = jnp.exp2(m_prev - m_new)
        p = jnp.exp2(s_sc[slot] - m_new)
        l_new = alpha * l_prev + jnp.sum(p, axis=0, keepdims=True)
        vt = vt_ref[jnp.minimum(j, last)]
        acc_sc[...] = alpha * acc_sc[...] + _dot(vt, p.astype(BF16))
        return m_new, l_new

    scores_into(0, 0)

    def body(jj, carry):
        a = 2 * jj
        scores_into(1, a + 1)
        carry = absorb(0, a, carry)
        scores_into(0, a + 2)
        return absorb(1, a + 1, carry)

    neg = jnp.full((1, 2 * tq), NEG, F32)
    zero = jnp.zeros((1, 2 * tq), F32)
    _, l = lax.fori_loop(0, (last + 2) // 2, body, (neg, zero))
    inv = 1.0 / l
    o = acc_sc[:, :tq] * inv[:, :tq] - lam * (acc_sc[:, tq:] * inv[:, tq:])
    o = o * lax.rsqrt(jnp.mean(o * o, axis=0, keepdims=True) + EPS) * g_ref[...] * (1.0 - lam_init)
    o_ref[...] = o.T.astype(o_ref.dtype)


def _diff_attn(proj, vt, lamv, bias_a, subln_g, lam_init, tq, d_min, d_const):
    B, S, _ = proj.shape
    nd, tk = bias_a.shape[1], bias_a.shape[2]
    return pl.pallas_call(
        functools.partial(_diff_attn_kernel, tq=tq, tk=tk, d_min=d_min, d_const=d_const,
                          lam_init=lam_init),
        grid=(A_HEADS, B, S // tq),
        in_specs=[pl.BlockSpec((4, A_QK), lambda h, b, i: (0, 0)),
                  pl.BlockSpec((None, tq, LANES), lambda h, b, i: (b, i, h)),
                  pl.BlockSpec((None, S, LANES), lambda h, b, i: (b, 0, A_HEADS + h)),
                  pl.BlockSpec((None, None, S // tk, A_V, tk), lambda h, b, i: (b, h, 0, 0, 0)),
                  pl.BlockSpec((None, nd, tk, tq), lambda h, b, i: (h, 0, 0, 0)),
                  pl.BlockSpec((A_V, 1), lambda h, b, i: (0, 0))],
        out_specs=pl.BlockSpec((None, tq, LANES), lambda h, b, i: (b, i, h)),
        out_shape=jax.ShapeDtypeStruct((B, S, A_HEADS * A_V), BF16),
        scratch_shapes=[pltpu.VMEM((2, tk, 2 * tq), F32), pltpu.VMEM((A_V, 2 * tq), F32)],
        compiler_params=_cparams(("parallel", "parallel", "arbitrary")),
        name="diff_attn",
    )(lamv, proj, proj, vt, bias_a, subln_g)


def _dil_kernel(q_ref, k_ref, v_ref, bias_ref, o_ref, n_sc, m_sc, l_sc, *, seq, patterns):
    blk = DIL_BLK
    lane = lax.broadcasted_iota(jnp.int32, (blk, LANES), 1)
    first = lane < B_DIM

    def attend(base, r, with_prev, p_idx):
        q = q_ref[pl.ds(base, blk, stride=r), :] * (B_DIM ** -0.5)
        kc = k_ref[pl.ds(base, blk, stride=r), :].astype(BF16)
        vc = v_ref[pl.ds(base, blk, stride=r), :].astype(BF16)
        if with_prev:
            kp = k_ref[pl.ds(base - blk * r, blk, stride=r), :].astype(BF16)
            vp = v_ref[pl.ds(base - blk * r, blk, stride=r), :].astype(BF16)
        outs, ms, ls = [], [], []
        for hh in range(2):
            qh = jnp.where(first if hh == 0 else ~first, q, 0.0).astype(BF16)
            sc = _dot_nt(qh, kc) + bias_ref[p_idx, hh, 1]
            mx = jnp.max(sc, axis=-1, keepdims=True)
            if with_prev:
                sp = _dot_nt(qh, kp) + bias_ref[p_idx, hh, 0]
                mx = jnp.maximum(mx, jnp.max(sp, axis=-1, keepdims=True))
            pc = jnp.exp(sc - mx)
            l = jnp.sum(pc, axis=-1, keepdims=True)
            o = _dot(pc.astype(BF16), vc)
            if with_prev:
                pp = jnp.exp(sp - mx)
                l = l + jnp.sum(pp, axis=-1, keepdims=True)
                o = o + _dot(pp.astype(BF16), vp)
            outs.append(o)
            ms.append(jnp.broadcast_to(mx, (blk, LANES)))
            ls.append(jnp.broadcast_to(l, (blk, LANES)))
        return (jnp.where(first, outs[0], outs[1]), jnp.where(first, ms[0], ms[1]),
                jnp.where(first, ls[0], ls[1]))

    def merge(base, r, res, is_first):
        o, m, l = res
        rows = pl.ds(base, blk, stride=r)
        if is_first:
            n_sc[rows, :] = o
            m_sc[rows, :] = m
            l_sc[rows, :] = l
        else:
            m_old = m_sc[rows, :]
            m_new = jnp.maximum(m_old, m)
            a = jnp.exp(m_old - m_new)
            b = jnp.exp(m - m_new)
            n_sc[rows, :] = a * n_sc[rows, :] + b * o
            l_sc[rows, :] = a * l_sc[rows, :] + b * l
            m_sc[rows, :] = m_new

    for p_idx, (_, r) in enumerate(patterns):
        nb = seq // (r * blk)

        def head_blocks(c, carry, r=r, p_idx=p_idx):
            merge(c, r, attend(c, r, False, p_idx), p_idx == 0)
            return carry

        lax.fori_loop(0, r, head_blocks, 0)

        def later_blocks(idx, carry, r=r, p_idx=p_idx):
            n = idx // r + 1
            c = idx % r
            base = n * blk * r + c
            merge(base, r, attend(base, r, True, p_idx), p_idx == 0)
            return carry

        lax.fori_loop(0, r * (nb - 1), later_blocks, 0)

    o_ref[...] = (n_sc[...] / l_sc[...]).astype(o_ref.dtype)


def _dil_attn(proj_b, bias_b):
    B, S, _ = proj_b.shape
    npat = bias_b.shape[0]
    return pl.pallas_call(
        functools.partial(_dil_kernel, seq=S, patterns=DILATED_PATTERNS),
        grid=(B, B_HEADS // 2),
        in_specs=[pl.BlockSpec((None, S, LANES), lambda b, hp: (b, 0, hp)),
                  pl.BlockSpec((None, S, LANES), lambda b, hp: (b, 0, 2 + hp)),
                  pl.BlockSpec((None, S, LANES), lambda b, hp: (b, 0, 4 + hp)),
                  pl.BlockSpec((npat, 2, 2, DIL_BLK, DIL_BLK), lambda b, hp: (0, hp, 0, 0, 0))],
        out_specs=pl.BlockSpec((None, S, LANES), lambda b, hp: (b, 0, hp)),
        out_shape=jax.ShapeDtypeStruct((B, S, B_HEADS * B_DIM), BF16),
        scratch_shapes=[pltpu.VMEM((S, LANES), F32)] * 3,
        compiler_params=_cparams(("parallel", "arbitrary")),
        name="dilated_attn",
    )(proj_b, proj_b, proj_b, bias_b)


def _float_key(f):
    b = pltpu.bitcast(f, jnp.int32)
    return jnp.where(b >= 0, b, b ^ jnp.int32(0x7FFFFFFF))


def _key_float(k):
    return pltpu.bitcast(jnp.where(k >= 0, k, k ^ jnp.int32(0x7FFFFFFF)), F32)


def _dsa_kernel(iq_ref, ik_ref, iwt_ref, cq_ref, ck_ref, cvt_ref, bias_ref, o_ref, sc_ref, s_sc, acc_sc,
                *, tq, ch, ca, topk):
    qi = pl.program_id(1)
    q0 = qi * tq
    nch = (q0 + tq - 1) // ch + 1
    qidx = q0 + lax.broadcasted_iota(jnp.int32, (1, tq), 1)
    lane = lax.broadcasted_iota(jnp.int32, (tq, LANES), 1)
    first = lane < IDX_DIM

    w = iwt_ref[...] * (IDX_HEADS ** -0.5)
    qs = []
    for h in range(IDX_HEADS):
        tile = iq_ref[:, (h // 2) * LANES:(h // 2 + 1) * LANES].astype(F32) * (IDX_DIM ** -0.5)
        qs.append(jnp.where(first if h % 2 == 0 else ~first, tile, 0.0).astype(BF16))

    def score_body(j, carry):
        mx, mn = carry
        start = pl.multiple_of(j * ch, ch)
        kk = ik_ref[pl.ds(start, ch), :].astype(BF16)
        sc = jnp.zeros((ch, tq), F32)
        for h in range(IDX_HEADS):
            sc = sc + jnp.maximum(_dot_nt(kk, qs[h]), 0.0) * w[h:h + 1, :]
        kidx = start + lax.broadcasted_iota(jnp.int32, (ch, 1), 0)
        valid = kidx <= qidx
        mx = jnp.maximum(mx, jnp.max(jnp.where(valid, sc, NEG), axis=0, keepdims=True))
        mn = jnp.minimum(mn, jnp.min(jnp.where(valid, sc, -NEG), axis=0, keepdims=True))
        sc_ref[j] = jnp.where(valid, sc, NEG)
        return mx, mn

    mx, mn = lax.fori_loop(0, nch, score_body,
                           (jnp.full((1, tq), NEG, F32), jnp.full((1, tq), -NEG, F32)))

    def count_where(pred):
        rows = 32
        def body(j, acc):
            hit = jnp.where(pred(sc_ref[j]), 1.0, 0.0)
            return acc + jnp.sum(hit.reshape(ch // rows, rows, tq), axis=0)
        acc = lax.fori_loop(0, nch, body, jnp.zeros((rows, tq), F32))
        return jnp.sum(acc, axis=0, keepdims=True)

    def count_ge(thr):
        return count_where(lambda sc: sc >= thr)

    kf = float(topk)
    n_valid = (qidx + 1).astype(F32)
    lo0 = _float_key(mn)
    hi0 = jnp.where(n_valid > kf, _float_key(mx), lo0)

    def search_cond(st):
        lo, hi, _, it = st
        return jnp.logical_and(jnp.max(jnp.where(lo < hi, 1.0, 0.0)) > 0.0, it < 40)

    def search_body(st):
        lo, hi, c_lo, it = st
        mid = (lo >> 1) + (hi >> 1) + ((lo | hi) & 1)
        c = count_ge(_key_float(mid))
        active = lo < hi
        ge = jnp.logical_and(active, c >= kf)
        exact = jnp.logical_and(active, c == kf)
        lt = jnp.logical_and(active, c < kf)
        lo = jnp.where(ge, mid, lo)
        c_lo = jnp.where(ge, c, c_lo)
        hi = jnp.where(exact, mid, jnp.where(lt, mid - 1, hi))
        return lo, hi, c_lo, it + 1

    lo, _, c_lo, _ = lax.while_loop(search_cond, search_body, (lo0, hi0, n_valid, jnp.int32(0)))
    thr = _key_float(lo)

    tie_rows = jnp.logical_and(n_valid > kf, c_lo > kf)

    @pl.when(jnp.max(jnp.where(tie_rows, 1.0, 0.0)) > 0.0)
    def _():
        need = kf - count_where(lambda sc: sc > thr)
        lower = (lax.broadcasted_iota(jnp.int32, (ch, ch), 1)
                 <= lax.broadcasted_iota(jnp.int32, (ch, ch), 0)).astype(BF16)

        def drop_body(j, seen):
            sc = sc_ref[j]
            eq = sc == thr
            eqf = jnp.where(eq, 1.0, 0.0)
            rank = _dot(lower, eqf.astype(BF16)) + seen
            drop = jnp.logical_and(jnp.logical_and(eq, rank > need), tie_rows)
            sc_ref[j] = jnp.where(drop, NEG, sc)
            return seen + jnp.sum(eqf, axis=0, keepdims=True)

        lax.fori_loop(0, nch, drop_body, jnp.zeros((1, tq), F32))

    cw = C_HEADS * C_DIM
    lane_c = lax.broadcasted_iota(jnp.int32, (tq, cw), 1)
    cq = cq_ref[...].astype(F32) * (C_DIM ** -0.5 * LOG2E)
    qcat = jnp.concatenate([jnp.where(lane_c // C_DIM == h, cq, 0.0) for h in range(C_HEADS)],
                           axis=0).astype(BF16)
    acc_sc[...] = jnp.zeros(acc_sc.shape, F32)
    nd = bias_ref.shape[1]
    last = (q0 + tq - 1) // ca

    def scores_into(slot, j):
        jc = jnp.minimum(j, last)
        kc = ck_ref[pl.ds(pl.multiple_of(jc * ca, ca), ca), :]
        sub = pl.multiple_of((jc % (ch // ca)) * ca, ca)
        sel = sc_ref[jc // (ch // ca), pl.ds(sub, ca), :] >= jnp.where(j <= last, thr, -NEG)
        bias = jnp.concatenate([
            jnp.concatenate([
                bias_ref[h, jnp.clip(q0 // LANES + c - (jc * (ca // LANES) + u), 0, nd - 1)]
                for h in range(C_HEADS) for c in range(tq // LANES)], axis=1)
            for u in range(ca // LANES)], axis=0)
        s_sc[slot] = jnp.where(jnp.concatenate([sel] * C_HEADS, axis=1),
                               _dot_nt(kc, qcat) + bias, NEG)

    def absorb(slot, j, carry):
        m_prev, l_prev = carry
        m_new = jnp.maximum(m_prev, jnp.max(s_sc[slot], axis=0, keepdims=True))
        alpha = jnp.exp2(m_prev - m_new)
        p = jnp.exp2(s_sc[slot] - m_new)
        l_new = alpha * l_prev + jnp.sum(p, axis=0, keepdims=True)
        pb = p.astype(BF16)
        jc = jnp.minimum(j, last)
        for h in range(C_HEADS):
            cols = slice(h * tq, (h + 1) * tq)
            vt = cvt_ref[jc, h * C_DIM:(h + 1) * C_DIM, :]
            acc_sc[h] = alpha[:, cols] * acc_sc[h] + _dot(vt, pb[:, cols])
        return m_new, l_new

    scores_into(0, 0)

    def attn_body(jj, carry):
        a = 2 * jj
        scores_into(1, a + 1)
        carry = absorb(0, a, carry)
        scores_into(0, a + 2)
        return absorb(1, a + 1, carry)

    neg = jnp.full((1, C_HEADS * tq), NEG, F32)
    zero = jnp.zeros((1, C_HEADS * tq), F32)
    _, l = lax.fori_loop(0, (last + 2) // 2, attn_body, (neg, zero))
    inv = 1.0 / l
    o_t = jnp.concatenate([acc_sc[h] * inv[:, h * tq:(h + 1) * tq] for h in range(C_HEADS)], axis=0)
    o_ref[...] = o_t.T.astype(o_ref.dtype)


def _dsa_attn(proj, proj_i, iwt, cvt, bias_c, tq, ch):
    B, S, _ = proj.shape
    ca = cvt.shape[-1]
    topk = min(TOPK_MAX, S // 4)
    nd = bias_c.shape[1]
    cw = C_HEADS * C_DIM
    return pl.pallas_call(
        functools.partial(_dsa_kernel, tq=tq, ch=ch, ca=ca, topk=topk),
        grid=(B, S // tq),
        in_specs=[pl.BlockSpec((None, tq, IDX_HEADS * IDX_DIM), lambda b, i: (b, i, 3)),
                  pl.BlockSpec((None, S, LANES), lambda b, i: (b, 0, 0)),
                  pl.BlockSpec((None, IDX_HEADS, tq), lambda b, i: (b, 0, i)),
                  pl.BlockSpec((None, tq, cw), lambda b, i: (b, i, 8)),
                  pl.BlockSpec((None, S, cw), lambda b, i: (b, 0, 9)),
                  pl.BlockSpec((None, S // ca, cw, ca), lambda b, i: (b, 0, 0, 0)),
                  pl.BlockSpec((C_HEADS, nd, LANES, LANES), lambda b, i: (0, 0, 0, 0))],
        out_specs=pl.BlockSpec((None, tq, cw), lambda b, i: (b, i, 0)),
        out_shape=jax.ShapeDtypeStruct((B, S, cw), BF16),
        scratch_shapes=[pltpu.VMEM((S // ch, ch, tq), F32),
                        pltpu.VMEM((2, ca, C_HEADS * tq), F32),
                        pltpu.VMEM((C_HEADS, C_DIM, tq), F32)],
        compiler_params=_cparams(("parallel", "arbitrary")),
        name="dsa_attn",
    )(proj, proj_i, iwt, proj, proj, cvt, bias_c)


def _memkv_kernel(mem_ref, g_ref, w_ref, o_ref):
    h = _rms(mem_ref[...], g_ref[...]).astype(BF16)
    o_ref[...] = _dot(h, w_ref[...]).astype(o_ref.dtype)


def _memkv(mem, g, w):
    B, M, D = mem.shape
    L = w.shape[0]
    return pl.pallas_call(
        _memkv_kernel,
        grid=(L, B),
        in_specs=[pl.BlockSpec((None, M, D), lambda l, b: (b, 0, 0)),
                  pl.BlockSpec((None, 1, D), lambda l, b: (l, 0, 0)),
                  pl.BlockSpec((None, D, 2 * D), lambda l, b: (l, 0, 0))],
        out_specs=pl.BlockSpec((None, None, M, 2 * D), lambda l, b: (l, b, 0, 0)),
        out_shape=jax.ShapeDtypeStruct((L, B, M, 2 * D), BF16),
        compiler_params=_cparams(("parallel", "arbitrary")),
        name="mem_kv",
    )(mem, g, w)


def _outmem_kernel(x_ref, oa_ref, ob_ref, oc_ref, wo_ref, g_ref, wq_ref, kv_ref, wmo_ref, o_ref):
    D = x_ref.shape[-1]
    na, nb = oa_ref.shape[-1], ob_ref.shape[-1]
    x = (x_ref[...] + _dot(oa_ref[...], wo_ref[0:na]) + _dot(ob_ref[...], wo_ref[na:na + nb])
         + _dot(oc_ref[...], wo_ref[na + nb:]))
    h = _rms(x, g_ref[...]).astype(BF16)
    hd = D // MEM_HEADS
    q = (_dot(h, wq_ref[...]) * (hd ** -0.5)).astype(BF16)
    outs = []
    for hh in range(MEM_HEADS):
        k = kv_ref[:, hh * hd:(hh + 1) * hd]
        v = kv_ref[:, D + hh * hd:D + (hh + 1) * hd]
        s = _dot_nt(q[:, hh * hd:(hh + 1) * hd], k)
        p = jnp.exp(s - jnp.max(s, axis=-1, keepdims=True))
        p = p / jnp.sum(p, axis=-1, keepdims=True)
        outs.append(_dot(p.astype(BF16), v).astype(BF16))
    o = jnp.concatenate(outs, axis=-1)
    o_ref[...] = x + _dot(o, wmo_ref[...])


def _outmem(x, oa, ob, oc, wo, g, wq, kv, wmo, tm):
    B, S, D = x.shape
    M = kv.shape[1]
    const = lambda b, i: (0, 0)
    return pl.pallas_call(
        _outmem_kernel,
        grid=(B, S // tm),
        in_specs=[pl.BlockSpec((None, tm, D), lambda b, i: (b, i, 0)),
                  pl.BlockSpec((None, tm, oa.shape[-1]), lambda b, i: (b, i, 0)),
                  pl.BlockSpec((None, tm, ob.shape[-1]), lambda b, i: (b, i, 0)),
                  pl.BlockSpec((None, tm, oc.shape[-1]), lambda b, i: (b, i, 0)),
                  pl.BlockSpec(wo.shape, const),
                  pl.BlockSpec((1, D), const),
                  pl.BlockSpec(wq.shape, const),
                  pl.BlockSpec((None, M, 2 * D), lambda b, i: (b, 0, 0)),
                  pl.BlockSpec(wmo.shape, const)],
        out_specs=pl.BlockSpec((None, tm, D), lambda b, i: (b, i, 0)),
        out_shape=jax.ShapeDtypeStruct((B, S, D), F32),
        compiler_params=_cparams(("parallel", "arbitrary")),
        name="outproj_memattn",
    )(x, oa, ob, oc, wo, g, wq, kv, wmo)


HALO = 8


def _ffn_kernel(x_ref, xp_ref, g_ref, wg_ref, wv_ref, cwg_ref, cwv_ref, cbg_ref, cbv_ref, wd_ref,
                o_ref, *, tm, fc):
    i = pl.program_id(1)
    x = x_ref[...]
    g = g_ref[...]
    hp = _rms(xp_ref[...], g) * jnp.where(i > 0, 1.0, 0.0)
    h = jnp.concatenate([hp, _rms(x, g)], axis=0).astype(BF16)
    F = wd_ref.shape[0]

    def conv(u, cw_ref, cb_ref, c, e):
        out = cb_ref[:, c:e]
        for j in range(CONV_WIDTH):
            shift = CONV_WIDTH - 1 - j
            out = out + cw_ref[j:j + 1, c:e] * u[HALO - shift:HALO - shift + tm]
        return out

    acc = x
    for c in range(0, F, fc):
        e = min(c + fc, F)
        gate = conv(_dot(h, wg_ref[:, c:e]), cwg_ref, cbg_ref, c, e)
        val = conv(_dot(h, wv_ref[:, c:e]), cwv_ref, cbv_ref, c, e)
        act = (gate * jax.nn.sigmoid(gate) * val).astype(BF16)
        acc = acc + _dot(act, wd_ref[c:e, :])
    o_ref[...] = acc


def _ffn(x, g, wg, wv, cwg, cwv, cbg, cbv, wd, tm, fc):
    B, S, D = x.shape
    F = wd.shape[0]
    const = lambda b, i: (0, 0)
    hb = tm // HALO
    return pl.pallas_call(
        functools.partial(_ffn_kernel, tm=tm, fc=fc),
        grid=(B, S // tm),
        in_specs=[pl.BlockSpec((None, tm, D), lambda b, i: (b, i, 0)),
                  pl.BlockSpec((None, HALO, D), lambda b, i: (b, jnp.maximum(i * hb - 1, 0), 0)),
                  pl.BlockSpec((1, D), const),
                  pl.BlockSpec((D, F), const), pl.BlockSpec((D, F), const),
                  pl.BlockSpec((CONV_WIDTH, F), const), pl.BlockSpec((CONV_WIDTH, F), const),
                  pl.BlockSpec((1, F), const), pl.BlockSpec((1, F), const),
                  pl.BlockSpec((F, D), const)],
        out_specs=pl.BlockSpec((None, tm, D), lambda b, i: (b, i, 0)),
        out_shape=jax.ShapeDtypeStruct((B, S, D), F32),
        compiler_params=_cparams(("parallel", "arbitrary")),
        name="conv_ffn",
    )(x, x, g, wg, wv, cwg, cwv, cbg, cbv, wd)


def _final_norm_kernel(x_ref, g_ref, o_ref):
    o_ref[...] = _rms(x_ref[...], g_ref[...])


def _final_norm(x2, g, tm):
    T, D = x2.shape
    return pl.pallas_call(
        _final_norm_kernel,
        grid=(T // tm,),
        in_specs=[pl.BlockSpec((tm, D), lambda i: (i, 0)), pl.BlockSpec((1, D), lambda i: (0, 0))],
        out_specs=pl.BlockSpec((tm, D), lambda i: (i, 0)),
        out_shape=jax.ShapeDtypeStruct((T, D), F32),
        compiler_params=_cparams(("parallel",)),
        name="final_norm",
    )(x2, g)


def _tile_params(S, tq_a, tk_a):
    big = 1 << 30
    gran, (d_min, d_const) = min(tq_a, tk_a), _a_offsets(tq_a, tk_a)
    pa = []
    for h in range(A_HEADS):
        pa += [(h, d * gran, 1, 0, big, -1) for d in range(d_min, d_const + 1)]
        pa.append((h, 0, 1, 1, 0, -1))
    pb = []
    for _, r in DILATED_PATTERNS:
        for h in range(B_HEADS):
            pb.append((A_HEADS + h, DIL_BLK, r, 1, DIL_BLK, 1))
            pb.append((A_HEADS + h, 0, r, 0, DIL_BLK, 1))
    pc = [(A_HEADS + B_HEADS + h, d * LANES, 1, -big, big, -1)
          for h in range(C_HEADS) for d in range(_n_offsets(S, LANES))]
    to = lambda p: jnp.asarray(p, jnp.int32)
    return to(pa), to(pb), to(pc)


def _n_offsets(S, blk):
    return min(S // blk, REL_MAX_DIST // blk + 2)


def _a_offsets(tq, tk):
    gran = min(tq, tk)
    return -(tq // gran - 1), -(-(REL_MAX_DIST - 1 + tk) // gran)


def _in_weights(w_in_l):
    sizes = (512, 512, 512, 256, 256, 256, 256, 256, 256, 512, 64, 8)
    offs = [0]
    for s in sizes:
        offs.append(offs[-1] + s)
    (aq, ak, av, bq, bk, bv, cq, ck, cv, iq, ik, iw) = [w_in_l[:, offs[i]:offs[i + 1]] for i in range(12)]
    pad = jnp.zeros((w_in_l.shape[0], N_I - 2 * IDX_DIM - IDX_HEADS), w_in_l.dtype)
    return jnp.concatenate([aq, ak, av, iq, cq, ck, cv, bq, bk, bv, ik, ik, iw, pad], axis=1).astype(BF16)


def _forward(x, mem, rel_bias, norm_mix, w_in, lam_q1, lam_k1, lam_q2, lam_k2, subln, w_out,
             norm_mem, norm_memkv, w_mq, w_mkv, w_mo, norm_ffn, w_up, conv_w, conv_b, w_down,
             norm_final, *, tq_a, tk_a, tq_c, ch_c, ca_c, tm_proj, tm_mem, tm_ffn, fc):
    B, S, D = x.shape
    L = w_in.shape[0]
    F = w_down.shape[1]
    pa, pb, pc = _tile_params(S, tq_a, tk_a)
    bias_a = _bias_tiles(pa, rel_bias, tk_a, tq_a, LOG2E).reshape(A_HEADS, -1, tk_a, tq_a)
    d_min, d_const = _a_offsets(tq_a, tk_a)
    bias_b = _bias_tiles(pb, rel_bias, DIL_BLK, DIL_BLK).reshape(
        len(DILATED_PATTERNS), B_HEADS, 2, DIL_BLK, DIL_BLK)
    bias_c = _bias_tiles(pc, rel_bias, LANES, LANES, LOG2E).reshape(C_HEADS, -1, LANES, LANES)
    kv_all = _memkv(mem, norm_memkv.reshape(L, 1, D), w_mkv.astype(BF16))

    for l in range(L):
        lam_init = 0.8 - 0.6 * math.exp(-0.3 * l)
        proj, proj_b, proj_i = _inproj(x.reshape(B * S, D), norm_mix[l].reshape(1, D),
                                       _in_weights(w_in[l]), tm_proj)
        proj = proj.reshape(B, S, N_MAIN)
        lamv = jnp.stack([lam_q1[l], lam_k1[l], lam_q2[l], lam_k2[l]], axis=0)
        avt = proj[:, :, 2 * A_HEADS * LANES:3 * A_HEADS * LANES].reshape(
            B, S // tk_a, tk_a, A_HEADS, A_V).transpose(0, 3, 1, 4, 2)
        cw = C_HEADS * C_DIM
        cvt = proj[:, :, N_MAIN - cw:].reshape(B, S // ca_c, ca_c, cw).transpose(0, 1, 3, 2)
        proj_i = proj_i.reshape(B, S, N_I)
        iwt = proj_i[:, :, 2 * IDX_DIM:2 * IDX_DIM + IDX_HEADS].transpose(0, 2, 1)
        o_a = _diff_attn(proj, avt, lamv, bias_a, subln[l].reshape(A_V, 1), lam_init, tq_a,
                         d_min, d_const)
        o_b = _dil_attn(proj_b.reshape(B, S, N_B), bias_b)
        o_c = _dsa_attn(proj, proj_i, iwt, cvt, bias_c, tq_c, ch_c)
        x = _outmem(x, o_a, o_b, o_c, w_out[l].astype(BF16), norm_mem[l].reshape(1, D),
                    w_mq[l].astype(BF16), kv_all[l], w_mo[l].astype(BF16), tm_mem)
        wu = w_up[l].astype(BF16)
        x = _ffn(x, norm_ffn[l].reshape(1, D), wu[:, :F], wu[:, F:], conv_w[l][:, :F], conv_w[l][:, F:],
                 conv_b[l][:F].reshape(1, F), conv_b[l][F:].reshape(1, F), w_down[l].astype(BF16),
                 tm_ffn, fc)
    return _final_norm(x.reshape(B * S, D), norm_final.reshape(1, D), tm_proj).reshape(B, S, D)


def kernel(x, mem, rel_bias, norm_mix, w_in, lam_q1, lam_k1, lam_q2, lam_k2, subln, w_out,
           norm_mem, norm_memkv, w_mq, w_mkv, w_mo, norm_ffn, w_up, conv_w, conv_b, w_down,
           norm_final):
    return _forward(x, mem, rel_bias, norm_mix, w_in, lam_q1, lam_k1, lam_q2, lam_k2, subln, w_out,
                    norm_mem, norm_memkv, w_mq, w_mkv, w_mo, norm_ffn, w_up, conv_w, conv_b, w_down,
                    norm_final, tq_a=512, tk_a=256, tq_c=256, ch_c=512, ca_c=256, tm_proj=512, tm_mem=512, tm_ffn=512,
                    fc=256)
```

```python
import functools
import math

import jax
import jax.numpy as jnp
from jax import lax
from jax.experimental import pallas as pl
from jax.experimental.pallas import tpu as pltpu

F32 = jnp.float32
BF16 = jnp.bfloat16

EPS = 1e-6
NEG = -1e30
LOG2E = math.log2(math.e)
LANES = 128
VMEM_LIMIT = 56 * 1024 * 1024

A_HEADS, A_QK, A_V = 4, 64, 128
B_HEADS, B_DIM = 4, 64
C_HEADS, C_DIM = 4, 64
IDX_HEADS, IDX_DIM = 8, 64
TOPK_MAX = 256
DILATED_PATTERNS = ((128, 1), (512, 4), (2048, 16))
DIL_BLK = 128
REL_BUCKETS, REL_MAX_DIST = 32, 2048
MEM_HEADS = 4
CONV_WIDTH = 3

N_MAIN = 2816
N_B = 768
N_I = 256


def _cparams(sem):
    return pltpu.CompilerParams(dimension_semantics=sem, vmem_limit_bytes=VMEM_LIMIT)


def _dot(a, b):
    return jnp.dot(a, b, preferred_element_type=F32)


def _dot_nt(a, b):
    return lax.dot_general(a, b, (((1,), (1,)), ((), ())), preferred_element_type=F32)


def _rms(x, g):
    return x * lax.rsqrt(jnp.mean(x * x, axis=-1, keepdims=True) + EPS) * g


def _rel_bucket(n):
    max_exact = REL_BUCKETS // 2
    nf = jnp.maximum(n, 1).astype(F32)
    large = max_exact + (jnp.log(nf / max_exact) / math.log(REL_MAX_DIST / max_exact)
                         * (REL_BUCKETS - max_exact)).astype(jnp.int32)
    large = jnp.minimum(large, REL_BUCKETS - 1)
    return jnp.where(n < max_exact, n, large)


def _bias_tiles_kernel(par_ref, tab_ref, o_ref, *, tq, tk, scale):
    t = pl.program_id(0)
    head, off, mult = par_ref[t, 0], par_ref[t, 1], par_ref[t, 2]
    lo, hi, sgn = par_ref[t, 3], par_ref[t, 4], par_ref[t, 5]
    dist = off + sgn * (lax.broadcasted_iota(jnp.int32, (tq, tk), 0)
                        - lax.broadcasted_iota(jnp.int32, (tq, tk), 1))
    bucket = _rel_bucket(jnp.maximum(dist * mult, 0))
    val = jnp.zeros((tq, tk), F32)
    for b in range(REL_BUCKETS):
        val = jnp.where(bucket == b, tab_ref[b, head] * scale, val)
    o_ref[...] = jnp.where((dist >= lo) & (dist <= hi), val, NEG)


def _bias_tiles(params, table, tq, tk, scale=1.0):
    n = params.shape[0]
    return pl.pallas_call(
        functools.partial(_bias_tiles_kernel, tq=tq, tk=tk, scale=scale),
        grid=(n,),
        in_specs=[pl.BlockSpec(memory_space=pltpu.SMEM), pl.BlockSpec(memory_space=pltpu.SMEM)],
        out_specs=pl.BlockSpec((None, tq, tk), lambda t: (t, 0, 0)),
        out_shape=jax.ShapeDtypeStruct((n, tq, tk), F32),
        compiler_params=_cparams(("arbitrary",)),
        name="bias_tiles",
    )(params, table)


def _inproj_kernel(x_ref, g_ref, w_ref, om_ref, ob_ref, oi_ref):
    h = _rms(x_ref[...], g_ref[...]).astype(BF16)
    step = 512
    for c in range(0, N_MAIN, step):
        e = min(c + step, N_MAIN)
        om_ref[:, c:e] = _dot(h, w_ref[:, c:e]).astype(BF16)
    ob_ref[...] = _dot(h, w_ref[:, N_MAIN:N_MAIN + N_B])
    oi_ref[...] = _dot(h, w_ref[:, N_MAIN + N_B:])


def _inproj(x2, g, w, tm):
    T, D = x2.shape
    n_all = N_MAIN + N_B + N_I
    return pl.pallas_call(
        _inproj_kernel,
        grid=(T // tm,),
        in_specs=[pl.BlockSpec((tm, D), lambda i: (i, 0)),
                  pl.BlockSpec((1, D), lambda i: (0, 0)),
                  pl.BlockSpec((D, n_all), lambda i: (0, 0))],
        out_specs=[pl.BlockSpec((tm, N_MAIN), lambda i: (i, 0)),
                   pl.BlockSpec((tm, N_B), lambda i: (i, 0)),
                   pl.BlockSpec((tm, N_I), lambda i: (i, 0))],
        out_shape=[jax.ShapeDtypeStruct((T, N_MAIN), BF16),
                   jax.ShapeDtypeStruct((T, N_B), F32),
                   jax.ShapeDtypeStruct((T, N_I), F32)],
        compiler_params=_cparams(("parallel",)),
        name="inproj",
    )(x2, g, w)


def _diff_attn_kernel(lam_ref, q_ref, k_ref, vt_ref, bias_ref, g_ref, o_ref, s_sc, acc_sc,
                      *, tq, tk, d_min, d_const, lam_init):
    qi = pl.program_id(2)
    gran = min(tq, tk)
    last = (qi * tq + tq - 1) // tk
    masked_tile = d_const - d_min + 1
    lv = lam_ref[...]
    lam = (jnp.exp(jnp.sum(lv[0:1] * lv[1:2], axis=-1, keepdims=True))
           - jnp.exp(jnp.sum(lv[2:3] * lv[3:4], axis=-1, keepdims=True)) + lam_init)
    lane = lax.broadcasted_iota(jnp.int32, (tq, LANES), 1)
    q = q_ref[...].astype(F32) * (A_QK ** -0.5 * LOG2E)
    qcat = jnp.concatenate([jnp.where(lane < A_QK, q, 0.0), jnp.where(lane >= A_QK, q, 0.0)],
                           axis=0).astype(BF16)
    acc_sc[...] = jnp.zeros(acc_sc.shape, F32)

    def scores_into(slot, j):
        jc = jnp.minimum(j, last)
        k = k_ref[pl.ds(pl.multiple_of(jc * tk, tk), tk), :]
        d = qi * (tq // gran) - jc * (tk // gran)
        bias = bias_ref[jnp.where(j <= last, jnp.minimum(d, d_const) - d_min, masked_tile)]
        s_sc[slot] = _dot_nt(k, qcat) + jnp.concatenate([bias, bias], axis=1)

    def absorb(slot, j, carry):
        m_prev, l_prev = carry
        m_new = jnp.maximum(m_prev, jnp.max(s_sc[slot], axis=0, keepdims=True))
        alpha = jnp.exp2(m_prev - m_new)
        p = jnp.exp2(s_sc[slot] - m_new)
        l_new = alpha * l_prev + jnp.sum(p, axis=0, keepdims=True)
        vt = vt_ref[jnp.minimum(j, last)]
        acc_sc[...] = alpha * acc_sc[...] + _dot(vt, p.astype(BF16))
        return m_new, l_new

    scores_into(0, 0)

    def body(jj, carry):
        a = 2 * jj
        scores_into(1, a + 1)
        carry = absorb(0, a, carry)
        scores_into(0, a + 2)
        return absorb(1, a + 1, carry)

    neg = jnp.full((1, 2 * tq), NEG, F32)
    zero = jnp.zeros((1, 2 * tq), F32)
    _, l = lax.fori_loop(0, (last + 2) // 2, body, (neg, zero))
    inv = 1.0 / l
    o = acc_sc[:, :tq] * inv[:, :tq] - lam * (acc_sc[:, tq:] * inv[:, tq:])
    o = o * lax.rsqrt(jnp.mean(o * o, axis=0, keepdims=True) + EPS) * g_ref[...] * (1.0 - lam_init)
    o_ref[...] = o.T.astype(o_ref.dtype)


def _diff_attn(proj, vt, lamv, bias_a, subln_g, lam_init, tq, d_min, d_const):
    B, S, _ = proj.shape
    nd, tk = bias_a.shape[1], bias_a.shape[2]
    return pl.pallas_call(
        functools.partial(_diff_attn_kernel, tq=tq, tk=tk, d_min=d_min, d_const=d_const,
                          lam_init=lam_init),
        grid=(A_HEADS, B, S // tq),
        in_specs=[pl.BlockSpec((4, A_QK), lambda h, b, i: (0, 0)),
                  pl.BlockSpec((None, tq, LANES), lambda h, b, i: (b, i, h)),
                  pl.BlockSpec((None, S, LANES), lambda h, b, i: (b, 0, A_HEADS + h)),
                  pl.BlockSpec((None, None, S // tk, A_V, tk), lambda h, b, i: (b, h, 0, 0, 0)),
                  pl.BlockSpec((None, nd, tk, tq), lambda h, b, i: (h, 0, 0, 0)),
                  pl.BlockSpec((A_V, 1), lambda h, b, i: (0, 0))],
        out_specs=pl.BlockSpec((None, tq, LANES), lambda h, b, i: (b, i, h)),
        out_shape=jax.ShapeDtypeStruct((B, S, A_HEADS * A_V), BF16),
        scratch_shapes=[pltpu.VMEM((2, tk, 2 * tq), F32), pltpu.VMEM((A_V, 2 * tq), F32)],
        compiler_params=_cparams(("parallel", "parallel", "arbitrary")),
        name="diff_attn",
    )(lamv, proj, proj, vt, bias_a, subln_g)


def _dil_kernel(q_ref, k_ref, v_ref, bias_ref, o_ref, n_sc, m_sc, l_sc, *, seq, patterns):
    blk = DIL_BLK
    lane = lax.broadcasted_iota(jnp.int32, (blk, LANES), 1)
    first = lane < B_DIM

    def attend(base, r, with_prev, p_idx):
        q = q_ref[pl.ds(base, blk, stride=r), :] * (B_DIM ** -0.5)
        kc = k_ref[pl.ds(base, blk, stride=r), :].astype(BF16)
        vc = v_ref[pl.ds(base, blk, stride=r), :].astype(BF16)
        if with_prev:
            kp = k_ref[pl.ds(base - blk * r, blk, stride=r), :].astype(BF16)
            vp = v_ref[pl.ds(base - blk * r, blk, stride=r), :].astype(BF16)
        outs, ms, ls = [], [], []
        for hh in range(2):
            qh = jnp.where(first if hh == 0 else ~first, q, 0.0).astype(BF16)
            sc = _dot_nt(qh, kc) + bias_ref[p_idx, hh, 1]
            mx = jnp.max(sc, axis=-1, keepdims=True)
            if with_prev:
                sp = _dot_nt(qh, kp) + bias_ref[p_idx, hh, 0]
                mx = jnp.maximum(mx, jnp.max(sp, axis=-1, keepdims=True))
            pc = jnp.exp(sc - mx)
            l = jnp.sum(pc, axis=-1, keepdims=True)
            o = _dot(pc.astype(BF16), vc)
            if with_prev:
                pp = jnp.exp(sp - mx)
                l = l + jnp.sum(pp, axis=-1, keepdims=True)
                o = o + _dot(pp.astype(BF16), vp)
            outs.append(o)
            ms.append(jnp.broadcast_to(mx, (blk, LANES)))
            ls.append(jnp.broadcast_to(l, (blk, LANES)))
        return (jnp.where(first, outs[0], outs[1]), jnp.where(first, ms[0], ms[1]),
                jnp.where(first, ls[0], ls[1]))

    def merge(base, r, res, is_first):
        o, m, l = res
        rows = pl.ds(base, blk, stride=r)
        if is_first:
            n_sc[rows, :] = o
            m_sc[rows, :] = m
            l_sc[rows, :] = l
        else:
            m_old = m_sc[rows, :]
            m_new = jnp.maximum(m_old, m)
            a = jnp.exp(m_old - m_new)
            b = jnp.exp(m - m_new)
            n_sc[rows, :] = a * n_sc[rows, :] + b * o
            l_sc[rows, :] = a * l_sc[rows, :] + b * l
            m_sc[rows, :] = m_new

    for p_idx, (_, r) in enumerate(patterns):
        nb = seq // (r * blk)

        def head_blocks(c, carry, r=r, p_idx=p_idx):
            merge(c, r, attend(c, r, False, p_idx), p_idx == 0)
            return carry

        lax.fori_loop(0, r, head_blocks, 0)

        def later_blocks(idx, carry, r=r, p_idx=p_idx):
            n = idx // r + 1
            c = idx % r
            base = n * blk * r + c
            merge(base, r, attend(base, r, True, p_idx), p_idx == 0)
            return carry

        lax.fori_loop(0, r * (nb - 1), later_blocks, 0)

    o_ref[...] = (n_sc[...] / l_sc[...]).astype(o_ref.dtype)


def _dil_attn(proj_b, bias_b):
    B, S, _ = proj_b.shape
    npat = bias_b.shape[0]
    return pl.pallas_call(
        functools.partial(_dil_kernel, seq=S, patterns=DILATED_PATTERNS),
        grid=(B, B_HEADS // 2),
        in_specs=[pl.BlockSpec((None, S, LANES), lambda b, hp: (b, 0, hp)),
                  pl.BlockSpec((None, S, LANES), lambda b, hp: (b, 0, 2 + hp)),
                  pl.BlockSpec((None, S, LANES), lambda b, hp: (b, 0, 4 + hp)),
                  pl.BlockSpec((npat, 2, 2, DIL_BLK, DIL_BLK), lambda b, hp: (0, hp, 0, 0, 0))],
        out_specs=pl.BlockSpec((None, S, LANES), lambda b, hp: (b, 0, hp)),
        out_shape=jax.ShapeDtypeStruct((B, S, B_HEADS * B_DIM), BF16),
        scratch_shapes=[pltpu.VMEM((S, LANES), F32)] * 3,
        compiler_params=_cparams(("parallel", "arbitrary")),
        name="dilated_attn",
    )(proj_b, proj_b, proj_b, bias_b)


SEL16 = jnp.bfloat16


def _key_float(k):
    return pltpu.bitcast(jnp.where(k >= 0, k, k ^ jnp.int32(0x7FFFFFFF)), F32)


def _high_half(f):
    return pltpu.bitcast(pltpu.bitcast(f, jnp.int32) & jnp.int32(-65536), F32)


def _dsa_kernel(iq_ref, ik_ref, iwt_ref, cq_ref, ck_ref, cvt_ref, bias_ref, o_ref, sc_ref, hi_ref, s_sc,
                acc_sc,
                *, tq, ch, ca, topk):
    qi = pl.program_id(1)
    q0 = qi * tq
    nch = (q0 + tq - 1) // ch + 1
    qidx = q0 + lax.broadcasted_iota(jnp.int32, (1, tq), 1)
    lane = lax.broadcasted_iota(jnp.int32, (tq, LANES), 1)
    first = lane < IDX_DIM

    w = iwt_ref[...] * (IDX_HEADS ** -0.5)
    qs = []
    for h in range(IDX_HEADS):
        tile = iq_ref[:, (h // 2) * LANES:(h // 2 + 1) * LANES].astype(F32) * (IDX_DIM ** -0.5)
        qs.append(jnp.where(first if h % 2 == 0 else ~first, tile, 0.0).astype(BF16))

    def score_body(j, carry):
        start = pl.multiple_of(j * ch, ch)
        kk = ik_ref[pl.ds(start, ch), :].astype(BF16)
        sc = jnp.zeros((ch, tq), F32)
        for h in range(IDX_HEADS):
            sc = sc + jnp.maximum(_dot_nt(kk, qs[h]), 0.0) * w[h:h + 1, :]
        kidx = start + lax.broadcasted_iota(jnp.int32, (ch, 1), 0)
        sc = jnp.where(kidx <= qidx, sc, NEG)
        sc_ref[j] = sc
        hi_ref[j] = _high_half(sc).astype(SEL16)
        return carry

    lax.fori_loop(0, nch, score_body, 0)

    rows = 32

    def count_where(pred):
        def body(j, acc):
            hit = jnp.where(pred(sc_ref[j]), 1.0, 0.0)
            return acc + jnp.sum(hit.reshape(ch // rows, rows, tq), axis=0)
        acc = lax.fori_loop(0, nch, body, jnp.zeros((rows, tq), F32))
        return jnp.sum(acc, axis=0, keepdims=True)

    def count_high_ge(v):
        one, zero16 = jnp.ones((), SEL16), jnp.zeros((), SEL16)
        def body(j, acc):
            hit = jnp.where(hi_ref[j] >= v, one, zero16).reshape(ch // rows, rows, tq)
            parts = [hit[i] for i in range(ch // rows)]
            while len(parts) > 1:
                parts = [parts[i] + parts[i + 1] for i in range(0, len(parts), 2)]
            return acc + parts[0]
        acc = lax.fori_loop(0, nch, body, jnp.zeros((rows, tq), SEL16))
        return jnp.sum(acc.astype(F32), axis=0, keepdims=True)

    kf = float(topk)
    n_valid = (qidx + 1).astype(F32)
    half_bits = 16

    def high_body(i, st):
        u, c_lo = st
        cand = u | jnp.left_shift(1, half_bits - 1 - i)
        v = _high_half(_key_float((cand - (1 << (half_bits - 1))) << half_bits)).astype(SEL16)
        c = count_high_ge(v)
        keep = c >= kf
        return jnp.where(keep, cand, u), jnp.where(keep, c, c_lo)

    u, c_lo = lax.fori_loop(0, half_bits, high_body, (jnp.zeros((1, tq), jnp.int32), n_valid))
    key_hi = (u - (1 << (half_bits - 1))) << half_bits

    def low_body(i, st):
        lo_bits, c_lo = st
        cand = lo_bits | jnp.left_shift(1, half_bits - 1 - i)
        c = count_where(lambda sc: sc >= _key_float(key_hi | cand))
        keep = c >= kf
        return jnp.where(keep, cand, lo_bits), jnp.where(keep, c, c_lo)

    lo_bits, c_lo = lax.fori_loop(0, half_bits, low_body, (jnp.zeros((1, tq), jnp.int32), c_lo))
    thr = jnp.where(n_valid > kf, _key_float(key_hi | lo_bits), 0.5 * NEG)

    tie_rows = jnp.logical_and(n_valid > kf, c_lo > kf)

    @pl.when(jnp.max(jnp.where(tie_rows, 1.0, 0.0)) > 0.0)
    def _():
        need = kf - count_where(lambda sc: sc > thr)
        lower = (lax.broadcasted_iota(jnp.int32, (ch, ch), 1)
                 <= lax.broadcasted_iota(jnp.int32, (ch, ch), 0)).astype(BF16)

        def drop_body(j, seen):
            sc = sc_ref[j]
            eq = sc == thr
            eqf = jnp.where(eq, 1.0, 0.0)
            rank = _dot(lower, eqf.astype(BF16)) + seen
            drop = jnp.logical_and(jnp.logical_and(eq, rank > need), tie_rows)
            sc_ref[j] = jnp.where(drop, NEG, sc)
            return seen + jnp.sum(eqf, axis=0, keepdims=True)

        lax.fori_loop(0, nch, drop_body, jnp.zeros((1, tq), F32))

    cw = C_HEADS * C_DIM
    lane_c = lax.broadcasted_iota(jnp.int32, (tq, cw), 1)
    cq = cq_ref[...].astype(F32) * (C_DIM ** -0.5 * LOG2E)
    qcat = jnp.concatenate([jnp.where(lane_c // C_DIM == h, cq, 0.0) for h in range(C_HEADS)],
                           axis=0).astype(BF16)
    acc_sc[...] = jnp.zeros(acc_sc.shape, F32)
    nd = bias_ref.shape[1]
    last = (q0 + tq - 1) // ca

    def scores_into(slot, j):
        jc = jnp.minimum(j, last)
        kc = ck_ref[pl.ds(pl.multiple_of(jc * ca, ca), ca), :]
        sub = pl.multiple_of((jc % (ch // ca)) * ca, ca)
        sel = sc_ref[jc // (ch // ca), pl.ds(sub, ca), :] >= jnp.where(j <= last, thr, -NEG)
        bias = jnp.concatenate([
            jnp.concatenate([
                bias_ref[h, jnp.clip(q0 // LANES + c - (jc * (ca // LANES) + u), 0, nd - 1)]
                for h in range(C_HEADS) for c in range(tq // LANES)], axis=1)
            for u in range(ca // LANES)], axis=0)
        s_sc[slot] = jnp.where(jnp.concatenate([sel] * C_HEADS, axis=1),
                               _dot_nt(kc, qcat) + bias, NEG)

    def absorb(slot, j, carry):
        m_prev, l_prev = carry
        m_new = jnp.maximum(m_prev, jnp.max(s_sc[slot], axis=0, keepdims=True))
        alpha = jnp.exp2(m_prev - m_new)
        p = jnp.exp2(s_sc[slot] - m_new)
        l_new = alpha * l_prev + jnp.sum(p, axis=0, keepdims=True)
        pb = p.astype(BF16)
        jc = jnp.minimum(j, last)
        for h in range(C_HEADS):
            cols = slice(h * tq, (h + 1) * tq)
            vt = cvt_ref[jc, h * C_DIM:(h + 1) * C_DIM, :]
            acc_sc[h] = alpha[:, cols] * acc_sc[h] + _dot(vt, pb[:, cols])
        return m_new, l_new

    scores_into(0, 0)

    def attn_body(jj, carry):
        a = 2 * jj
        scores_into(1, a + 1)
        carry = absorb(0, a, carry)
        scores_into(0, a + 2)
        return absorb(1, a + 1, carry)

    neg = jnp.full((1, C_HEADS * tq), NEG, F32)
    zero = jnp.zeros((1, C_HEADS * tq), F32)
    _, l = lax.fori_loop(0, (last + 2) // 2, attn_body, (neg, zero))
    inv = 1.0 / l
    o_t = jnp.concatenate([acc_sc[h] * inv[:, h * tq:(h + 1) * tq] for h in range(C_HEADS)], axis=0)
    o_ref[...] = o_t.T.astype(o_ref.dtype)


def _dsa_attn(proj, proj_i, iwt, cvt, bias_c, tq, ch):
    B, S, _ = proj.shape
    ca = cvt.shape[-1]
    topk = min(TOPK_MAX, S // 4)
    nd = bias_c.shape[1]
    cw = C_HEADS * C_DIM
    return pl.pallas_call(
        functools.partial(_dsa_kernel, tq=tq, ch=ch, ca=ca, topk=topk),
        grid=(B, S // tq),
        in_specs=[pl.BlockSpec((None, tq, IDX_HEADS * IDX_DIM), lambda b, i: (b, i, 3)),
                  pl.BlockSpec((None, S, LANES), lambda b, i: (b, 0, 0)),
                  pl.BlockSpec((None, IDX_HEADS, tq), lambda b, i: (b, 0, i)),
                  pl.BlockSpec((None, tq, cw), lambda b, i: (b, i, 8)),
                  pl.BlockSpec((None, S, cw), lambda b, i: (b, 0, 9)),
                  pl.BlockSpec((None, S // ca, cw, ca), lambda b, i: (b, 0, 0, 0)),
                  pl.BlockSpec((C_HEADS, nd, LANES, LANES), lambda b, i: (0, 0, 0, 0))],
        out_specs=pl.BlockSpec((None, tq, cw), lambda b, i: (b, i, 0)),
        out_shape=jax.ShapeDtypeStruct((B, S, cw), BF16),
        scratch_shapes=[pltpu.VMEM((S // ch, ch, tq), F32),
                        pltpu.VMEM((S // ch, ch, tq), SEL16),
                        pltpu.VMEM((2, ca, C_HEADS * tq), F32),
                        pltpu.VMEM((C_HEADS, C_DIM, tq), F32)],
        compiler_params=_cparams(("parallel", "arbitrary")),
        name="dsa_attn",
    )(proj, proj_i, iwt, proj, proj, cvt, bias_c)


def _memkv_kernel(mem_ref, g_ref, w_ref, o_ref):
    h = _rms(mem_ref[...], g_ref[...]).astype(BF16)
    o_ref[...] = _dot(h, w_ref[...]).astype(o_ref.dtype)


def _memkv(mem, g, w):
    B, M, D = mem.shape
    L = w.shape[0]
    return pl.pallas_call(
        _memkv_kernel,
        grid=(L, B),
        in_specs=[pl.BlockSpec((None, M, D), lambda l, b: (b, 0, 0)),
                  pl.BlockSpec((None, 1, D), lambda l, b: (l, 0, 0)),
                  pl.BlockSpec((None, D, 2 * D), lambda l, b: (l, 0, 0))],
        out_specs=pl.BlockSpec((None, None, M, 2 * D), lambda l, b: (l, b, 0, 0)),
        out_shape=jax.ShapeDtypeStruct((L, B, M, 2 * D), BF16),
        compiler_params=_cparams(("parallel", "arbitrary")),
        name="mem_kv",
    )(mem, g, w)


def _outmem_kernel(x_ref, oa_ref, ob_ref, oc_ref, wo_ref, g_ref, wq_ref, kv_ref, wmo_ref, o_ref):
    D = x_ref.shape[-1]
    na, nb = oa_ref.shape[-1], ob_ref.shape[-1]
    x = (x_ref[...] + _dot(oa_ref[...], wo_ref[0:na]) + _dot(ob_ref[...], wo_ref[na:na + nb])
         + _dot(oc_ref[...], wo_ref[na + nb:]))
    h = _rms(x, g_ref[...]).astype(BF16)
    hd = D // MEM_HEADS
    q = (_dot(h, wq_ref[...]) * (hd ** -0.5)).astype(BF16)
    outs = []
    for hh in range(MEM_HEADS):
        k = kv_ref[:, hh * hd:(hh + 1) * hd]
        v = kv_ref[:, D + hh * hd:D + (hh + 1) * hd]
        s = _dot_nt(q[:, hh * hd:(hh + 1) * hd], k)
        p = jnp.exp(s - jnp.max(s, axis=-1, keepdims=True))
        p = p / jnp.sum(p, axis=-1, keepdims=True)
        outs.append(_dot(p.astype(BF16), v).astype(BF16))
    o = jnp.concatenate(outs, axis=-1)
    o_ref[...] = x + _dot(o, wmo_ref[...])


def _outmem(x, oa, ob, oc, wo, g, wq, kv, wmo, tm):
    B, S, D = x.shape
    M = kv.shape[1]
    const = lambda b, i: (0, 0)
    return pl.pallas_call(
        _outmem_kernel,
        grid=(B, S // tm),
        in_specs=[pl.BlockSpec((None, tm, D), lambda b, i: (b, i, 0)),
                  pl.BlockSpec((None, tm, oa.shape[-1]), lambda b, i: (b, i, 0)),
                  pl.BlockSpec((None, tm, ob.shape[-1]), lambda b, i: (b, i, 0)),
                  pl.BlockSpec((None, tm, oc.shape[-1]), lambda b, i: (b, i, 0)),
                  pl.BlockSpec(wo.shape, const),
                  pl.BlockSpec((1, D), const),
                  pl.BlockSpec(wq.shape, const),
                  pl.BlockSpec((None, M, 2 * D), lambda b, i: (b, 0, 0)),
                  pl.BlockSpec(wmo.shape, const)],
        out_specs=pl.BlockSpec((None, tm, D), lambda b, i: (b, i, 0)),
        out_shape=jax.ShapeDtypeStruct((B, S, D), F32),
        compiler_params=_cparams(("parallel", "arbitrary")),
        name="outproj_memattn",
    )(x, oa, ob, oc, wo, g, wq, kv, wmo)


HALO = 8


def _ffn_kernel(x_ref, xp_ref, g_ref, wg_ref, wv_ref, cwg_ref, cwv_ref, cbg_ref, cbv_ref, wd_ref,
                o_ref, *, tm, fc):
    i = pl.program_id(1)
    x = x_ref[...]
    g = g_ref[...]
    hp = _rms(xp_ref[...], g) * jnp.where(i > 0, 1.0, 0.0)
    h = jnp.concatenate([hp, _rms(x, g)], axis=0).astype(BF16)
    F = wd_ref.shape[0]

    def conv(u, cw_ref, cb_ref, c, e):
        out = cb_ref[:, c:e]
        for j in range(CONV_WIDTH):
            shift = CONV_WIDTH - 1 - j
            out = out + cw_ref[j:j + 1, c:e] * u[HALO - shift:HALO - shift + tm]
        return out

    acc = x
    for c in range(0, F, fc):
        e = min(c + fc, F)
        gate = conv(_dot(h, wg_ref[:, c:e]), cwg_ref, cbg_ref, c, e)
        val = conv(_dot(h, wv_ref[:, c:e]), cwv_ref, cbv_ref, c, e)
        act = (gate * jax.nn.sigmoid(gate) * val).astype(BF16)
        acc = acc + _dot(act, wd_ref[c:e, :])
    o_ref[...] = acc


def _ffn(x, g, wg, wv, cwg, cwv, cbg, cbv, wd, tm, fc):
    B, S, D = x.shape
    F = wd.shape[0]
    const = lambda b, i: (0, 0)
    hb = tm // HALO
    return pl.pallas_call(
        functools.partial(_ffn_kernel, tm=tm, fc=fc),
        grid=(B, S // tm),
        in_specs=[pl.BlockSpec((None, tm, D), lambda b, i: (b, i, 0)),
                  pl.BlockSpec((None, HALO, D), lambda b, i: (b, jnp.maximum(i * hb - 1, 0), 0)),
                  pl.BlockSpec((1, D), const),
                  pl.BlockSpec((D, F), const), pl.BlockSpec((D, F), const),
                  pl.BlockSpec((CONV_WIDTH, F), const), pl.BlockSpec((CONV_WIDTH, F), const),
                  pl.BlockSpec((1, F), const), pl.BlockSpec((1, F), const),
                  pl.BlockSpec((F, D), const)],
        out_specs=pl.BlockSpec((None, tm, D), lambda b, i: (b, i, 0)),
        out_shape=jax.ShapeDtypeStruct((B, S, D), F32),
        compiler_params=_cparams(("parallel", "arbitrary")),
        name="conv_ffn",
    )(x, x, g, wg, wv, cwg, cwv, cbg, cbv, wd)


def _final_norm_kernel(x_ref, g_ref, o_ref):
    o_ref[...] = _rms(x_ref[...], g_ref[...])


def _final_norm(x2, g, tm):
    T, D = x2.shape
    return pl.pallas_call(
        _final_norm_kernel,
        grid=(T // tm,),
        in_specs=[pl.BlockSpec((tm, D), lambda i: (i, 0)), pl.BlockSpec((1, D), lambda i: (0, 0))],
        out_specs=pl.BlockSpec((tm, D), lambda i: (i, 0)),
        out_shape=jax.ShapeDtypeStruct((T, D), F32),
        compiler_params=_cparams(("parallel",)),
        name="final_norm",
    )(x2, g)


def _tile_params(S, tq_a, tk_a):
    big = 1 << 30
    gran, (d_min, d_const) = min(tq_a, tk_a), _a_offsets(tq_a, tk_a)
    pa = []
    for h in range(A_HEADS):
        pa += [(h, d * gran, 1, 0, big, -1) for d in range(d_min, d_const + 1)]
        pa.append((h, 0, 1, 1, 0, -1))
    pb = []
    for _, r in DILATED_PATTERNS:
        for h in range(B_HEADS):
            pb.append((A_HEADS + h, DIL_BLK, r, 1, DIL_BLK, 1))
            pb.append((A_HEADS + h, 0, r, 0, DIL_BLK, 1))
    pc = [(A_HEADS + B_HEADS + h, d * LANES, 1, -big, big, -1)
          for h in range(C_HEADS) for d in range(_n_offsets(S, LANES))]
    to = lambda p: jnp.asarray(p, jnp.int32)
    return to(pa), to(pb), to(pc)


def _n_offsets(S, blk):
    return min(S // blk, REL_MAX_DIST // blk + 2)


def _a_offsets(tq, tk):
    gran = min(tq, tk)
    return -(tq // gran - 1), -(-(REL_MAX_DIST - 1 + tk) // gran)


def _in_weights(w_in_l):
    sizes = (512, 512, 512, 256, 256, 256, 256, 256, 256, 512, 64, 8)
    offs = [0]
    for s in sizes:
        offs.append(offs[-1] + s)
    (aq, ak, av, bq, bk, bv, cq, ck, cv, iq, ik, iw) = [w_in_l[:, offs[i]:offs[i + 1]] for i in range(12)]
    pad = jnp.zeros((w_in_l.shape[0], N_I - 2 * IDX_DIM - IDX_HEADS), w_in_l.dtype)
    return jnp.concatenate([aq, ak, av, iq, cq, ck, cv, bq, bk, bv, ik, ik, iw, pad], axis=1).astype(BF16)


def _forward(x, mem, rel_bias, norm_mix, w_in, lam_q1, lam_k1, lam_q2, lam_k2, subln, w_out,
             norm_mem, norm_memkv, w_mq, w_mkv, w_mo, norm_ffn, w_up, conv_w, conv_b, w_down,
             norm_final, *, tq_a, tk_a, tq_c, ch_c, ca_c, tm_proj, tm_mem, tm_ffn, fc):
    B, S, D = x.shape
    L = w_in.shape[0]
    F = w_down.shape[1]
    pa, pb, pc = _tile_params(S, tq_a, tk_a)
    bias_a = _bias_tiles(pa, rel_bias, tk_a, tq_a, LOG2E).reshape(A_HEADS, -1, tk_a, tq_a)
    d_min, d_const = _a_offsets(tq_a, tk_a)
    bias_b = _bias_tiles(pb, rel_bias, DIL_BLK, DIL_BLK).reshape(
        len(DILATED_PATTERNS), B_HEADS, 2, DIL_BLK, DIL_BLK)
    bias_c = _bias_tiles(pc, rel_bias, LANES, LANES, LOG2E).reshape(C_HEADS, -1, LANES, LANES)
    kv_all = _memkv(mem, norm_memkv.reshape(L, 1, D), w_mkv.astype(BF16))

    for l in range(L):
        lam_init = 0.8 - 0.6 * math.exp(-0.3 * l)
        proj, proj_b, proj_i = _inproj(x.reshape(B * S, D), norm_mix[l].reshape(1, D),
                                       _in_weights(w_in[l]), tm_proj)
        proj = proj.reshape(B, S, N_MAIN)
        lamv = jnp.stack([lam_q1[l], lam_k1[l], lam_q2[l], lam_k2[l]], axis=0)
        avt = proj[:, :, 2 * A_HEADS * LANES:3 * A_HEADS * LANES].reshape(
            B, S // tk_a, tk_a, A_HEADS, A_V).transpose(0, 3, 1, 4, 2)
        cw = C_HEADS * C_DIM
        cvt = proj[:, :, N_MAIN - cw:].reshape(B, S // ca_c, ca_c, cw).transpose(0, 1, 3, 2)
        proj_i = proj_i.reshape(B, S, N_I)
        iwt = proj_i[:, :, 2 * IDX_DIM:2 * IDX_DIM + IDX_HEADS].transpose(0, 2, 1)
        o_a = _diff_attn(proj, avt, lamv, bias_a, subln[l].reshape(A_V, 1), lam_init, tq_a,
                         d_min, d_const)
        o_b = _dil_attn(proj_b.reshape(B, S, N_B), bias_b)
        o_c = _dsa_attn(proj, proj_i, iwt, cvt, bias_c, tq_c, ch_c)
        x = _outmem(x, o_a, o_b, o_c, w_out[l].astype(BF16), norm_mem[l].reshape(1, D),
                    w_mq[l].astype(BF16), kv_all[l], w_mo[l].astype(BF16), tm_mem)
        wu = w_up[l].astype(BF16)
        x = _ffn(x, norm_ffn[l].reshape(1, D), wu[:, :F], wu[:, F:], conv_w[l][:, :F], conv_w[l][:, F:],
                 conv_b[l][:F].reshape(1, F), conv_b[l][F:].reshape(1, F), w_down[l].astype(BF16),
                 tm_ffn, fc)
    return _final_norm(x.reshape(B * S, D), norm_final.reshape(1, D), tm_proj).reshape(B, S, D)


def kernel(x, mem, rel_bias, norm_mix, w_in, lam_q1, lam_k1, lam_q2, lam_k2, subln, w_out,
           norm_mem, norm_memkv, w_mq, w_mkv, w_mo, norm_ffn, w_up, conv_w, conv_b, w_down,
           norm_final):
    return _forward(x, mem, rel_bias, norm_mix, w_in, lam_q1, lam_k1, lam_q2, lam_k2, subln, w_out,
                    norm_mem, norm_memkv, w_mq, w_mkv, w_mo, norm_ffn, w_up, conv_w, conv_b, w_down,
                    norm_final, tq_a=512, tk_a=256, tq_c=256, ch_c=512, ca_c=256, tm_proj=512, tm_mem=512, tm_ffn=512,
                    fc=256)
```

```python
import functools
import math

import jax
import jax.numpy as jnp
from jax import lax
from jax.experimental import pallas as pl
from jax.experimental.pallas import tpu as pltpu

F32 = jnp.float32
BF16 = jnp.bfloat16

EPS = 1e-6
NEG = -1e30
LOG2E = math.log2(math.e)
LANES = 128
VMEM_LIMIT = 56 * 1024 * 1024

A_HEADS, A_QK, A_V = 4, 64, 128
B_HEADS, B_DIM = 4, 64
C_HEADS, C_DIM = 4, 64
IDX_HEADS, IDX_DIM = 8, 64
TOPK_MAX = 256
DILATED_PATTERNS = ((128, 1), (512, 4), (2048, 16))
DIL_BLK = 128
DIL_UNROLL = 4
REL_BUCKETS, REL_MAX_DIST = 32, 2048
MEM_HEADS = 4
CONV_WIDTH = 3

N_MAIN = 2816
N_B = 768
N_I = 256


def _cparams(sem):
    return pltpu.CompilerParams(dimension_semantics=sem, vmem_limit_bytes=VMEM_LIMIT)


def _dot(a, b):
    return jnp.dot(a, b, preferred_element_type=F32)


def _dot_nt(a, b):
    return lax.dot_general(a, b, (((1,), (1,)), ((), ())), preferred_element_type=F32)


def _rms(x, g):
    return x * lax.rsqrt(jnp.mean(x * x, axis=-1, keepdims=True) + EPS) * g


def _rel_bucket(n):
    max_exact = REL_BUCKETS // 2
    nf = jnp.maximum(n, 1).astype(F32)
    large = max_exact + (jnp.log(nf / max_exact) / math.log(REL_MAX_DIST / max_exact)
                         * (REL_BUCKETS - max_exact)).astype(jnp.int32)
    large = jnp.minimum(large, REL_BUCKETS - 1)
    return jnp.where(n < max_exact, n, large)


def _bias_tiles_kernel(par_ref, tab_ref, o_ref, *, tq, tk, scale):
    t = pl.program_id(0)
    head, off, mult = par_ref[t, 0], par_ref[t, 1], par_ref[t, 2]
    lo, hi, sgn = par_ref[t, 3], par_ref[t, 4], par_ref[t, 5]
    dist = off + sgn * (lax.broadcasted_iota(jnp.int32, (tq, tk), 0)
                        - lax.broadcasted_iota(jnp.int32, (tq, tk), 1))
    bucket = _rel_bucket(jnp.maximum(dist * mult, 0))
    val = jnp.zeros((tq, tk), F32)
    for b in range(REL_BUCKETS):
        val = jnp.where(bucket == b, tab_ref[b, head] * scale, val)
    o_ref[...] = jnp.where((dist >= lo) & (dist <= hi), val, NEG)


def _bias_tiles(params, table, tq, tk, scale=1.0):
    n = params.shape[0]
    return pl.pallas_call(
        functools.partial(_bias_tiles_kernel, tq=tq, tk=tk, scale=scale),
        grid=(n,),
        in_specs=[pl.BlockSpec(memory_space=pltpu.SMEM), pl.BlockSpec(memory_space=pltpu.SMEM)],
        out_specs=pl.BlockSpec((None, tq, tk), lambda t: (t, 0, 0)),
        out_shape=jax.ShapeDtypeStruct((n, tq, tk), F32),
        compiler_params=_cparams(("arbitrary",)),
        name="bias_tiles",
    )(params, table)


def _inproj_kernel(x_ref, g_ref, w_ref, om_ref, ob_ref, oi_ref):
    h = _rms(x_ref[...], g_ref[...]).astype(BF16)
    step = 512
    for c in range(0, N_MAIN, step):
        e = min(c + step, N_MAIN)
        om_ref[:, c:e] = _dot(h, w_ref[:, c:e]).astype(BF16)
    ob_ref[...] = _dot(h, w_ref[:, N_MAIN:N_MAIN + N_B])
    oi_ref[...] = _dot(h, w_ref[:, N_MAIN + N_B:])


def _inproj(x2, g, w, tm):
    T, D = x2.shape
    n_all = N_MAIN + N_B + N_I
    return pl.pallas_call(
        _inproj_kernel,
        grid=(T // tm,),
        in_specs=[pl.BlockSpec((tm, D), lambda i: (i, 0)),
                  pl.BlockSpec((1, D), lambda i: (0, 0)),
                  pl.BlockSpec((D, n_all), lambda i: (0, 0))],
        out_specs=[pl.BlockSpec((tm, N_MAIN), lambda i: (i, 0)),
                   pl.BlockSpec((tm, N_B), lambda i: (i, 0)),
                   pl.BlockSpec((tm, N_I), lambda i: (i, 0))],
        out_shape=[jax.ShapeDtypeStruct((T, N_MAIN), BF16),
                   jax.ShapeDtypeStruct((T, N_B), F32),
                   jax.ShapeDtypeStruct((T, N_I), F32)],
        compiler_params=_cparams(("parallel",)),
        name="inproj",
    )(x2, g, w)


def _diff_attn_kernel(lam_ref, q_ref, k_ref, vt_ref, bias_ref, g_ref, o_ref, s_sc, acc_sc,
                      *, tq, tk, d_min, d_const, lam_init):
    qi = pl.program_id(2)
    gran = min(tq, tk)
    last = (qi * tq + tq - 1) // tk
    masked_tile = d_const - d_min + 1
    lv = lam_ref[...]
    lam = (jnp.exp(jnp.sum(lv[0:1] * lv[1:2], axis=-1, keepdims=True))
           - jnp.exp(jnp.sum(lv[2:3] * lv[3:4], axis=-1, keepdims=True)) + lam_init)
    lane = lax.broadcasted_iota(jnp.int32, (tq, LANES), 1)
    q = q_ref[...].astype(F32) * (A_QK ** -0.5 * LOG2E)
    qcat = jnp.concatenate([jnp.where(lane < A_QK, q, 0.0), jnp.where(lane >= A_QK, q, 0.0)],
                           axis=0).astype(BF16)
    acc_sc[...] = jnp.zeros(acc_sc.shape, F32)

    def scores_into(slot, j):
        jc = jnp.minimum(j, last)
        k = k_ref[pl.ds(pl.multiple_of(jc * tk, tk), tk), :]
        d = qi * (tq // gran) - jc * (tk // gran)
        bias = bias_ref[jnp.where(j <= last, jnp.minimum(d, d_const) - d_min, masked_tile)]
        s_sc[slot] = _dot_nt(k, qcat) + jnp.concatenate([bias, bias], axis=1)

    def absorb(slot, j, carry):
        m_prev, l_prev = carry
        m_new = jnp.maximum(m_prev, jnp.max(s_sc[slot], axis=0, keepdims=True))
        alpha = jnp.exp2(m_prev - m_new)
        p = jnp.exp2(s_sc[slot] - m_new)
        l_new = alpha * l_prev + jnp.sum(p, axis=0, keepdims=True)
        vt = vt_ref[jnp.minimum(j, last)]
        acc_sc[...] = alpha * acc_sc[...] + _dot(vt, p.astype(BF16))
        return m_new, l_new

    scores_into(0, 0)

    def body(jj, carry):
        a = 2 * jj
        scores_into(1, a + 1)
        carry = absorb(0, a, carry)
        scores_into(0, a + 2)
        return absorb(1, a + 1, carry)

    neg = jnp.full((1, 2 * tq), NEG, F32)
    zero = jnp.zeros((1, 2 * tq), F32)
    _, l = lax.fori_loop(0, (last + 2) // 2, body, (neg, zero))
    inv = 1.0 / l
    o = acc_sc[:, :tq] * inv[:, :tq] - lam * (acc_sc[:, tq:] * inv[:, tq:])
    o = o * lax.rsqrt(jnp.mean(o * o, axis=0, keepdims=True) + EPS) * g_ref[...] * (1.0 - lam_init)
    o_ref[...] = o.T.astype(o_ref.dtype)


def _diff_attn(proj, vt, lamv, bias_a, subln_g, lam_init, tq, d_min, d_const):
    B, S, _ = proj.shape
    nd, tk = bias_a.shape[1], bias_a.shape[2]
    return pl.pallas_call(
        functools.partial(_diff_attn_kernel, tq=tq, tk=tk, d_min=d_min, d_const=d_const,
                          lam_init=lam_init),
        grid=(A_HEADS, B, S // tq),
        in_specs=[pl.BlockSpec((4, A_QK), lambda h, b, i: (0, 0)),
                  pl.BlockSpec((None, tq, LANES), lambda h, b, i: (b, i, h)),
                  pl.BlockSpec((None, S, LANES), lambda h, b, i: (b, 0, A_HEADS + h)),
                  pl.BlockSpec((None, None, S // tk, A_V, tk), lambda h, b, i: (b, h, 0, 0, 0)),
                  pl.BlockSpec((None, nd, tk, tq), lambda h, b, i: (h, 0, 0, 0)),
                  pl.BlockSpec((A_V, 1), lambda h, b, i: (0, 0))],
        out_specs=pl.BlockSpec((None, tq, LANES), lambda h, b, i: (b, i, h)),
        out_shape=jax.ShapeDtypeStruct((B, S, A_HEADS * A_V), BF16),
        scratch_shapes=[pltpu.VMEM((2, tk, 2 * tq), F32), pltpu.VMEM((A_V, 2 * tq), F32)],
        compiler_params=_cparams(("parallel", "parallel", "arbitrary")),
        name="diff_attn",
    )(lamv, proj, proj, vt, bias_a, subln_g)


class _LaneHalves:
    def __init__(self, *refs):
        self.refs = refs

    def __getitem__(self, idx):
        return jnp.concatenate([r[idx] for r in self.refs], axis=1)

    def __setitem__(self, idx, val):
        for i, r in enumerate(self.refs):
            r[idx] = val[:, i * LANES:(i + 1) * LANES]


def _dil_kernel(q0, q1, k0, k1, v0, v1, bias_ref, o_ref, n0, n1, m0, m1, l0, l1, *, seq, patterns):
    q_ref, k_ref, v_ref = _LaneHalves(q0, q1), _LaneHalves(k0, k1), _LaneHalves(v0, v1)
    n_sc, m_sc, l_sc = _LaneHalves(n0, n1), _LaneHalves(m0, m1), _LaneHalves(l0, l1)
    blk = DIL_BLK
    width = B_HEADS * B_DIM
    head_of_lane = lax.broadcasted_iota(jnp.int32, (blk, width), 1) // B_DIM

    def per_head(x):
        parts = [jnp.broadcast_to(x[h * blk:(h + 1) * blk], (blk, width)) for h in range(B_HEADS)]
        out = parts[-1]
        for h in range(B_HEADS - 2, -1, -1):
            out = jnp.where(head_of_lane == h, parts[h], out)
        return out

    def attend(base, r, with_prev, p_idx):
        q = q_ref[pl.ds(base, blk, stride=r), :] * (B_DIM ** -0.5 * LOG2E)
        qcat = jnp.concatenate([jnp.where(head_of_lane == h, q, 0.0) for h in range(B_HEADS)],
                               axis=0).astype(BF16)
        if with_prev:
            keys = pl.ds(base - blk * r, 2 * blk, stride=r)
            bias = bias_ref[p_idx]
        else:
            keys = pl.ds(base, blk, stride=r)
            bias = bias_ref[p_idx, :, blk:]
        s = _dot_nt(qcat, k_ref[keys, :].astype(BF16)) + bias
        mx = jnp.max(s, axis=-1, keepdims=True)
        p = jnp.exp2(s - mx)
        l = jnp.sum(p, axis=-1, keepdims=True)
        o = _dot(p.astype(BF16), v_ref[keys, :].astype(BF16))
        return per_head(o), per_head(mx), per_head(l)

    def merge(base, r, res, is_first):
        o, m, l = res
        rows = pl.ds(base, blk, stride=r)
        if is_first:
            n_sc[rows, :] = o
            m_sc[rows, :] = m
            l_sc[rows, :] = l
        else:
            m_old = m_sc[rows, :]
            m_new = jnp.maximum(m_old, m)
            a = jnp.exp2(m_old - m_new)
            b = jnp.exp2(m - m_new)
            n_sc[rows, :] = a * n_sc[rows, :] + b * o
            l_sc[rows, :] = a * l_sc[rows, :] + b * l
            m_sc[rows, :] = m_new

    def sweep(count, base_of, r, with_prev, p_idx):
        def trip(width):
            def body(i, carry):
                bases = [base_of(i * width + u) for u in range(width)]
                results = [attend(b, r, with_prev, p_idx) for b in bases]
                for b, res in zip(bases, results):
                    merge(b, r, res, p_idx == len(patterns) - 1)
                return carry
            return body

        full = count // DIL_UNROLL
        lax.fori_loop(0, full, trip(DIL_UNROLL), 0)
        rest = count - full * DIL_UNROLL
        if rest:
            lax.fori_loop(full * DIL_UNROLL, count, trip(1), 0)

    for p_idx, (_, r) in reversed(list(enumerate(patterns))):
        nb = seq // (r * blk)
        sweep(r, lambda c: c, r, False, p_idx)
        sweep(r * (nb - 1), lambda idx, r=r: (idx // r + 1) * blk * r + idx % r, r, True, p_idx)

    everything = (slice(None), slice(None))
    o_ref[...] = (n_sc[everything] / l_sc[everything]).astype(o_ref.dtype)


def _dil_attn(proj_b, bias_b):
    B, S, _ = proj_b.shape
    width = B_HEADS * B_DIM
    return pl.pallas_call(
        functools.partial(_dil_kernel, seq=S, patterns=DILATED_PATTERNS),
        grid=(B,),
        in_specs=[pl.BlockSpec((None, S, LANES), functools.partial(lambda i, b: (b, 0, i), i))
                  for i in range(3 * width // LANES)]
                 + [pl.BlockSpec(bias_b.shape, lambda b: (0, 0, 0))],
        out_specs=pl.BlockSpec((None, S, width), lambda b: (b, 0, 0)),
        out_shape=jax.ShapeDtypeStruct((B, S, width), BF16),
        scratch_shapes=[pltpu.VMEM((S, LANES), F32)] * (3 * width // LANES),
        compiler_params=_cparams(("parallel",)),
        name="dilated_attn",
    )(*([proj_b] * (3 * width // LANES)), bias_b)


SEL16 = jnp.bfloat16


def _key_float(k):
    return pltpu.bitcast(jnp.where(k >= 0, k, k ^ jnp.int32(0x7FFFFFFF)), F32)


def _high_half(f):
    return pltpu.bitcast(pltpu.bitcast(f, jnp.int32) & jnp.int32(-65536), F32)


def _dsa_kernel(iq_ref, ik_ref, iwt_ref, cq_ref, ck_ref, cvt_ref, bias_ref, o_ref, sc_ref, hi_ref, s_sc,
                acc_sc,
                *, tq, ch, ca, topk):
    qi = pl.program_id(1)
    q0 = qi * tq
    nch = (q0 + tq - 1) // ch + 1
    qidx = q0 + lax.broadcasted_iota(jnp.int32, (1, tq), 1)
    lane = lax.broadcasted_iota(jnp.int32, (tq, LANES), 1)
    first = lane < IDX_DIM

    w = iwt_ref[...] * (IDX_HEADS ** -0.5)
    qs = []
    for h in range(IDX_HEADS):
        tile = iq_ref[:, (h // 2) * LANES:(h // 2 + 1) * LANES].astype(F32) * (IDX_DIM ** -0.5)
        qs.append(jnp.where(first if h % 2 == 0 else ~first, tile, 0.0).astype(BF16))

    def score_body(j, carry):
        start = pl.multiple_of(j * ch, ch)
        kk = ik_ref[pl.ds(start, ch), :].astype(BF16)
        sc = jnp.zeros((ch, tq), F32)
        for h in range(IDX_HEADS):
            sc = sc + jnp.maximum(_dot_nt(kk, qs[h]), 0.0) * w[h:h + 1, :]
        kidx = start + lax.broadcasted_iota(jnp.int32, (ch, 1), 0)
        sc = jnp.where(kidx <= qidx, sc, NEG)
        sc_ref[j] = sc
        hi_ref[j] = _high_half(sc).astype(SEL16)
        return carry

    lax.fori_loop(0, nch, score_body, 0)

    rows = 32

    def count_where(pred):
        def body(j, acc):
            hit = jnp.where(pred(sc_ref[j]), 1.0, 0.0)
            return acc + jnp.sum(hit.reshape(ch // rows, rows, tq), axis=0)
        acc = lax.fori_loop(0, nch, body, jnp.zeros((rows, tq), F32))
        return jnp.sum(acc, axis=0, keepdims=True)

    def count_high_ge(v):
        one, zero16 = jnp.ones((), SEL16), jnp.zeros((), SEL16)
        def body(j, acc):
            hit = jnp.where(hi_ref[j] >= v, one, zero16).reshape(ch // rows, rows, tq)
            parts = [hit[i] for i in range(ch // rows)]
            while len(parts) > 1:
                parts = [parts[i] + parts[i + 1] for i in range(0, len(parts), 2)]
            return acc + parts[0]
        acc = lax.fori_loop(0, nch, body, jnp.zeros((rows, tq), SEL16))
        return jnp.sum(acc.astype(F32), axis=0, keepdims=True)

    kf = float(topk)
    n_valid = (qidx + 1).astype(F32)
    half_bits = 16

    def high_body(i, st):
        u, c_lo = st
        cand = u | jnp.left_shift(1, half_bits - 1 - i)
        v = _high_half(_key_float((cand - (1 << (half_bits - 1))) << half_bits)).astype(SEL16)
        c = count_high_ge(v)
        keep = c >= kf
        return jnp.where(keep, cand, u), jnp.where(keep, c, c_lo)

    u, c_lo = lax.fori_loop(0, half_bits, high_body, (jnp.zeros((1, tq), jnp.int32), n_valid))
    key_hi = (u - (1 << (half_bits - 1))) << half_bits

    def low_body(i, st):
        lo_bits, c_lo = st
        cand = lo_bits | jnp.left_shift(1, half_bits - 1 - i)
        c = count_where(lambda sc: sc >= _key_float(key_hi | cand))
        keep = c >= kf
        return jnp.where(keep, cand, lo_bits), jnp.where(keep, c, c_lo)

    lo_bits, c_lo = lax.fori_loop(0, half_bits, low_body, (jnp.zeros((1, tq), jnp.int32), c_lo))
    thr = jnp.where(n_valid > kf, _key_float(key_hi | lo_bits), 0.5 * NEG)

    tie_rows = jnp.logical_and(n_valid > kf, c_lo > kf)

    @pl.when(jnp.max(jnp.where(tie_rows, 1.0, 0.0)) > 0.0)
    def _():
        need = kf - count_where(lambda sc: sc > thr)
        lower = (lax.broadcasted_iota(jnp.int32, (ch, ch), 1)
                 <= lax.broadcasted_iota(jnp.int32, (ch, ch), 0)).astype(BF16)

        def drop_body(j, seen):
            sc = sc_ref[j]
            eq = sc == thr
            eqf = jnp.where(eq, 1.0, 0.0)
            rank = _dot(lower, eqf.astype(BF16)) + seen
            drop = jnp.logical_and(jnp.logical_and(eq, rank > need), tie_rows)
            sc_ref[j] = jnp.where(drop, NEG, sc)
            return seen + jnp.sum(eqf, axis=0, keepdims=True)

        lax.fori_loop(0, nch, drop_body, jnp.zeros((1, tq), F32))

    cw = C_HEADS * C_DIM
    lane_c = lax.broadcasted_iota(jnp.int32, (tq, cw), 1)
    cq = cq_ref[...].astype(F32) * (C_DIM ** -0.5 * LOG2E)
    qcat = jnp.concatenate([jnp.where(lane_c // C_DIM == h, cq, 0.0) for h in range(C_HEADS)],
                           axis=0).astype(BF16)
    acc_sc[...] = jnp.zeros(acc_sc.shape, F32)
    nd = bias_ref.shape[1]
    last = (q0 + tq - 1) // ca

    def scores_into(slot, j):
        jc = jnp.minimum(j, last)
        kc = ck_ref[pl.ds(pl.multiple_of(jc * ca, ca), ca), :]
        sub = pl.multiple_of((jc % (ch // ca)) * ca, ca)
        sel = sc_ref[jc // (ch // ca), pl.ds(sub, ca), :] >= jnp.where(j <= last, thr, -NEG)
        bias = jnp.concatenate([
            jnp.concatenate([
                bias_ref[h, jnp.clip(q0 // LANES + c - (jc * (ca // LANES) + u), 0, nd - 1)]
                for h in range(C_HEADS) for c in range(tq // LANES)], axis=1)
            for u in range(ca // LANES)], axis=0)
        s_sc[slot] = jnp.where(jnp.concatenate([sel] * C_HEADS, axis=1),
                               _dot_nt(kc, qcat) + bias, NEG)

    def absorb(slot, j, carry):
        m_prev, l_prev = carry
        m_new = jnp.maximum(m_prev, jnp.max(s_sc[slot], axis=0, keepdims=True))
        alpha = jnp.exp2(m_prev - m_new)
        p = jnp.exp2(s_sc[slot] - m_new)
        l_new = alpha * l_prev + jnp.sum(p, axis=0, keepdims=True)
        pb = p.astype(BF16)
        jc = jnp.minimum(j, last)
        for h in range(C_HEADS):
            cols = slice(h * tq, (h + 1) * tq)
            vt = cvt_ref[jc, h * C_DIM:(h + 1) * C_DIM, :]
            acc_sc[h] = alpha[:, cols] * acc_sc[h] + _dot(vt, pb[:, cols])
        return m_new, l_new

    scores_into(0, 0)

    def attn_body(jj, carry):
        a = 2 * jj
        scores_into(1, a + 1)
        carry = absorb(0, a, carry)
        scores_into(0, a + 2)
        return absorb(1, a + 1, carry)

    neg = jnp.full((1, C_HEADS * tq), NEG, F32)
    zero = jnp.zeros((1, C_HEADS * tq), F32)
    _, l = lax.fori_loop(0, (last + 2) // 2, attn_body, (neg, zero))
    inv = 1.0 / l
    o_t = jnp.concatenate([acc_sc[h] * inv[:, h * tq:(h + 1) * tq] for h in range(C_HEADS)], axis=0)
    o_ref[...] = o_t.T.astype(o_ref.dtype)


def _dsa_attn(proj, proj_i, iwt, cvt, bias_c, tq, ch):
    B, S, _ = proj.shape
    ca = cvt.shape[-1]
    topk = min(TOPK_MAX, S // 4)
    nd = bias_c.shape[1]
    cw = C_HEADS * C_DIM
    return pl.pallas_call(
        functools.partial(_dsa_kernel, tq=tq, ch=ch, ca=ca, topk=topk),
        grid=(B, S // tq),
        in_specs=[pl.BlockSpec((None, tq, IDX_HEADS * IDX_DIM), lambda b, i: (b, i, 3)),
                  pl.BlockSpec((None, S, LANES), lambda b, i: (b, 0, 0)),
                  pl.BlockSpec((None, IDX_HEADS, tq), lambda b, i: (b, 0, i)),
                  pl.BlockSpec((None, tq, cw), lambda b, i: (b, i, 8)),
                  pl.BlockSpec((None, S, cw), lambda b, i: (b, 0, 9)),
                  pl.BlockSpec((None, S // ca, cw, ca), lambda b, i: (b, 0, 0, 0)),
                  pl.BlockSpec((C_HEADS, nd, LANES, LANES), lambda b, i: (0, 0, 0, 0))],
        out_specs=pl.BlockSpec((None, tq, cw), lambda b, i: (b, i, 0)),
        out_shape=jax.ShapeDtypeStruct((B, S, cw), BF16),
        scratch_shapes=[pltpu.VMEM((S // ch, ch, tq), F32),
                        pltpu.VMEM((S // ch, ch, tq), SEL16),
                        pltpu.VMEM((2, ca, C_HEADS * tq), F32),
                        pltpu.VMEM((C_HEADS, C_DIM, tq), F32)],
        compiler_params=_cparams(("parallel", "arbitrary")),
        name="dsa_attn",
    )(proj, proj_i, iwt, proj, proj, cvt, bias_c)


def _memkv_kernel(mem_ref, g_ref, w_ref, o_ref):
    h = _rms(mem_ref[...], g_ref[...]).astype(BF16)
    o_ref[...] = _dot(h, w_ref[...]).astype(o_ref.dtype)


def _memkv(mem, g, w):
    B, M, D = mem.shape
    L = w.shape[0]
    return pl.pallas_call(
        _memkv_kernel,
        grid=(L, B),
        in_specs=[pl.BlockSpec((None, M, D), lambda l, b: (b, 0, 0)),
                  pl.BlockSpec((None, 1, D), lambda l, b: (l, 0, 0)),
                  pl.BlockSpec((None, D, 2 * D), lambda l, b: (l, 0, 0))],
        out_specs=pl.BlockSpec((None, None, M, 2 * D), lambda l, b: (l, b, 0, 0)),
        out_shape=jax.ShapeDtypeStruct((L, B, M, 2 * D), BF16),
        compiler_params=_cparams(("parallel", "arbitrary")),
        name="mem_kv",
    )(mem, g, w)


def _outmem_kernel(x_ref, oa_ref, ob_ref, oc_ref, wo_ref, g_ref, wq_ref, kv_ref, wmo_ref, o_ref):
    D = x_ref.shape[-1]
    na, nb = oa_ref.shape[-1], ob_ref.shape[-1]
    x = (x_ref[...] + _dot(oa_ref[...], wo_ref[0:na]) + _dot(ob_ref[...], wo_ref[na:na + nb])
         + _dot(oc_ref[...], wo_ref[na + nb:]))
    h = _rms(x, g_ref[...]).astype(BF16)
    hd = D // MEM_HEADS
    q = (_dot(h, wq_ref[...]) * (hd ** -0.5)).astype(BF16)
    outs = []
    for hh in range(MEM_HEADS):
        k = kv_ref[:, hh * hd:(hh + 1) * hd]
        v = kv_ref[:, D + hh * hd:D + (hh + 1) * hd]
        s = _dot_nt(q[:, hh * hd:(hh + 1) * hd], k)
        p = jnp.exp(s - jnp.max(s, axis=-1, keepdims=True))
        p = p / jnp.sum(p, axis=-1, keepdims=True)
        outs.append(_dot(p.astype(BF16), v).astype(BF16))
    o = jnp.concatenate(outs, axis=-1)
    o_ref[...] = x + _dot(o, wmo_ref[...])


def _outmem(x, oa, ob, oc, wo, g, wq, kv, wmo, tm):
    B, S, D = x.shape
    M = kv.shape[1]
    const = lambda b, i: (0, 0)
    return pl.pallas_call(
        _outmem_kernel,
        grid=(B, S // tm),
        in_specs=[pl.BlockSpec((None, tm, D), lambda b, i: (b, i, 0)),
                  pl.BlockSpec((None, tm, oa.shape[-1]), lambda b, i: (b, i, 0)),
                  pl.BlockSpec((None, tm, ob.shape[-1]), lambda b, i: (b, i, 0)),
                  pl.BlockSpec((None, tm, oc.shape[-1]), lambda b, i: (b, i, 0)),
                  pl.BlockSpec(wo.shape, const),
                  pl.BlockSpec((1, D), const),
                  pl.BlockSpec(wq.shape, const),
                  pl.BlockSpec((None, M, 2 * D), lambda b, i: (b, 0, 0)),
                  pl.BlockSpec(wmo.shape, const)],
        out_specs=pl.BlockSpec((None, tm, D), lambda b, i: (b, i, 0)),
        out_shape=jax.ShapeDtypeStruct((B, S, D), F32),
        compiler_params=_cparams(("parallel", "arbitrary")),
        name="outproj_memattn",
    )(x, oa, ob, oc, wo, g, wq, kv, wmo)


HALO = 8


def _ffn_kernel(x_ref, xp_ref, g_ref, wg_ref, wv_ref, cwg_ref, cwv_ref, cbg_ref, cbv_ref, wd_ref,
                o_ref, *, tm, fc):
    i = pl.program_id(1)
    x = x_ref[...]
    g = g_ref[...]
    hp = _rms(xp_ref[...], g) * jnp.where(i > 0, 1.0, 0.0)
    h = jnp.concatenate([hp, _rms(x, g)], axis=0).astype(BF16)
    F = wd_ref.shape[0]

    def conv(u, cw_ref, cb_ref, c, e):
        out = cb_ref[:, c:e]
        for j in range(CONV_WIDTH):
            shift = CONV_WIDTH - 1 - j
            out = out + cw_ref[j:j + 1, c:e] * u[HALO - shift:HALO - shift + tm]
        return out

    acc = x
    for c in range(0, F, fc):
        e = min(c + fc, F)
        gate = conv(_dot(h, wg_ref[:, c:e]), cwg_ref, cbg_ref, c, e)
        val = conv(_dot(h, wv_ref[:, c:e]), cwv_ref, cbv_ref, c, e)
        act = (gate * jax.nn.sigmoid(gate) * val).astype(BF16)
        acc = acc + _dot(act, wd_ref[c:e, :])
    o_ref[...] = acc


def _ffn(x, g, wg, wv, cwg, cwv, cbg, cbv, wd, tm, fc):
    B, S, D = x.shape
    F = wd.shape[0]
    const = lambda b, i: (0, 0)
    hb = tm // HALO
    return pl.pallas_call(
        functools.partial(_ffn_kernel, tm=tm, fc=fc),
        grid=(B, S // tm),
        in_specs=[pl.BlockSpec((None, tm, D), lambda b, i: (b, i, 0)),
                  pl.BlockSpec((None, HALO, D), lambda b, i: (b, jnp.maximum(i * hb - 1, 0), 0)),
                  pl.BlockSpec((1, D), const),
                  pl.BlockSpec((D, F), const), pl.BlockSpec((D, F), const),
                  pl.BlockSpec((CONV_WIDTH, F), const), pl.BlockSpec((CONV_WIDTH, F), const),
                  pl.BlockSpec((1, F), const), pl.BlockSpec((1, F), const),
                  pl.BlockSpec((F, D), const)],
        out_specs=pl.BlockSpec((None, tm, D), lambda b, i: (b, i, 0)),
        out_shape=jax.ShapeDtypeStruct((B, S, D), F32),
        compiler_params=_cparams(("parallel", "arbitrary")),
        name="conv_ffn",
    )(x, x, g, wg, wv, cwg, cwv, cbg, cbv, wd)


def _final_norm_kernel(x_ref, g_ref, o_ref):
    o_ref[...] = _rms(x_ref[...], g_ref[...])


def _final_norm(x2, g, tm):
    T, D = x2.shape
    return pl.pallas_call(
        _final_norm_kernel,
        grid=(T // tm,),
        in_specs=[pl.BlockSpec((tm, D), lambda i: (i, 0)), pl.BlockSpec((1, D), lambda i: (0, 0))],
        out_specs=pl.BlockSpec((tm, D), lambda i: (i, 0)),
        out_shape=jax.ShapeDtypeStruct((T, D), F32),
        compiler_params=_cparams(("parallel",)),
        name="final_norm",
    )(x2, g)


def _tile_params(S, tq_a, tk_a):
    big = 1 << 30
    gran, (d_min, d_const) = min(tq_a, tk_a), _a_offsets(tq_a, tk_a)
    pa = []
    for h in range(A_HEADS):
        pa += [(h, d * gran, 1, 0, big, -1) for d in range(d_min, d_const + 1)]
        pa.append((h, 0, 1, 1, 0, -1))
    pb = []
    for _, r in DILATED_PATTERNS:
        for h in range(B_HEADS):
            pb.append((A_HEADS + h, DIL_BLK, r, 1, DIL_BLK, 1))
            pb.append((A_HEADS + h, 0, r, 0, DIL_BLK, 1))
    pc = [(A_HEADS + B_HEADS + h, d * LANES, 1, -big, big, -1)
          for h in range(C_HEADS) for d in range(_n_offsets(S, LANES))]
    to = lambda p: jnp.asarray(p, jnp.int32)
    return to(pa), to(pb), to(pc)


def _dil_bias_layout(tiles):
    t = tiles.reshape(len(DILATED_PATTERNS), B_HEADS, 2, DIL_BLK, DIL_BLK)
    return jnp.concatenate([t[:, :, 0], t[:, :, 1]], axis=-1).reshape(
        len(DILATED_PATTERNS), B_HEADS * DIL_BLK, 2 * DIL_BLK)


def _n_offsets(S, blk):
    return min(S // blk, REL_MAX_DIST // blk + 2)


def _a_offsets(tq, tk):
    gran = min(tq, tk)
    return -(tq // gran - 1), -(-(REL_MAX_DIST - 1 + tk) // gran)


def _in_weights(w_in_l):
    sizes = (512, 512, 512, 256, 256, 256, 256, 256, 256, 512, 64, 8)
    offs = [0]
    for s in sizes:
        offs.append(offs[-1] + s)
    (aq, ak, av, bq, bk, bv, cq, ck, cv, iq, ik, iw) = [w_in_l[:, offs[i]:offs[i + 1]] for i in range(12)]
    pad = jnp.zeros((w_in_l.shape[0], N_I - 2 * IDX_DIM - IDX_HEADS), w_in_l.dtype)
    return jnp.concatenate([aq, ak, av, iq, cq, ck, cv, bq, bk, bv, ik, ik, iw, pad], axis=1).astype(BF16)


def _forward(x, mem, rel_bias, norm_mix, w_in, lam_q1, lam_k1, lam_q2, lam_k2, subln, w_out,
             norm_mem, norm_memkv, w_mq, w_mkv, w_mo, norm_ffn, w_up, conv_w, conv_b, w_down,
             norm_final, *, tq_a, tk_a, tq_c, ch_c, ca_c, tm_proj, tm_mem, tm_ffn, fc):
    B, S, D = x.shape
    L = w_in.shape[0]
    F = w_down.shape[1]
    pa, pb, pc = _tile_params(S, tq_a, tk_a)
    bias_a = _bias_tiles(pa, rel_bias, tk_a, tq_a, LOG2E).reshape(A_HEADS, -1, tk_a, tq_a)
    d_min, d_const = _a_offsets(tq_a, tk_a)
    bias_b = _dil_bias_layout(_bias_tiles(pb, rel_bias, DIL_BLK, DIL_BLK, LOG2E))
    bias_c = _bias_tiles(pc, rel_bias, LANES, LANES, LOG2E).reshape(C_HEADS, -1, LANES, LANES)
    kv_all = _memkv(mem, norm_memkv.reshape(L, 1, D), w_mkv.astype(BF16))

    for l in range(L):
        lam_init = 0.8 - 0.6 * math.exp(-0.3 * l)
        proj, proj_b, proj_i = _inproj(x.reshape(B * S, D), norm_mix[l].reshape(1, D),
                                       _in_weights(w_in[l]), tm_proj)
        proj = proj.reshape(B, S, N_MAIN)
        lamv = jnp.stack([lam_q1[l], lam_k1[l], lam_q2[l], lam_k2[l]], axis=0)
        avt = proj[:, :, 2 * A_HEADS * LANES:3 * A_HEADS * LANES].reshape(
            B, S // tk_a, tk_a, A_HEADS, A_V).transpose(0, 3, 1, 4, 2)
        cw = C_HEADS * C_DIM
        cvt = proj[:, :, N_MAIN - cw:].reshape(B, S // ca_c, ca_c, cw).transpose(0, 1, 3, 2)
        proj_i = proj_i.reshape(B, S, N_I)
        iwt = proj_i[:, :, 2 * IDX_DIM:2 * IDX_DIM + IDX_HEADS].transpose(0, 2, 1)
        o_a = _diff_attn(proj, avt, lamv, bias_a, subln[l].reshape(A_V, 1), lam_init, tq_a,
                         d_min, d_const)
        o_b = _dil_attn(proj_b.reshape(B, S, N_B), bias_b)
        o_c = _dsa_attn(proj, proj_i, iwt, cvt, bias_c, tq_c, ch_c)
        x = _outmem(x, o_a, o_b, o_c, w_out[l].astype(BF16), norm_mem[l].reshape(1, D),
                    w_mq[l].astype(BF16), kv_all[l], w_mo[l].astype(BF16), tm_mem)
        wu = w_up[l].astype(BF16)
        x = _ffn(x, norm_ffn[l].reshape(1, D), wu[:, :F], wu[:, F:], conv_w[l][:, :F], conv_w[l][:, F:],
                 conv_b[l][:F].reshape(1, F), conv_b[l][F:].reshape(1, F), w_down[l].astype(BF16),
                 tm_ffn, fc)
    return _final_norm(x.reshape(B * S, D), norm_final.reshape(1, D), tm_proj).reshape(B, S, D)


def kernel(x, mem, rel_bias, norm_mix, w_in, lam_q1, lam_k1, lam_q2, lam_k2, subln, w_out,
           norm_mem, norm_memkv, w_mq, w_mkv, w_mo, norm_ffn, w_up, conv_w, conv_b, w_down,
           norm_final):
    return _forward(x, mem, rel_bias, norm_mix, w_in, lam_q1, lam_k1, lam_q2, lam_k2, subln, w_out,
                    norm_mem, norm_memkv, w_mq, w_mkv, w_mo, norm_ffn, w_up, conv_w, conv_b, w_down,
                    norm_final, tq_a=512, tk_a=256, tq_c=256, ch_c=512, ca_c=256, tm_proj=512, tm_mem=512, tm_ffn=512,
                    fc=256)
```

```python
import functools
import math

import jax
import jax.numpy as jnp
from jax import lax
from jax.experimental import pallas as pl
from jax.experimental.pallas import tpu as pltpu

F32 = jnp.float32
BF16 = jnp.bfloat16

EPS = 1e-6
NEG = -1e30
LOG2E = math.log2(math.e)
LANES = 128
VMEM_LIMIT = 56 * 1024 * 1024

A_HEADS, A_QK, A_V = 4, 64, 128
B_HEADS, B_DIM = 4, 64
C_HEADS, C_DIM = 4, 64
IDX_HEADS, IDX_DIM = 8, 64
TOPK_MAX = 256
DILATED_PATTERNS = ((128, 1), (512, 4), (2048, 16))
ONES_ROWS = 16
DIL_BLK = 128
DIL_UNROLL = 4
REL_BUCKETS, REL_MAX_DIST = 32, 2048
MEM_HEADS = 4
CONV_WIDTH = 3

N_MAIN = 2816
N_B = 768
N_I = 256


def _cparams(sem):
    return pltpu.CompilerParams(dimension_semantics=sem, vmem_limit_bytes=VMEM_LIMIT)


def _dot(a, b):
    return jnp.dot(a, b, preferred_element_type=F32)


def _dot_nt(a, b):
    return lax.dot_general(a, b, (((1,), (1,)), ((), ())), preferred_element_type=F32)


def _rms(x, g):
    return x * lax.rsqrt(jnp.mean(x * x, axis=-1, keepdims=True) + EPS) * g


def _rel_bucket(n):
    max_exact = REL_BUCKETS // 2
    nf = jnp.maximum(n, 1).astype(F32)
    large = max_exact + (jnp.log(nf / max_exact) / math.log(REL_MAX_DIST / max_exact)
                         * (REL_BUCKETS - max_exact)).astype(jnp.int32)
    large = jnp.minimum(large, REL_BUCKETS - 1)
    return jnp.where(n < max_exact, n, large)


def _bias_tiles_kernel(par_ref, tab_ref, o_ref, *, tq, tk, scale):
    t = pl.program_id(0)
    head, off, mult = par_ref[t, 0], par_ref[t, 1], par_ref[t, 2]
    lo, hi, sgn = par_ref[t, 3], par_ref[t, 4], par_ref[t, 5]
    dist = off + sgn * (lax.broadcasted_iota(jnp.int32, (tq, tk), 0)
                        - lax.broadcasted_iota(jnp.int32, (tq, tk), 1))
    bucket = _rel_bucket(jnp.maximum(dist * mult, 0))
    val = jnp.zeros((tq, tk), F32)
    for b in range(REL_BUCKETS):
        val = jnp.where(bucket == b, tab_ref[b, head] * scale, val)
    o_ref[...] = jnp.where((dist >= lo) & (dist <= hi), val, NEG)


def _bias_tiles(params, table, tq, tk, scale=1.0):
    n = params.shape[0]
    return pl.pallas_call(
        functools.partial(_bias_tiles_kernel, tq=tq, tk=tk, scale=scale),
        grid=(n,),
        in_specs=[pl.BlockSpec(memory_space=pltpu.SMEM), pl.BlockSpec(memory_space=pltpu.SMEM)],
        out_specs=pl.BlockSpec((None, tq, tk), lambda t: (t, 0, 0)),
        out_shape=jax.ShapeDtypeStruct((n, tq, tk), F32),
        compiler_params=_cparams(("arbitrary",)),
        name="bias_tiles",
    )(params, table)


def _inproj_kernel(x_ref, g_ref, w_ref, om_ref, ob_ref, oi_ref):
    h = _rms(x_ref[...], g_ref[...]).astype(BF16)
    step = 512
    for c in range(0, N_MAIN, step):
        e = min(c + step, N_MAIN)
        om_ref[:, c:e] = _dot(h, w_ref[:, c:e]).astype(BF16)
    ob_ref[...] = _dot(h, w_ref[:, N_MAIN:N_MAIN + N_B])
    oi_ref[...] = _dot(h, w_ref[:, N_MAIN + N_B:])


def _inproj(x2, g, w, tm):
    T, D = x2.shape
    n_all = N_MAIN + N_B + N_I
    return pl.pallas_call(
        _inproj_kernel,
        grid=(T // tm,),
        in_specs=[pl.BlockSpec((tm, D), lambda i: (i, 0)),
                  pl.BlockSpec((1, D), lambda i: (0, 0)),
                  pl.BlockSpec((D, n_all), lambda i: (0, 0))],
        out_specs=[pl.BlockSpec((tm, N_MAIN), lambda i: (i, 0)),
                   pl.BlockSpec((tm, N_B), lambda i: (i, 0)),
                   pl.BlockSpec((tm, N_I), lambda i: (i, 0))],
        out_shape=[jax.ShapeDtypeStruct((T, N_MAIN), BF16),
                   jax.ShapeDtypeStruct((T, N_B), F32),
                   jax.ShapeDtypeStruct((T, N_I), F32)],
        compiler_params=_cparams(("parallel",)),
        name="inproj",
    )(x2, g, w)


def _diff_attn_kernel(lam_ref, q_ref, k_ref, vt_ref, bias_ref, g_ref, o_ref, s_sc, acc_sc,
                      *, tq, tk, d_min, d_const, lam_init):
    qi = pl.program_id(2)
    gran = min(tq, tk)
    last = (qi * tq + tq - 1) // tk
    masked_tile = d_const - d_min + 1
    lv = lam_ref[...]
    lam = (jnp.exp(jnp.sum(lv[0:1] * lv[1:2], axis=-1, keepdims=True))
           - jnp.exp(jnp.sum(lv[2:3] * lv[3:4], axis=-1, keepdims=True)) + lam_init)
    lane = lax.broadcasted_iota(jnp.int32, (tq, LANES), 1)
    q = q_ref[...].astype(F32) * (A_QK ** -0.5 * LOG2E)
    qcat = jnp.concatenate([jnp.where(lane < A_QK, q, 0.0), jnp.where(lane >= A_QK, q, 0.0)],
                           axis=0).astype(BF16)
    acc_sc[...] = jnp.zeros(acc_sc.shape, F32)

    def scores_into(slot, j):
        jc = jnp.minimum(j, last)
        k = k_ref[pl.ds(pl.multiple_of(jc * tk, tk), tk), :]
        d = qi * (tq // gran) - jc * (tk // gran)
        bias = bias_ref[jnp.where(j <= last, jnp.minimum(d, d_const) - d_min, masked_tile)]
        s_sc[slot] = _dot_nt(k, qcat) + jnp.concatenate([bias, bias], axis=1)

    def absorb(slot, j, m_prev):
        m_new = jnp.maximum(m_prev, jnp.max(s_sc[slot], axis=0, keepdims=True))
        alpha = jnp.exp2(m_prev - m_new)
        p = jnp.exp2(s_sc[slot] - m_new)
        vt = vt_ref[jnp.minimum(j, last)]
        acc_sc[...] = alpha * acc_sc[...] + _dot(vt, p.astype(BF16))
        return m_new

    scores_into(0, 0)

    def body(jj, m):
        a = 2 * jj
        scores_into(1, a + 1)
        m = absorb(0, a, m)
        scores_into(0, a + 2)
        return absorb(1, a + 1, m)

    lax.fori_loop(0, (last + 2) // 2, body, jnp.full((1, 2 * tq), NEG, F32))
    inv = 1.0 / acc_sc[A_V:A_V + 1, :]
    o = acc_sc[:A_V, :tq] * inv[:, :tq] - lam * (acc_sc[:A_V, tq:] * inv[:, tq:])
    o = o * lax.rsqrt(jnp.mean(o * o, axis=0, keepdims=True) + EPS) * g_ref[...] * (1.0 - lam_init)
    o_ref[...] = o.T.astype(o_ref.dtype)


def _diff_attn(proj, vt, lamv, bias_a, subln_g, lam_init, tq, d_min, d_const):
    B, S, _ = proj.shape
    nd, tk = bias_a.shape[1], bias_a.shape[2]
    return pl.pallas_call(
        functools.partial(_diff_attn_kernel, tq=tq, tk=tk, d_min=d_min, d_const=d_const,
                          lam_init=lam_init),
        grid=(A_HEADS, B, S // tq),
        in_specs=[pl.BlockSpec((4, A_QK), lambda h, b, i: (0, 0)),
                  pl.BlockSpec((None, tq, LANES), lambda h, b, i: (b, i, h)),
                  pl.BlockSpec((None, S, LANES), lambda h, b, i: (b, 0, A_HEADS + h)),
                  pl.BlockSpec((None, None, S // tk, A_V + ONES_ROWS, tk),
                               lambda h, b, i: (b, h, 0, 0, 0)),
                  pl.BlockSpec((None, nd, tk, tq), lambda h, b, i: (h, 0, 0, 0)),
                  pl.BlockSpec((A_V, 1), lambda h, b, i: (0, 0))],
        out_specs=pl.BlockSpec((None, tq, LANES), lambda h, b, i: (b, i, h)),
        out_shape=jax.ShapeDtypeStruct((B, S, A_HEADS * A_V), BF16),
        scratch_shapes=[pltpu.VMEM((2, tk, 2 * tq), F32),
                        pltpu.VMEM((A_V + ONES_ROWS, 2 * tq), F32)],
        compiler_params=_cparams(("parallel", "parallel", "arbitrary")),
        name="diff_attn",
    )(lamv, proj, proj, vt, bias_a, subln_g)


class _LaneHalves:
    def __init__(self, *refs):
        self.refs = refs

    def __getitem__(self, idx):
        return jnp.concatenate([r[idx] for r in self.refs], axis=1)

    def __setitem__(self, idx, val):
        for i, r in enumerate(self.refs):
            r[idx] = val[:, i * LANES:(i + 1) * LANES]


def _dil_kernel(q0, q1, k0, k1, v0, v1, bias_ref, o_ref, n0, n1, m0, m1, l0, l1, *, seq, patterns):
    q_ref, k_ref, v_ref = _LaneHalves(q0, q1), _LaneHalves(k0, k1), _LaneHalves(v0, v1)
    n_sc, m_sc, l_sc = _LaneHalves(n0, n1), _LaneHalves(m0, m1), _LaneHalves(l0, l1)
    blk = DIL_BLK
    width = B_HEADS * B_DIM
    head_of_lane = lax.broadcasted_iota(jnp.int32, (blk, width), 1) // B_DIM

    def per_head(x):
        parts = [jnp.broadcast_to(x[h * blk:(h + 1) * blk], (blk, width)) for h in range(B_HEADS)]
        out = parts[-1]
        for h in range(B_HEADS - 2, -1, -1):
            out = jnp.where(head_of_lane == h, parts[h], out)
        return out

    def attend(base, r, with_prev, p_idx):
        q = q_ref[pl.ds(base, blk, stride=r), :] * (B_DIM ** -0.5 * LOG2E)
        qcat = jnp.concatenate([jnp.where(head_of_lane == h, q, 0.0) for h in range(B_HEADS)],
                               axis=0).astype(BF16)
        if with_prev:
            keys = pl.ds(base - blk * r, 2 * blk, stride=r)
            bias = bias_ref[p_idx]
        else:
            keys = pl.ds(base, blk, stride=r)
            bias = bias_ref[p_idx, :, blk:]
        s = _dot_nt(qcat, k_ref[keys, :].astype(BF16)) + bias
        mx = jnp.max(s, axis=-1, keepdims=True)
        p = jnp.exp2(s - mx)
        l = jnp.sum(p, axis=-1, keepdims=True)
        o = _dot(p.astype(BF16), v_ref[keys, :].astype(BF16))
        return per_head(o), per_head(mx), per_head(l)

    def merge(base, r, res, is_first):
        o, m, l = res
        rows = pl.ds(base, blk, stride=r)
        if is_first:
            n_sc[rows, :] = o
            m_sc[rows, :] = m
            l_sc[rows, :] = l
        else:
            m_old = m_sc[rows, :]
            m_new = jnp.maximum(m_old, m)
            a = jnp.exp2(m_old - m_new)
            b = jnp.exp2(m - m_new)
            n_sc[rows, :] = a * n_sc[rows, :] + b * o
            l_sc[rows, :] = a * l_sc[rows, :] + b * l
            m_sc[rows, :] = m_new

    def sweep(count, base_of, r, with_prev, p_idx):
        def trip(width):
            def body(i, carry):
                bases = [base_of(i * width + u) for u in range(width)]
                results = [attend(b, r, with_prev, p_idx) for b in bases]
                for b, res in zip(bases, results):
                    merge(b, r, res, p_idx == len(patterns) - 1)
                return carry
            return body

        full = count // DIL_UNROLL
        lax.fori_loop(0, full, trip(DIL_UNROLL), 0)
        rest = count - full * DIL_UNROLL
        if rest:
            lax.fori_loop(full * DIL_UNROLL, count, trip(1), 0)

    for p_idx, (_, r) in reversed(list(enumerate(patterns))):
        nb = seq // (r * blk)
        sweep(r, lambda c: c, r, False, p_idx)
        sweep(r * (nb - 1), lambda idx, r=r: (idx // r + 1) * blk * r + idx % r, r, True, p_idx)

    everything = (slice(None), slice(None))
    o_ref[...] = (n_sc[everything] / l_sc[everything]).astype(o_ref.dtype)


def _dil_attn(proj_b, bias_b):
    B, S, _ = proj_b.shape
    width = B_HEADS * B_DIM
    return pl.pallas_call(
        functools.partial(_dil_kernel, seq=S, patterns=DILATED_PATTERNS),
        grid=(B,),
        in_specs=[pl.BlockSpec((None, S, LANES), functools.partial(lambda i, b: (b, 0, i), i))
                  for i in range(3 * width // LANES)]
                 + [pl.BlockSpec(bias_b.shape, lambda b: (0, 0, 0))],
        out_specs=pl.BlockSpec((None, S, width), lambda b: (b, 0, 0)),
        out_shape=jax.ShapeDtypeStruct((B, S, width), BF16),
        scratch_shapes=[pltpu.VMEM((S, LANES), F32)] * (3 * width // LANES),
        compiler_params=_cparams(("parallel",)),
        name="dilated_attn",
    )(*([proj_b] * (3 * width // LANES)), bias_b)


SEL16 = jnp.bfloat16


def _key_float(k):
    return pltpu.bitcast(jnp.where(k >= 0, k, k ^ jnp.int32(0x7FFFFFFF)), F32)


def _high_half(f):
    return pltpu.bitcast(pltpu.bitcast(f, jnp.int32) & jnp.int32(-65536), F32)


def _dsa_kernel(iq_ref, ik_ref, iwt_ref, cq_ref, ck_ref, cvt_ref, bias_ref, o_ref, sc_ref, hi_ref, s_sc,
                acc_sc,
                *, tq, ch, ca, topk):
    qi = pl.program_id(1)
    q0 = qi * tq
    nch = (q0 + tq - 1) // ch + 1
    qidx = q0 + lax.broadcasted_iota(jnp.int32, (1, tq), 1)
    lane = lax.broadcasted_iota(jnp.int32, (tq, LANES), 1)
    first = lane < IDX_DIM

    w = iwt_ref[...] * (IDX_HEADS ** -0.5)
    qs = []
    for h in range(IDX_HEADS):
        tile = iq_ref[:, (h // 2) * LANES:(h // 2 + 1) * LANES].astype(F32) * (IDX_DIM ** -0.5)
        qs.append(jnp.where(first if h % 2 == 0 else ~first, tile, 0.0).astype(BF16))

    def score_body(j, carry):
        start = pl.multiple_of(j * ch, ch)
        kk = ik_ref[pl.ds(start, ch), :].astype(BF16)
        sc = jnp.zeros((ch, tq), F32)
        for h in range(IDX_HEADS):
            sc = sc + jnp.maximum(_dot_nt(kk, qs[h]), 0.0) * w[h:h + 1, :]
        kidx = start + lax.broadcasted_iota(jnp.int32, (ch, 1), 0)
        sc = jnp.where(kidx <= qidx, sc, NEG)
        sc_ref[j] = sc
        hi_ref[j] = _high_half(sc).astype(SEL16)
        return carry

    lax.fori_loop(0, nch, score_body, 0)

    rows = 32

    def count_where(pred):
        def body(j, acc):
            hit = jnp.where(pred(sc_ref[j]), 1.0, 0.0)
            return acc + jnp.sum(hit.reshape(ch // rows, rows, tq), axis=0)
        acc = lax.fori_loop(0, nch, body, jnp.zeros((rows, tq), F32))
        return jnp.sum(acc, axis=0, keepdims=True)

    def count_high_ge(v):
        one, zero16 = jnp.ones((), SEL16), jnp.zeros((), SEL16)
        def body(j, acc):
            hit = jnp.where(hi_ref[j] >= v, one, zero16).reshape(ch // rows, rows, tq)
            parts = [hit[i] for i in range(ch // rows)]
            while len(parts) > 1:
                parts = [parts[i] + parts[i + 1] for i in range(0, len(parts), 2)]
            return acc + parts[0]
        acc = lax.fori_loop(0, nch, body, jnp.zeros((rows, tq), SEL16))
        return jnp.sum(acc.astype(F32), axis=0, keepdims=True)

    kf = float(topk)
    n_valid = (qidx + 1).astype(F32)
    half_bits = 16

    def high_body(i, st):
        u, c_lo = st
        cand = u | jnp.left_shift(1, half_bits - 1 - i)
        v = _high_half(_key_float((cand - (1 << (half_bits - 1))) << half_bits)).astype(SEL16)
        c = count_high_ge(v)
        keep = c >= kf
        return jnp.where(keep, cand, u), jnp.where(keep, c, c_lo)

    u, c_lo = lax.fori_loop(0, half_bits, high_body, (jnp.zeros((1, tq), jnp.int32), n_valid))
    key_hi = (u - (1 << (half_bits - 1))) << half_bits

    def low_body(i, st):
        lo_bits, c_lo = st
        cand = lo_bits | jnp.left_shift(1, half_bits - 1 - i)
        c = count_where(lambda sc: sc >= _key_float(key_hi | cand))
        keep = c >= kf
        return jnp.where(keep, cand, lo_bits), jnp.where(keep, c, c_lo)

    lo_bits, c_lo = lax.fori_loop(0, half_bits, low_body, (jnp.zeros((1, tq), jnp.int32), c_lo))
    thr = jnp.where(n_valid > kf, _key_float(key_hi | lo_bits), 0.5 * NEG)

    tie_rows = jnp.logical_and(n_valid > kf, c_lo > kf)

    @pl.when(jnp.max(jnp.where(tie_rows, 1.0, 0.0)) > 0.0)
    def _():
        need = kf - count_where(lambda sc: sc > thr)
        lower = (lax.broadcasted_iota(jnp.int32, (ch, ch), 1)
                 <= lax.broadcasted_iota(jnp.int32, (ch, ch), 0)).astype(BF16)

        def drop_body(j, seen):
            sc = sc_ref[j]
            eq = sc == thr
            eqf = jnp.where(eq, 1.0, 0.0)
            rank = _dot(lower, eqf.astype(BF16)) + seen
            drop = jnp.logical_and(jnp.logical_and(eq, rank > need), tie_rows)
            sc_ref[j] = jnp.where(drop, NEG, sc)
            return seen + jnp.sum(eqf, axis=0, keepdims=True)

        lax.fori_loop(0, nch, drop_body, jnp.zeros((1, tq), F32))

    cw = C_HEADS * C_DIM
    lane_c = lax.broadcasted_iota(jnp.int32, (tq, cw), 1)
    cq = cq_ref[...].astype(F32) * (C_DIM ** -0.5 * LOG2E)
    qcat = jnp.concatenate([jnp.where(lane_c // C_DIM == h, cq, 0.0) for h in range(C_HEADS)],
                           axis=0).astype(BF16)
    acc_sc[...] = jnp.zeros(acc_sc.shape, F32)
    nd = bias_ref.shape[1]
    last = (q0 + tq - 1) // ca

    def scores_into(slot, j):
        jc = jnp.minimum(j, last)
        kc = ck_ref[pl.ds(pl.multiple_of(jc * ca, ca), ca), :]
        sub = pl.multiple_of((jc % (ch // ca)) * ca, ca)
        sel = sc_ref[jc // (ch // ca), pl.ds(sub, ca), :] >= jnp.where(j <= last, thr, -NEG)
        bias = jnp.concatenate([
            jnp.concatenate([
                bias_ref[h, jnp.clip(q0 // LANES + c - (jc * (ca // LANES) + u), 0, nd - 1)]
                for h in range(C_HEADS) for c in range(tq // LANES)], axis=1)
            for u in range(ca // LANES)], axis=0)
        s_sc[slot] = jnp.where(jnp.concatenate([sel] * C_HEADS, axis=1),
                               _dot_nt(kc, qcat) + bias, NEG)

    vrows = C_DIM + ONES_ROWS

    def absorb(slot, j, m_prev):
        m_new = jnp.maximum(m_prev, jnp.max(s_sc[slot], axis=0, keepdims=True))
        alpha = jnp.exp2(m_prev - m_new)
        pb = jnp.exp2(s_sc[slot] - m_new).astype(BF16)
        jc = jnp.minimum(j, last)
        for h in range(C_HEADS):
            cols = slice(h * tq, (h + 1) * tq)
            vt = cvt_ref[jc, h * vrows:(h + 1) * vrows, :]
            acc_sc[h] = alpha[:, cols] * acc_sc[h] + _dot(vt, pb[:, cols])
        return m_new

    scores_into(0, 0)

    def attn_body(jj, m):
        a = 2 * jj
        scores_into(1, a + 1)
        m = absorb(0, a, m)
        scores_into(0, a + 2)
        return absorb(1, a + 1, m)

    lax.fori_loop(0, (last + 2) // 2, attn_body, jnp.full((1, C_HEADS * tq), NEG, F32))
    o_t = jnp.concatenate([acc_sc[h, :C_DIM] * (1.0 / acc_sc[h, C_DIM:C_DIM + 1])
                           for h in range(C_HEADS)], axis=0)
    o_ref[...] = o_t.T.astype(o_ref.dtype)


def _dsa_attn(proj, proj_i, iwt, cvt, bias_c, tq, ch):
    B, S, _ = proj.shape
    ca = cvt.shape[-1]
    topk = min(TOPK_MAX, S // 4)
    nd = bias_c.shape[1]
    cw = C_HEADS * C_DIM
    return pl.pallas_call(
        functools.partial(_dsa_kernel, tq=tq, ch=ch, ca=ca, topk=topk),
        grid=(B, S // tq),
        in_specs=[pl.BlockSpec((None, tq, IDX_HEADS * IDX_DIM), lambda b, i: (b, i, 3)),
                  pl.BlockSpec((None, S, LANES), lambda b, i: (b, 0, 0)),
                  pl.BlockSpec((None, IDX_HEADS, tq), lambda b, i: (b, 0, i)),
                  pl.BlockSpec((None, tq, cw), lambda b, i: (b, i, 8)),
                  pl.BlockSpec((None, S, cw), lambda b, i: (b, 0, 9)),
                  pl.BlockSpec((None, S // ca, cvt.shape[2], ca), lambda b, i: (b, 0, 0, 0)),
                  pl.BlockSpec((C_HEADS, nd, LANES, LANES), lambda b, i: (0, 0, 0, 0))],
        out_specs=pl.BlockSpec((None, tq, cw), lambda b, i: (b, i, 0)),
        out_shape=jax.ShapeDtypeStruct((B, S, cw), BF16),
        scratch_shapes=[pltpu.VMEM((S // ch, ch, tq), F32),
                        pltpu.VMEM((S // ch, ch, tq), SEL16),
                        pltpu.VMEM((2, ca, C_HEADS * tq), F32),
                        pltpu.VMEM((C_HEADS, C_DIM + ONES_ROWS, tq), F32)],
        compiler_params=_cparams(("parallel", "arbitrary")),
        name="dsa_attn",
    )(proj, proj_i, iwt, proj, proj, cvt, bias_c)


def _memkv_kernel(mem_ref, g_ref, w_ref, o_ref):
    h = _rms(mem_ref[...], g_ref[...]).astype(BF16)
    o_ref[...] = _dot(h, w_ref[...]).astype(o_ref.dtype)


def _memkv(mem, g, w):
    B, M, D = mem.shape
    L = w.shape[0]
    return pl.pallas_call(
        _memkv_kernel,
        grid=(L, B),
        in_specs=[pl.BlockSpec((None, M, D), lambda l, b: (b, 0, 0)),
                  pl.BlockSpec((None, 1, D), lambda l, b: (l, 0, 0)),
                  pl.BlockSpec((None, D, 2 * D), lambda l, b: (l, 0, 0))],
        out_specs=pl.BlockSpec((None, None, M, 2 * D), lambda l, b: (l, b, 0, 0)),
        out_shape=jax.ShapeDtypeStruct((L, B, M, 2 * D), BF16),
        compiler_params=_cparams(("parallel", "arbitrary")),
        name="mem_kv",
    )(mem, g, w)


def _outmem_kernel(x_ref, oa_ref, ob_ref, oc_ref, wo_ref, g_ref, wq_ref, kv_ref, wmo_ref, o_ref):
    D = x_ref.shape[-1]
    na, nb = oa_ref.shape[-1], ob_ref.shape[-1]
    x = (x_ref[...] + _dot(oa_ref[...], wo_ref[0:na]) + _dot(ob_ref[...], wo_ref[na:na + nb])
         + _dot(oc_ref[...], wo_ref[na + nb:]))
    h = _rms(x, g_ref[...]).astype(BF16)
    hd = D // MEM_HEADS
    q = (_dot(h, wq_ref[...]) * (hd ** -0.5)).astype(BF16)
    outs = []
    for hh in range(MEM_HEADS):
        k = kv_ref[:, hh * hd:(hh + 1) * hd]
        v = kv_ref[:, D + hh * hd:D + (hh + 1) * hd]
        s = _dot_nt(q[:, hh * hd:(hh + 1) * hd], k)
        p = jnp.exp(s - jnp.max(s, axis=-1, keepdims=True))
        p = p / jnp.sum(p, axis=-1, keepdims=True)
        outs.append(_dot(p.astype(BF16), v).astype(BF16))
    o = jnp.concatenate(outs, axis=-1)
    o_ref[...] = x + _dot(o, wmo_ref[...])


def _outmem(x, oa, ob, oc, wo, g, wq, kv, wmo, tm):
    B, S, D = x.shape
    M = kv.shape[1]
    const = lambda b, i: (0, 0)
    return pl.pallas_call(
        _outmem_kernel,
        grid=(B, S // tm),
        in_specs=[pl.BlockSpec((None, tm, D), lambda b, i: (b, i, 0)),
                  pl.BlockSpec((None, tm, oa.shape[-1]), lambda b, i: (b, i, 0)),
                  pl.BlockSpec((None, tm, ob.shape[-1]), lambda b, i: (b, i, 0)),
                  pl.BlockSpec((None, tm, oc.shape[-1]), lambda b, i: (b, i, 0)),
                  pl.BlockSpec(wo.shape, const),
                  pl.BlockSpec((1, D), const),
                  pl.BlockSpec(wq.shape, const),
                  pl.BlockSpec((None, M, 2 * D), lambda b, i: (b, 0, 0)),
                  pl.BlockSpec(wmo.shape, const)],
        out_specs=pl.BlockSpec((None, tm, D), lambda b, i: (b, i, 0)),
        out_shape=jax.ShapeDtypeStruct((B, S, D), F32),
        compiler_params=_cparams(("parallel", "arbitrary")),
        name="outproj_memattn",
    )(x, oa, ob, oc, wo, g, wq, kv, wmo)


HALO = 8


def _ffn_kernel(x_ref, xp_ref, g_ref, wg_ref, wv_ref, cwg_ref, cwv_ref, cbg_ref, cbv_ref, wd_ref,
                o_ref, *, tm, fc):
    i = pl.program_id(1)
    x = x_ref[...]
    g = g_ref[...]
    hp = _rms(xp_ref[...], g) * jnp.where(i > 0, 1.0, 0.0)
    h = jnp.concatenate([hp, _rms(x, g)], axis=0).astype(BF16)
    F = wd_ref.shape[0]

    def conv(u, cw_ref, cb_ref, c, e):
        out = cb_ref[:, c:e]
        for j in range(CONV_WIDTH):
            shift = CONV_WIDTH - 1 - j
            out = out + cw_ref[j:j + 1, c:e] * u[HALO - shift:HALO - shift + tm]
        return out

    acc = x
    for c in range(0, F, fc):
        e = min(c + fc, F)
        gate = conv(_dot(h, wg_ref[:, c:e]), cwg_ref, cbg_ref, c, e)
        val = conv(_dot(h, wv_ref[:, c:e]), cwv_ref, cbv_ref, c, e)
        act = (gate * jax.nn.sigmoid(gate) * val).astype(BF16)
        acc = acc + _dot(act, wd_ref[c:e, :])
    o_ref[...] = acc


def _ffn(x, g, wg, wv, cwg, cwv, cbg, cbv, wd, tm, fc):
    B, S, D = x.shape
    F = wd.shape[0]
    const = lambda b, i: (0, 0)
    hb = tm // HALO
    return pl.pallas_call(
        functools.partial(_ffn_kernel, tm=tm, fc=fc),
        grid=(B, S // tm),
        in_specs=[pl.BlockSpec((None, tm, D), lambda b, i: (b, i, 0)),
                  pl.BlockSpec((None, HALO, D), lambda b, i: (b, jnp.maximum(i * hb - 1, 0), 0)),
                  pl.BlockSpec((1, D), const),
                  pl.BlockSpec((D, F), const), pl.BlockSpec((D, F), const),
                  pl.BlockSpec((CONV_WIDTH, F), const), pl.BlockSpec((CONV_WIDTH, F), const),
                  pl.BlockSpec((1, F), const), pl.BlockSpec((1, F), const),
                  pl.BlockSpec((F, D), const)],
        out_specs=pl.BlockSpec((None, tm, D), lambda b, i: (b, i, 0)),
        out_shape=jax.ShapeDtypeStruct((B, S, D), F32),
        compiler_params=_cparams(("parallel", "arbitrary")),
        name="conv_ffn",
    )(x, x, g, wg, wv, cwg, cwv, cbg, cbv, wd)


def _final_norm_kernel(x_ref, g_ref, o_ref):
    o_ref[...] = _rms(x_ref[...], g_ref[...])


def _final_norm(x2, g, tm):
    T, D = x2.shape
    return pl.pallas_call(
        _final_norm_kernel,
        grid=(T // tm,),
        in_specs=[pl.BlockSpec((tm, D), lambda i: (i, 0)), pl.BlockSpec((1, D), lambda i: (0, 0))],
        out_specs=pl.BlockSpec((tm, D), lambda i: (i, 0)),
        out_shape=jax.ShapeDtypeStruct((T, D), F32),
        compiler_params=_cparams(("parallel",)),
        name="final_norm",
    )(x2, g)


def _tile_params(S, tq_a, tk_a):
    big = 1 << 30
    gran, (d_min, d_const) = min(tq_a, tk_a), _a_offsets(tq_a, tk_a)
    pa = []
    for h in range(A_HEADS):
        pa += [(h, d * gran, 1, 0, big, -1) for d in range(d_min, d_const + 1)]
        pa.append((h, 0, 1, 1, 0, -1))
    pb = []
    for _, r in DILATED_PATTERNS:
        for h in range(B_HEADS):
            pb.append((A_HEADS + h, DIL_BLK, r, 1, DIL_BLK, 1))
            pb.append((A_HEADS + h, 0, r, 0, DIL_BLK, 1))
    pc = [(A_HEADS + B_HEADS + h, d * LANES, 1, -big, big, -1)
          for h in range(C_HEADS) for d in range(_n_offsets(S, LANES))]
    to = lambda p: jnp.asarray(p, jnp.int32)
    return to(pa), to(pb), to(pc)


def _with_ones_rows(vt):
    lead = vt.shape[:-2]
    ones = jnp.ones(lead + (1, vt.shape[-1]), vt.dtype)
    zeros = jnp.zeros(lead + (ONES_ROWS - 1, vt.shape[-1]), vt.dtype)
    return jnp.concatenate([vt, ones, zeros], axis=-2)


def _dil_bias_layout(tiles):
    t = tiles.reshape(len(DILATED_PATTERNS), B_HEADS, 2, DIL_BLK, DIL_BLK)
    return jnp.concatenate([t[:, :, 0], t[:, :, 1]], axis=-1).reshape(
        len(DILATED_PATTERNS), B_HEADS * DIL_BLK, 2 * DIL_BLK)


def _n_offsets(S, blk):
    return min(S // blk, REL_MAX_DIST // blk + 2)


def _a_offsets(tq, tk):
    gran = min(tq, tk)
    return -(tq // gran - 1), -(-(REL_MAX_DIST - 1 + tk) // gran)


def _in_weights(w_in_l):
    sizes = (512, 512, 512, 256, 256, 256, 256, 256, 256, 512, 64, 8)
    offs = [0]
    for s in sizes:
        offs.append(offs[-1] + s)
    (aq, ak, av, bq, bk, bv, cq, ck, cv, iq, ik, iw) = [w_in_l[:, offs[i]:offs[i + 1]] for i in range(12)]
    pad = jnp.zeros((w_in_l.shape[0], N_I - 2 * IDX_DIM - IDX_HEADS), w_in_l.dtype)
    return jnp.concatenate([aq, ak, av, iq, cq, ck, cv, bq, bk, bv, ik, ik, iw, pad], axis=1).astype(BF16)


def _forward(x, mem, rel_bias, norm_mix, w_in, lam_q1, lam_k1, lam_q2, lam_k2, subln, w_out,
             norm_mem, norm_memkv, w_mq, w_mkv, w_mo, norm_ffn, w_up, conv_w, conv_b, w_down,
             norm_final, *, tq_a, tk_a, tq_c, ch_c, ca_c, tm_proj, tm_mem, tm_ffn, fc):
    B, S, D = x.shape
    L = w_in.shape[0]
    F = w_down.shape[1]
    pa, pb, pc = _tile_params(S, tq_a, tk_a)
    bias_a = _bias_tiles(pa, rel_bias, tk_a, tq_a, LOG2E).reshape(A_HEADS, -1, tk_a, tq_a)
    d_min, d_const = _a_offsets(tq_a, tk_a)
    bias_b = _dil_bias_layout(_bias_tiles(pb, rel_bias, DIL_BLK, DIL_BLK, LOG2E))
    bias_c = _bias_tiles(pc, rel_bias, LANES, LANES, LOG2E).reshape(C_HEADS, -1, LANES, LANES)
    kv_all = _memkv(mem, norm_memkv.reshape(L, 1, D), w_mkv.astype(BF16))

    for l in range(L):
        lam_init = 0.8 - 0.6 * math.exp(-0.3 * l)
        proj, proj_b, proj_i = _inproj(x.reshape(B * S, D), norm_mix[l].reshape(1, D),
                                       _in_weights(w_in[l]), tm_proj)
        proj = proj.reshape(B, S, N_MAIN)
        lamv = jnp.stack([lam_q1[l], lam_k1[l], lam_q2[l], lam_k2[l]], axis=0)
        avt = _with_ones_rows(proj[:, :, 2 * A_HEADS * LANES:3 * A_HEADS * LANES].reshape(
            B, S // tk_a, tk_a, A_HEADS, A_V).transpose(0, 3, 1, 4, 2))
        cw = C_HEADS * C_DIM
        cvt = _with_ones_rows(proj[:, :, N_MAIN - cw:].reshape(
            B, S // ca_c, ca_c, C_HEADS, C_DIM).transpose(0, 1, 3, 4, 2))
        cvt = cvt.reshape(B, S // ca_c, C_HEADS * (C_DIM + ONES_ROWS), ca_c)
        proj_i = proj_i.reshape(B, S, N_I)
        iwt = proj_i[:, :, 2 * IDX_DIM:2 * IDX_DIM + IDX_HEADS].transpose(0, 2, 1)
        o_a = _diff_attn(proj, avt, lamv, bias_a, subln[l].reshape(A_V, 1), lam_init, tq_a,
                         d_min, d_const)
        o_b = _dil_attn(proj_b.reshape(B, S, N_B), bias_b)
        o_c = _dsa_attn(proj, proj_i, iwt, cvt, bias_c, tq_c, ch_c)
        x = _outmem(x, o_a, o_b, o_c, w_out[l].astype(BF16), norm_mem[l].reshape(1, D),
                    w_mq[l].astype(BF16), kv_all[l], w_mo[l].astype(BF16), tm_mem)
        wu = w_up[l].astype(BF16)
        x = _ffn(x, norm_ffn[l].reshape(1, D), wu[:, :F], wu[:, F:], conv_w[l][:, :F], conv_w[l][:, F:],
                 conv_b[l][:F].reshape(1, F), conv_b[l][F:].reshape(1, F), w_down[l].astype(BF16),
                 tm_ffn, fc)
    return _final_norm(x.reshape(B * S, D), norm_final.reshape(1, D), tm_proj).reshape(B, S, D)


def kernel(x, mem, rel_bias, norm_mix, w_in, lam_q1, lam_k1, lam_q2, lam_k2, subln, w_out,
           norm_mem, norm_memkv, w_mq, w_mkv, w_mo, norm_ffn, w_up, conv_w, conv_b, w_down,
           norm_final):
    return _forward(x, mem, rel_bias, norm_mix, w_in, lam_q1, lam_k1, lam_q2, lam_k2, subln, w_out,
                    norm_mem, norm_memkv, w_mq, w_mkv, w_mo, norm_ffn, w_up, conv_w, conv_b, w_down,
                    norm_final, tq_a=512, tk_a=256, tq_c=512, ch_c=512, ca_c=256, tm_proj=512, tm_mem=512, tm_ffn=512,
                    fc=512)
```

```python
import functools
import math

import jax
import jax.numpy as jnp
from jax import lax
from jax.experimental import pallas as pl
from jax.experimental.pallas import tpu as pltpu

F32 = jnp.float32
BF16 = jnp.bfloat16

EPS = 1e-6
NEG = -1e30
LOG2E = math.log2(math.e)
LANES = 128
VMEM_LIMIT = 56 * 1024 * 1024

A_HEADS, A_QK, A_V = 4, 64, 128
B_HEADS, B_DIM = 4, 64
C_HEADS, C_DIM = 4, 64
IDX_HEADS, IDX_DIM = 8, 64
TOPK_MAX = 256
DILATED_PATTERNS = ((128, 1), (512, 4), (2048, 16))
ONES_ROWS = 16
DIL_BLK = 128
DIL_UNROLL = 4
REL_BUCKETS, REL_MAX_DIST = 32, 2048
MEM_HEADS = 4
CONV_WIDTH = 3

N_MAIN = 2816
N_B = 768
N_I = 256


def _cparams(sem):
    return pltpu.CompilerParams(dimension_semantics=sem, vmem_limit_bytes=VMEM_LIMIT)


def _dot(a, b):
    return jnp.dot(a, b, preferred_element_type=F32)


def _dot_nt(a, b):
    return lax.dot_general(a, b, (((1,), (1,)), ((), ())), preferred_element_type=F32)


def _ones_rows(n):
    row = lax.broadcasted_iota(jnp.int32, (ONES_ROWS, n), 0)
    return jnp.where(row == 0, 1.0, 0.0).astype(BF16)


def _rms(x, g):
    return x * lax.rsqrt(jnp.mean(x * x, axis=-1, keepdims=True) + EPS) * g


def _rel_bucket(n):
    max_exact = REL_BUCKETS // 2
    nf = jnp.maximum(n, 1).astype(F32)
    large = max_exact + (jnp.log(nf / max_exact) / math.log(REL_MAX_DIST / max_exact)
                         * (REL_BUCKETS - max_exact)).astype(jnp.int32)
    large = jnp.minimum(large, REL_BUCKETS - 1)
    return jnp.where(n < max_exact, n, large)


def _bias_tiles_kernel(par_ref, tab_ref, o_ref, *, tq, tk, scale):
    t = pl.program_id(0)
    head, off, mult = par_ref[t, 0], par_ref[t, 1], par_ref[t, 2]
    lo, hi, sgn = par_ref[t, 3], par_ref[t, 4], par_ref[t, 5]
    dist = off + sgn * (lax.broadcasted_iota(jnp.int32, (tq, tk), 0)
                        - lax.broadcasted_iota(jnp.int32, (tq, tk), 1))
    bucket = _rel_bucket(jnp.maximum(dist * mult, 0))
    val = jnp.zeros((tq, tk), F32)
    for b in range(REL_BUCKETS):
        val = jnp.where(bucket == b, tab_ref[b, head] * scale, val)
    o_ref[...] = jnp.where((dist >= lo) & (dist <= hi), val, NEG)


def _bias_tiles(params, table, tq, tk, scale=1.0):
    n = params.shape[0]
    return pl.pallas_call(
        functools.partial(_bias_tiles_kernel, tq=tq, tk=tk, scale=scale),
        grid=(n,),
        in_specs=[pl.BlockSpec(memory_space=pltpu.SMEM), pl.BlockSpec(memory_space=pltpu.SMEM)],
        out_specs=pl.BlockSpec((None, tq, tk), lambda t: (t, 0, 0)),
        out_shape=jax.ShapeDtypeStruct((n, tq, tk), F32),
        compiler_params=_cparams(("arbitrary",)),
        name="bias_tiles",
    )(params, table)


def _inproj_kernel(x_ref, g_ref, w_ref, om_ref, ob_ref, oi_ref):
    h = _rms(x_ref[...], g_ref[...]).astype(BF16)
    step = 512
    for c in range(0, N_MAIN, step):
        e = min(c + step, N_MAIN)
        om_ref[:, c:e] = _dot(h, w_ref[:, c:e]).astype(BF16)
    ob_ref[...] = _dot(h, w_ref[:, N_MAIN:N_MAIN + N_B])
    oi_ref[...] = _dot(h, w_ref[:, N_MAIN + N_B:])


def _inproj(x2, g, w, tm):
    T, D = x2.shape
    n_all = N_MAIN + N_B + N_I
    return pl.pallas_call(
        _inproj_kernel,
        grid=(T // tm,),
        in_specs=[pl.BlockSpec((tm, D), lambda i: (i, 0)),
                  pl.BlockSpec((1, D), lambda i: (0, 0)),
                  pl.BlockSpec((D, n_all), lambda i: (0, 0))],
        out_specs=[pl.BlockSpec((tm, N_MAIN), lambda i: (i, 0)),
                   pl.BlockSpec((tm, N_B), lambda i: (i, 0)),
                   pl.BlockSpec((tm, N_I), lambda i: (i, 0))],
        out_shape=[jax.ShapeDtypeStruct((T, N_MAIN), BF16),
                   jax.ShapeDtypeStruct((T, N_B), F32),
                   jax.ShapeDtypeStruct((T, N_I), F32)],
        compiler_params=_cparams(("parallel",)),
        name="inproj",
    )(x2, g, w)


def _diff_attn_kernel(lam_ref, q_ref, k_ref, vt_ref, bias_ref, g_ref, o_ref, s_sc, acc_sc,
                      *, tq, tk, d_min, d_const, lam_init):
    qi = pl.program_id(2)
    gran = min(tq, tk)
    last = (qi * tq + tq - 1) // tk
    masked_tile = d_const - d_min + 1
    lv = lam_ref[...]
    lam = (jnp.exp(jnp.sum(lv[0:1] * lv[1:2], axis=-1, keepdims=True))
           - jnp.exp(jnp.sum(lv[2:3] * lv[3:4], axis=-1, keepdims=True)) + lam_init)
    lane = lax.broadcasted_iota(jnp.int32, (tq, LANES), 1)
    q = q_ref[...].astype(F32) * (A_QK ** -0.5 * LOG2E)
    qcat = jnp.concatenate([jnp.where(lane < A_QK, q, 0.0), jnp.where(lane >= A_QK, q, 0.0)],
                           axis=0).astype(BF16)
    acc_sc[...] = jnp.zeros(acc_sc.shape, F32)

    def scores_into(slot, j):
        jc = jnp.minimum(j, last)
        k = k_ref[pl.ds(pl.multiple_of(jc * tk, tk), tk), :]
        d = qi * (tq // gran) - jc * (tk // gran)
        bias = bias_ref[jnp.where(j <= last, jnp.minimum(d, d_const) - d_min, masked_tile)]
        s_sc[slot] = _dot_nt(k, qcat) + jnp.concatenate([bias, bias], axis=1)

    def absorb(slot, j, m_prev):
        m_new = jnp.maximum(m_prev, jnp.max(s_sc[slot], axis=0, keepdims=True))
        alpha = jnp.exp2(m_prev - m_new)
        p = jnp.exp2(s_sc[slot] - m_new)
        vt = jnp.concatenate([vt_ref[jnp.minimum(j, last)], _ones_rows(tk)], axis=0)
        acc_sc[...] = alpha * acc_sc[...] + _dot(vt, p.astype(BF16))
        return m_new

    scores_into(0, 0)

    def body(jj, m):
        a = 2 * jj
        scores_into(1, a + 1)
        m = absorb(0, a, m)
        scores_into(0, a + 2)
        return absorb(1, a + 1, m)

    lax.fori_loop(0, (last + 2) // 2, body, jnp.full((1, 2 * tq), NEG, F32))
    inv = 1.0 / acc_sc[A_V:A_V + 1, :]
    o = acc_sc[:A_V, :tq] * inv[:, :tq] - lam * (acc_sc[:A_V, tq:] * inv[:, tq:])
    o = o * lax.rsqrt(jnp.mean(o * o, axis=0, keepdims=True) + EPS) * g_ref[...] * (1.0 - lam_init)
    o_ref[...] = o.T.astype(o_ref.dtype)


def _diff_attn(proj, vt, lamv, bias_a, subln_g, lam_init, tq, d_min, d_const):
    B, S, _ = proj.shape
    nd, tk = bias_a.shape[1], bias_a.shape[2]
    return pl.pallas_call(
        functools.partial(_diff_attn_kernel, tq=tq, tk=tk, d_min=d_min, d_const=d_const,
                          lam_init=lam_init),
        grid=(A_HEADS, B, S // tq),
        in_specs=[pl.BlockSpec((4, A_QK), lambda h, b, i: (0, 0)),
                  pl.BlockSpec((None, tq, LANES), lambda h, b, i: (b, i, h)),
                  pl.BlockSpec((None, S, LANES), lambda h, b, i: (b, 0, A_HEADS + h)),
                  pl.BlockSpec((None, None, S // tk, A_V, tk), lambda h, b, i: (b, h, 0, 0, 0)),
                  pl.BlockSpec((None, nd, tk, tq), lambda h, b, i: (h, 0, 0, 0)),
                  pl.BlockSpec((A_V, 1), lambda h, b, i: (0, 0))],
        out_specs=pl.BlockSpec((None, tq, LANES), lambda h, b, i: (b, i, h)),
        out_shape=jax.ShapeDtypeStruct((B, S, A_HEADS * A_V), BF16),
        scratch_shapes=[pltpu.VMEM((2, tk, 2 * tq), F32),
                        pltpu.VMEM((A_V + ONES_ROWS, 2 * tq), F32)],
        compiler_params=_cparams(("parallel", "parallel", "arbitrary")),
        name="diff_attn",
    )(lamv, proj, proj, vt, bias_a, subln_g)


class _LaneHalves:
    def __init__(self, *refs):
        self.refs = refs

    def __getitem__(self, idx):
        return jnp.concatenate([r[idx] for r in self.refs], axis=1)

    def __setitem__(self, idx, val):
        for i, r in enumerate(self.refs):
            r[idx] = val[:, i * LANES:(i + 1) * LANES]


def _dil_kernel(q0, q1, k0, k1, v0, v1, bias_ref, o_ref, n0, n1, m0, m1, l0, l1, *, seq, patterns):
    q_ref, k_ref, v_ref = _LaneHalves(q0, q1), _LaneHalves(k0, k1), _LaneHalves(v0, v1)
    n_sc, m_sc, l_sc = _LaneHalves(n0, n1), _LaneHalves(m0, m1), _LaneHalves(l0, l1)
    blk = DIL_BLK
    width = B_HEADS * B_DIM
    head_of_lane = lax.broadcasted_iota(jnp.int32, (blk, width), 1) // B_DIM

    def per_head(x):
        parts = [jnp.broadcast_to(x[h * blk:(h + 1) * blk], (blk, width)) for h in range(B_HEADS)]
        out = parts[-1]
        for h in range(B_HEADS - 2, -1, -1):
            out = jnp.where(head_of_lane == h, parts[h], out)
        return out

    def attend(base, r, with_prev, p_idx):
        q = q_ref[pl.ds(base, blk, stride=r), :] * (B_DIM ** -0.5 * LOG2E)
        qcat = jnp.concatenate([jnp.where(head_of_lane == h, q, 0.0) for h in range(B_HEADS)],
                               axis=0).astype(BF16)
        if with_prev:
            keys = pl.ds(base - blk * r, 2 * blk, stride=r)
            bias = bias_ref[p_idx]
        else:
            keys = pl.ds(base, blk, stride=r)
            bias = bias_ref[p_idx, :, blk:]
        s = _dot_nt(qcat, k_ref[keys, :].astype(BF16)) + bias
        mx = jnp.max(s, axis=-1, keepdims=True)
        p = jnp.exp2(s - mx)
        l = jnp.sum(p, axis=-1, keepdims=True)
        o = _dot(p.astype(BF16), v_ref[keys, :].astype(BF16))
        return per_head(o), per_head(mx), per_head(l)

    def merge(base, r, res, is_first):
        o, m, l = res
        rows = pl.ds(base, blk, stride=r)
        if is_first:
            n_sc[rows, :] = o
            m_sc[rows, :] = m
            l_sc[rows, :] = l
        else:
            m_old = m_sc[rows, :]
            m_new = jnp.maximum(m_old, m)
            a = jnp.exp2(m_old - m_new)
            b = jnp.exp2(m - m_new)
            n_sc[rows, :] = a * n_sc[rows, :] + b * o
            l_sc[rows, :] = a * l_sc[rows, :] + b * l
            m_sc[rows, :] = m_new

    def sweep(count, base_of, r, with_prev, p_idx):
        def trip(width):
            def body(i, carry):
                bases = [base_of(i * width + u) for u in range(width)]
                results = [attend(b, r, with_prev, p_idx) for b in bases]
                for b, res in zip(bases, results):
                    merge(b, r, res, p_idx == len(patterns) - 1)
                return carry
            return body

        full = count // DIL_UNROLL
        lax.fori_loop(0, full, trip(DIL_UNROLL), 0)
        rest = count - full * DIL_UNROLL
        if rest:
            lax.fori_loop(full * DIL_UNROLL, count, trip(1), 0)

    for p_idx, (_, r) in reversed(list(enumerate(patterns))):
        nb = seq // (r * blk)
        sweep(r, lambda c: c, r, False, p_idx)
        sweep(r * (nb - 1), lambda idx, r=r: (idx // r + 1) * blk * r + idx % r, r, True, p_idx)

    everything = (slice(None), slice(None))
    o_ref[...] = (n_sc[everything] / l_sc[everything]).astype(o_ref.dtype)


def _dil_attn(proj_b, bias_b):
    B, S, _ = proj_b.shape
    width = B_HEADS * B_DIM
    return pl.pallas_call(
        functools.partial(_dil_kernel, seq=S, patterns=DILATED_PATTERNS),
        grid=(B,),
        in_specs=[pl.BlockSpec((None, S, LANES), functools.partial(lambda i, b: (b, 0, i), i))
                  for i in range(3 * width // LANES)]
                 + [pl.BlockSpec(bias_b.shape, lambda b: (0, 0, 0))],
        out_specs=pl.BlockSpec((None, S, width), lambda b: (b, 0, 0)),
        out_shape=jax.ShapeDtypeStruct((B, S, width), BF16),
        scratch_shapes=[pltpu.VMEM((S, LANES), F32)] * (3 * width // LANES),
        compiler_params=_cparams(("parallel",)),
        name="dilated_attn",
    )(*([proj_b] * (3 * width // LANES)), bias_b)


SEL16 = jnp.bfloat16


def _key_float(k):
    return pltpu.bitcast(jnp.where(k >= 0, k, k ^ jnp.int32(0x7FFFFFFF)), F32)


def _high_half(f):
    return pltpu.bitcast(pltpu.bitcast(f, jnp.int32) & jnp.int32(-65536), F32)


def _dsa_kernel(iq_ref, ik_ref, iwt_ref, cq_ref, ck_ref, cvt_ref, bias_ref, o_ref, sc_ref, hi_ref, s_sc,
                acc_sc,
                *, tq, ch, ca, topk):
    qi = pl.program_id(1)
    q0 = qi * tq
    nch = (q0 + tq - 1) // ch + 1
    qidx = q0 + lax.broadcasted_iota(jnp.int32, (1, tq), 1)
    lane = lax.broadcasted_iota(jnp.int32, (tq, LANES), 1)
    first = lane < IDX_DIM

    w = iwt_ref[...] * (IDX_HEADS ** -0.5)
    qs = []
    for h in range(IDX_HEADS):
        tile = iq_ref[:, (h // 2) * LANES:(h // 2 + 1) * LANES].astype(F32) * (IDX_DIM ** -0.5)
        qs.append(jnp.where(first if h % 2 == 0 else ~first, tile, 0.0).astype(BF16))

    def score_body(j, carry):
        start = pl.multiple_of(j * ch, ch)
        kk = ik_ref[pl.ds(start, ch), :].astype(BF16)
        sc = jnp.zeros((ch, tq), F32)
        for h in range(IDX_HEADS):
            sc = sc + jnp.maximum(_dot_nt(kk, qs[h]), 0.0) * w[h:h + 1, :]
        kidx = start + lax.broadcasted_iota(jnp.int32, (ch, 1), 0)
        sc = jnp.where(kidx <= qidx, sc, NEG)
        sc_ref[j] = sc
        hi_ref[j] = _high_half(sc).astype(SEL16)
        return carry

    lax.fori_loop(0, nch, score_body, 0)

    rows = 32

    def count_where(pred):
        def body(j, acc):
            hit = jnp.where(pred(sc_ref[j]), 1.0, 0.0)
            return acc + jnp.sum(hit.reshape(ch // rows, rows, tq), axis=0)
        acc = lax.fori_loop(0, nch, body, jnp.zeros((rows, tq), F32))
        return jnp.sum(acc, axis=0, keepdims=True)

    def count_high_ge(v):
        one, zero16 = jnp.ones((), SEL16), jnp.zeros((), SEL16)
        def body(j, acc):
            hit = jnp.where(hi_ref[j] >= v, one, zero16).reshape(ch // rows, rows, tq)
            parts = [hit[i] for i in range(ch // rows)]
            while len(parts) > 1:
                parts = [parts[i] + parts[i + 1] for i in range(0, len(parts), 2)]
            return acc + parts[0]
        acc = lax.fori_loop(0, nch, body, jnp.zeros((rows, tq), SEL16))
        return jnp.sum(acc.astype(F32), axis=0, keepdims=True)

    kf = float(topk)
    n_valid = (qidx + 1).astype(F32)
    half_bits = 16

    def high_body(i, st):
        u, c_lo = st
        cand = u | jnp.left_shift(1, half_bits - 1 - i)
        v = _high_half(_key_float((cand - (1 << (half_bits - 1))) << half_bits)).astype(SEL16)
        c = count_high_ge(v)
        keep = c >= kf
        return jnp.where(keep, cand, u), jnp.where(keep, c, c_lo)

    u, c_lo = lax.fori_loop(0, half_bits, high_body, (jnp.zeros((1, tq), jnp.int32), n_valid))
    key_hi = (u - (1 << (half_bits - 1))) << half_bits

    def low_body(i, st):
        lo_bits, c_lo = st
        cand = lo_bits | jnp.left_shift(1, half_bits - 1 - i)
        c = count_where(lambda sc: sc >= _key_float(key_hi | cand))
        keep = c >= kf
        return jnp.where(keep, cand, lo_bits), jnp.where(keep, c, c_lo)

    lo_bits, c_lo = lax.fori_loop(0, half_bits, low_body, (jnp.zeros((1, tq), jnp.int32), c_lo))
    thr = jnp.where(n_valid > kf, _key_float(key_hi | lo_bits), 0.5 * NEG)

    tie_rows = jnp.logical_and(n_valid > kf, c_lo > kf)

    @pl.when(jnp.max(jnp.where(tie_rows, 1.0, 0.0)) > 0.0)
    def _():
        need = kf - count_where(lambda sc: sc > thr)
        lower = (lax.broadcasted_iota(jnp.int32, (ch, ch), 1)
                 <= lax.broadcasted_iota(jnp.int32, (ch, ch), 0)).astype(BF16)

        def drop_body(j, seen):
            sc = sc_ref[j]
            eq = sc == thr
            eqf = jnp.where(eq, 1.0, 0.0)
            rank = _dot(lower, eqf.astype(BF16)) + seen
            drop = jnp.logical_and(jnp.logical_and(eq, rank > need), tie_rows)
            sc_ref[j] = jnp.where(drop, NEG, sc)
            return seen + jnp.sum(eqf, axis=0, keepdims=True)

        lax.fori_loop(0, nch, drop_body, jnp.zeros((1, tq), F32))

    cw = C_HEADS * C_DIM
    lane_c = lax.broadcasted_iota(jnp.int32, (tq, cw), 1)
    cq = cq_ref[...].astype(F32) * (C_DIM ** -0.5 * LOG2E)
    qcat = jnp.concatenate([jnp.where(lane_c // C_DIM == h, cq, 0.0) for h in range(C_HEADS)],
                           axis=0).astype(BF16)
    acc_sc[...] = jnp.zeros(acc_sc.shape, F32)
    nd = bias_ref.shape[1]
    last = (q0 + tq - 1) // ca

    def scores_into(slot, j):
        jc = jnp.minimum(j, last)
        kc = ck_ref[pl.ds(pl.multiple_of(jc * ca, ca), ca), :]
        sub = pl.multiple_of((jc % (ch // ca)) * ca, ca)
        sel = sc_ref[jc // (ch // ca), pl.ds(sub, ca), :] >= jnp.where(j <= last, thr, -NEG)
        bias = jnp.concatenate([
            jnp.concatenate([
                bias_ref[h, jnp.clip(q0 // LANES + c - (jc * (ca // LANES) + u), 0, nd - 1)]
                for h in range(C_HEADS) for c in range(tq // LANES)], axis=1)
            for u in range(ca // LANES)], axis=0)
        s_sc[slot] = jnp.where(jnp.concatenate([sel] * C_HEADS, axis=1),
                               _dot_nt(kc, qcat) + bias, NEG)

    def absorb(slot, j, m_prev):
        m_new = jnp.maximum(m_prev, jnp.max(s_sc[slot], axis=0, keepdims=True))
        alpha = jnp.exp2(m_prev - m_new)
        pb = jnp.exp2(s_sc[slot] - m_new).astype(BF16)
        jc = jnp.minimum(j, last)
        for h in range(C_HEADS):
            cols = slice(h * tq, (h + 1) * tq)
            vt = jnp.concatenate([cvt_ref[jc, h * C_DIM:(h + 1) * C_DIM, :], _ones_rows(ca)], axis=0)
            acc_sc[h] = alpha[:, cols] * acc_sc[h] + _dot(vt, pb[:, cols])
        return m_new

    scores_into(0, 0)

    def attn_body(jj, m):
        a = 2 * jj
        scores_into(1, a + 1)
        m = absorb(0, a, m)
        scores_into(0, a + 2)
        return absorb(1, a + 1, m)

    lax.fori_loop(0, (last + 2) // 2, attn_body, jnp.full((1, C_HEADS * tq), NEG, F32))
    o_t = jnp.concatenate([acc_sc[h, :C_DIM] * (1.0 / acc_sc[h, C_DIM:C_DIM + 1])
                           for h in range(C_HEADS)], axis=0)
    o_ref[...] = o_t.T.astype(o_ref.dtype)


def _dsa_attn(proj, proj_i, iwt, cvt, bias_c, tq, ch):
    B, S, _ = proj.shape
    ca = cvt.shape[-1]
    topk = min(TOPK_MAX, S // 4)
    nd = bias_c.shape[1]
    cw = C_HEADS * C_DIM
    return pl.pallas_call(
        functools.partial(_dsa_kernel, tq=tq, ch=ch, ca=ca, topk=topk),
        grid=(B, S // tq),
        in_specs=[pl.BlockSpec((None, tq, IDX_HEADS * IDX_DIM), lambda b, i: (b, i, 3)),
                  pl.BlockSpec((None, S, LANES), lambda b, i: (b, 0, 0)),
                  pl.BlockSpec((None, IDX_HEADS, tq), lambda b, i: (b, 0, i)),
                  pl.BlockSpec((None, tq, cw), lambda b, i: (b, i, 8)),
                  pl.BlockSpec((None, S, cw), lambda b, i: (b, 0, 9)),
                  pl.BlockSpec((None, S // ca, cw, ca), lambda b, i: (b, 0, 0, 0)),
                  pl.BlockSpec((C_HEADS, nd, LANES, LANES), lambda b, i: (0, 0, 0, 0))],
        out_specs=pl.BlockSpec((None, tq, cw), lambda b, i: (b, i, 0)),
        out_shape=jax.ShapeDtypeStruct((B, S, cw), BF16),
        scratch_shapes=[pltpu.VMEM((S // ch, ch, tq), F32),
                        pltpu.VMEM((S // ch, ch, tq), SEL16),
                        pltpu.VMEM((2, ca, C_HEADS * tq), F32),
                        pltpu.VMEM((C_HEADS, C_DIM + ONES_ROWS, tq), F32)],
        compiler_params=_cparams(("parallel", "arbitrary")),
        name="dsa_attn",
    )(proj, proj_i, iwt, proj, proj, cvt, bias_c)


def _memkv_kernel(mem_ref, g_ref, w_ref, o_ref):
    h = _rms(mem_ref[...], g_ref[...]).astype(BF16)
    o_ref[...] = _dot(h, w_ref[...]).astype(o_ref.dtype)


def _memkv(mem, g, w):
    B, M, D = mem.shape
    L = w.shape[0]
    return pl.pallas_call(
        _memkv_kernel,
        grid=(L, B),
        in_specs=[pl.BlockSpec((None, M, D), lambda l, b: (b, 0, 0)),
                  pl.BlockSpec((None, 1, D), lambda l, b: (l, 0, 0)),
                  pl.BlockSpec((None, D, 2 * D), lambda l, b: (l, 0, 0))],
        out_specs=pl.BlockSpec((None, None, M, 2 * D), lambda l, b: (l, b, 0, 0)),
        out_shape=jax.ShapeDtypeStruct((L, B, M, 2 * D), BF16),
        compiler_params=_cparams(("parallel", "arbitrary")),
        name="mem_kv",
    )(mem, g, w)


def _outmem_kernel(x_ref, oa_ref, ob_ref, oc_ref, wo_ref, g_ref, wq_ref, kv_ref, wmo_ref, o_ref):
    D = x_ref.shape[-1]
    na, nb = oa_ref.shape[-1], ob_ref.shape[-1]
    x = (x_ref[...] + _dot(oa_ref[...], wo_ref[0:na]) + _dot(ob_ref[...], wo_ref[na:na + nb])
         + _dot(oc_ref[...], wo_ref[na + nb:]))
    h = _rms(x, g_ref[...]).astype(BF16)
    hd = D // MEM_HEADS
    q = (_dot(h, wq_ref[...]) * (hd ** -0.5)).astype(BF16)
    outs = []
    for hh in range(MEM_HEADS):
        k = kv_ref[:, hh * hd:(hh + 1) * hd]
        v = kv_ref[:, D + hh * hd:D + (hh + 1) * hd]
        s = _dot_nt(q[:, hh * hd:(hh + 1) * hd], k)
        p = jnp.exp(s - jnp.max(s, axis=-1, keepdims=True))
        p = p / jnp.sum(p, axis=-1, keepdims=True)
        outs.append(_dot(p.astype(BF16), v).astype(BF16))
    o = jnp.concatenate(outs, axis=-1)
    o_ref[...] = x + _dot(o, wmo_ref[...])


def _outmem(x, oa, ob, oc, wo, g, wq, kv, layer, wmo, tm):
    B, S, D = x.shape
    M = kv.shape[2]
    const = lambda b, i: (0, 0)
    return pl.pallas_call(
        _outmem_kernel,
        grid=(B, S // tm),
        in_specs=[pl.BlockSpec((None, tm, D), lambda b, i: (b, i, 0)),
                  pl.BlockSpec((None, tm, oa.shape[-1]), lambda b, i: (b, i, 0)),
                  pl.BlockSpec((None, tm, ob.shape[-1]), lambda b, i: (b, i, 0)),
                  pl.BlockSpec((None, tm, oc.shape[-1]), lambda b, i: (b, i, 0)),
                  pl.BlockSpec(wo.shape, const),
                  pl.BlockSpec((1, D), const),
                  pl.BlockSpec(wq.shape, const),
                  pl.BlockSpec((None, None, M, 2 * D), lambda b, i: (layer, b, 0, 0)),
                  pl.BlockSpec(wmo.shape, const)],
        out_specs=pl.BlockSpec((None, tm, D), lambda b, i: (b, i, 0)),
        out_shape=jax.ShapeDtypeStruct((B, S, D), F32),
        compiler_params=_cparams(("parallel", "arbitrary")),
        name="outproj_memattn",
    )(x, oa, ob, oc, wo, g, wq, kv, wmo)


HALO = 8


def _ffn_kernel(x_ref, xp_ref, g_ref, wg_ref, wv_ref, cwg_ref, cwv_ref, cbg_ref, cbv_ref, wd_ref,
                o_ref, *, tm, fc):
    i = pl.program_id(1)
    x = x_ref[...]
    g = g_ref[...]
    hp = _rms(xp_ref[...], g) * jnp.where(i > 0, 1.0, 0.0)
    h = jnp.concatenate([hp, _rms(x, g)], axis=0).astype(BF16)
    F = wd_ref.shape[0]

    def conv(u, cw_ref, cb_ref, c, e):
        out = cb_ref[:, c:e]
        for j in range(CONV_WIDTH):
            shift = CONV_WIDTH - 1 - j
            out = out + cw_ref[j:j + 1, c:e] * u[HALO - shift:HALO - shift + tm]
        return out

    acc = x
    for c in range(0, F, fc):
        e = min(c + fc, F)
        gate = conv(_dot(h, wg_ref[:, c:e]), cwg_ref, cbg_ref, c, e)
        val = conv(_dot(h, wv_ref[:, c:e]), cwv_ref, cbv_ref, c, e)
        act = (gate * jax.nn.sigmoid(gate) * val).astype(BF16)
        acc = acc + _dot(act, wd_ref[c:e, :])
    o_ref[...] = acc


def _ffn(x, g, wg, wv, cwg, cwv, cbg, cbv, wd, tm, fc):
    B, S, D = x.shape
    F = wd.shape[0]
    const = lambda b, i: (0, 0)
    hb = tm // HALO
    return pl.pallas_call(
        functools.partial(_ffn_kernel, tm=tm, fc=fc),
        grid=(B, S // tm),
        in_specs=[pl.BlockSpec((None, tm, D), lambda b, i: (b, i, 0)),
                  pl.BlockSpec((None, HALO, D), lambda b, i: (b, jnp.maximum(i * hb - 1, 0), 0)),
                  pl.BlockSpec((1, D), const),
                  pl.BlockSpec((D, F), const), pl.BlockSpec((D, F), lambda b, i: (0, 1)),
                  pl.BlockSpec((CONV_WIDTH, F), const), pl.BlockSpec((CONV_WIDTH, F), const),
                  pl.BlockSpec((1, F), const), pl.BlockSpec((1, F), const),
                  pl.BlockSpec((F, D), const)],
        out_specs=pl.BlockSpec((None, tm, D), lambda b, i: (b, i, 0)),
        out_shape=jax.ShapeDtypeStruct((B, S, D), F32),
        compiler_params=_cparams(("parallel", "arbitrary")),
        name="conv_ffn",
    )(x, x, g, wg, wv, cwg, cwv, cbg, cbv, wd)


def _final_norm_kernel(x_ref, g_ref, o_ref):
    o_ref[...] = _rms(x_ref[...], g_ref[...])


def _final_norm(x2, g, tm):
    T, D = x2.shape
    return pl.pallas_call(
        _final_norm_kernel,
        grid=(T // tm,),
        in_specs=[pl.BlockSpec((tm, D), lambda i: (i, 0)), pl.BlockSpec((1, D), lambda i: (0, 0))],
        out_specs=pl.BlockSpec((tm, D), lambda i: (i, 0)),
        out_shape=jax.ShapeDtypeStruct((T, D), F32),
        compiler_params=_cparams(("parallel",)),
        name="final_norm",
    )(x2, g)


def _tile_params(S, tq_a, tk_a):
    big = 1 << 30
    gran, (d_min, d_const) = min(tq_a, tk_a), _a_offsets(tq_a, tk_a)
    pa = []
    for h in range(A_HEADS):
        pa += [(h, d * gran, 1, 0, big, -1) for d in range(d_min, d_const + 1)]
        pa.append((h, 0, 1, 1, 0, -1))
    pb = []
    for _, r in DILATED_PATTERNS:
        for h in range(B_HEADS):
            pb.append((A_HEADS + h, DIL_BLK, r, 1, DIL_BLK, 1))
            pb.append((A_HEADS + h, 0, r, 0, DIL_BLK, 1))
    pc = [(A_HEADS + B_HEADS + h, d * LANES, 1, -big, big, -1)
          for h in range(C_HEADS) for d in range(_n_offsets(S, LANES))]
    to = lambda p: jnp.asarray(p, jnp.int32)
    return to(pa), to(pb), to(pc)


def _dil_bias_layout(tiles):
    t = tiles.reshape(len(DILATED_PATTERNS), B_HEADS, 2, DIL_BLK, DIL_BLK)
    return jnp.concatenate([t[:, :, 0], t[:, :, 1]], axis=-1).reshape(
        len(DILATED_PATTERNS), B_HEADS * DIL_BLK, 2 * DIL_BLK)


def _n_offsets(S, blk):
    return min(S // blk, REL_MAX_DIST // blk + 2)


def _a_offsets(tq, tk):
    gran = min(tq, tk)
    return -(tq // gran - 1), -(-(REL_MAX_DIST - 1 + tk) // gran)


def _in_weights(w_in_l):
    sizes = (512, 512, 512, 256, 256, 256, 256, 256, 256, 512, 64, 8)
    offs = [0]
    for s in sizes:
        offs.append(offs[-1] + s)
    (aq, ak, av, bq, bk, bv, cq, ck, cv, iq, ik, iw) = [w_in_l[:, offs[i]:offs[i + 1]] for i in range(12)]
    pad = jnp.zeros((w_in_l.shape[0], N_I - 2 * IDX_DIM - IDX_HEADS), w_in_l.dtype)
    return jnp.concatenate([aq, ak, av, iq, cq, ck, cv, bq, bk, bv, ik, ik, iw, pad], axis=1).astype(BF16)


def _forward(x, mem, rel_bias, norm_mix, w_in, lam_q1, lam_k1, lam_q2, lam_k2, subln, w_out,
             norm_mem, norm_memkv, w_mq, w_mkv, w_mo, norm_ffn, w_up, conv_w, conv_b, w_down,
             norm_final, *, tq_a, tk_a, tq_c, ch_c, ca_c, tm_proj, tm_mem, tm_ffn, fc):
    B, S, D = x.shape
    L = w_in.shape[0]
    F = w_down.shape[1]
    pa, pb, pc = _tile_params(S, tq_a, tk_a)
    bias_a = _bias_tiles(pa, rel_bias, tk_a, tq_a, LOG2E).reshape(A_HEADS, -1, tk_a, tq_a)
    d_min, d_const = _a_offsets(tq_a, tk_a)
    bias_b = _dil_bias_layout(_bias_tiles(pb, rel_bias, DIL_BLK, DIL_BLK, LOG2E))
    bias_c = _bias_tiles(pc, rel_bias, LANES, LANES, LOG2E).reshape(C_HEADS, -1, LANES, LANES)
    kv_all = _memkv(mem, norm_memkv.reshape(L, 1, D), w_mkv.astype(BF16))

    for l in range(L):
        lam_init = 0.8 - 0.6 * math.exp(-0.3 * l)
        proj, proj_b, proj_i = _inproj(x.reshape(B * S, D), norm_mix[l].reshape(1, D),
                                       _in_weights(w_in[l]), tm_proj)
        proj = proj.reshape(B, S, N_MAIN)
        lamv = jnp.stack([lam_q1[l], lam_k1[l], lam_q2[l], lam_k2[l]], axis=0)
        avt = proj[:, :, 2 * A_HEADS * LANES:3 * A_HEADS * LANES].reshape(
            B, S // tk_a, tk_a, A_HEADS, A_V).transpose(0, 3, 1, 4, 2)
        cw = C_HEADS * C_DIM
        cvt = proj[:, :, N_MAIN - cw:].reshape(B, S // ca_c, ca_c, cw).transpose(0, 1, 3, 2)
        proj_i = proj_i.reshape(B, S, N_I)
        iwt = proj_i[:, :, 2 * IDX_DIM:2 * IDX_DIM + IDX_HEADS].transpose(0, 2, 1)
        o_a = _diff_attn(proj, avt, lamv, bias_a, subln[l].reshape(A_V, 1), lam_init, tq_a,
                         d_min, d_const)
        o_b = _dil_attn(proj_b.reshape(B, S, N_B), bias_b)
        o_c = _dsa_attn(proj, proj_i, iwt, cvt, bias_c, tq_c, ch_c)
        x = _outmem(x, o_a, o_b, o_c, w_out[l].astype(BF16), norm_mem[l].reshape(1, D),
                    w_mq[l].astype(BF16), kv_all, l, w_mo[l].astype(BF16), tm_mem)
        wu = w_up[l].astype(BF16)
        x = _ffn(x, norm_ffn[l].reshape(1, D), wu, wu, conv_w[l][:, :F], conv_w[l][:, F:],
                 conv_b[l][:F].reshape(1, F), conv_b[l][F:].reshape(1, F), w_down[l].astype(BF16),
                 tm_ffn, fc)
    return _final_norm(x.reshape(B * S, D), norm_final.reshape(1, D), tm_proj).reshape(B, S, D)


def kernel(x, mem, rel_bias, norm_mix, w_in, lam_q1, lam_k1, lam_q2, lam_k2, subln, w_out,
           norm_mem, norm_memkv, w_mq, w_mkv, w_mo, norm_ffn, w_up, conv_w, conv_b, w_down,
           norm_final):
    return _forward(x, mem, rel_bias, norm_mix, w_in, lam_q1, lam_k1, lam_q2, lam_k2, subln, w_out,
                    norm_mem, norm_memkv, w_mq, w_mkv, w_mo, norm_ffn, w_up, conv_w, conv_b, w_down,
                    norm_final, tq_a=512, tk_a=256, tq_c=512, ch_c=512, ca_c=256, tm_proj=512, tm_mem=512, tm_ffn=512,
                    fc=512)
```

```python
import functools
import math

import jax
import jax.numpy as jnp
from jax import lax
from jax.experimental import pallas as pl
from jax.experimental.pallas import tpu as pltpu

F32 = jnp.float32
BF16 = jnp.bfloat16

EPS = 1e-6
NEG = -1e30
LOG2E = math.log2(math.e)
LANES = 128
VMEM_LIMIT = 56 * 1024 * 1024

A_HEADS, A_QK, A_V = 4, 64, 128
B_HEADS, B_DIM = 4, 64
C_HEADS, C_DIM = 4, 64
IDX_HEADS, IDX_DIM = 8, 64
TOPK_MAX = 256
DILATED_PATTERNS = ((128, 1), (512, 4), (2048, 16))
ONES_ROWS = 16
DIL_BLK = 128
DIL_UNROLL = 4
REL_BUCKETS, REL_MAX_DIST = 32, 2048
MEM_HEADS = 4
CONV_WIDTH = 3

N_MAIN = 2816
N_B = 768
N_I = 256


def _cparams(sem):
    return pltpu.CompilerParams(dimension_semantics=sem, vmem_limit_bytes=VMEM_LIMIT)


def _dot(a, b):
    return jnp.dot(a, b, preferred_element_type=F32)


def _dot_nt(a, b):
    return lax.dot_general(a, b, (((1,), (1,)), ((), ())), preferred_element_type=F32)


def _ones_rows(n):
    row = lax.broadcasted_iota(jnp.int32, (ONES_ROWS, n), 0)
    return jnp.where(row == 0, 1.0, 0.0).astype(BF16)


def _rms(x, g):
    return x * lax.rsqrt(jnp.mean(x * x, axis=-1, keepdims=True) + EPS) * g


def _rel_bucket(n):
    max_exact = REL_BUCKETS // 2
    nf = jnp.maximum(n, 1).astype(F32)
    large = max_exact + (jnp.log(nf / max_exact) / math.log(REL_MAX_DIST / max_exact)
                         * (REL_BUCKETS - max_exact)).astype(jnp.int32)
    large = jnp.minimum(large, REL_BUCKETS - 1)
    return jnp.where(n < max_exact, n, large)


def _bias_tiles_kernel(par_ref, tab_ref, o_ref, *, tq, tk, scale):
    t = pl.program_id(0)
    head, off, mult = par_ref[t, 0], par_ref[t, 1], par_ref[t, 2]
    lo, hi, sgn = par_ref[t, 3], par_ref[t, 4], par_ref[t, 5]
    dist = off + sgn * (lax.broadcasted_iota(jnp.int32, (tq, tk), 0)
                        - lax.broadcasted_iota(jnp.int32, (tq, tk), 1))
    bucket = _rel_bucket(jnp.maximum(dist * mult, 0))
    val = jnp.zeros((tq, tk), F32)
    for b in range(REL_BUCKETS):
        val = jnp.where(bucket == b, tab_ref[b, head] * scale, val)
    o_ref[...] = jnp.where((dist >= lo) & (dist <= hi), val, NEG)


def _bias_tiles(params, table, tq, tk, scale=1.0):
    n = params.shape[0]
    return pl.pallas_call(
        functools.partial(_bias_tiles_kernel, tq=tq, tk=tk, scale=scale),
        grid=(n,),
        in_specs=[pl.BlockSpec(memory_space=pltpu.SMEM), pl.BlockSpec(memory_space=pltpu.SMEM)],
        out_specs=pl.BlockSpec((None, tq, tk), lambda t: (t, 0, 0)),
        out_shape=jax.ShapeDtypeStruct((n, tq, tk), F32),
        compiler_params=_cparams(("arbitrary",)),
        name="bias_tiles",
    )(params, table)


def _inproj_kernel(x_ref, g_ref, w_ref, om_ref, ob_ref, oi_ref):
    h = _rms(x_ref[...], g_ref[...]).astype(BF16)
    step = 512
    for c in range(0, N_MAIN, step):
        e = min(c + step, N_MAIN)
        om_ref[:, c:e] = _dot(h, w_ref[:, c:e]).astype(BF16)
    ob_ref[...] = _dot(h, w_ref[:, N_MAIN:N_MAIN + N_B])
    oi_ref[...] = _dot(h, w_ref[:, N_MAIN + N_B:])


def _inproj(x2, g, w, tm):
    T, D = x2.shape
    n_all = N_MAIN + N_B + N_I
    return pl.pallas_call(
        _inproj_kernel,
        grid=(T // tm,),
        in_specs=[pl.BlockSpec((tm, D), lambda i: (i, 0)),
                  pl.BlockSpec((1, D), lambda i: (0, 0)),
                  pl.BlockSpec((D, n_all), lambda i: (0, 0))],
        out_specs=[pl.BlockSpec((tm, N_MAIN), lambda i: (i, 0)),
                   pl.BlockSpec((tm, N_B), lambda i: (i, 0)),
                   pl.BlockSpec((tm, N_I), lambda i: (i, 0))],
        out_shape=[jax.ShapeDtypeStruct((T, N_MAIN), BF16),
                   jax.ShapeDtypeStruct((T, N_B), F32),
                   jax.ShapeDtypeStruct((T, N_I), F32)],
        compiler_params=_cparams(("parallel",)),
        name="inproj",
    )(x2, g, w)


def _diff_attn_kernel(lam_ref, q_ref, k_ref, vt_ref, bias_ref, g_ref, o_ref, s_sc, acc_sc,
                      *, tq, tk, d_min, d_const, lam_init):
    qi = pl.program_id(2)
    gran = min(tq, tk)
    last = (qi * tq + tq - 1) // tk
    masked_tile = d_const - d_min + 1
    lv = lam_ref[...]
    lam = (jnp.exp(jnp.sum(lv[0:1] * lv[1:2], axis=-1, keepdims=True))
           - jnp.exp(jnp.sum(lv[2:3] * lv[3:4], axis=-1, keepdims=True)) + lam_init)
    lane = lax.broadcasted_iota(jnp.int32, (tq, LANES), 1)
    q = q_ref[...].astype(F32) * (A_QK ** -0.5 * LOG2E)
    qcat = jnp.concatenate([jnp.where(lane < A_QK, q, 0.0), jnp.where(lane >= A_QK, q, 0.0)],
                           axis=0).astype(BF16)
    acc_sc[...] = jnp.zeros(acc_sc.shape, F32)

    def scores_into(slot, j):
        jc = jnp.minimum(j, last)
        k = k_ref[pl.ds(pl.multiple_of(jc * tk, tk), tk), :]
        d = qi * (tq // gran) - jc * (tk // gran)
        bias = bias_ref[jnp.where(j <= last, jnp.minimum(d, d_const) - d_min, masked_tile)]
        s_sc[slot] = _dot_nt(k, qcat) + jnp.concatenate([bias, bias], axis=1)

    def absorb(slot, j, m_prev):
        m_new = jnp.maximum(m_prev, jnp.max(s_sc[slot], axis=0, keepdims=True))
        alpha = jnp.exp2(m_prev - m_new)
        p = jnp.exp2(s_sc[slot] - m_new)
        vt = jnp.concatenate([vt_ref[jnp.minimum(j, last)], _ones_rows(tk)], axis=0)
        acc_sc[...] = alpha * acc_sc[...] + _dot(vt, p.astype(BF16))
        return m_new

    scores_into(0, 0)

    def body(jj, m):
        a = 2 * jj
        scores_into(1, a + 1)
        m = absorb(0, a, m)
        scores_into(0, a + 2)
        return absorb(1, a + 1, m)

    lax.fori_loop(0, (last + 2) // 2, body, jnp.full((1, 2 * tq), NEG, F32))
    inv = 1.0 / acc_sc[A_V:A_V + 1, :]
    o = acc_sc[:A_V, :tq] * inv[:, :tq] - lam * (acc_sc[:A_V, tq:] * inv[:, tq:])
    o = o * lax.rsqrt(jnp.mean(o * o, axis=0, keepdims=True) + EPS) * g_ref[...] * (1.0 - lam_init)
    o_ref[...] = o.T.astype(o_ref.dtype)


def _diff_attn(proj, vt, lamv, bias_a, subln_g, lam_init, tq, d_min, d_const):
    B, S, _ = proj.shape
    nd, tk = bias_a.shape[1], bias_a.shape[2]
    return pl.pallas_call(
        functools.partial(_diff_attn_kernel, tq=tq, tk=tk, d_min=d_min, d_const=d_const,
                          lam_init=lam_init),
        grid=(A_HEADS, B, S // tq),
        in_specs=[pl.BlockSpec((4, A_QK), lambda h, b, i: (0, 0)),
                  pl.BlockSpec((None, tq, LANES), lambda h, b, i: (b, i, h)),
                  pl.BlockSpec((None, S, LANES), lambda h, b, i: (b, 0, A_HEADS + h)),
                  pl.BlockSpec((None, None, S // tk, A_V, tk), lambda h, b, i: (b, h, 0, 0, 0)),
                  pl.BlockSpec((None, nd, tk, tq), lambda h, b, i: (h, 0, 0, 0)),
                  pl.BlockSpec((A_V, 1), lambda h, b, i: (0, 0))],
        out_specs=pl.BlockSpec((None, tq, LANES), lambda h, b, i: (b, i, h)),
        out_shape=jax.ShapeDtypeStruct((B, S, A_HEADS * A_V), BF16),
        scratch_shapes=[pltpu.VMEM((2, tk, 2 * tq), F32),
                        pltpu.VMEM((A_V + ONES_ROWS, 2 * tq), F32)],
        compiler_params=_cparams(("parallel", "parallel", "arbitrary")),
        name="diff_attn",
    )(lamv, proj, proj, vt, bias_a, subln_g)


class _LaneHalves:
    def __init__(self, *refs):
        self.refs = refs

    def __getitem__(self, idx):
        return jnp.concatenate([r[idx] for r in self.refs], axis=1)

    def __setitem__(self, idx, val):
        for i, r in enumerate(self.refs):
            r[idx] = val[:, i * LANES:(i + 1) * LANES]


def _dil_kernel(q0, q1, k0, k1, v0, v1, bias_ref, o_ref, n0, n1, m0, m1, l0, l1, *, seq, patterns):
    q_ref, k_ref, v_ref = _LaneHalves(q0, q1), _LaneHalves(k0, k1), _LaneHalves(v0, v1)
    n_sc, m_sc, l_sc = _LaneHalves(n0, n1), _LaneHalves(m0, m1), _LaneHalves(l0, l1)
    blk = DIL_BLK
    width = B_HEADS * B_DIM
    head_of_lane = lax.broadcasted_iota(jnp.int32, (blk, width), 1) // B_DIM

    def per_head(x):
        parts = [jnp.broadcast_to(x[h * blk:(h + 1) * blk], (blk, width)) for h in range(B_HEADS)]
        out = parts[-1]
        for h in range(B_HEADS - 2, -1, -1):
            out = jnp.where(head_of_lane == h, parts[h], out)
        return out

    def attend(base, r, with_prev, p_idx):
        q = q_ref[pl.ds(base, blk, stride=r), :] * (B_DIM ** -0.5 * LOG2E)
        qcat = jnp.concatenate([jnp.where(head_of_lane == h, q, 0.0) for h in range(B_HEADS)],
                               axis=0).astype(BF16)
        if with_prev:
            keys = pl.ds(base - blk * r, 2 * blk, stride=r)
            bias = bias_ref[p_idx]
        else:
            keys = pl.ds(base, blk, stride=r)
            bias = bias_ref[p_idx, :, blk:]
        s = _dot_nt(qcat, k_ref[keys, :].astype(BF16)) + bias
        mx = jnp.max(s, axis=-1, keepdims=True)
        p = jnp.exp2(s - mx)
        l = jnp.sum(p, axis=-1, keepdims=True)
        o = _dot(p.astype(BF16), v_ref[keys, :].astype(BF16))
        return per_head(o), per_head(mx), per_head(l)

    def merge(base, r, res, is_first):
        o, m, l = res
        rows = pl.ds(base, blk, stride=r)
        if is_first:
            n_sc[rows, :] = o
            m_sc[rows, :] = m
            l_sc[rows, :] = l
        else:
            m_old = m_sc[rows, :]
            m_new = jnp.maximum(m_old, m)
            a = jnp.exp2(m_old - m_new)
            b = jnp.exp2(m - m_new)
            n_sc[rows, :] = a * n_sc[rows, :] + b * o
            l_sc[rows, :] = a * l_sc[rows, :] + b * l
            m_sc[rows, :] = m_new

    def sweep(count, base_of, r, with_prev, p_idx):
        def trip(width):
            def body(i, carry):
                bases = [base_of(i * width + u) for u in range(width)]
                results = [attend(b, r, with_prev, p_idx) for b in bases]
                for b, res in zip(bases, results):
                    merge(b, r, res, p_idx == len(patterns) - 1)
                return carry
            return body

        full = count // DIL_UNROLL
        lax.fori_loop(0, full, trip(DIL_UNROLL), 0)
        rest = count - full * DIL_UNROLL
        if rest:
            lax.fori_loop(full * DIL_UNROLL, count, trip(1), 0)

    for p_idx, (_, r) in reversed(list(enumerate(patterns))):
        nb = seq // (r * blk)
        sweep(r, lambda c: c, r, False, p_idx)
        sweep(r * (nb - 1), lambda idx, r=r: (idx // r + 1) * blk * r + idx % r, r, True, p_idx)

    everything = (slice(None), slice(None))
    o_ref[...] = (n_sc[everything] / l_sc[everything]).astype(o_ref.dtype)


def _dil_attn(proj_b, bias_b):
    B, S, _ = proj_b.shape
    width = B_HEADS * B_DIM
    return pl.pallas_call(
        functools.partial(_dil_kernel, seq=S, patterns=DILATED_PATTERNS),
        grid=(B,),
        in_specs=[pl.BlockSpec((None, S, LANES), functools.partial(lambda i, b: (b, 0, i), i))
                  for i in range(3 * width // LANES)]
                 + [pl.BlockSpec(bias_b.shape, lambda b: (0, 0, 0))],
        out_specs=pl.BlockSpec((None, S, width), lambda b: (b, 0, 0)),
        out_shape=jax.ShapeDtypeStruct((B, S, width), BF16),
        scratch_shapes=[pltpu.VMEM((S, LANES), F32)] * (3 * width // LANES),
        compiler_params=_cparams(("parallel",)),
        name="dilated_attn",
    )(*([proj_b] * (3 * width // LANES)), bias_b)


SEL16 = jnp.bfloat16


RADIX_LOW_BITS = 5
PEEL_CAP = 6


def _float_key(f):
    b = pltpu.bitcast(f, jnp.int32)
    return jnp.where(b >= 0, b, b ^ jnp.int32(0x7FFFFFFF))


def _key_float(k):
    return pltpu.bitcast(jnp.where(k >= 0, k, k ^ jnp.int32(0x7FFFFFFF)), F32)


def _high_half(f):
    return pltpu.bitcast(pltpu.bitcast(f, jnp.int32) & jnp.int32(-65536), F32)


def _dsa_kernel(iq_ref, ik_ref, iwt_ref, cq_ref, ck_ref, cvt_ref, bias_ref, o_ref, sc_ref, hi_ref, s_sc,
                acc_sc,
                *, tq, ch, ca, topk):
    qi = pl.program_id(1)
    q0 = qi * tq
    nch = (q0 + tq - 1) // ch + 1
    qidx = q0 + lax.broadcasted_iota(jnp.int32, (1, tq), 1)
    lane = lax.broadcasted_iota(jnp.int32, (tq, LANES), 1)
    first = lane < IDX_DIM

    w = iwt_ref[...] * (IDX_HEADS ** -0.5)
    qs = []
    for h in range(IDX_HEADS):
        tile = iq_ref[:, (h // 2) * LANES:(h // 2 + 1) * LANES].astype(F32) * (IDX_DIM ** -0.5)
        qs.append(jnp.where(first if h % 2 == 0 else ~first, tile, 0.0).astype(BF16))

    def score_body(j, carry):
        start = pl.multiple_of(j * ch, ch)
        kk = ik_ref[pl.ds(start, ch), :].astype(BF16)
        sc = jnp.zeros((ch, tq), F32)
        for h in range(IDX_HEADS):
            sc = sc + jnp.maximum(_dot_nt(kk, qs[h]), 0.0) * w[h:h + 1, :]
        kidx = start + lax.broadcasted_iota(jnp.int32, (ch, 1), 0)
        sc = jnp.where(kidx <= qidx, sc, NEG)
        sc_ref[j] = sc
        hi_ref[j] = _high_half(sc).astype(SEL16)
        return carry

    lax.fori_loop(0, nch, score_body, 0)

    rows = 32

    def count_where(pred):
        def body(j, acc):
            hit = jnp.where(pred(sc_ref[j]), 1.0, 0.0)
            return acc + jnp.sum(hit.reshape(ch // rows, rows, tq), axis=0)
        acc = lax.fori_loop(0, nch, body, jnp.zeros((rows, tq), F32))
        return jnp.sum(acc, axis=0, keepdims=True)

    def count_high_ge(v):
        one, zero16 = jnp.ones((), SEL16), jnp.zeros((), SEL16)
        def body(j, acc):
            hit = jnp.where(hi_ref[j] >= v, one, zero16).reshape(ch // rows, rows, tq)
            parts = [hit[i] for i in range(ch // rows)]
            while len(parts) > 1:
                parts = [parts[i] + parts[i + 1] for i in range(0, len(parts), 2)]
            return acc + parts[0]
        acc = lax.fori_loop(0, nch, body, jnp.zeros((rows, tq), SEL16))
        return jnp.sum(acc.astype(F32), axis=0, keepdims=True)

    kf = float(topk)
    n_valid = (qidx + 1).astype(F32)
    half_bits = 16

    def high_body(i, st):
        u, c_lo = st
        cand = u | jnp.left_shift(1, half_bits - 1 - i)
        v = _high_half(_key_float((cand - (1 << (half_bits - 1))) << half_bits)).astype(SEL16)
        c = count_high_ge(v)
        keep = c >= kf
        return jnp.where(keep, cand, u), jnp.where(keep, c, c_lo)

    u, c_lo = lax.fori_loop(0, half_bits, high_body, (jnp.zeros((1, tq), jnp.int32), n_valid))
    key_hi = (u - (1 << (half_bits - 1))) << half_bits

    def low_body(i, st):
        lo_bits, c_lo = st
        cand = lo_bits | jnp.left_shift(1, half_bits - 1 - i)
        c = count_where(lambda sc: sc >= _key_float(key_hi | cand))
        keep = c >= kf
        return jnp.where(keep, cand, lo_bits), jnp.where(keep, c, c_lo)

    lo_bits, c_lo = lax.fori_loop(0, RADIX_LOW_BITS, low_body, (jnp.zeros((1, tq), jnp.int32), c_lo))

    def min_ge(t):
        def body(j, acc):
            sc = sc_ref[j]
            x = jnp.where(sc >= t, sc, -NEG)
            return jnp.minimum(acc, jnp.min(x.reshape(ch // rows, rows, tq), axis=0))
        acc = lax.fori_loop(0, nch, body, jnp.full((rows, tq), -NEG, F32))
        return jnp.min(acc, axis=0, keepdims=True)

    def open_rows(c_lo, done):
        return jnp.logical_and(jnp.logical_and(n_valid > kf, c_lo > kf), done == 0.0)

    def any_row(mask):
        return jnp.max(jnp.where(mask, 1.0, 0.0)) > 0.0

    def peel_cond(st):
        _, c_lo, done, it = st
        return jnp.logical_and(it < PEEL_CAP, any_row(open_rows(c_lo, done)))

    def peel_body(st):
        key, c_lo, done, it = st
        is_open = open_rows(c_lo, done)
        smallest = min_ge(_key_float(key))
        cand = _float_key(smallest) + 1
        c = count_where(lambda sc: sc >= _key_float(cand))
        enough = jnp.logical_and(is_open, c >= kf)
        at_tie = jnp.logical_and(is_open, c < kf)
        key = jnp.where(enough, cand, jnp.where(at_tie, _float_key(smallest), key))
        return key, jnp.where(enough, c, c_lo), jnp.where(at_tie, 1.0, done), it + 1

    key, c_peel, done, _ = lax.while_loop(
        peel_cond, peel_body, (key_hi | lo_bits, c_lo, jnp.zeros((1, tq), F32), jnp.int32(0)))

    def finish_by_radix(_):
        bits, c = lax.fori_loop(RADIX_LOW_BITS, half_bits, low_body, (lo_bits, c_lo))
        return key_hi | bits, c

    key, c_lo = lax.cond(any_row(open_rows(c_peel, done)), finish_by_radix,
                         lambda _: (key, c_peel), 0)
    thr = jnp.where(n_valid > kf, _key_float(key), 0.5 * NEG)

    tie_rows = jnp.logical_and(n_valid > kf, c_lo > kf)

    @pl.when(jnp.max(jnp.where(tie_rows, 1.0, 0.0)) > 0.0)
    def _():
        need = kf - count_where(lambda sc: sc > thr)
        lower = (lax.broadcasted_iota(jnp.int32, (ch, ch), 1)
                 <= lax.broadcasted_iota(jnp.int32, (ch, ch), 0)).astype(BF16)

        def drop_body(j, seen):
            sc = sc_ref[j]
            eq = sc == thr
            eqf = jnp.where(eq, 1.0, 0.0)
            rank = _dot(lower, eqf.astype(BF16)) + seen
            drop = jnp.logical_and(jnp.logical_and(eq, rank > need), tie_rows)
            sc_ref[j] = jnp.where(drop, NEG, sc)
            return seen + jnp.sum(eqf, axis=0, keepdims=True)

        lax.fori_loop(0, nch, drop_body, jnp.zeros((1, tq), F32))

    cw = C_HEADS * C_DIM
    lane_c = lax.broadcasted_iota(jnp.int32, (tq, cw), 1)
    cq = cq_ref[...].astype(F32) * (C_DIM ** -0.5 * LOG2E)
    qcat = jnp.concatenate([jnp.where(lane_c // C_DIM == h, cq, 0.0) for h in range(C_HEADS)],
                           axis=0).astype(BF16)
    acc_sc[...] = jnp.zeros(acc_sc.shape, F32)
    nd = bias_ref.shape[1]
    last = (q0 + tq - 1) // ca

    def scores_into(slot, j):
        jc = jnp.minimum(j, last)
        kc = ck_ref[pl.ds(pl.multiple_of(jc * ca, ca), ca), :]
        sub = pl.multiple_of((jc % (ch // ca)) * ca, ca)
        sel = sc_ref[jc // (ch // ca), pl.ds(sub, ca), :] >= jnp.where(j <= last, thr, -NEG)
        bias = jnp.concatenate([
            jnp.concatenate([
                bias_ref[h, jnp.clip(q0 // LANES + c - (jc * (ca // LANES) + u), 0, nd - 1)]
                for h in range(C_HEADS) for c in range(tq // LANES)], axis=1)
            for u in range(ca // LANES)], axis=0)
        s_sc[slot] = jnp.where(jnp.concatenate([sel] * C_HEADS, axis=1),
                               _dot_nt(kc, qcat) + bias, NEG)

    def absorb(slot, j, m_prev):
        m_new = jnp.maximum(m_prev, jnp.max(s_sc[slot], axis=0, keepdims=True))
        alpha = jnp.exp2(m_prev - m_new)
        pb = jnp.exp2(s_sc[slot] - m_new).astype(BF16)
        jc = jnp.minimum(j, last)
        for h in range(C_HEADS):
            cols = slice(h * tq, (h + 1) * tq)
            vt = jnp.concatenate([cvt_ref[jc, h * C_DIM:(h + 1) * C_DIM, :], _ones_rows(ca)], axis=0)
            acc_sc[h] = alpha[:, cols] * acc_sc[h] + _dot(vt, pb[:, cols])
        return m_new

    scores_into(0, 0)

    def attn_body(jj, m):
        a = 2 * jj
        scores_into(1, a + 1)
        m = absorb(0, a, m)
        scores_into(0, a + 2)
        return absorb(1, a + 1, m)

    lax.fori_loop(0, (last + 2) // 2, attn_body, jnp.full((1, C_HEADS * tq), NEG, F32))
    o_t = jnp.concatenate([acc_sc[h, :C_DIM] * (1.0 / acc_sc[h, C_DIM:C_DIM + 1])
                           for h in range(C_HEADS)], axis=0)
    o_ref[...] = o_t.T.astype(o_ref.dtype)


def _dsa_attn(proj, proj_i, iwt, cvt, bias_c, tq, ch):
    B, S, _ = proj.shape
    ca = cvt.shape[-1]
    topk = min(TOPK_MAX, S // 4)
    nd = bias_c.shape[1]
    cw = C_HEADS * C_DIM
    return pl.pallas_call(
        functools.partial(_dsa_kernel, tq=tq, ch=ch, ca=ca, topk=topk),
        grid=(B, S // tq),
        in_specs=[pl.BlockSpec((None, tq, IDX_HEADS * IDX_DIM), lambda b, i: (b, i, 3)),
                  pl.BlockSpec((None, S, LANES), lambda b, i: (b, 0, 0)),
                  pl.BlockSpec((None, IDX_HEADS, tq), lambda b, i: (b, 0, i)),
                  pl.BlockSpec((None, tq, cw), lambda b, i: (b, i, 8)),
                  pl.BlockSpec((None, S, cw), lambda b, i: (b, 0, 9)),
                  pl.BlockSpec((None, S // ca, cw, ca), lambda b, i: (b, 0, 0, 0)),
                  pl.BlockSpec((C_HEADS, nd, LANES, LANES), lambda b, i: (0, 0, 0, 0))],
        out_specs=pl.BlockSpec((None, tq, cw), lambda b, i: (b, i, 0)),
        out_shape=jax.ShapeDtypeStruct((B, S, cw), BF16),
        scratch_shapes=[pltpu.VMEM((S // ch, ch, tq), F32),
                        pltpu.VMEM((S // ch, ch, tq), SEL16),
                        pltpu.VMEM((2, ca, C_HEADS * tq), F32),
                        pltpu.VMEM((C_HEADS, C_DIM + ONES_ROWS, tq), F32)],
        compiler_params=_cparams(("parallel", "arbitrary")),
        name="dsa_attn",
    )(proj, proj_i, iwt, proj, proj, cvt, bias_c)


def _memkv_kernel(mem_ref, g_ref, w_ref, o_ref):
    h = _rms(mem_ref[...], g_ref[...]).astype(BF16)
    o_ref[...] = _dot(h, w_ref[...]).astype(o_ref.dtype)


def _memkv(mem, g, w):
    B, M, D = mem.shape
    L = w.shape[0]
    return pl.pallas_call(
        _memkv_kernel,
        grid=(L, B),
        in_specs=[pl.BlockSpec((None, M, D), lambda l, b: (b, 0, 0)),
                  pl.BlockSpec((None, 1, D), lambda l, b: (l, 0, 0)),
                  pl.BlockSpec((None, D, 2 * D), lambda l, b: (l, 0, 0))],
        out_specs=pl.BlockSpec((None, None, M, 2 * D), lambda l, b: (l, b, 0, 0)),
        out_shape=jax.ShapeDtypeStruct((L, B, M, 2 * D), BF16),
        compiler_params=_cparams(("parallel", "arbitrary")),
        name="mem_kv",
    )(mem, g, w)


def _outmem_kernel(x_ref, oa_ref, ob_ref, oc_ref, wo_ref, g_ref, wq_ref, kv_ref, wmo_ref, o_ref):
    D = x_ref.shape[-1]
    na, nb = oa_ref.shape[-1], ob_ref.shape[-1]
    x = (x_ref[...] + _dot(oa_ref[...], wo_ref[0:na]) + _dot(ob_ref[...], wo_ref[na:na + nb])
         + _dot(oc_ref[...], wo_ref[na + nb:]))
    h = _rms(x, g_ref[...]).astype(BF16)
    hd = D // MEM_HEADS
    q = (_dot(h, wq_ref[...]) * (hd ** -0.5)).astype(BF16)
    outs = []
    for hh in range(MEM_HEADS):
        k = kv_ref[:, hh * hd:(hh + 1) * hd]
        v = kv_ref[:, D + hh * hd:D + (hh + 1) * hd]
        s = _dot_nt(q[:, hh * hd:(hh + 1) * hd], k)
        p = jnp.exp(s - jnp.max(s, axis=-1, keepdims=True))
        p = p / jnp.sum(p, axis=-1, keepdims=True)
        outs.append(_dot(p.astype(BF16), v).astype(BF16))
    o = jnp.concatenate(outs, axis=-1)
    o_ref[...] = x + _dot(o, wmo_ref[...])


def _outmem(x, oa, ob, oc, wo, g, wq, kv, layer, wmo, tm):
    B, S, D = x.shape
    M = kv.shape[2]
    const = lambda b, i: (0, 0)
    return pl.pallas_call(
        _outmem_kernel,
        grid=(B, S // tm),
        in_specs=[pl.BlockSpec((None, tm, D), lambda b, i: (b, i, 0)),
                  pl.BlockSpec((None, tm, oa.shape[-1]), lambda b, i: (b, i, 0)),
                  pl.BlockSpec((None, tm, ob.shape[-1]), lambda b, i: (b, i, 0)),
                  pl.BlockSpec((None, tm, oc.shape[-1]), lambda b, i: (b, i, 0)),
                  pl.BlockSpec(wo.shape, const),
                  pl.BlockSpec((1, D), const),
                  pl.BlockSpec(wq.shape, const),
                  pl.BlockSpec((None, None, M, 2 * D), lambda b, i: (layer, b, 0, 0)),
                  pl.BlockSpec(wmo.shape, const)],
        out_specs=pl.BlockSpec((None, tm, D), lambda b, i: (b, i, 0)),
        out_shape=jax.ShapeDtypeStruct((B, S, D), F32),
        compiler_params=_cparams(("parallel", "arbitrary")),
        name="outproj_memattn",
    )(x, oa, ob, oc, wo, g, wq, kv, wmo)


HALO = 8


def _ffn_kernel(x_ref, xp_ref, g_ref, wg_ref, wv_ref, cwg_ref, cwv_ref, cbg_ref, cbv_ref, wd_ref,
                o_ref, *, tm, fc):
    i = pl.program_id(1)
    x = x_ref[...]
    g = g_ref[...]
    hp = _rms(xp_ref[...], g) * jnp.where(i > 0, 1.0, 0.0)
    h = jnp.concatenate([hp, _rms(x, g)], axis=0).astype(BF16)
    F = wd_ref.shape[0]

    def conv(u, cw_ref, cb_ref, c, e):
        out = cb_ref[:, c:e]
        for j in range(CONV_WIDTH):
            shift = CONV_WIDTH - 1 - j
            out = out + cw_ref[j:j + 1, c:e] * u[HALO - shift:HALO - shift + tm]
        return out

    acc = x
    for c in range(0, F, fc):
        e = min(c + fc, F)
        gate = conv(_dot(h, wg_ref[:, c:e]), cwg_ref, cbg_ref, c, e)
        val = conv(_dot(h, wv_ref[:, c:e]), cwv_ref, cbv_ref, c, e)
        act = (gate * jax.nn.sigmoid(gate) * val).astype(BF16)
        acc = acc + _dot(act, wd_ref[c:e, :])
    o_ref[...] = acc


def _ffn(x, g, wg, wv, cwg, cwv, cbg, cbv, wd, tm, fc):
    B, S, D = x.shape
    F = wd.shape[0]
    const = lambda b, i: (0, 0)
    hb = tm // HALO
    return pl.pallas_call(
        functools.partial(_ffn_kernel, tm=tm, fc=fc),
        grid=(B, S // tm),
        in_specs=[pl.BlockSpec((None, tm, D), lambda b, i: (b, i, 0)),
                  pl.BlockSpec((None, HALO, D), lambda b, i: (b, jnp.maximum(i * hb - 1, 0), 0)),
                  pl.BlockSpec((1, D), const),
                  pl.BlockSpec((D, F), const), pl.BlockSpec((D, F), lambda b, i: (0, 1)),
                  pl.BlockSpec((CONV_WIDTH, F), const), pl.BlockSpec((CONV_WIDTH, F), const),
                  pl.BlockSpec((1, F), const), pl.BlockSpec((1, F), const),
                  pl.BlockSpec((F, D), const)],
        out_specs=pl.BlockSpec((None, tm, D), lambda b, i: (b, i, 0)),
        out_shape=jax.ShapeDtypeStruct((B, S, D), F32),
        compiler_params=_cparams(("parallel", "arbitrary")),
        name="conv_ffn",
    )(x, x, g, wg, wv, cwg, cwv, cbg, cbv, wd)


def _final_norm_kernel(x_ref, g_ref, o_ref):
    o_ref[...] = _rms(x_ref[...], g_ref[...])


def _final_norm(x2, g, tm):
    T, D = x2.shape
    return pl.pallas_call(
        _final_norm_kernel,
        grid=(T // tm,),
        in_specs=[pl.BlockSpec((tm, D), lambda i: (i, 0)), pl.BlockSpec((1, D), lambda i: (0, 0))],
        out_specs=pl.BlockSpec((tm, D), lambda i: (i, 0)),
        out_shape=jax.ShapeDtypeStruct((T, D), F32),
        compiler_params=_cparams(("parallel",)),
        name="final_norm",
    )(x2, g)


def _tile_params(S, tq_a, tk_a):
    big = 1 << 30
    gran, (d_min, d_const) = min(tq_a, tk_a), _a_offsets(tq_a, tk_a)
    pa = []
    for h in range(A_HEADS):
        pa += [(h, d * gran, 1, 0, big, -1) for d in range(d_min, d_const + 1)]
        pa.append((h, 0, 1, 1, 0, -1))
    pb = []
    for _, r in DILATED_PATTERNS:
        for h in range(B_HEADS):
            pb.append((A_HEADS + h, DIL_BLK, r, 1, DIL_BLK, 1))
            pb.append((A_HEADS + h, 0, r, 0, DIL_BLK, 1))
    pc = [(A_HEADS + B_HEADS + h, d * LANES, 1, -big, big, -1)
          for h in range(C_HEADS) for d in range(_n_offsets(S, LANES))]
    to = lambda p: jnp.asarray(p, jnp.int32)
    return to(pa), to(pb), to(pc)


def _dil_bias_layout(tiles):
    t = tiles.reshape(len(DILATED_PATTERNS), B_HEADS, 2, DIL_BLK, DIL_BLK)
    return jnp.concatenate([t[:, :, 0], t[:, :, 1]], axis=-1).reshape(
        len(DILATED_PATTERNS), B_HEADS * DIL_BLK, 2 * DIL_BLK)


def _n_offsets(S, blk):
    return min(S // blk, REL_MAX_DIST // blk + 2)


def _a_offsets(tq, tk):
    gran = min(tq, tk)
    return -(tq // gran - 1), -(-(REL_MAX_DIST - 1 + tk) // gran)


def _in_weights(w_in_l):
    sizes = (512, 512, 512, 256, 256, 256, 256, 256, 256, 512, 64, 8)
    offs = [0]
    for s in sizes:
        offs.append(offs[-1] + s)
    (aq, ak, av, bq, bk, bv, cq, ck, cv, iq, ik, iw) = [w_in_l[:, offs[i]:offs[i + 1]] for i in range(12)]
    pad = jnp.zeros((w_in_l.shape[0], N_I - 2 * IDX_DIM - IDX_HEADS), w_in_l.dtype)
    return jnp.concatenate([aq, ak, av, iq, cq, ck, cv, bq, bk, bv, ik, ik, iw, pad], axis=1).astype(BF16)


def _forward(x, mem, rel_bias, norm_mix, w_in, lam_q1, lam_k1, lam_q2, lam_k2, subln, w_out,
             norm_mem, norm_memkv, w_mq, w_mkv, w_mo, norm_ffn, w_up, conv_w, conv_b, w_down,
             norm_final, *, tq_a, tk_a, tq_c, ch_c, ca_c, tm_proj, tm_mem, tm_ffn, fc):
    B, S, D = x.shape
    L = w_in.shape[0]
    F = w_down.shape[1]
    pa, pb, pc = _tile_params(S, tq_a, tk_a)
    bias_a = _bias_tiles(pa, rel_bias, tk_a, tq_a, LOG2E).reshape(A_HEADS, -1, tk_a, tq_a)
    d_min, d_const = _a_offsets(tq_a, tk_a)
    bias_b = _dil_bias_layout(_bias_tiles(pb, rel_bias, DIL_BLK, DIL_BLK, LOG2E))
    bias_c = _bias_tiles(pc, rel_bias, LANES, LANES, LOG2E).reshape(C_HEADS, -1, LANES, LANES)
    kv_all = _memkv(mem, norm_memkv.reshape(L, 1, D), w_mkv.astype(BF16))

    for l in range(L):
        lam_init = 0.8 - 0.6 * math.exp(-0.3 * l)
        proj, proj_b, proj_i = _inproj(x.reshape(B * S, D), norm_mix[l].reshape(1, D),
                                       _in_weights(w_in[l]), tm_proj)
        proj = proj.reshape(B, S, N_MAIN)
        lamv = jnp.stack([lam_q1[l], lam_k1[l], lam_q2[l], lam_k2[l]], axis=0)
        avt = proj[:, :, 2 * A_HEADS * LANES:3 * A_HEADS * LANES].reshape(
            B, S // tk_a, tk_a, A_HEADS, A_V).transpose(0, 3, 1, 4, 2)
        cw = C_HEADS * C_DIM
        cvt = proj[:, :, N_MAIN - cw:].reshape(B, S // ca_c, ca_c, cw).transpose(0, 1, 3, 2)
        proj_i = proj_i.reshape(B, S, N_I)
        iwt = proj_i[:, :, 2 * IDX_DIM:2 * IDX_DIM + IDX_HEADS].transpose(0, 2, 1)
        o_a = _diff_attn(proj, avt, lamv, bias_a, subln[l].reshape(A_V, 1), lam_init, tq_a,
                         d_min, d_const)
        o_b = _dil_attn(proj_b.reshape(B, S, N_B), bias_b)
        o_c = _dsa_attn(proj, proj_i, iwt, cvt, bias_c, tq_c, ch_c)
        x = _outmem(x, o_a, o_b, o_c, w_out[l].astype(BF16), norm_mem[l].reshape(1, D),
                    w_mq[l].astype(BF16), kv_all, l, w_mo[l].astype(BF16), tm_mem)
        wu = w_up[l].astype(BF16)
        x = _ffn(x, norm_ffn[l].reshape(1, D), wu, wu, conv_w[l][:, :F], conv_w[l][:, F:],
                 conv_b[l][:F].reshape(1, F), conv_b[l][F:].reshape(1, F), w_down[l].astype(BF16),
                 tm_ffn, fc)
    return _final_norm(x.reshape(B * S, D), norm_final.reshape(1, D), tm_proj).reshape(B, S, D)


def kernel(x, mem, rel_bias, norm_mix, w_in, lam_q1, lam_k1, lam_q2, lam_k2, subln, w_out,
           norm_mem, norm_memkv, w_mq, w_mkv, w_mo, norm_ffn, w_up, conv_w, conv_b, w_down,
           norm_final):
    return _forward(x, mem, rel_bias, norm_mix, w_in, lam_q1, lam_k1, lam_q2, lam_k2, subln, w_out,
                    norm_mem, norm_memkv, w_mq, w_mkv, w_mo, norm_ffn, w_up, conv_w, conv_b, w_down,
                    norm_final, tq_a=512, tk_a=256, tq_c=512, ch_c=512, ca_c=256, tm_proj=512, tm_mem=512, tm_ffn=512,
                    fc=512)
```

```python
import functools
import math

import jax
import jax.numpy as jnp
from jax import lax
from jax.experimental import pallas as pl
from jax.experimental.pallas import tpu as pltpu

F32 = jnp.float32
BF16 = jnp.bfloat16

EPS = 1e-6
NEG = -1e30
LOG2E = math.log2(math.e)
LANES = 128
VMEM_LIMIT = 56 * 1024 * 1024

A_HEADS, A_QK, A_V = 4, 64, 128
B_HEADS, B_DIM = 4, 64
C_HEADS, C_DIM = 4, 64
IDX_HEADS, IDX_DIM = 8, 64
TOPK_MAX = 256
DILATED_PATTERNS = ((128, 1), (512, 4), (2048, 16))
ONES_ROWS = 16
DIL_BLK = 128
DIL_UNROLL = 4
REL_BUCKETS, REL_MAX_DIST = 32, 2048
MEM_HEADS = 4
CONV_WIDTH = 3

N_MAIN = 2816
N_B = 768
N_I = 256


def _cparams(sem):
    return pltpu.CompilerParams(dimension_semantics=sem, vmem_limit_bytes=VMEM_LIMIT)


def _dot(a, b):
    return jnp.dot(a, b, preferred_element_type=F32)


def _dot_nt(a, b):
    return lax.dot_general(a, b, (((1,), (1,)), ((), ())), preferred_element_type=F32)


def _ones_rows(n):
    row = lax.broadcasted_iota(jnp.int32, (ONES_ROWS, n), 0)
    return jnp.where(row == 0, 1.0, 0.0).astype(BF16)


def _rms(x, g):
    return x * lax.rsqrt(jnp.mean(x * x, axis=-1, keepdims=True) + EPS) * g


def _rel_bucket(n):
    max_exact = REL_BUCKETS // 2
    nf = jnp.maximum(n, 1).astype(F32)
    large = max_exact + (jnp.log(nf / max_exact) / math.log(REL_MAX_DIST / max_exact)
                         * (REL_BUCKETS - max_exact)).astype(jnp.int32)
    large = jnp.minimum(large, REL_BUCKETS - 1)
    return jnp.where(n < max_exact, n, large)


def _bias_tiles_kernel(par_ref, tab_ref, o_ref, *, tq, tk, scale):
    t = pl.program_id(0)
    head, off, mult = par_ref[t, 0], par_ref[t, 1], par_ref[t, 2]
    lo, hi, sgn = par_ref[t, 3], par_ref[t, 4], par_ref[t, 5]
    dist = off + sgn * (lax.broadcasted_iota(jnp.int32, (tq, tk), 0)
                        - lax.broadcasted_iota(jnp.int32, (tq, tk), 1))
    bucket = _rel_bucket(jnp.maximum(dist * mult, 0))
    val = jnp.zeros((tq, tk), F32)
    for b in range(REL_BUCKETS):
        val = jnp.where(bucket == b, tab_ref[b, head] * scale, val)
    o_ref[...] = jnp.where((dist >= lo) & (dist <= hi), val, NEG)


def _bias_tiles(params, table, tq, tk, scale=1.0):
    n = params.shape[0]
    return pl.pallas_call(
        functools.partial(_bias_tiles_kernel, tq=tq, tk=tk, scale=scale),
        grid=(n,),
        in_specs=[pl.BlockSpec(memory_space=pltpu.SMEM), pl.BlockSpec(memory_space=pltpu.SMEM)],
        out_specs=pl.BlockSpec((None, tq, tk), lambda t: (t, 0, 0)),
        out_shape=jax.ShapeDtypeStruct((n, tq, tk), F32),
        compiler_params=_cparams(("arbitrary",)),
        name="bias_tiles",
    )(params, table)


def _inproj_kernel(x_ref, g_ref, w_ref, om_ref, ob_ref, oi_ref):
    h = _rms(x_ref[...], g_ref[...]).astype(BF16)
    step = 512
    for c in range(0, N_MAIN, step):
        e = min(c + step, N_MAIN)
        om_ref[:, c:e] = _dot(h, w_ref[:, c:e]).astype(BF16)
    ob_ref[...] = _dot(h, w_ref[:, N_MAIN:N_MAIN + N_B])
    oi_ref[...] = _dot(h, w_ref[:, N_MAIN + N_B:])


def _inproj(x2, g, w, tm):
    T, D = x2.shape
    n_all = N_MAIN + N_B + N_I
    return pl.pallas_call(
        _inproj_kernel,
        grid=(T // tm,),
        in_specs=[pl.BlockSpec((tm, D), lambda i: (i, 0)),
                  pl.BlockSpec((1, D), lambda i: (0, 0)),
                  pl.BlockSpec((D, n_all), lambda i: (0, 0))],
        out_specs=[pl.BlockSpec((tm, N_MAIN), lambda i: (i, 0)),
                   pl.BlockSpec((tm, N_B), lambda i: (i, 0)),
                   pl.BlockSpec((tm, N_I), lambda i: (i, 0))],
        out_shape=[jax.ShapeDtypeStruct((T, N_MAIN), BF16),
                   jax.ShapeDtypeStruct((T, N_B), F32),
                   jax.ShapeDtypeStruct((T, N_I), F32)],
        compiler_params=_cparams(("parallel",)),
        name="inproj",
    )(x2, g, w)


def _diff_attn_kernel(lam_ref, q_ref, k_ref, vt_ref, bias_ref, g_ref, o_ref, s_sc, acc_sc,
                      *, tq, tk, d_min, d_const, lam_init):
    qi = pl.program_id(2)
    gran = min(tq, tk)
    last = (qi * tq + tq - 1) // tk
    masked_tile = d_const - d_min + 1
    lv = lam_ref[...]
    lam = (jnp.exp(jnp.sum(lv[0:1] * lv[1:2], axis=-1, keepdims=True))
           - jnp.exp(jnp.sum(lv[2:3] * lv[3:4], axis=-1, keepdims=True)) + lam_init)
    lane = lax.broadcasted_iota(jnp.int32, (tq, LANES), 1)
    q = q_ref[...].astype(F32) * (A_QK ** -0.5 * LOG2E)
    qcat = jnp.concatenate([jnp.where(lane < A_QK, q, 0.0), jnp.where(lane >= A_QK, q, 0.0)],
                           axis=0).astype(BF16)
    acc_sc[...] = jnp.zeros(acc_sc.shape, F32)

    def scores_into(slot, j):
        jc = jnp.minimum(j, last)
        k = k_ref[pl.ds(pl.multiple_of(jc * tk, tk), tk), :]
        d = qi * (tq // gran) - jc * (tk // gran)
        bias = bias_ref[jnp.where(j <= last, jnp.minimum(d, d_const) - d_min, masked_tile)]
        s_sc[slot] = _dot_nt(k, qcat) + jnp.concatenate([bias, bias], axis=1)

    def absorb(slot, j, m_prev):
        m_new = jnp.maximum(m_prev, jnp.max(s_sc[slot], axis=0, keepdims=True))
        alpha = jnp.exp2(m_prev - m_new)
        p = jnp.exp2(s_sc[slot] - m_new)
        vt = jnp.concatenate([vt_ref[jnp.minimum(j, last)], _ones_rows(tk)], axis=0)
        acc_sc[...] = alpha * acc_sc[...] + _dot(vt, p.astype(BF16))
        return m_new

    scores_into(0, 0)

    def body(jj, m):
        a = 2 * jj
        scores_into(1, a + 1)
        m = absorb(0, a, m)
        scores_into(0, a + 2)
        return absorb(1, a + 1, m)

    lax.fori_loop(0, (last + 2) // 2, body, jnp.full((1, 2 * tq), NEG, F32))
    inv = 1.0 / acc_sc[A_V:A_V + 1, :]
    o = acc_sc[:A_V, :tq] * inv[:, :tq] - lam * (acc_sc[:A_V, tq:] * inv[:, tq:])
    o = o * lax.rsqrt(jnp.mean(o * o, axis=0, keepdims=True) + EPS) * g_ref[...] * (1.0 - lam_init)
    o_ref[...] = o.T.astype(o_ref.dtype)


def _diff_attn(proj, vt, lamv, bias_a, subln_g, lam_init, tq, d_min, d_const):
    B, S, _ = proj.shape
    nd, tk = bias_a.shape[1], bias_a.shape[2]
    return pl.pallas_call(
        functools.partial(_diff_attn_kernel, tq=tq, tk=tk, d_min=d_min, d_const=d_const,
                          lam_init=lam_init),
        grid=(A_HEADS, B, S // tq),
        in_specs=[pl.BlockSpec((4, A_QK), lambda h, b, i: (0, 0)),
                  pl.BlockSpec((None, tq, LANES), lambda h, b, i: (b, i, h)),
                  pl.BlockSpec((None, S, LANES), lambda h, b, i: (b, 0, A_HEADS + h)),
                  pl.BlockSpec((None, None, S // tk, A_V, tk), lambda h, b, i: (b, h, 0, 0, 0)),
                  pl.BlockSpec((None, nd, tk, tq), lambda h, b, i: (h, 0, 0, 0)),
                  pl.BlockSpec((A_V, 1), lambda h, b, i: (0, 0))],
        out_specs=pl.BlockSpec((None, tq, LANES), lambda h, b, i: (b, i, h)),
        out_shape=jax.ShapeDtypeStruct((B, S, A_HEADS * A_V), BF16),
        scratch_shapes=[pltpu.VMEM((2, tk, 2 * tq), F32),
                        pltpu.VMEM((A_V + ONES_ROWS, 2 * tq), F32)],
        compiler_params=_cparams(("parallel", "parallel", "arbitrary")),
        name="diff_attn",
    )(lamv, proj, proj, vt, bias_a, subln_g)


class _LaneHalves:
    def __init__(self, *refs):
        self.refs = refs

    def __getitem__(self, idx):
        return jnp.concatenate([r[idx] for r in self.refs], axis=1)

    def __setitem__(self, idx, val):
        for i, r in enumerate(self.refs):
            r[idx] = val[:, i * LANES:(i + 1) * LANES]


def _dil_kernel(q0, q1, k0, k1, v0, v1, bias_ref, o_ref, n0, n1, m0, m1, l0, l1, *, seq, patterns):
    q_ref, k_ref, v_ref = _LaneHalves(q0, q1), _LaneHalves(k0, k1), _LaneHalves(v0, v1)
    n_sc, m_sc, l_sc = _LaneHalves(n0, n1), _LaneHalves(m0, m1), _LaneHalves(l0, l1)
    blk = DIL_BLK
    width = B_HEADS * B_DIM
    head_of_lane = lax.broadcasted_iota(jnp.int32, (blk, width), 1) // B_DIM

    def per_head(x):
        parts = [jnp.broadcast_to(x[h * blk:(h + 1) * blk], (blk, width)) for h in range(B_HEADS)]
        out = parts[-1]
        for h in range(B_HEADS - 2, -1, -1):
            out = jnp.where(head_of_lane == h, parts[h], out)
        return out

    def attend(base, r, with_prev, p_idx):
        q = q_ref[pl.ds(base, blk, stride=r), :] * (B_DIM ** -0.5 * LOG2E)
        qcat = jnp.concatenate([jnp.where(head_of_lane == h, q, 0.0) for h in range(B_HEADS)],
                               axis=0).astype(BF16)
        if with_prev:
            keys = pl.ds(base - blk * r, 2 * blk, stride=r)
            bias = bias_ref[p_idx]
        else:
            keys = pl.ds(base, blk, stride=r)
            bias = bias_ref[p_idx, :, blk:]
        s = _dot_nt(qcat, k_ref[keys, :].astype(BF16)) + bias
        mx = jnp.max(s, axis=-1, keepdims=True)
        p = jnp.exp2(s - mx)
        l = jnp.sum(p, axis=-1, keepdims=True)
        o = _dot(p.astype(BF16), v_ref[keys, :].astype(BF16))
        return per_head(o), per_head(mx), per_head(l)

    def merge(base, r, res, is_first):
        o, m, l = res
        rows = pl.ds(base, blk, stride=r)
        if is_first:
            n_sc[rows, :] = o
            m_sc[rows, :] = m
            l_sc[rows, :] = l
        else:
            m_old = m_sc[rows, :]
            m_new = jnp.maximum(m_old, m)
            a = jnp.exp2(m_old - m_new)
            b = jnp.exp2(m - m_new)
            n_sc[rows, :] = a * n_sc[rows, :] + b * o
            l_sc[rows, :] = a * l_sc[rows, :] + b * l
            m_sc[rows, :] = m_new

    def sweep(count, base_of, r, with_prev, p_idx):
        def trip(width):
            def body(i, carry):
                bases = [base_of(i * width + u) for u in range(width)]
                results = [attend(b, r, with_prev, p_idx) for b in bases]
                for b, res in zip(bases, results):
                    merge(b, r, res, p_idx == len(patterns) - 1)
                return carry
            return body

        full = count // DIL_UNROLL
        lax.fori_loop(0, full, trip(DIL_UNROLL), 0)
        rest = count - full * DIL_UNROLL
        if rest:
            lax.fori_loop(full * DIL_UNROLL, count, trip(1), 0)

    for p_idx, (_, r) in reversed(list(enumerate(patterns))):
        nb = seq // (r * blk)
        sweep(r, lambda c: c, r, False, p_idx)
        sweep(r * (nb - 1), lambda idx, r=r: (idx // r + 1) * blk * r + idx % r, r, True, p_idx)

    everything = (slice(None), slice(None))
    o_ref[...] = (n_sc[everything] / l_sc[everything]).astype(o_ref.dtype)


def _dil_attn(proj_b, bias_b):
    B, S, _ = proj_b.shape
    width = B_HEADS * B_DIM
    return pl.pallas_call(
        functools.partial(_dil_kernel, seq=S, patterns=DILATED_PATTERNS),
        grid=(B,),
        in_specs=[pl.BlockSpec((None, S, LANES), functools.partial(lambda i, b: (b, 0, i), i))
                  for i in range(3 * width // LANES)]
                 + [pl.BlockSpec(bias_b.shape, lambda b: (0, 0, 0))],
        out_specs=pl.BlockSpec((None, S, width), lambda b: (b, 0, 0)),
        out_shape=jax.ShapeDtypeStruct((B, S, width), BF16),
        scratch_shapes=[pltpu.VMEM((S, LANES), F32)] * (3 * width // LANES),
        compiler_params=_cparams(("parallel",)),
        name="dilated_attn",
    )(*([proj_b] * (3 * width // LANES)), bias_b)


SEL16 = jnp.bfloat16


RADIX_LOW_BITS = 5
PEEL_CAP = 6
FLT_MIN_NORMAL = 2.0 ** -126


def _float_key(f):
    b = pltpu.bitcast(f, jnp.int32)
    return jnp.where(b >= 0, b, b ^ jnp.int32(0x7FFFFFFF))


def _key_float(k):
    return pltpu.bitcast(jnp.where(k >= 0, k, k ^ jnp.int32(0x7FFFFFFF)), F32)


def _high_half(f):
    return pltpu.bitcast(pltpu.bitcast(f, jnp.int32) & jnp.int32(-65536), F32)


def _dsa_kernel(iq_ref, ik_ref, iwt_ref, cq_ref, ck_ref, cvt_ref, bias_ref, o_ref, sc_ref, hi_ref, s_sc,
                acc_sc,
                *, tq, ch, ca, topk):
    qi = pl.program_id(1)
    q0 = qi * tq
    nch = (q0 + tq - 1) // ch + 1
    qidx = q0 + lax.broadcasted_iota(jnp.int32, (1, tq), 1)
    lane = lax.broadcasted_iota(jnp.int32, (tq, LANES), 1)
    first = lane < IDX_DIM

    w = iwt_ref[...] * (IDX_HEADS ** -0.5)
    qs = []
    for h in range(IDX_HEADS):
        tile = iq_ref[:, (h // 2) * LANES:(h // 2 + 1) * LANES].astype(F32) * (IDX_DIM ** -0.5)
        qs.append(jnp.where(first if h % 2 == 0 else ~first, tile, 0.0).astype(BF16))

    def score_body(j, carry):
        start = pl.multiple_of(j * ch, ch)
        kk = ik_ref[pl.ds(start, ch), :].astype(BF16)
        sc = jnp.zeros((ch, tq), F32)
        for h in range(IDX_HEADS):
            sc = sc + jnp.maximum(_dot_nt(kk, qs[h]), 0.0) * w[h:h + 1, :]
        kidx = start + lax.broadcasted_iota(jnp.int32, (ch, 1), 0)
        sc = jnp.where(kidx <= qidx, sc, NEG)
        sc_ref[j] = sc
        hi_ref[j] = _high_half(sc).astype(SEL16)
        return carry

    lax.fori_loop(0, nch, score_body, 0)

    rows = 32

    def count_where(pred):
        def body(j, acc):
            hit = jnp.where(pred(sc_ref[j]), 1.0, 0.0)
            return acc + jnp.sum(hit.reshape(ch // rows, rows, tq), axis=0)
        acc = lax.fori_loop(0, nch, body, jnp.zeros((rows, tq), F32))
        return jnp.sum(acc, axis=0, keepdims=True)

    def count_high_ge(v):
        one, zero16 = jnp.ones((), SEL16), jnp.zeros((), SEL16)
        def body(j, acc):
            hit = jnp.where(hi_ref[j] >= v, one, zero16).reshape(ch // rows, rows, tq)
            parts = [hit[i] for i in range(ch // rows)]
            while len(parts) > 1:
                parts = [parts[i] + parts[i + 1] for i in range(0, len(parts), 2)]
            return acc + parts[0]
        acc = lax.fori_loop(0, nch, body, jnp.zeros((rows, tq), SEL16))
        return jnp.sum(acc.astype(F32), axis=0, keepdims=True)

    kf = float(topk)
    n_valid = (qidx + 1).astype(F32)
    half_bits = 16

    def high_body(i, st):
        u, c_lo = st
        cand = u | jnp.left_shift(1, half_bits - 1 - i)
        v = _high_half(_key_float((cand - (1 << (half_bits - 1))) << half_bits)).astype(SEL16)
        c = count_high_ge(v)
        keep = c >= kf
        return jnp.where(keep, cand, u), jnp.where(keep, c, c_lo)

    u, c_lo = lax.fori_loop(0, half_bits, high_body, (jnp.zeros((1, tq), jnp.int32), n_valid))
    key_hi = (u - (1 << (half_bits - 1))) << half_bits

    def low_body(i, st):
        lo_bits, c_lo = st
        cand = lo_bits | jnp.left_shift(1, half_bits - 1 - i)
        c = count_where(lambda sc: sc >= _key_float(key_hi | cand))
        keep = c >= kf
        return jnp.where(keep, cand, lo_bits), jnp.where(keep, c, c_lo)

    lo_bits, c_lo = lax.fori_loop(0, RADIX_LOW_BITS, low_body, (jnp.zeros((1, tq), jnp.int32), c_lo))

    def min_ge(t):
        def body(j, acc):
            sc = sc_ref[j]
            x = jnp.where(sc >= t, sc, -NEG)
            return jnp.minimum(acc, jnp.min(x.reshape(ch // rows, rows, tq), axis=0))
        acc = lax.fori_loop(0, nch, body, jnp.full((rows, tq), -NEG, F32))
        return jnp.min(acc, axis=0, keepdims=True)

    def open_rows(c_lo, done):
        return jnp.logical_and(jnp.logical_and(n_valid > kf, c_lo > kf), done == 0.0)

    def any_row(mask):
        return jnp.max(jnp.where(mask, 1.0, 0.0)) > 0.0

    def peel_cond(st):
        _, c_lo, done, it = st
        return jnp.logical_and(it < PEEL_CAP, any_row(open_rows(c_lo, done)))

    def peel_body(st):
        t, c_lo, done, it = st
        is_open = open_rows(c_lo, done)
        smallest = min_ge(t)
        above = jnp.where(smallest == 0.0, FLT_MIN_NORMAL, _key_float(_float_key(smallest) + 1))
        c = count_where(lambda sc: sc >= above)
        enough = jnp.logical_and(is_open, c >= kf)
        at_tie = jnp.logical_and(is_open, c < kf)
        t = jnp.where(enough, above, jnp.where(at_tie, smallest, t))
        return t, jnp.where(enough, c, c_lo), jnp.where(at_tie, 1.0, done), it + 1

    t_peel, c_peel, done, _ = lax.while_loop(
        peel_cond, peel_body,
        (_key_float(key_hi | lo_bits), c_lo, jnp.zeros((1, tq), F32), jnp.int32(0)))

    def finish_by_radix(_):
        bits, c = lax.fori_loop(RADIX_LOW_BITS, half_bits, low_body, (lo_bits, c_lo))
        return _key_float(key_hi | bits), c

    thr, c_lo = lax.cond(any_row(open_rows(c_peel, done)), finish_by_radix,
                         lambda _: (t_peel, c_peel), 0)
    thr = jnp.where(n_valid > kf, thr, 0.5 * NEG)

    tie_rows = jnp.logical_and(n_valid > kf, c_lo > kf)

    @pl.when(jnp.max(jnp.where(tie_rows, 1.0, 0.0)) > 0.0)
    def _():
        need = kf - count_where(lambda sc: sc > thr)
        lower = (lax.broadcasted_iota(jnp.int32, (ch, ch), 1)
                 <= lax.broadcasted_iota(jnp.int32, (ch, ch), 0)).astype(BF16)

        def drop_body(j, seen):
            sc = sc_ref[j]
            eq = sc == thr
            eqf = jnp.where(eq, 1.0, 0.0)
            rank = _dot(lower, eqf.astype(BF16)) + seen
            drop = jnp.logical_and(jnp.logical_and(eq, rank > need), tie_rows)
            sc_ref[j] = jnp.where(drop, NEG, sc)
            return seen + jnp.sum(eqf, axis=0, keepdims=True)

        lax.fori_loop(0, nch, drop_body, jnp.zeros((1, tq), F32))

    cw = C_HEADS * C_DIM
    lane_c = lax.broadcasted_iota(jnp.int32, (tq, cw), 1)
    cq = cq_ref[...].astype(F32) * (C_DIM ** -0.5 * LOG2E)
    qcat = jnp.concatenate([jnp.where(lane_c // C_DIM == h, cq, 0.0) for h in range(C_HEADS)],
                           axis=0).astype(BF16)
    acc_sc[...] = jnp.zeros(acc_sc.shape, F32)
    nd = bias_ref.shape[1]
    last = (q0 + tq - 1) // ca

    def scores_into(slot, j):
        jc = jnp.minimum(j, last)
        kc = ck_ref[pl.ds(pl.multiple_of(jc * ca, ca), ca), :]
        sub = pl.multiple_of((jc % (ch // ca)) * ca, ca)
        sel = sc_ref[jc // (ch // ca), pl.ds(sub, ca), :] >= jnp.where(j <= last, thr, -NEG)
        bias = jnp.concatenate([
            jnp.concatenate([
                bias_ref[h, jnp.clip(q0 // LANES + c - (jc * (ca // LANES) + u), 0, nd - 1)]
                for h in range(C_HEADS) for c in range(tq // LANES)], axis=1)
            for u in range(ca // LANES)], axis=0)
        s_sc[slot] = jnp.where(jnp.concatenate([sel] * C_HEADS, axis=1),
                               _dot_nt(kc, qcat) + bias, NEG)

    def absorb(slot, j, m_prev):
        m_new = jnp.maximum(m_prev, jnp.max(s_sc[slot], axis=0, keepdims=True))
        alpha = jnp.exp2(m_prev - m_new)
        pb = jnp.exp2(s_sc[slot] - m_new).astype(BF16)
        jc = jnp.minimum(j, last)
        for h in range(C_HEADS):
            cols = slice(h * tq, (h + 1) * tq)
            vt = jnp.concatenate([cvt_ref[jc, h * C_DIM:(h + 1) * C_DIM, :], _ones_rows(ca)], axis=0)
            acc_sc[h] = alpha[:, cols] * acc_sc[h] + _dot(vt, pb[:, cols])
        return m_new

    scores_into(0, 0)

    def attn_body(jj, m):
        a = 2 * jj
        scores_into(1, a + 1)
        m = absorb(0, a, m)
        scores_into(0, a + 2)
        return absorb(1, a + 1, m)

    lax.fori_loop(0, (last + 2) // 2, attn_body, jnp.full((1, C_HEADS * tq), NEG, F32))
    o_t = jnp.concatenate([acc_sc[h, :C_DIM] * (1.0 / acc_sc[h, C_DIM:C_DIM + 1])
                           for h in range(C_HEADS)], axis=0)
    o_ref[...] = o_t.T.astype(o_ref.dtype)


def _dsa_attn(proj, proj_i, iwt, cvt, bias_c, tq, ch):
    B, S, _ = proj.shape
    ca = cvt.shape[-1]
    topk = min(TOPK_MAX, S // 4)
    nd = bias_c.shape[1]
    cw = C_HEADS * C_DIM
    return pl.pallas_call(
        functools.partial(_dsa_kernel, tq=tq, ch=ch, ca=ca, topk=topk),
        grid=(B, S // tq),
        in_specs=[pl.BlockSpec((None, tq, IDX_HEADS * IDX_DIM), lambda b, i: (b, i, 3)),
                  pl.BlockSpec((None, S, LANES), lambda b, i: (b, 0, 0)),
                  pl.BlockSpec((None, IDX_HEADS, tq), lambda b, i: (b, 0, i)),
                  pl.BlockSpec((None, tq, cw), lambda b, i: (b, i, 8)),
                  pl.BlockSpec((None, S, cw), lambda b, i: (b, 0, 9)),
                  pl.BlockSpec((None, S // ca, cw, ca), lambda b, i: (b, 0, 0, 0)),
                  pl.BlockSpec((C_HEADS, nd, LANES, LANES), lambda b, i: (0, 0, 0, 0))],
        out_specs=pl.BlockSpec((None, tq, cw), lambda b, i: (b, i, 0)),
        out_shape=jax.ShapeDtypeStruct((B, S, cw), BF16),
        scratch_shapes=[pltpu.VMEM((S // ch, ch, tq), F32),
                        pltpu.VMEM((S // ch, ch, tq), SEL16),
                        pltpu.VMEM((2, ca, C_HEADS * tq), F32),
                        pltpu.VMEM((C_HEADS, C_DIM + ONES_ROWS, tq), F32)],
        compiler_params=_cparams(("parallel", "arbitrary")),
        name="dsa_attn",
    )(proj, proj_i, iwt, proj, proj, cvt, bias_c)


def _memkv_kernel(mem_ref, g_ref, w_ref, o_ref):
    h = _rms(mem_ref[...], g_ref[...]).astype(BF16)
    o_ref[...] = _dot(h, w_ref[...]).astype(o_ref.dtype)


def _memkv(mem, g, w):
    B, M, D = mem.shape
    L = w.shape[0]
    return pl.pallas_call(
        _memkv_kernel,
        grid=(L, B),
        in_specs=[pl.BlockSpec((None, M, D), lambda l, b: (b, 0, 0)),
                  pl.BlockSpec((None, 1, D), lambda l, b: (l, 0, 0)),
                  pl.BlockSpec((None, D, 2 * D), lambda l, b: (l, 0, 0))],
        out_specs=pl.BlockSpec((None, None, M, 2 * D), lambda l, b: (l, b, 0, 0)),
        out_shape=jax.ShapeDtypeStruct((L, B, M, 2 * D), BF16),
        compiler_params=_cparams(("parallel", "arbitrary")),
        name="mem_kv",
    )(mem, g, w)


def _outmem_kernel(x_ref, oa_ref, ob_ref, oc_ref, wo_ref, g_ref, wq_ref, kv_ref, wmo_ref, o_ref):
    D = x_ref.shape[-1]
    na, nb = oa_ref.shape[-1], ob_ref.shape[-1]
    x = (x_ref[...] + _dot(oa_ref[...], wo_ref[0:na]) + _dot(ob_ref[...], wo_ref[na:na + nb])
         + _dot(oc_ref[...], wo_ref[na + nb:]))
    h = _rms(x, g_ref[...]).astype(BF16)
    hd = D // MEM_HEADS
    q = (_dot(h, wq_ref[...]) * (hd ** -0.5)).astype(BF16)
    outs = []
    for hh in range(MEM_HEADS):
        k = kv_ref[:, hh * hd:(hh + 1) * hd]
        v = kv_ref[:, D + hh * hd:D + (hh + 1) * hd]
        s = _dot_nt(q[:, hh * hd:(hh + 1) * hd], k)
        p = jnp.exp(s - jnp.max(s, axis=-1, keepdims=True))
        p = p / jnp.sum(p, axis=-1, keepdims=True)
        outs.append(_dot(p.astype(BF16), v).astype(BF16))
    o = jnp.concatenate(outs, axis=-1)
    o_ref[...] = x + _dot(o, wmo_ref[...])


def _outmem(x, oa, ob, oc, wo, g, wq, kv, layer, wmo, tm):
    B, S, D = x.shape
    M = kv.shape[2]
    const = lambda b, i: (0, 0)
    return pl.pallas_call(
        _outmem_kernel,
        grid=(B, S // tm),
        in_specs=[pl.BlockSpec((None, tm, D), lambda b, i: (b, i, 0)),
                  pl.BlockSpec((None, tm, oa.shape[-1]), lambda b, i: (b, i, 0)),
                  pl.BlockSpec((None, tm, ob.shape[-1]), lambda b, i: (b, i, 0)),
                  pl.BlockSpec((None, tm, oc.shape[-1]), lambda b, i: (b, i, 0)),
                  pl.BlockSpec(wo.shape, const),
                  pl.BlockSpec((1, D), const),
                  pl.BlockSpec(wq.shape, const),
                  pl.BlockSpec((None, None, M, 2 * D), lambda b, i: (layer, b, 0, 0)),
                  pl.BlockSpec(wmo.shape, const)],
        out_specs=pl.BlockSpec((None, tm, D), lambda b, i: (b, i, 0)),
        out_shape=jax.ShapeDtypeStruct((B, S, D), F32),
        compiler_params=_cparams(("parallel", "arbitrary")),
        name="outproj_memattn",
    )(x, oa, ob, oc, wo, g, wq, kv, wmo)


HALO = 8


def _ffn_kernel(x_ref, xp_ref, g_ref, wg_ref, wv_ref, cwg_ref, cwv_ref, cbg_ref, cbv_ref, wd_ref,
                o_ref, *, tm, fc):
    i = pl.program_id(1)
    x = x_ref[...]
    g = g_ref[...]
    hp = _rms(xp_ref[...], g) * jnp.where(i > 0, 1.0, 0.0)
    h = jnp.concatenate([hp, _rms(x, g)], axis=0).astype(BF16)
    F = wd_ref.shape[0]

    def conv(u, cw_ref, cb_ref, c, e):
        out = cb_ref[:, c:e]
        for j in range(CONV_WIDTH):
            shift = CONV_WIDTH - 1 - j
            out = out + cw_ref[j:j + 1, c:e] * u[HALO - shift:HALO - shift + tm]
        return out

    acc = x
    for c in range(0, F, fc):
        e = min(c + fc, F)
        gate = conv(_dot(h, wg_ref[:, c:e]), cwg_ref, cbg_ref, c, e)
        val = conv(_dot(h, wv_ref[:, c:e]), cwv_ref, cbv_ref, c, e)
        act = (gate * jax.nn.sigmoid(gate) * val).astype(BF16)
        acc = acc + _dot(act, wd_ref[c:e, :])
    o_ref[...] = acc


def _ffn(x, g, wg, wv, cwg, cwv, cbg, cbv, wd, tm, fc):
    B, S, D = x.shape
    F = wd.shape[0]
    const = lambda b, i: (0, 0)
    hb = tm // HALO
    return pl.pallas_call(
        functools.partial(_ffn_kernel, tm=tm, fc=fc),
        grid=(B, S // tm),
        in_specs=[pl.BlockSpec((None, tm, D), lambda b, i: (b, i, 0)),
                  pl.BlockSpec((None, HALO, D), lambda b, i: (b, jnp.maximum(i * hb - 1, 0), 0)),
                  pl.BlockSpec((1, D), const),
                  pl.BlockSpec((D, F), const), pl.BlockSpec((D, F), lambda b, i: (0, 1)),
                  pl.BlockSpec((CONV_WIDTH, F), const), pl.BlockSpec((CONV_WIDTH, F), const),
                  pl.BlockSpec((1, F), const), pl.BlockSpec((1, F), const),
                  pl.BlockSpec((F, D), const)],
        out_specs=pl.BlockSpec((None, tm, D), lambda b, i: (b, i, 0)),
        out_shape=jax.ShapeDtypeStruct((B, S, D), F32),
        compiler_params=_cparams(("parallel", "arbitrary")),
        name="conv_ffn",
    )(x, x, g, wg, wv, cwg, cwv, cbg, cbv, wd)


def _final_norm_kernel(x_ref, g_ref, o_ref):
    o_ref[...] = _rms(x_ref[...], g_ref[...])


def _final_norm(x2, g, tm):
    T, D = x2.shape
    return pl.pallas_call(
        _final_norm_kernel,
        grid=(T // tm,),
        in_specs=[pl.BlockSpec((tm, D), lambda i: (i, 0)), pl.BlockSpec((1, D), lambda i: (0, 0))],
        out_specs=pl.BlockSpec((tm, D), lambda i: (i, 0)),
        out_shape=jax.ShapeDtypeStruct((T, D), F32),
        compiler_params=_cparams(("parallel",)),
        name="final_norm",
    )(x2, g)


def _tile_params(S, tq_a, tk_a):
    big = 1 << 30
    gran, (d_min, d_const) = min(tq_a, tk_a), _a_offsets(tq_a, tk_a)
    pa = []
    for h in range(A_HEADS):
        pa += [(h, d * gran, 1, 0, big, -1) for d in range(d_min, d_const + 1)]
        pa.append((h, 0, 1, 1, 0, -1))
    pb = []
    for _, r in DILATED_PATTERNS:
        for h in range(B_HEADS):
            pb.append((A_HEADS + h, DIL_BLK, r, 1, DIL_BLK, 1))
            pb.append((A_HEADS + h, 0, r, 0, DIL_BLK, 1))
    pc = [(A_HEADS + B_HEADS + h, d * LANES, 1, -big, big, -1)
          for h in range(C_HEADS) for d in range(_n_offsets(S, LANES))]
    to = lambda p: jnp.asarray(p, jnp.int32)
    return to(pa), to(pb), to(pc)


def _dil_bias_layout(tiles):
    t = tiles.reshape(len(DILATED_PATTERNS), B_HEADS, 2, DIL_BLK, DIL_BLK)
    return jnp.concatenate([t[:, :, 0], t[:, :, 1]], axis=-1).reshape(
        len(DILATED_PATTERNS), B_HEADS * DIL_BLK, 2 * DIL_BLK)


def _n_offsets(S, blk):
    return min(S // blk, REL_MAX_DIST // blk + 2)


def _a_offsets(tq, tk):
    gran = min(tq, tk)
    return -(tq // gran - 1), -(-(REL_MAX_DIST - 1 + tk) // gran)


def _in_weights(w_in_l):
    sizes = (512, 512, 512, 256, 256, 256, 256, 256, 256, 512, 64, 8)
    offs = [0]
    for s in sizes:
        offs.append(offs[-1] + s)
    (aq, ak, av, bq, bk, bv, cq, ck, cv, iq, ik, iw) = [w_in_l[:, offs[i]:offs[i + 1]] for i in range(12)]
    pad = jnp.zeros((w_in_l.shape[0], N_I - 2 * IDX_DIM - IDX_HEADS), w_in_l.dtype)
    return jnp.concatenate([aq, ak, av, iq, cq, ck, cv, bq, bk, bv, ik, ik, iw, pad], axis=1).astype(BF16)


def _forward(x, mem, rel_bias, norm_mix, w_in, lam_q1, lam_k1, lam_q2, lam_k2, subln, w_out,
             norm_mem, norm_memkv, w_mq, w_mkv, w_mo, norm_ffn, w_up, conv_w, conv_b, w_down,
             norm_final, *, tq_a, tk_a, tq_c, ch_c, ca_c, tm_proj, tm_mem, tm_ffn, fc):
    B, S, D = x.shape
    L = w_in.shape[0]
    F = w_down.shape[1]
    pa, pb, pc = _tile_params(S, tq_a, tk_a)
    bias_a = _bias_tiles(pa, rel_bias, tk_a, tq_a, LOG2E).reshape(A_HEADS, -1, tk_a, tq_a)
    d_min, d_const = _a_offsets(tq_a, tk_a)
    bias_b = _dil_bias_layout(_bias_tiles(pb, rel_bias, DIL_BLK, DIL_BLK, LOG2E))
    bias_c = _bias_tiles(pc, rel_bias, LANES, LANES, LOG2E).reshape(C_HEADS, -1, LANES, LANES)
    kv_all = _memkv(mem, norm_memkv.reshape(L, 1, D), w_mkv.astype(BF16))

    for l in range(L):
        lam_init = 0.8 - 0.6 * math.exp(-0.3 * l)
        proj, proj_b, proj_i = _inproj(x.reshape(B * S, D), norm_mix[l].reshape(1, D),
                                       _in_weights(w_in[l]), tm_proj)
        proj = proj.reshape(B, S, N_MAIN)
        lamv = jnp.stack([lam_q1[l], lam_k1[l], lam_q2[l], lam_k2[l]], axis=0)
        avt = proj[:, :, 2 * A_HEADS * LANES:3 * A_HEADS * LANES].reshape(
            B, S // tk_a, tk_a, A_HEADS, A_V).transpose(0, 3, 1, 4, 2)
        cw = C_HEADS * C_DIM
        cvt = proj[:, :, N_MAIN - cw:].reshape(B, S // ca_c, ca_c, cw).transpose(0, 1, 3, 2)
        proj_i = proj_i.reshape(B, S, N_I)
        iwt = proj_i[:, :, 2 * IDX_DIM:2 * IDX_DIM + IDX_HEADS].transpose(0, 2, 1)
        o_a = _diff_attn(proj, avt, lamv, bias_a, subln[l].reshape(A_V, 1), lam_init, tq_a,
                         d_min, d_const)
        o_b = _dil_attn(proj_b.reshape(B, S, N_B), bias_b)
        o_c = _dsa_attn(proj, proj_i, iwt, cvt, bias_c, tq_c, ch_c)
        x = _outmem(x, o_a, o_b, o_c, w_out[l].astype(BF16), norm_mem[l].reshape(1, D),
                    w_mq[l].astype(BF16), kv_all, l, w_mo[l].astype(BF16), tm_mem)
        wu = w_up[l].astype(BF16)
        x = _ffn(x, norm_ffn[l].reshape(1, D), wu, wu, conv_w[l][:, :F], conv_w[l][:, F:],
                 conv_b[l][:F].reshape(1, F), conv_b[l][F:].reshape(1, F), w_down[l].astype(BF16),
                 tm_ffn, fc)
    return _final_norm(x.reshape(B * S, D), norm_final.reshape(1, D), tm_proj).reshape(B, S, D)


def kernel(x, mem, rel_bias, norm_mix, w_in, lam_q1, lam_k1, lam_q2, lam_k2, subln, w_out,
           norm_mem, norm_memkv, w_mq, w_mkv, w_mo, norm_ffn, w_up, conv_w, conv_b, w_down,
           norm_final):
    return _forward(x, mem, rel_bias, norm_mix, w_in, lam_q1, lam_k1, lam_q2, lam_k2, subln, w_out,
                    norm_mem, norm_memkv, w_mq, w_mkv, w_mo, norm_ffn, w_up, conv_w, conv_b, w_down,
                    norm_final, tq_a=512, tk_a=256, tq_c=512, ch_c=512, ca_c=256, tm_proj=512, tm_mem=512, tm_ffn=512,
                    fc=512)
```

```python
import functools
import math

import jax
import jax.numpy as jnp
from jax import lax
from jax.experimental import pallas as pl
from jax.experimental.pallas import tpu as pltpu

F32 = jnp.float32
BF16 = jnp.bfloat16

EPS = 1e-6
NEG = -1e30
LOG2E = math.log2(math.e)
LANES = 128
VMEM_LIMIT = 56 * 1024 * 1024

A_HEADS, A_QK, A_V = 4, 64, 128
B_HEADS, B_DIM = 4, 64
C_HEADS, C_DIM = 4, 64
IDX_HEADS, IDX_DIM = 8, 64
TOPK_MAX = 256
DILATED_PATTERNS = ((128, 1), (512, 4), (2048, 16))
ONES_ROWS = 16
DIL_BLK = 128
DIL_UNROLL = 4
REL_BUCKETS, REL_MAX_DIST = 32, 2048
MEM_HEADS = 4
CONV_WIDTH = 3

N_MAIN = 2816
N_B = 768
N_I = 256


def _cparams(sem):
    return pltpu.CompilerParams(dimension_semantics=sem, vmem_limit_bytes=VMEM_LIMIT)


def _dot(a, b):
    return jnp.dot(a, b, preferred_element_type=F32)


def _dot_nt(a, b):
    return lax.dot_general(a, b, (((1,), (1,)), ((), ())), preferred_element_type=F32)


def _ones_rows(n):
    row = lax.broadcasted_iota(jnp.int32, (ONES_ROWS, n), 0)
    return jnp.where(row == 0, 1.0, 0.0).astype(BF16)


def _rms(x, g):
    return x * lax.rsqrt(jnp.mean(x * x, axis=-1, keepdims=True) + EPS) * g


def _rel_bucket(n):
    max_exact = REL_BUCKETS // 2
    nf = jnp.maximum(n, 1).astype(F32)
    large = max_exact + (jnp.log(nf / max_exact) / math.log(REL_MAX_DIST / max_exact)
                         * (REL_BUCKETS - max_exact)).astype(jnp.int32)
    large = jnp.minimum(large, REL_BUCKETS - 1)
    return jnp.where(n < max_exact, n, large)


def _bias_tiles_kernel(par_ref, tab_ref, o_ref, *, tq, tk, scale):
    t = pl.program_id(0)
    head, off, mult = par_ref[t, 0], par_ref[t, 1], par_ref[t, 2]
    lo, hi, sgn = par_ref[t, 3], par_ref[t, 4], par_ref[t, 5]
    dist = off + sgn * (lax.broadcasted_iota(jnp.int32, (tq, tk), 0)
                        - lax.broadcasted_iota(jnp.int32, (tq, tk), 1))
    bucket = _rel_bucket(jnp.maximum(dist * mult, 0))
    val = jnp.zeros((tq, tk), F32)
    for b in range(REL_BUCKETS):
        val = jnp.where(bucket == b, tab_ref[b, head] * scale, val)
    o_ref[...] = jnp.where((dist >= lo) & (dist <= hi), val, NEG)


def _bias_tiles(params, table, tq, tk, scale=1.0):
    n = params.shape[0]
    return pl.pallas_call(
        functools.partial(_bias_tiles_kernel, tq=tq, tk=tk, scale=scale),
        grid=(n,),
        in_specs=[pl.BlockSpec(memory_space=pltpu.SMEM), pl.BlockSpec(memory_space=pltpu.SMEM)],
        out_specs=pl.BlockSpec((None, tq, tk), lambda t: (t, 0, 0)),
        out_shape=jax.ShapeDtypeStruct((n, tq, tk), F32),
        compiler_params=_cparams(("arbitrary",)),
        name="bias_tiles",
    )(params, table)


def _inproj_kernel(x_ref, g_ref, w_ref, om_ref, ob_ref, oi_ref):
    h = _rms(x_ref[...], g_ref[...]).astype(BF16)
    step = 512
    for c in range(0, N_MAIN, step):
        e = min(c + step, N_MAIN)
        om_ref[:, c:e] = _dot(h, w_ref[:, c:e]).astype(BF16)
    ob_ref[...] = _dot(h, w_ref[:, N_MAIN:N_MAIN + N_B])
    oi_ref[...] = _dot(h, w_ref[:, N_MAIN + N_B:])


def _inproj(x2, g, w, tm):
    T, D = x2.shape
    n_all = N_MAIN + N_B + N_I
    return pl.pallas_call(
        _inproj_kernel,
        grid=(T // tm,),
        in_specs=[pl.BlockSpec((tm, D), lambda i: (i, 0)),
                  pl.BlockSpec((1, D), lambda i: (0, 0)),
                  pl.BlockSpec((D, n_all), lambda i: (0, 0))],
        out_specs=[pl.BlockSpec((tm, N_MAIN), lambda i: (i, 0)),
                   pl.BlockSpec((tm, N_B), lambda i: (i, 0)),
                   pl.BlockSpec((tm, N_I), lambda i: (i, 0))],
        out_shape=[jax.ShapeDtypeStruct((T, N_MAIN), BF16),
                   jax.ShapeDtypeStruct((T, N_B), F32),
                   jax.ShapeDtypeStruct((T, N_I), F32)],
        compiler_params=_cparams(("parallel",)),
        name="inproj",
    )(x2, g, w)


def _diff_attn_kernel(lam_ref, q_ref, k_ref, vt_ref, bias_ref, g_ref, o_ref, s_sc, acc_sc,
                      *, tq, tk, d_min, d_const, lam_init):
    qi = pl.program_id(2)
    gran = min(tq, tk)
    last = (qi * tq + tq - 1) // tk
    masked_tile = d_const - d_min + 1
    lv = lam_ref[...]
    lam = (jnp.exp(jnp.sum(lv[0:1] * lv[1:2], axis=-1, keepdims=True))
           - jnp.exp(jnp.sum(lv[2:3] * lv[3:4], axis=-1, keepdims=True)) + lam_init)
    lane = lax.broadcasted_iota(jnp.int32, (tq, LANES), 1)
    q = q_ref[...].astype(F32) * (A_QK ** -0.5 * LOG2E)
    qcat = jnp.concatenate([jnp.where(lane < A_QK, q, 0.0), jnp.where(lane >= A_QK, q, 0.0)],
                           axis=0).astype(BF16)
    acc_sc[...] = jnp.zeros(acc_sc.shape, F32)

    def scores_into(slot, j):
        jc = jnp.minimum(j, last)
        k = k_ref[pl.ds(pl.multiple_of(jc * tk, tk), tk), :]
        d = qi * (tq // gran) - jc * (tk // gran)
        bias = bias_ref[jnp.where(j <= last, jnp.minimum(d, d_const) - d_min, masked_tile)]
        s_sc[slot] = _dot_nt(k, qcat) + jnp.concatenate([bias, bias], axis=1)

    def absorb(slot, j, m_prev):
        m_new = jnp.maximum(m_prev, jnp.max(s_sc[slot], axis=0, keepdims=True))
        alpha = jnp.exp2(m_prev - m_new)
        p = jnp.exp2(s_sc[slot] - m_new)
        vt = jnp.concatenate([vt_ref[jnp.minimum(j, last)], _ones_rows(tk)], axis=0)
        acc_sc[...] = alpha * acc_sc[...] + _dot(vt, p.astype(BF16))
        return m_new

    scores_into(0, 0)

    def body(jj, m):
        a = 2 * jj
        scores_into(1, a + 1)
        m = absorb(0, a, m)
        scores_into(0, a + 2)
        return absorb(1, a + 1, m)

    lax.fori_loop(0, (last + 2) // 2, body, jnp.full((1, 2 * tq), NEG, F32))
    inv = 1.0 / acc_sc[A_V:A_V + 1, :]
    o = acc_sc[:A_V, :tq] * inv[:, :tq] - lam * (acc_sc[:A_V, tq:] * inv[:, tq:])
    o = o * lax.rsqrt(jnp.mean(o * o, axis=0, keepdims=True) + EPS) * g_ref[...] * (1.0 - lam_init)
    o_ref[...] = o.T.astype(o_ref.dtype)


def _diff_attn(proj, vt, lamv, bias_a, subln_g, lam_init, tq, d_min, d_const):
    B, S, _ = proj.shape
    nd, tk = bias_a.shape[1], bias_a.shape[2]
    return pl.pallas_call(
        functools.partial(_diff_attn_kernel, tq=tq, tk=tk, d_min=d_min, d_const=d_const,
                          lam_init=lam_init),
        grid=(A_HEADS, B, S // tq),
        in_specs=[pl.BlockSpec((4, A_QK), lambda h, b, i: (0, 0)),
                  pl.BlockSpec((None, tq, LANES), lambda h, b, i: (b, i, h)),
                  pl.BlockSpec((None, S, LANES), lambda h, b, i: (b, 0, A_HEADS + h)),
                  pl.BlockSpec((None, None, S // tk, A_V, tk), lambda h, b, i: (b, h, 0, 0, 0)),
                  pl.BlockSpec((None, nd, tk, tq), lambda h, b, i: (h, 0, 0, 0)),
                  pl.BlockSpec((A_V, 1), lambda h, b, i: (0, 0))],
        out_specs=pl.BlockSpec((None, tq, LANES), lambda h, b, i: (b, i, h)),
        out_shape=jax.ShapeDtypeStruct((B, S, A_HEADS * A_V), BF16),
        scratch_shapes=[pltpu.VMEM((2, tk, 2 * tq), F32),
                        pltpu.VMEM((A_V + ONES_ROWS, 2 * tq), F32)],
        compiler_params=_cparams(("parallel", "parallel", "arbitrary")),
        name="diff_attn",
    )(lamv, proj, proj, vt, bias_a, subln_g)


class _LaneHalves:
    def __init__(self, *refs):
        self.refs = refs

    def __getitem__(self, idx):
        return jnp.concatenate([r[idx] for r in self.refs], axis=1)

    def __setitem__(self, idx, val):
        for i, r in enumerate(self.refs):
            r[idx] = val[:, i * LANES:(i + 1) * LANES]


def _dil_kernel(q0, q1, k0, k1, v0, v1, bias_ref, o_ref, n0, n1, m0, m1, l0, l1, *, seq, patterns):
    q_ref, k_ref, v_ref = _LaneHalves(q0, q1), _LaneHalves(k0, k1), _LaneHalves(v0, v1)
    n_sc, m_sc, l_sc = _LaneHalves(n0, n1), _LaneHalves(m0, m1), _LaneHalves(l0, l1)
    blk = DIL_BLK
    width = B_HEADS * B_DIM
    head_of_lane = lax.broadcasted_iota(jnp.int32, (blk, width), 1) // B_DIM

    def per_head(x):
        parts = [jnp.broadcast_to(x[h * blk:(h + 1) * blk], (blk, width)) for h in range(B_HEADS)]
        out = parts[-1]
        for h in range(B_HEADS - 2, -1, -1):
            out = jnp.where(head_of_lane == h, parts[h], out)
        return out

    def attend(base, r, with_prev, p_idx):
        q = q_ref[pl.ds(base, blk, stride=r), :] * (B_DIM ** -0.5 * LOG2E)
        qcat = jnp.concatenate([jnp.where(head_of_lane == h, q, 0.0) for h in range(B_HEADS)],
                               axis=0).astype(BF16)
        if with_prev:
            keys = pl.ds(base - blk * r, 2 * blk, stride=r)
            bias = bias_ref[p_idx]
        else:
            keys = pl.ds(base, blk, stride=r)
            bias = bias_ref[p_idx, :, blk:]
        s = _dot_nt(qcat, k_ref[keys, :].astype(BF16)) + bias
        mx = jnp.max(s, axis=-1, keepdims=True)
        p = jnp.exp2(s - mx)
        l = jnp.sum(p, axis=-1, keepdims=True)
        o = _dot(p.astype(BF16), v_ref[keys, :].astype(BF16))
        return per_head(o), per_head(mx), per_head(l)

    def merge(base, r, res, is_first):
        o, m, l = res
        rows = pl.ds(base, blk, stride=r)
        if is_first:
            n_sc[rows, :] = o
            m_sc[rows, :] = m
            l_sc[rows, :] = l
        else:
            m_old = m_sc[rows, :]
            m_new = jnp.maximum(m_old, m)
            a = jnp.exp2(m_old - m_new)
            b = jnp.exp2(m - m_new)
            n_sc[rows, :] = a * n_sc[rows, :] + b * o
            l_sc[rows, :] = a * l_sc[rows, :] + b * l
            m_sc[rows, :] = m_new

    def sweep(count, base_of, r, with_prev, p_idx):
        def trip(width):
            def body(i, carry):
                bases = [base_of(i * width + u) for u in range(width)]
                results = [attend(b, r, with_prev, p_idx) for b in bases]
                for b, res in zip(bases, results):
                    merge(b, r, res, p_idx == len(patterns) - 1)
                return carry
            return body

        full = count // DIL_UNROLL
        lax.fori_loop(0, full, trip(DIL_UNROLL), 0)
        rest = count - full * DIL_UNROLL
        if rest:
            lax.fori_loop(full * DIL_UNROLL, count, trip(1), 0)

    for p_idx, (_, r) in reversed(list(enumerate(patterns))):
        nb = seq // (r * blk)
        sweep(r, lambda c: c, r, False, p_idx)
        sweep(r * (nb - 1), lambda idx, r=r: (idx // r + 1) * blk * r + idx % r, r, True, p_idx)

    everything = (slice(None), slice(None))
    o_ref[...] = (n_sc[everything] / l_sc[everything]).astype(o_ref.dtype)


def _dil_attn(proj_b, bias_b):
    B, S, _ = proj_b.shape
    width = B_HEADS * B_DIM
    return pl.pallas_call(
        functools.partial(_dil_kernel, seq=S, patterns=DILATED_PATTERNS),
        grid=(B,),
        in_specs=[pl.BlockSpec((None, S, LANES), functools.partial(lambda i, b: (b, 0, i), i))
                  for i in range(3 * width // LANES)]
                 + [pl.BlockSpec(bias_b.shape, lambda b: (0, 0, 0))],
        out_specs=pl.BlockSpec((None, S, width), lambda b: (b, 0, 0)),
        out_shape=jax.ShapeDtypeStruct((B, S, width), BF16),
        scratch_shapes=[pltpu.VMEM((S, LANES), F32)] * (3 * width // LANES),
        compiler_params=_cparams(("parallel",)),
        name="dilated_attn",
    )(*([proj_b] * (3 * width // LANES)), bias_b)


SEL16 = jnp.bfloat16


RADIX_LOW_BITS = 5
PEEL_CAP = 6
FLT_MIN_NORMAL = 2.0 ** -126


def _float_key(f):
    b = pltpu.bitcast(f, jnp.int32)
    return jnp.where(b >= 0, b, b ^ jnp.int32(0x7FFFFFFF))


def _key_float(k):
    return pltpu.bitcast(jnp.where(k >= 0, k, k ^ jnp.int32(0x7FFFFFFF)), F32)


def _high_half(f):
    return pltpu.bitcast(pltpu.bitcast(f, jnp.int32) & jnp.int32(-65536), F32)


def _dsa_kernel(iq_ref, ik_ref, iwt_ref, cq_ref, ck_ref, cvt_ref, bias_ref, o_ref, sc_ref, hi_ref, s_sc,
                acc_sc,
                *, tq, ch, ca, topk):
    qi = pl.program_id(1)
    q0 = qi * tq
    nch = (q0 + tq - 1) // ch + 1
    qidx = q0 + lax.broadcasted_iota(jnp.int32, (1, tq), 1)
    lane = lax.broadcasted_iota(jnp.int32, (tq, LANES), 1)
    first = lane < IDX_DIM

    w = iwt_ref[...] * (IDX_HEADS ** -0.5)
    qs = []
    for h in range(IDX_HEADS):
        tile = iq_ref[:, (h // 2) * LANES:(h // 2 + 1) * LANES].astype(F32) * (IDX_DIM ** -0.5)
        qs.append(jnp.where(first if h % 2 == 0 else ~first, tile, 0.0).astype(BF16))

    def score_body(j, carry):
        start = pl.multiple_of(j * ch, ch)
        kk = ik_ref[pl.ds(start, ch), :].astype(BF16)
        sc = jnp.zeros((ch, tq), F32)
        for h in range(IDX_HEADS):
            sc = sc + jnp.maximum(_dot_nt(kk, qs[h]), 0.0) * w[h:h + 1, :]
        kidx = start + lax.broadcasted_iota(jnp.int32, (ch, 1), 0)
        sc = jnp.where(kidx <= qidx, sc, NEG)
        sc_ref[j] = sc
        hi_ref[j] = _high_half(sc).astype(SEL16)
        return carry

    lax.fori_loop(0, nch, score_body, 0)

    rows = 32

    def count_where(pred):
        def body(j, acc):
            hit = jnp.where(pred(sc_ref[j]), 1.0, 0.0)
            return acc + jnp.sum(hit.reshape(ch // rows, rows, tq), axis=0)
        acc = lax.fori_loop(0, nch, body, jnp.zeros((rows, tq), F32))
        return jnp.sum(acc, axis=0, keepdims=True)

    def count_high_ge(v):
        one, zero16 = jnp.ones((), SEL16), jnp.zeros((), SEL16)
        def body(j, acc):
            hit = jnp.where(hi_ref[j] >= v, one, zero16).reshape(ch // rows, rows, tq)
            parts = [hit[i] for i in range(ch // rows)]
            while len(parts) > 1:
                parts = [parts[i] + parts[i + 1] for i in range(0, len(parts), 2)]
            return acc + parts[0]
        acc = lax.fori_loop(0, nch, body, jnp.zeros((rows, tq), SEL16))
        return jnp.sum(acc.astype(F32), axis=0, keepdims=True)

    kf = float(topk)
    n_valid = (qidx + 1).astype(F32)
    half_bits = 16

    def high_body(i, st):
        u, c_lo = st
        cand = u | jnp.left_shift(1, half_bits - 1 - i)
        v = _high_half(_key_float((cand - (1 << (half_bits - 1))) << half_bits)).astype(SEL16)
        c = count_high_ge(v)
        keep = c >= kf
        return jnp.where(keep, cand, u), jnp.where(keep, c, c_lo)

    u, c_lo = lax.fori_loop(0, half_bits, high_body, (jnp.zeros((1, tq), jnp.int32), n_valid))
    key_hi = (u - (1 << (half_bits - 1))) << half_bits

    def low_body(i, st):
        lo_bits, c_lo = st
        cand = lo_bits | jnp.left_shift(1, half_bits - 1 - i)
        c = count_where(lambda sc: sc >= _key_float(key_hi | cand))
        keep = c >= kf
        return jnp.where(keep, cand, lo_bits), jnp.where(keep, c, c_lo)

    lo_bits, c_lo = lax.fori_loop(0, RADIX_LOW_BITS, low_body, (jnp.zeros((1, tq), jnp.int32), c_lo))

    def min_ge(t):
        def body(j, acc):
            sc = sc_ref[j]
            x = jnp.where(sc >= t, sc, -NEG)
            return jnp.minimum(acc, jnp.min(x.reshape(ch // rows, rows, tq), axis=0))
        acc = lax.fori_loop(0, nch, body, jnp.full((rows, tq), -NEG, F32))
        return jnp.min(acc, axis=0, keepdims=True)

    def open_rows(c_lo, done):
        return jnp.logical_and(jnp.logical_and(n_valid > kf, c_lo > kf), done == 0.0)

    def any_row(mask):
        return jnp.max(jnp.where(mask, 1.0, 0.0)) > 0.0

    def peel_cond(st):
        _, c_lo, done, it = st
        return jnp.logical_and(it < PEEL_CAP, any_row(open_rows(c_lo, done)))

    def peel_body(st):
        t, c_lo, done, it = st
        is_open = open_rows(c_lo, done)
        smallest = min_ge(t)
        above = jnp.where(smallest == 0.0, FLT_MIN_NORMAL, _key_float(_float_key(smallest) + 1))
        c = count_where(lambda sc: sc >= above)
        enough = jnp.logical_and(is_open, c >= kf)
        at_tie = jnp.logical_and(is_open, c < kf)
        t = jnp.where(enough, above, jnp.where(at_tie, smallest, t))
        return t, jnp.where(enough, c, c_lo), jnp.where(at_tie, 1.0, done), it + 1

    t_peel, c_peel, done, _ = lax.while_loop(
        peel_cond, peel_body,
        (_key_float(key_hi | lo_bits), c_lo, jnp.zeros((1, tq), F32), jnp.int32(0)))

    def finish_by_radix(_):
        bits, c = lax.fori_loop(RADIX_LOW_BITS, half_bits, low_body, (lo_bits, c_lo))
        return _key_float(key_hi | bits), c

    thr, c_lo = lax.cond(any_row(open_rows(c_peel, done)), finish_by_radix,
                         lambda _: (t_peel, c_peel), 0)
    thr = jnp.where(n_valid > kf, thr, 0.5 * NEG)

    tie_rows = jnp.logical_and(n_valid > kf, c_lo > kf)

    @pl.when(jnp.max(jnp.where(tie_rows, 1.0, 0.0)) > 0.0)
    def _():
        need = kf - count_where(lambda sc: sc > thr)
        lower = (lax.broadcasted_iota(jnp.int32, (ch, ch), 1)
                 <= lax.broadcasted_iota(jnp.int32, (ch, ch), 0)).astype(BF16)

        def drop_body(j, seen):
            sc = sc_ref[j]
            eq = sc == thr
            eqf = jnp.where(eq, 1.0, 0.0)
            rank = _dot(lower, eqf.astype(BF16)) + seen
            drop = jnp.logical_and(jnp.logical_and(eq, rank > need), tie_rows)
            sc_ref[j] = jnp.where(drop, NEG, sc)
            return seen + jnp.sum(eqf, axis=0, keepdims=True)

        lax.fori_loop(0, nch, drop_body, jnp.zeros((1, tq), F32))

    cw = C_HEADS * C_DIM
    lane_c = lax.broadcasted_iota(jnp.int32, (tq, cw), 1)
    cq = cq_ref[...].astype(F32) * (C_DIM ** -0.5 * LOG2E)
    qcat = jnp.concatenate([jnp.where(lane_c // C_DIM == h, cq, 0.0) for h in range(C_HEADS)],
                           axis=0).astype(BF16)
    acc_sc[...] = jnp.zeros(acc_sc.shape, F32)
    nd = bias_ref.shape[1]
    last = (q0 + tq - 1) // ca

    def scores_into(slot, j):
        jc = jnp.minimum(j, last)
        kc = ck_ref[pl.ds(pl.multiple_of(jc * ca, ca), ca), :]
        sub = pl.multiple_of((jc % (ch // ca)) * ca, ca)
        sel = sc_ref[jc // (ch // ca), pl.ds(sub, ca), :] >= jnp.where(j <= last, thr, -NEG)
        bias = jnp.concatenate([
            jnp.concatenate([
                bias_ref[h, jnp.clip(q0 // LANES + c - (jc * (ca // LANES) + u), 0, nd - 1)]
                for h in range(C_HEADS) for c in range(tq // LANES)], axis=1)
            for u in range(ca // LANES)], axis=0)
        s_sc[slot] = jnp.where(jnp.concatenate([sel] * C_HEADS, axis=1),
                               _dot_nt(kc, qcat) + bias, NEG)

    def absorb(slot, j, m_prev):
        m_new = jnp.maximum(m_prev, jnp.max(s_sc[slot], axis=0, keepdims=True))
        alpha = jnp.exp2(m_prev - m_new)
        pb = jnp.exp2(s_sc[slot] - m_new).astype(BF16)
        jc = jnp.minimum(j, last)
        for h in range(C_HEADS):
            cols = slice(h * tq, (h + 1) * tq)
            vt = jnp.concatenate([cvt_ref[jc, h * C_DIM:(h + 1) * C_DIM, :], _ones_rows(ca)], axis=0)
            acc_sc[h] = alpha[:, cols] * acc_sc[h] + _dot(vt, pb[:, cols])
        return m_new

    scores_into(0, 0)

    def attn_body(jj, m):
        a = 2 * jj
        scores_into(1, a + 1)
        m = absorb(0, a, m)
        scores_into(0, a + 2)
        return absorb(1, a + 1, m)

    lax.fori_loop(0, (last + 2) // 2, attn_body, jnp.full((1, C_HEADS * tq), NEG, F32))
    o_t = jnp.concatenate([acc_sc[h, :C_DIM] * (1.0 / acc_sc[h, C_DIM:C_DIM + 1])
                           for h in range(C_HEADS)], axis=0)
    o_ref[...] = o_t.T.astype(o_ref.dtype)


def _dsa_attn(proj, proj_i, iwt, cvt, bias_c, tq, ch):
    B, S, _ = proj.shape
    ca = cvt.shape[-1]
    topk = min(TOPK_MAX, S // 4)
    nd = bias_c.shape[1]
    cw = C_HEADS * C_DIM
    return pl.pallas_call(
        functools.partial(_dsa_kernel, tq=tq, ch=ch, ca=ca, topk=topk),
        grid=(B, S // tq),
        in_specs=[pl.BlockSpec((None, tq, IDX_HEADS * IDX_DIM), lambda b, i: (b, i, 3)),
                  pl.BlockSpec((None, S, LANES), lambda b, i: (b, 0, 0)),
                  pl.BlockSpec((None, IDX_HEADS, tq), lambda b, i: (b, 0, i)),
                  pl.BlockSpec((None, tq, cw), lambda b, i: (b, i, 8)),
                  pl.BlockSpec((None, S, cw), lambda b, i: (b, 0, 9)),
                  pl.BlockSpec((None, S // ca, cw, ca), lambda b, i: (b, 0, 0, 0)),
                  pl.BlockSpec((C_HEADS, nd, LANES, LANES), lambda b, i: (0, 0, 0, 0))],
        out_specs=pl.BlockSpec((None, tq, cw), lambda b, i: (b, i, 0)),
        out_shape=jax.ShapeDtypeStruct((B, S, cw), BF16),
        scratch_shapes=[pltpu.VMEM((S // ch, ch, tq), F32),
                        pltpu.VMEM((S // ch, ch, tq), SEL16),
                        pltpu.VMEM((2, ca, C_HEADS * tq), F32),
                        pltpu.VMEM((C_HEADS, C_DIM + ONES_ROWS, tq), F32)],
        compiler_params=_cparams(("parallel", "arbitrary")),
        name="dsa_attn",
    )(proj, proj_i, iwt, proj, proj, cvt, bias_c)


def _memkv_kernel(mem_ref, g_ref, w_ref, o_ref):
    h = _rms(mem_ref[...], g_ref[...]).astype(BF16)
    o_ref[...] = _dot(h, w_ref[...]).astype(o_ref.dtype)


def _memkv(mem, g, w):
    B, M, D = mem.shape
    L = w.shape[0]
    return pl.pallas_call(
        _memkv_kernel,
        grid=(L, B),
        in_specs=[pl.BlockSpec((None, M, D), lambda l, b: (b, 0, 0)),
                  pl.BlockSpec((None, 1, D), lambda l, b: (l, 0, 0)),
                  pl.BlockSpec((None, D, 2 * D), lambda l, b: (l, 0, 0))],
        out_specs=pl.BlockSpec((None, None, M, 2 * D), lambda l, b: (l, b, 0, 0)),
        out_shape=jax.ShapeDtypeStruct((L, B, M, 2 * D), BF16),
        compiler_params=_cparams(("parallel", "arbitrary")),
        name="mem_kv",
    )(mem, g, w)


def _outmem_kernel(x_ref, oa_ref, ob_ref, oc_ref, wo_ref, g_ref, wq_ref, kv_ref, wmo_ref, o_ref):
    D = x_ref.shape[-1]
    na, nb = oa_ref.shape[-1], ob_ref.shape[-1]
    x = (x_ref[...] + _dot(oa_ref[...], wo_ref[0:na]) + _dot(ob_ref[...], wo_ref[na:na + nb])
         + _dot(oc_ref[...], wo_ref[na + nb:]))
    h = _rms(x, g_ref[...]).astype(BF16)
    hd = D // MEM_HEADS
    q = (_dot(h, wq_ref[...]) * (hd ** -0.5)).astype(BF16)
    outs = []
    for hh in range(MEM_HEADS):
        k = kv_ref[:, hh * hd:(hh + 1) * hd]
        v = kv_ref[:, D + hh * hd:D + (hh + 1) * hd]
        s = _dot_nt(q[:, hh * hd:(hh + 1) * hd], k)
        p = jnp.exp(s - jnp.max(s, axis=-1, keepdims=True))
        p = p / jnp.sum(p, axis=-1, keepdims=True)
        outs.append(_dot(p.astype(BF16), v).astype(BF16))
    o = jnp.concatenate(outs, axis=-1)
    o_ref[...] = x + _dot(o, wmo_ref[...])


def _outmem(x, oa, ob, oc, wo, g, wq, kv, layer, wmo, tm):
    B, S, D = x.shape
    M = kv.shape[2]
    const = lambda b, i: (0, 0)
    return pl.pallas_call(
        _outmem_kernel,
        grid=(B, S // tm),
        in_specs=[pl.BlockSpec((None, tm, D), lambda b, i: (b, i, 0)),
                  pl.BlockSpec((None, tm, oa.shape[-1]), lambda b, i: (b, i, 0)),
                  pl.BlockSpec((None, tm, ob.shape[-1]), lambda b, i: (b, i, 0)),
                  pl.BlockSpec((None, tm, oc.shape[-1]), lambda b, i: (b, i, 0)),
                  pl.BlockSpec(wo.shape, const),
                  pl.BlockSpec((1, D), const),
                  pl.BlockSpec(wq.shape, const),
                  pl.BlockSpec((None, None, M, 2 * D), lambda b, i: (layer, b, 0, 0)),
                  pl.BlockSpec(wmo.shape, const)],
        out_specs=pl.BlockSpec((None, tm, D), lambda b, i: (b, i, 0)),
        out_shape=jax.ShapeDtypeStruct((B, S, D), F32),
        compiler_params=_cparams(("parallel", "arbitrary")),
        name="outproj_memattn",
    )(x, oa, ob, oc, wo, g, wq, kv, wmo)


HALO = 8


def _ffn_kernel(x_ref, xp_ref, g_ref, wg_ref, wv_ref, cwg_ref, cwv_ref, cbg_ref, cbv_ref, wd_ref,
                o_ref, *, tm, fc):
    i = pl.program_id(1)
    x = x_ref[...]
    g = g_ref[...]
    hp = _rms(xp_ref[...], g) * jnp.where(i > 0, 1.0, 0.0)
    h = jnp.concatenate([hp, _rms(x, g)], axis=0).astype(BF16)
    F = wd_ref.shape[0]

    def conv(u, cw_ref, cb_ref, c, e):
        out = cb_ref[:, c:e]
        for j in range(CONV_WIDTH):
            shift = CONV_WIDTH - 1 - j
            out = out + cw_ref[j:j + 1, c:e] * u[HALO - shift:HALO - shift + tm]
        return out

    acc = x
    for c in range(0, F, fc):
        e = min(c + fc, F)
        gate = conv(_dot(h, wg_ref[:, c:e]), cwg_ref, cbg_ref, c, e)
        val = conv(_dot(h, wv_ref[:, c:e]), cwv_ref, cbv_ref, c, e)
        act = (gate * jax.nn.sigmoid(gate) * val).astype(BF16)
        acc = acc + _dot(act, wd_ref[c:e, :])
    o_ref[...] = acc


def _ffn(x, g, wg, wv, cwg, cwv, cbg, cbv, wd, tm, fc):
    B, S, D = x.shape
    F = wd.shape[0]
    const = lambda b, i: (0, 0)
    hb = tm // HALO
    return pl.pallas_call(
        functools.partial(_ffn_kernel, tm=tm, fc=fc),
        grid=(B, S // tm),
        in_specs=[pl.BlockSpec((None, tm, D), lambda b, i: (b, i, 0)),
                  pl.BlockSpec((None, HALO, D), lambda b, i: (b, jnp.maximum(i * hb - 1, 0), 0)),
                  pl.BlockSpec((1, D), const),
                  pl.BlockSpec((D, F), const, pipeline_mode=pl.Buffered(1)),
                  pl.BlockSpec((D, F), lambda b, i: (0, 1), pipeline_mode=pl.Buffered(1)),
                  pl.BlockSpec((CONV_WIDTH, F), const), pl.BlockSpec((CONV_WIDTH, F), const),
                  pl.BlockSpec((1, F), const), pl.BlockSpec((1, F), const),
                  pl.BlockSpec((F, D), const, pipeline_mode=pl.Buffered(1))],
        out_specs=pl.BlockSpec((None, tm, D), lambda b, i: (b, i, 0)),
        out_shape=jax.ShapeDtypeStruct((B, S, D), F32),
        compiler_params=_cparams(("parallel", "arbitrary")),
        name="conv_ffn",
    )(x, x, g, wg, wv, cwg, cwv, cbg, cbv, wd)


def _final_norm_kernel(x_ref, g_ref, o_ref):
    o_ref[...] = _rms(x_ref[...], g_ref[...])


def _final_norm(x2, g, tm):
    T, D = x2.shape
    return pl.pallas_call(
        _final_norm_kernel,
        grid=(T // tm,),
        in_specs=[pl.BlockSpec((tm, D), lambda i: (i, 0)), pl.BlockSpec((1, D), lambda i: (0, 0))],
        out_specs=pl.BlockSpec((tm, D), lambda i: (i, 0)),
        out_shape=jax.ShapeDtypeStruct((T, D), F32),
        compiler_params=_cparams(("parallel",)),
        name="final_norm",
    )(x2, g)


def _tile_params(S, tq_a, tk_a):
    big = 1 << 30
    gran, (d_min, d_const) = min(tq_a, tk_a), _a_offsets(tq_a, tk_a)
    pa = []
    for h in range(A_HEADS):
        pa += [(h, d * gran, 1, 0, big, -1) for d in range(d_min, d_const + 1)]
        pa.append((h, 0, 1, 1, 0, -1))
    pb = []
    for _, r in DILATED_PATTERNS:
        for h in range(B_HEADS):
            pb.append((A_HEADS + h, DIL_BLK, r, 1, DIL_BLK, 1))
            pb.append((A_HEADS + h, 0, r, 0, DIL_BLK, 1))
    pc = [(A_HEADS + B_HEADS + h, d * LANES, 1, -big, big, -1)
          for h in range(C_HEADS) for d in range(_n_offsets(S, LANES))]
    to = lambda p: jnp.asarray(p, jnp.int32)
    return to(pa), to(pb), to(pc)


def _dil_bias_layout(tiles):
    t = tiles.reshape(len(DILATED_PATTERNS), B_HEADS, 2, DIL_BLK, DIL_BLK)
    return jnp.concatenate([t[:, :, 0], t[:, :, 1]], axis=-1).reshape(
        len(DILATED_PATTERNS), B_HEADS * DIL_BLK, 2 * DIL_BLK)


def _n_offsets(S, blk):
    return min(S // blk, REL_MAX_DIST // blk + 2)


def _a_offsets(tq, tk):
    gran = min(tq, tk)
    return -(tq // gran - 1), -(-(REL_MAX_DIST - 1 + tk) // gran)


def _in_weights(w_in_l):
    sizes = (512, 512, 512, 256, 256, 256, 256, 256, 256, 512, 64, 8)
    offs = [0]
    for s in sizes:
        offs.append(offs[-1] + s)
    (aq, ak, av, bq, bk, bv, cq, ck, cv, iq, ik, iw) = [w_in_l[:, offs[i]:offs[i + 1]] for i in range(12)]
    pad = jnp.zeros((w_in_l.shape[0], N_I - 2 * IDX_DIM - IDX_HEADS), w_in_l.dtype)
    return jnp.concatenate([aq, ak, av, iq, cq, ck, cv, bq, bk, bv, ik, ik, iw, pad], axis=1).astype(BF16)


def _forward(x, mem, rel_bias, norm_mix, w_in, lam_q1, lam_k1, lam_q2, lam_k2, subln, w_out,
             norm_mem, norm_memkv, w_mq, w_mkv, w_mo, norm_ffn, w_up, conv_w, conv_b, w_down,
             norm_final, *, tq_a, tk_a, tq_c, ch_c, ca_c, tm_proj, tm_mem, tm_ffn, fc):
    B, S, D = x.shape
    L = w_in.shape[0]
    F = w_down.shape[1]
    pa, pb, pc = _tile_params(S, tq_a, tk_a)
    bias_a = _bias_tiles(pa, rel_bias, tk_a, tq_a, LOG2E).reshape(A_HEADS, -1, tk_a, tq_a)
    d_min, d_const = _a_offsets(tq_a, tk_a)
    bias_b = _dil_bias_layout(_bias_tiles(pb, rel_bias, DIL_BLK, DIL_BLK, LOG2E))
    bias_c = _bias_tiles(pc, rel_bias, LANES, LANES, LOG2E).reshape(C_HEADS, -1, LANES, LANES)
    kv_all = _memkv(mem, norm_memkv.reshape(L, 1, D), w_mkv.astype(BF16))

    for l in range(L):
        lam_init = 0.8 - 0.6 * math.exp(-0.3 * l)
        proj, proj_b, proj_i = _inproj(x.reshape(B * S, D), norm_mix[l].reshape(1, D),
                                       _in_weights(w_in[l]), tm_proj)
        proj = proj.reshape(B, S, N_MAIN)
        lamv = jnp.stack([lam_q1[l], lam_k1[l], lam_q2[l], lam_k2[l]], axis=0)
        avt = proj[:, :, 2 * A_HEADS * LANES:3 * A_HEADS * LANES].reshape(
            B, S // tk_a, tk_a, A_HEADS, A_V).transpose(0, 3, 1, 4, 2)
        cw = C_HEADS * C_DIM
        cvt = proj[:, :, N_MAIN - cw:].reshape(B, S // ca_c, ca_c, cw).transpose(0, 1, 3, 2)
        proj_i = proj_i.reshape(B, S, N_I)
        iwt = proj_i[:, :, 2 * IDX_DIM:2 * IDX_DIM + IDX_HEADS].transpose(0, 2, 1)
        o_a = _diff_attn(proj, avt, lamv, bias_a, subln[l].reshape(A_V, 1), lam_init, tq_a,
                         d_min, d_const)
        o_b = _dil_attn(proj_b.reshape(B, S, N_B), bias_b)
        o_c = _dsa_attn(proj, proj_i, iwt, cvt, bias_c, tq_c, ch_c)
        x = _outmem(x, o_a, o_b, o_c, w_out[l].astype(BF16), norm_mem[l].reshape(1, D),
                    w_mq[l].astype(BF16), kv_all, l, w_mo[l].astype(BF16), tm_mem)
        wu = w_up[l].astype(BF16)
        x = _ffn(x, norm_ffn[l].reshape(1, D), wu, wu, conv_w[l][:, :F], conv_w[l][:, F:],
                 conv_b[l][:F].reshape(1, F), conv_b[l][F:].reshape(1, F), w_down[l].astype(BF16),
                 tm_ffn, fc)
    return _final_norm(x.reshape(B * S, D), norm_final.reshape(1, D), tm_proj).reshape(B, S, D)


def kernel(x, mem, rel_bias, norm_mix, w_in, lam_q1, lam_k1, lam_q2, lam_k2, subln, w_out,
           norm_mem, norm_memkv, w_mq, w_mkv, w_mo, norm_ffn, w_up, conv_w, conv_b, w_down,
           norm_final):
    return _forward(x, mem, rel_bias, norm_mix, w_in, lam_q1, lam_k1, lam_q2, lam_k2, subln, w_out,
                    norm_mem, norm_memkv, w_mq, w_mkv, w_mo, norm_ffn, w_up, conv_w, conv_b, w_down,
                    norm_final, tq_a=512, tk_a=256, tq_c=512, ch_c=512, ca_c=256, tm_proj=1024, tm_mem=1024, tm_ffn=1024,
                    fc=512)
```

```python
import functools
import math

import jax
import jax.numpy as jnp
from jax import lax
from jax.experimental import pallas as pl
from jax.experimental.pallas import tpu as pltpu

F32 = jnp.float32
BF16 = jnp.bfloat16

EPS = 1e-6
NEG = -1e30
LOG2E = math.log2(math.e)
LANES = 128
VMEM_LIMIT = 56 * 1024 * 1024

A_HEADS, A_QK, A_V = 4, 64, 128
B_HEADS, B_DIM = 4, 64
C_HEADS, C_DIM = 4, 64
IDX_HEADS, IDX_DIM = 8, 64
TOPK_MAX = 256
DILATED_PATTERNS = ((128, 1), (512, 4), (2048, 16))
ONES_ROWS = 16
DIL_BLK = 128
DIL_UNROLL = 4
REL_BUCKETS, REL_MAX_DIST = 32, 2048
MEM_HEADS = 4
CONV_WIDTH = 3

N_MAIN = 2816
N_B = 768
N_I = 256


def _cparams(sem):
    return pltpu.CompilerParams(dimension_semantics=sem, vmem_limit_bytes=VMEM_LIMIT)


def _dot(a, b):
    return jnp.dot(a, b, preferred_element_type=F32)


def _dot_nt(a, b):
    return lax.dot_general(a, b, (((1,), (1,)), ((), ())), preferred_element_type=F32)


def _ones_rows(n):
    row = lax.broadcasted_iota(jnp.int32, (ONES_ROWS, n), 0)
    return jnp.where(row == 0, 1.0, 0.0).astype(BF16)


def _rms(x, g):
    return x * lax.rsqrt(jnp.mean(x * x, axis=-1, keepdims=True) + EPS) * g


def _rel_bucket(n):
    max_exact = REL_BUCKETS // 2
    nf = jnp.maximum(n, 1).astype(F32)
    large = max_exact + (jnp.log(nf / max_exact) / math.log(REL_MAX_DIST / max_exact)
                         * (REL_BUCKETS - max_exact)).astype(jnp.int32)
    large = jnp.minimum(large, REL_BUCKETS - 1)
    return jnp.where(n < max_exact, n, large)


def _bias_tiles_kernel(par_ref, tab_ref, o_ref, *, tq, tk, scale):
    t = pl.program_id(0)
    head, off, mult = par_ref[t, 0], par_ref[t, 1], par_ref[t, 2]
    lo, hi, sgn = par_ref[t, 3], par_ref[t, 4], par_ref[t, 5]
    dist = off + sgn * (lax.broadcasted_iota(jnp.int32, (tq, tk), 0)
                        - lax.broadcasted_iota(jnp.int32, (tq, tk), 1))
    bucket = _rel_bucket(jnp.maximum(dist * mult, 0))
    val = jnp.zeros((tq, tk), F32)
    for b in range(REL_BUCKETS):
        val = jnp.where(bucket == b, tab_ref[b, head] * scale, val)
    o_ref[...] = jnp.where((dist >= lo) & (dist <= hi), val, NEG)


def _bias_tiles(params, table, tq, tk, scale=1.0):
    n = params.shape[0]
    return pl.pallas_call(
        functools.partial(_bias_tiles_kernel, tq=tq, tk=tk, scale=scale),
        grid=(n,),
        in_specs=[pl.BlockSpec(memory_space=pltpu.SMEM), pl.BlockSpec(memory_space=pltpu.SMEM)],
        out_specs=pl.BlockSpec((None, tq, tk), lambda t: (t, 0, 0)),
        out_shape=jax.ShapeDtypeStruct((n, tq, tk), F32),
        compiler_params=_cparams(("arbitrary",)),
        name="bias_tiles",
    )(params, table)


def _inproj_kernel(x_ref, g_ref, w_ref, om_ref, ob_ref, oi_ref, avt_ref, cvt_ref, iwt_ref):
    h = _rms(x_ref[...], g_ref[...]).astype(BF16)
    step = 512
    for c in range(0, N_MAIN, step):
        e = min(c + step, N_MAIN)
        om_ref[:, c:e] = _dot(h, w_ref[:, c:e]).astype(BF16)
    ob_ref[...] = _dot(h, w_ref[:, N_MAIN:N_MAIN + N_B])
    oi_ref[...] = _dot(h, w_ref[:, N_MAIN + N_B:])
    tk = avt_ref.shape[-1]
    a0 = 2 * A_HEADS * LANES
    for hd in range(A_HEADS):
        vt = om_ref[:, a0 + hd * A_V:a0 + (hd + 1) * A_V].astype(F32).T
        for j in range(avt_ref.shape[1]):
            avt_ref[hd, j] = vt[:, j * tk:(j + 1) * tk].astype(BF16)
    ca = cvt_ref.shape[-1]
    cw = C_HEADS * C_DIM
    for half in range(cw // LANES):
        c0 = N_MAIN - cw + half * LANES
        vt = om_ref[:, c0:c0 + LANES].astype(F32).T
        for j in range(cvt_ref.shape[0]):
            cvt_ref[j, half * LANES:(half + 1) * LANES, :] = vt[:, j * ca:(j + 1) * ca].astype(BF16)
    iwt_ref[...] = oi_ref[:, 2 * IDX_DIM:2 * IDX_DIM + LANES].T[:IDX_HEADS]


def _inproj(x, g, w, tm, tk_a, ca_c):
    B, S, D = x.shape
    T = B * S
    nt = S // tm
    n_all = N_MAIN + N_B + N_I
    cw = C_HEADS * C_DIM
    return pl.pallas_call(
        _inproj_kernel,
        grid=(T // tm,),
        in_specs=[pl.BlockSpec((tm, D), lambda i: (i, 0)),
                  pl.BlockSpec((1, D), lambda i: (0, 0)),
                  pl.BlockSpec((D, n_all), lambda i: (0, 0))],
        out_specs=[pl.BlockSpec((tm, N_MAIN), lambda i: (i, 0)),
                   pl.BlockSpec((tm, N_B), lambda i: (i, 0)),
                   pl.BlockSpec((tm, N_I), lambda i: (i, 0)),
                   pl.BlockSpec((None, A_HEADS, tm // tk_a, A_V, tk_a),
                                lambda i: (i // nt, 0, i % nt, 0, 0)),
                   pl.BlockSpec((None, tm // ca_c, cw, ca_c), lambda i: (i // nt, i % nt, 0, 0)),
                   pl.BlockSpec((None, IDX_HEADS, tm), lambda i: (i // nt, 0, i % nt))],
        out_shape=[jax.ShapeDtypeStruct((T, N_MAIN), BF16),
                   jax.ShapeDtypeStruct((T, N_B), F32),
                   jax.ShapeDtypeStruct((T, N_I), F32),
                   jax.ShapeDtypeStruct((B, A_HEADS, S // tk_a, A_V, tk_a), BF16),
                   jax.ShapeDtypeStruct((B, S // ca_c, cw, ca_c), BF16),
                   jax.ShapeDtypeStruct((B, IDX_HEADS, S), F32)],
        compiler_params=_cparams(("parallel",)),
        name="inproj",
    )(x.reshape(T, D), g, w)


def _diff_attn_kernel(lam_ref, q_ref, k_ref, vt_ref, bias_ref, g_ref, o_ref, s_sc, acc_sc,
                      *, tq, tk, d_min, d_const, lam_init):
    qi = pl.program_id(2)
    gran = min(tq, tk)
    last = (qi * tq + tq - 1) // tk
    masked_tile = d_const - d_min + 1
    lv = lam_ref[...]
    lam = (jnp.exp(jnp.sum(lv[0:1] * lv[1:2], axis=-1, keepdims=True))
           - jnp.exp(jnp.sum(lv[2:3] * lv[3:4], axis=-1, keepdims=True)) + lam_init)
    lane = lax.broadcasted_iota(jnp.int32, (tq, LANES), 1)
    q = q_ref[...].astype(F32) * (A_QK ** -0.5 * LOG2E)
    qcat = jnp.concatenate([jnp.where(lane < A_QK, q, 0.0), jnp.where(lane >= A_QK, q, 0.0)],
                           axis=0).astype(BF16)
    acc_sc[...] = jnp.zeros(acc_sc.shape, F32)

    def scores_into(slot, j):
        jc = jnp.minimum(j, last)
        k = k_ref[pl.ds(pl.multiple_of(jc * tk, tk), tk), :]
        d = qi * (tq // gran) - jc * (tk // gran)
        bias = bias_ref[jnp.where(j <= last, jnp.minimum(d, d_const) - d_min, masked_tile)]
        s_sc[slot] = _dot_nt(k, qcat) + jnp.concatenate([bias, bias], axis=1)

    def absorb(slot, j, m_prev):
        m_new = jnp.maximum(m_prev, jnp.max(s_sc[slot], axis=0, keepdims=True))
        alpha = jnp.exp2(m_prev - m_new)
        p = jnp.exp2(s_sc[slot] - m_new)
        vt = jnp.concatenate([vt_ref[jnp.minimum(j, last)], _ones_rows(tk)], axis=0)
        acc_sc[...] = alpha * acc_sc[...] + _dot(vt, p.astype(BF16))
        return m_new

    scores_into(0, 0)

    def body(jj, m):
        a = 2 * jj
        scores_into(1, a + 1)
        m = absorb(0, a, m)
        scores_into(0, a + 2)
        return absorb(1, a + 1, m)

    lax.fori_loop(0, (last + 2) // 2, body, jnp.full((1, 2 * tq), NEG, F32))
    inv = 1.0 / acc_sc[A_V:A_V + 1, :]
    o = acc_sc[:A_V, :tq] * inv[:, :tq] - lam * (acc_sc[:A_V, tq:] * inv[:, tq:])
    o = o * lax.rsqrt(jnp.mean(o * o, axis=0, keepdims=True) + EPS) * g_ref[...] * (1.0 - lam_init)
    o_ref[...] = o.T.astype(o_ref.dtype)


def _diff_attn(proj, vt, lamv, bias_a, subln_g, lam_init, tq, d_min, d_const):
    B, S, _ = proj.shape
    nd, tk = bias_a.shape[1], bias_a.shape[2]
    return pl.pallas_call(
        functools.partial(_diff_attn_kernel, tq=tq, tk=tk, d_min=d_min, d_const=d_const,
                          lam_init=lam_init),
        grid=(A_HEADS, B, S // tq),
        in_specs=[pl.BlockSpec((4, A_QK), lambda h, b, i: (0, 0)),
                  pl.BlockSpec((None, tq, LANES), lambda h, b, i: (b, i, h)),
                  pl.BlockSpec((None, S, LANES), lambda h, b, i: (b, 0, A_HEADS + h)),
                  pl.BlockSpec((None, None, S // tk, A_V, tk), lambda h, b, i: (b, h, 0, 0, 0)),
                  pl.BlockSpec((None, nd, tk, tq), lambda h, b, i: (h, 0, 0, 0)),
                  pl.BlockSpec((A_V, 1), lambda h, b, i: (0, 0))],
        out_specs=pl.BlockSpec((None, tq, LANES), lambda h, b, i: (b, i, h)),
        out_shape=jax.ShapeDtypeStruct((B, S, A_HEADS * A_V), BF16),
        scratch_shapes=[pltpu.VMEM((2, tk, 2 * tq), F32),
                        pltpu.VMEM((A_V + ONES_ROWS, 2 * tq), F32)],
        compiler_params=_cparams(("parallel", "parallel", "arbitrary")),
        name="diff_attn",
    )(lamv, proj, proj, vt, bias_a, subln_g)


class _LaneHalves:
    def __init__(self, *refs):
        self.refs = refs

    def __getitem__(self, idx):
        return jnp.concatenate([r[idx] for r in self.refs], axis=1)

    def __setitem__(self, idx, val):
        for i, r in enumerate(self.refs):
            r[idx] = val[:, i * LANES:(i + 1) * LANES]


def _dil_kernel(q0, q1, k0, k1, v0, v1, bias_ref, o_ref, n0, n1, m0, m1, l0, l1, *, seq, patterns):
    q_ref, k_ref, v_ref = _LaneHalves(q0, q1), _LaneHalves(k0, k1), _LaneHalves(v0, v1)
    n_sc, m_sc, l_sc = _LaneHalves(n0, n1), _LaneHalves(m0, m1), _LaneHalves(l0, l1)
    blk = DIL_BLK
    width = B_HEADS * B_DIM
    head_of_lane = lax.broadcasted_iota(jnp.int32, (blk, width), 1) // B_DIM

    def per_head(x):
        parts = [jnp.broadcast_to(x[h * blk:(h + 1) * blk], (blk, width)) for h in range(B_HEADS)]
        out = parts[-1]
        for h in range(B_HEADS - 2, -1, -1):
            out = jnp.where(head_of_lane == h, parts[h], out)
        return out

    def attend(base, r, with_prev, p_idx):
        q = q_ref[pl.ds(base, blk, stride=r), :] * (B_DIM ** -0.5 * LOG2E)
        qcat = jnp.concatenate([jnp.where(head_of_lane == h, q, 0.0) for h in range(B_HEADS)],
                               axis=0).astype(BF16)
        if with_prev:
            keys = pl.ds(base - blk * r, 2 * blk, stride=r)
            bias = bias_ref[p_idx]
        else:
            keys = pl.ds(base, blk, stride=r)
            bias = bias_ref[p_idx, :, blk:]
        s = _dot_nt(qcat, k_ref[keys, :].astype(BF16)) + bias
        mx = jnp.max(s, axis=-1, keepdims=True)
        p = jnp.exp2(s - mx)
        l = jnp.sum(p, axis=-1, keepdims=True)
        o = _dot(p.astype(BF16), v_ref[keys, :].astype(BF16))
        return per_head(o), per_head(mx), per_head(l)

    def merge(base, r, res, is_first):
        o, m, l = res
        rows = pl.ds(base, blk, stride=r)
        if is_first:
            n_sc[rows, :] = o
            m_sc[rows, :] = m
            l_sc[rows, :] = l
        else:
            m_old = m_sc[rows, :]
            m_new = jnp.maximum(m_old, m)
            a = jnp.exp2(m_old - m_new)
            b = jnp.exp2(m - m_new)
            n_sc[rows, :] = a * n_sc[rows, :] + b * o
            l_sc[rows, :] = a * l_sc[rows, :] + b * l
            m_sc[rows, :] = m_new

    def sweep(count, base_of, r, with_prev, p_idx):
        def trip(width):
            def body(i, carry):
                bases = [base_of(i * width + u) for u in range(width)]
                results = [attend(b, r, with_prev, p_idx) for b in bases]
                for b, res in zip(bases, results):
                    merge(b, r, res, p_idx == len(patterns) - 1)
                return carry
            return body

        full = count // DIL_UNROLL
        lax.fori_loop(0, full, trip(DIL_UNROLL), 0)
        rest = count - full * DIL_UNROLL
        if rest:
            lax.fori_loop(full * DIL_UNROLL, count, trip(1), 0)

    for p_idx, (_, r) in reversed(list(enumerate(patterns))):
        nb = seq // (r * blk)
        sweep(r, lambda c: c, r, False, p_idx)
        sweep(r * (nb - 1), lambda idx, r=r: (idx // r + 1) * blk * r + idx % r, r, True, p_idx)

    everything = (slice(None), slice(None))
    o_ref[...] = (n_sc[everything] / l_sc[everything]).astype(o_ref.dtype)


def _dil_attn(proj_b, bias_b):
    B, S, _ = proj_b.shape
    width = B_HEADS * B_DIM
    return pl.pallas_call(
        functools.partial(_dil_kernel, seq=S, patterns=DILATED_PATTERNS),
        grid=(B,),
        in_specs=[pl.BlockSpec((None, S, LANES), functools.partial(lambda i, b: (b, 0, i), i))
                  for i in range(3 * width // LANES)]
                 + [pl.BlockSpec(bias_b.shape, lambda b: (0, 0, 0))],
        out_specs=pl.BlockSpec((None, S, width), lambda b: (b, 0, 0)),
        out_shape=jax.ShapeDtypeStruct((B, S, width), BF16),
        scratch_shapes=[pltpu.VMEM((S, LANES), F32)] * (3 * width // LANES),
        compiler_params=_cparams(("parallel",)),
        name="dilated_attn",
    )(*([proj_b] * (3 * width // LANES)), bias_b)


SEL16 = jnp.bfloat16


RADIX_LOW_BITS = 5
PEEL_CAP = 6
FLT_MIN_NORMAL = 2.0 ** -126


def _float_key(f):
    b = pltpu.bitcast(f, jnp.int32)
    return jnp.where(b >= 0, b, b ^ jnp.int32(0x7FFFFFFF))


def _key_float(k):
    return pltpu.bitcast(jnp.where(k >= 0, k, k ^ jnp.int32(0x7FFFFFFF)), F32)


def _high_half(f):
    return pltpu.bitcast(pltpu.bitcast(f, jnp.int32) & jnp.int32(-65536), F32)


def _dsa_kernel(iq_ref, ik_ref, iwt_ref, cq_ref, ck_ref, cvt_ref, bias_ref, o_ref, sc_ref, hi_ref, s_sc,
                acc_sc,
                *, tq, ch, ca, topk):
    qi = pl.program_id(1)
    q0 = qi * tq
    nch = (q0 + tq - 1) // ch + 1
    qidx = q0 + lax.broadcasted_iota(jnp.int32, (1, tq), 1)
    lane = lax.broadcasted_iota(jnp.int32, (tq, LANES), 1)
    first = lane < IDX_DIM

    w = iwt_ref[...] * (IDX_HEADS ** -0.5)
    qs = []
    for h in range(IDX_HEADS):
        tile = iq_ref[:, (h // 2) * LANES:(h // 2 + 1) * LANES].astype(F32) * (IDX_DIM ** -0.5)
        qs.append(jnp.where(first if h % 2 == 0 else ~first, tile, 0.0).astype(BF16))

    def score_body(j, carry):
        start = pl.multiple_of(j * ch, ch)
        kk = ik_ref[pl.ds(start, ch), :].astype(BF16)
        sc = jnp.zeros((ch, tq), F32)
        for h in range(IDX_HEADS):
            sc = sc + jnp.maximum(_dot_nt(kk, qs[h]), 0.0) * w[h:h + 1, :]
        kidx = start + lax.broadcasted_iota(jnp.int32, (ch, 1), 0)
        sc = jnp.where(kidx <= qidx, sc, NEG)
        sc_ref[j] = sc
        hi_ref[j] = _high_half(sc).astype(SEL16)
        return carry

    lax.fori_loop(0, nch, score_body, 0)

    rows = 32

    def count_where(pred):
        def body(j, acc):
            hit = jnp.where(pred(sc_ref[j]), 1.0, 0.0)
            return acc + jnp.sum(hit.reshape(ch // rows, rows, tq), axis=0)
        acc = lax.fori_loop(0, nch, body, jnp.zeros((rows, tq), F32))
        return jnp.sum(acc, axis=0, keepdims=True)

    def count_high_ge(v):
        one, zero16 = jnp.ones((), SEL16), jnp.zeros((), SEL16)
        def body(j, acc):
            hit = jnp.where(hi_ref[j] >= v, one, zero16).reshape(ch // rows, rows, tq)
            parts = [hit[i] for i in range(ch // rows)]
            while len(parts) > 1:
                parts = [parts[i] + parts[i + 1] for i in range(0, len(parts), 2)]
            return acc + parts[0]
        acc = lax.fori_loop(0, nch, body, jnp.zeros((rows, tq), SEL16))
        return jnp.sum(acc.astype(F32), axis=0, keepdims=True)

    kf = float(topk)
    n_valid = (qidx + 1).astype(F32)
    half_bits = 16

    def high_body(i, st):
        u, c_lo = st
        cand = u | jnp.left_shift(1, half_bits - 1 - i)
        v = _high_half(_key_float((cand - (1 << (half_bits - 1))) << half_bits)).astype(SEL16)
        c = count_high_ge(v)
        keep = c >= kf
        return jnp.where(keep, cand, u), jnp.where(keep, c, c_lo)

    u, c_lo = lax.fori_loop(0, half_bits, high_body, (jnp.zeros((1, tq), jnp.int32), n_valid))
    key_hi = (u - (1 << (half_bits - 1))) << half_bits

    def low_body(i, st):
        lo_bits, c_lo = st
        cand = lo_bits | jnp.left_shift(1, half_bits - 1 - i)
        c = count_where(lambda sc: sc >= _key_float(key_hi | cand))
        keep = c >= kf
        return jnp.where(keep, cand, lo_bits), jnp.where(keep, c, c_lo)

    lo_bits, c_lo = lax.fori_loop(0, RADIX_LOW_BITS, low_body, (jnp.zeros((1, tq), jnp.int32), c_lo))

    def min_ge(t):
        def body(j, acc):
            sc = sc_ref[j]
            x = jnp.where(sc >= t, sc, -NEG)
            return jnp.minimum(acc, jnp.min(x.reshape(ch // rows, rows, tq), axis=0))
        acc = lax.fori_loop(0, nch, body, jnp.full((rows, tq), -NEG, F32))
        return jnp.min(acc, axis=0, keepdims=True)

    def open_rows(c_lo, done):
        return jnp.logical_and(jnp.logical_and(n_valid > kf, c_lo > kf), done == 0.0)

    def any_row(mask):
        return jnp.max(jnp.where(mask, 1.0, 0.0)) > 0.0

    def peel_cond(st):
        _, c_lo, done, it = st
        return jnp.logical_and(it < PEEL_CAP, any_row(open_rows(c_lo, done)))

    def peel_body(st):
        t, c_lo, done, it = st
        is_open = open_rows(c_lo, done)
        smallest = min_ge(t)
        above = jnp.where(smallest == 0.0, FLT_MIN_NORMAL, _key_float(_float_key(smallest) + 1))
        c = count_where(lambda sc: sc >= above)
        enough = jnp.logical_and(is_open, c >= kf)
        at_tie = jnp.logical_and(is_open, c < kf)
        t = jnp.where(enough, above, jnp.where(at_tie, smallest, t))
        return t, jnp.where(enough, c, c_lo), jnp.where(at_tie, 1.0, done), it + 1

    t_peel, c_peel, done, _ = lax.while_loop(
        peel_cond, peel_body,
        (_key_float(key_hi | lo_bits), c_lo, jnp.zeros((1, tq), F32), jnp.int32(0)))

    def finish_by_radix(_):
        bits, c = lax.fori_loop(RADIX_LOW_BITS, half_bits, low_body, (lo_bits, c_lo))
        return _key_float(key_hi | bits), c

    thr, c_lo = lax.cond(any_row(open_rows(c_peel, done)), finish_by_radix,
                         lambda _: (t_peel, c_peel), 0)
    thr = jnp.where(n_valid > kf, thr, 0.5 * NEG)

    tie_rows = jnp.logical_and(n_valid > kf, c_lo > kf)

    @pl.when(jnp.max(jnp.where(tie_rows, 1.0, 0.0)) > 0.0)
    def _():
        surplus = jnp.where(tie_rows, c_lo - kf, 0.0)
        upper = (lax.broadcasted_iota(jnp.int32, (ch, ch), 1)
                 >= lax.broadcasted_iota(jnp.int32, (ch, ch), 0)).astype(BF16)

        def drop_body(i, seen):
            j = nch - 1 - i
            sc = sc_ref[j]
            eq = sc == thr
            eqf = jnp.where(eq, 1.0, 0.0)
            rank = _dot(upper, eqf.astype(BF16))
            sc_ref[j] = jnp.where(eq, jnp.where(rank <= surplus - seen, NEG, sc), sc)
            return seen + jnp.sum(eqf, axis=0, keepdims=True)

        lax.fori_loop(0, nch, drop_body, jnp.zeros((1, tq), F32))

    cw = C_HEADS * C_DIM
    lane_c = lax.broadcasted_iota(jnp.int32, (tq, cw), 1)
    cq = cq_ref[...].astype(F32) * (C_DIM ** -0.5 * LOG2E)
    qcat = jnp.concatenate([jnp.where(lane_c // C_DIM == h, cq, 0.0) for h in range(C_HEADS)],
                           axis=0).astype(BF16)
    acc_sc[...] = jnp.zeros(acc_sc.shape, F32)
    nd = bias_ref.shape[1]
    last = (q0 + tq - 1) // ca

    def scores_into(slot, j):
        jc = jnp.minimum(j, last)
        kc = ck_ref[pl.ds(pl.multiple_of(jc * ca, ca), ca), :]
        sub = pl.multiple_of((jc % (ch // ca)) * ca, ca)
        sel = sc_ref[jc // (ch // ca), pl.ds(sub, ca), :] >= jnp.where(j <= last, thr, -NEG)
        bias = jnp.concatenate([
            jnp.concatenate([
                bias_ref[h, jnp.clip(q0 // LANES + c - (jc * (ca // LANES) + u), 0, nd - 1)]
                for h in range(C_HEADS) for c in range(tq // LANES)], axis=1)
            for u in range(ca // LANES)], axis=0)
        s_sc[slot] = jnp.where(jnp.concatenate([sel] * C_HEADS, axis=1),
                               _dot_nt(kc, qcat) + bias, NEG)

    def absorb(slot, j, m_prev):
        m_new = jnp.maximum(m_prev, jnp.max(s_sc[slot], axis=0, keepdims=True))
        alpha = jnp.exp2(m_prev - m_new)
        pb = jnp.exp2(s_sc[slot] - m_new).astype(BF16)
        jc = jnp.minimum(j, last)
        for h in range(C_HEADS):
            cols = slice(h * tq, (h + 1) * tq)
            vt = jnp.concatenate([cvt_ref[jc, h * C_DIM:(h + 1) * C_DIM, :], _ones_rows(ca)], axis=0)
            acc_sc[h] = alpha[:, cols] * acc_sc[h] + _dot(vt, pb[:, cols])
        return m_new

    scores_into(0, 0)

    def attn_body(jj, m):
        a = 2 * jj
        scores_into(1, a + 1)
        m = absorb(0, a, m)
        scores_into(0, a + 2)
        return absorb(1, a + 1, m)

    lax.fori_loop(0, (last + 2) // 2, attn_body, jnp.full((1, C_HEADS * tq), NEG, F32))
    o_t = jnp.concatenate([acc_sc[h, :C_DIM] * (1.0 / acc_sc[h, C_DIM:C_DIM + 1])
                           for h in range(C_HEADS)], axis=0)
    o_ref[...] = o_t.T.astype(o_ref.dtype)


def _dsa_attn(proj, proj_i, iwt, cvt, bias_c, tq, ch):
    B, S, _ = proj.shape
    ca = cvt.shape[-1]
    topk = min(TOPK_MAX, S // 4)
    nd = bias_c.shape[1]
    cw = C_HEADS * C_DIM
    return pl.pallas_call(
        functools.partial(_dsa_kernel, tq=tq, ch=ch, ca=ca, topk=topk),
        grid=(B, S // tq),
        in_specs=[pl.BlockSpec((None, tq, IDX_HEADS * IDX_DIM), lambda b, i: (b, i, 3)),
                  pl.BlockSpec((None, S, LANES), lambda b, i: (b, 0, 0)),
                  pl.BlockSpec((None, IDX_HEADS, tq), lambda b, i: (b, 0, i)),
                  pl.BlockSpec((None, tq, cw), lambda b, i: (b, i, 8)),
                  pl.BlockSpec((None, S, cw), lambda b, i: (b, 0, 9)),
                  pl.BlockSpec((None, S // ca, cw, ca), lambda b, i: (b, 0, 0, 0)),
                  pl.BlockSpec((C_HEADS, nd, LANES, LANES), lambda b, i: (0, 0, 0, 0))],
        out_specs=pl.BlockSpec((None, tq, cw), lambda b, i: (b, i, 0)),
        out_shape=jax.ShapeDtypeStruct((B, S, cw), BF16),
        scratch_shapes=[pltpu.VMEM((S // ch, ch, tq), F32),
                        pltpu.VMEM((S // ch, ch, tq), SEL16),
                        pltpu.VMEM((2, ca, C_HEADS * tq), F32),
                        pltpu.VMEM((C_HEADS, C_DIM + ONES_ROWS, tq), F32)],
        compiler_params=_cparams(("parallel", "arbitrary")),
        name="dsa_attn",
    )(proj, proj_i, iwt, proj, proj, cvt, bias_c)


def _memkv_kernel(mem_ref, g_ref, w_ref, o_ref):
    h = _rms(mem_ref[...], g_ref[...]).astype(BF16)
    o_ref[...] = _dot(h, w_ref[...]).astype(o_ref.dtype)


def _memkv(mem, g, w):
    B, M, D = mem.shape
    L = w.shape[0]
    return pl.pallas_call(
        _memkv_kernel,
        grid=(L, B),
        in_specs=[pl.BlockSpec((None, M, D), lambda l, b: (b, 0, 0)),
                  pl.BlockSpec((None, 1, D), lambda l, b: (l, 0, 0)),
                  pl.BlockSpec((None, D, 2 * D), lambda l, b: (l, 0, 0))],
        out_specs=pl.BlockSpec((None, None, M, 2 * D), lambda l, b: (l, b, 0, 0)),
        out_shape=jax.ShapeDtypeStruct((L, B, M, 2 * D), BF16),
        compiler_params=_cparams(("parallel", "arbitrary")),
        name="mem_kv",
    )(mem, g, w)


def _outmem_kernel(x_ref, oa_ref, ob_ref, oc_ref, wo_ref, g_ref, wq_ref, kv_ref, wmo_ref, o_ref):
    D = x_ref.shape[-1]
    na, nb = oa_ref.shape[-1], ob_ref.shape[-1]
    x = (x_ref[...] + _dot(oa_ref[...], wo_ref[0:na]) + _dot(ob_ref[...], wo_ref[na:na + nb])
         + _dot(oc_ref[...], wo_ref[na + nb:]))
    h = _rms(x, g_ref[...]).astype(BF16)
    hd = D // MEM_HEADS
    q = (_dot(h, wq_ref[...]) * (hd ** -0.5)).astype(BF16)
    outs = []
    for hh in range(MEM_HEADS):
        k = kv_ref[:, hh * hd:(hh + 1) * hd]
        v = kv_ref[:, D + hh * hd:D + (hh + 1) * hd]
        s = _dot_nt(q[:, hh * hd:(hh + 1) * hd], k)
        p = jnp.exp(s - jnp.max(s, axis=-1, keepdims=True))
        p = p / jnp.sum(p, axis=-1, keepdims=True)
        outs.append(_dot(p.astype(BF16), v).astype(BF16))
    o = jnp.concatenate(outs, axis=-1)
    o_ref[...] = x + _dot(o, wmo_ref[...])


def _outmem(x, oa, ob, oc, wo, g, wq, kv, layer, wmo, tm):
    B, S, D = x.shape
    M = kv.shape[2]
    const = lambda b, i: (0, 0)
    return pl.pallas_call(
        _outmem_kernel,
        grid=(B, S // tm),
        in_specs=[pl.BlockSpec((None, tm, D), lambda b, i: (b, i, 0)),
                  pl.BlockSpec((None, tm, oa.shape[-1]), lambda b, i: (b, i, 0)),
                  pl.BlockSpec((None, tm, ob.shape[-1]), lambda b, i: (b, i, 0)),
                  pl.BlockSpec((None, tm, oc.shape[-1]), lambda b, i: (b, i, 0)),
                  pl.BlockSpec(wo.shape, const),
                  pl.BlockSpec((1, D), const),
                  pl.BlockSpec(wq.shape, const),
                  pl.BlockSpec((None, None, M, 2 * D), lambda b, i: (layer, b, 0, 0)),
                  pl.BlockSpec(wmo.shape, const)],
        out_specs=pl.BlockSpec((None, tm, D), lambda b, i: (b, i, 0)),
        out_shape=jax.ShapeDtypeStruct((B, S, D), F32),
        compiler_params=_cparams(("parallel", "arbitrary")),
        name="outproj_memattn",
    )(x, oa, ob, oc, wo, g, wq, kv, wmo)


HALO = 8


def _ffn_kernel(x_ref, xp_ref, g_ref, wg_ref, wv_ref, cwg_ref, cwv_ref, cbg_ref, cbv_ref, wd_ref,
                o_ref, *, tm, fc):
    i = pl.program_id(1)
    x = x_ref[...]
    g = g_ref[...]
    hp = _rms(xp_ref[...], g) * jnp.where(i > 0, 1.0, 0.0)
    h = jnp.concatenate([hp, _rms(x, g)], axis=0).astype(BF16)
    F = wd_ref.shape[0]

    def conv(u, cw_ref, cb_ref, c, e):
        out = cb_ref[:, c:e]
        for j in range(CONV_WIDTH):
            shift = CONV_WIDTH - 1 - j
            out = out + cw_ref[j:j + 1, c:e] * u[HALO - shift:HALO - shift + tm]
        return out

    acc = x
    for c in range(0, F, fc):
        e = min(c + fc, F)
        gate = conv(_dot(h, wg_ref[:, c:e]), cwg_ref, cbg_ref, c, e)
        val = conv(_dot(h, wv_ref[:, c:e]), cwv_ref, cbv_ref, c, e)
        act = (gate * jax.nn.sigmoid(gate) * val).astype(BF16)
        acc = acc + _dot(act, wd_ref[c:e, :])
    o_ref[...] = acc


def _ffn(x, g, wg, wv, cwg, cwv, cbg, cbv, wd, tm, fc):
    B, S, D = x.shape
    F = wd.shape[0]
    const = lambda b, i: (0, 0)
    hb = tm // HALO
    return pl.pallas_call(
        functools.partial(_ffn_kernel, tm=tm, fc=fc),
        grid=(B, S // tm),
        in_specs=[pl.BlockSpec((None, tm, D), lambda b, i: (b, i, 0)),
                  pl.BlockSpec((None, HALO, D), lambda b, i: (b, jnp.maximum(i * hb - 1, 0), 0)),
                  pl.BlockSpec((1, D), const),
                  pl.BlockSpec((D, F), const, pipeline_mode=pl.Buffered(1)),
                  pl.BlockSpec((D, F), lambda b, i: (0, 1), pipeline_mode=pl.Buffered(1)),
                  pl.BlockSpec((CONV_WIDTH, F), const), pl.BlockSpec((CONV_WIDTH, F), const),
                  pl.BlockSpec((1, F), const), pl.BlockSpec((1, F), const),
                  pl.BlockSpec((F, D), const, pipeline_mode=pl.Buffered(1))],
        out_specs=pl.BlockSpec((None, tm, D), lambda b, i: (b, i, 0)),
        out_shape=jax.ShapeDtypeStruct((B, S, D), F32),
        compiler_params=_cparams(("parallel", "arbitrary")),
        name="conv_ffn",
    )(x, x, g, wg, wv, cwg, cwv, cbg, cbv, wd)


def _final_norm_kernel(x_ref, g_ref, o_ref):
    o_ref[...] = _rms(x_ref[...], g_ref[...])


def _final_norm(x2, g, tm):
    T, D = x2.shape
    return pl.pallas_call(
        _final_norm_kernel,
        grid=(T // tm,),
        in_specs=[pl.BlockSpec((tm, D), lambda i: (i, 0)), pl.BlockSpec((1, D), lambda i: (0, 0))],
        out_specs=pl.BlockSpec((tm, D), lambda i: (i, 0)),
        out_shape=jax.ShapeDtypeStruct((T, D), F32),
        compiler_params=_cparams(("parallel",)),
        name="final_norm",
    )(x2, g)


def _tile_params(S, tq_a, tk_a):
    big = 1 << 30
    gran, (d_min, d_const) = min(tq_a, tk_a), _a_offsets(tq_a, tk_a)
    pa = []
    for h in range(A_HEADS):
        pa += [(h, d * gran, 1, 0, big, -1) for d in range(d_min, d_const + 1)]
        pa.append((h, 0, 1, 1, 0, -1))
    pb = []
    for _, r in DILATED_PATTERNS:
        for h in range(B_HEADS):
            pb.append((A_HEADS + h, DIL_BLK, r, 1, DIL_BLK, 1))
            pb.append((A_HEADS + h, 0, r, 0, DIL_BLK, 1))
    pc = [(A_HEADS + B_HEADS + h, d * LANES, 1, -big, big, -1)
          for h in range(C_HEADS) for d in range(_n_offsets(S, LANES))]
    to = lambda p: jnp.asarray(p, jnp.int32)
    return to(pa), to(pb), to(pc)


def _dil_bias_layout(tiles):
    t = tiles.reshape(len(DILATED_PATTERNS), B_HEADS, 2, DIL_BLK, DIL_BLK)
    return jnp.concatenate([t[:, :, 0], t[:, :, 1]], axis=-1).reshape(
        len(DILATED_PATTERNS), B_HEADS * DIL_BLK, 2 * DIL_BLK)


def _n_offsets(S, blk):
    return min(S // blk, REL_MAX_DIST // blk + 2)


def _a_offsets(tq, tk):
    gran = min(tq, tk)
    return -(tq // gran - 1), -(-(REL_MAX_DIST - 1 + tk) // gran)


def _in_weights(w_in_l):
    sizes = (512, 512, 512, 256, 256, 256, 256, 256, 256, 512, 64, 8)
    offs = [0]
    for s in sizes:
        offs.append(offs[-1] + s)
    (aq, ak, av, bq, bk, bv, cq, ck, cv, iq, ik, iw) = [w_in_l[:, offs[i]:offs[i + 1]] for i in range(12)]
    pad = jnp.zeros((w_in_l.shape[0], N_I - 2 * IDX_DIM - IDX_HEADS), w_in_l.dtype)
    return jnp.concatenate([aq, ak, av, iq, cq, ck, cv, bq, bk, bv, ik, ik, iw, pad], axis=1).astype(BF16)


def _forward(x, mem, rel_bias, norm_mix, w_in, lam_q1, lam_k1, lam_q2, lam_k2, subln, w_out,
             norm_mem, norm_memkv, w_mq, w_mkv, w_mo, norm_ffn, w_up, conv_w, conv_b, w_down,
             norm_final, *, tq_a, tk_a, tq_c, ch_c, ca_c, tm_proj, tm_mem, tm_ffn, fc):
    B, S, D = x.shape
    L = w_in.shape[0]
    F = w_down.shape[1]
    pa, pb, pc = _tile_params(S, tq_a, tk_a)
    bias_a = _bias_tiles(pa, rel_bias, tk_a, tq_a, LOG2E).reshape(A_HEADS, -1, tk_a, tq_a)
    d_min, d_const = _a_offsets(tq_a, tk_a)
    bias_b = _dil_bias_layout(_bias_tiles(pb, rel_bias, DIL_BLK, DIL_BLK, LOG2E))
    bias_c = _bias_tiles(pc, rel_bias, LANES, LANES, LOG2E).reshape(C_HEADS, -1, LANES, LANES)
    kv_all = _memkv(mem, norm_memkv.reshape(L, 1, D), w_mkv.astype(BF16))

    for l in range(L):
        lam_init = 0.8 - 0.6 * math.exp(-0.3 * l)
        proj, proj_b, proj_i, avt, cvt, iwt = _inproj(
            x, norm_mix[l].reshape(1, D), _in_weights(w_in[l]), tm_proj, tk_a, ca_c)
        proj = proj.reshape(B, S, N_MAIN)
        proj_i = proj_i.reshape(B, S, N_I)
        lamv = jnp.stack([lam_q1[l], lam_k1[l], lam_q2[l], lam_k2[l]], axis=0)
        o_a = _diff_attn(proj, avt, lamv, bias_a, subln[l].reshape(A_V, 1), lam_init, tq_a,
                         d_min, d_const)
        o_b = _dil_attn(proj_b.reshape(B, S, N_B), bias_b)
        o_c = _dsa_attn(proj, proj_i, iwt, cvt, bias_c, tq_c, ch_c)
        x = _outmem(x, o_a, o_b, o_c, w_out[l].astype(BF16), norm_mem[l].reshape(1, D),
                    w_mq[l].astype(BF16), kv_all, l, w_mo[l].astype(BF16), tm_mem)
        wu = w_up[l].astype(BF16)
        x = _ffn(x, norm_ffn[l].reshape(1, D), wu, wu, conv_w[l][:, :F], conv_w[l][:, F:],
                 conv_b[l][:F].reshape(1, F), conv_b[l][F:].reshape(1, F), w_down[l].astype(BF16),
                 tm_ffn, fc)
    return _final_norm(x.reshape(B * S, D), norm_final.reshape(1, D), tm_proj).reshape(B, S, D)


def kernel(x, mem, rel_bias, norm_mix, w_in, lam_q1, lam_k1, lam_q2, lam_k2, subln, w_out,
           norm_mem, norm_memkv, w_mq, w_mkv, w_mo, norm_ffn, w_up, conv_w, conv_b, w_down,
           norm_final):
    return _forward(x, mem, rel_bias, norm_mix, w_in, lam_q1, lam_k1, lam_q2, lam_k2, subln, w_out,
                    norm_mem, norm_memkv, w_mq, w_mkv, w_mo, norm_ffn, w_up, conv_w, conv_b, w_down,
                    norm_final, tq_a=512, tk_a=256, tq_c=512, ch_c=512, ca_c=256, tm_proj=1024, tm_mem=1024, tm_ffn=1024,
                    fc=512)
```

```python
import functools
import math

import jax
import jax.numpy as jnp
from jax import lax
from jax.experimental import pallas as pl
from jax.experimental.pallas import tpu as pltpu

F32 = jnp.float32
BF16 = jnp.bfloat16

EPS = 1e-6
NEG = -1e30
LOG2E = math.log2(math.e)
LANES = 128
VMEM_LIMIT = 56 * 1024 * 1024

A_HEADS, A_QK, A_V = 4, 64, 128
B_HEADS, B_DIM = 4, 64
C_HEADS, C_DIM = 4, 64
IDX_HEADS, IDX_DIM = 8, 64
TOPK_MAX = 256
DILATED_PATTERNS = ((128, 1), (512, 4), (2048, 16))
ONES_ROWS = 16
DIL_BLK = 128
DIL_UNROLL = 4
REL_BUCKETS, REL_MAX_DIST = 32, 2048
MEM_HEADS = 4
CONV_WIDTH = 3

N_MAIN = 2816
N_B = 768
N_I = 256


def _cparams(sem):
    return pltpu.CompilerParams(dimension_semantics=sem, vmem_limit_bytes=VMEM_LIMIT)


def _dot(a, b):
    return jnp.dot(a, b, preferred_element_type=F32)


def _dot_nt(a, b):
    return lax.dot_general(a, b, (((1,), (1,)), ((), ())), preferred_element_type=F32)


def _ones_rows(n):
    row = lax.broadcasted_iota(jnp.int32, (ONES_ROWS, n), 0)
    return jnp.where(row == 0, 1.0, 0.0).astype(BF16)


def _rms(x, g):
    return x * lax.rsqrt(jnp.mean(x * x, axis=-1, keepdims=True) + EPS) * g


def _rel_bucket(n):
    max_exact = REL_BUCKETS // 2
    nf = jnp.maximum(n, 1).astype(F32)
    large = max_exact + (jnp.log(nf / max_exact) / math.log(REL_MAX_DIST / max_exact)
                         * (REL_BUCKETS - max_exact)).astype(jnp.int32)
    large = jnp.minimum(large, REL_BUCKETS - 1)
    return jnp.where(n < max_exact, n, large)


def _bias_tiles_kernel(par_ref, tab_ref, o_ref, *, tq, tk, scale):
    t = pl.program_id(0)
    head, off, mult = par_ref[t, 0], par_ref[t, 1], par_ref[t, 2]
    lo, hi, sgn = par_ref[t, 3], par_ref[t, 4], par_ref[t, 5]
    dist = off + sgn * (lax.broadcasted_iota(jnp.int32, (tq, tk), 0)
                        - lax.broadcasted_iota(jnp.int32, (tq, tk), 1))
    bucket = _rel_bucket(jnp.maximum(dist * mult, 0))
    val = jnp.zeros((tq, tk), F32)
    for b in range(REL_BUCKETS):
        val = jnp.where(bucket == b, tab_ref[b, head] * scale, val)
    o_ref[...] = jnp.where((dist >= lo) & (dist <= hi), val, NEG)


def _bias_tiles(params, table, tq, tk, scale=1.0):
    n = params.shape[0]
    return pl.pallas_call(
        functools.partial(_bias_tiles_kernel, tq=tq, tk=tk, scale=scale),
        grid=(n,),
        in_specs=[pl.BlockSpec(memory_space=pltpu.SMEM), pl.BlockSpec(memory_space=pltpu.SMEM)],
        out_specs=pl.BlockSpec((None, tq, tk), lambda t: (t, 0, 0)),
        out_shape=jax.ShapeDtypeStruct((n, tq, tk), F32),
        compiler_params=_cparams(("arbitrary",)),
        name="bias_tiles",
    )(params, table)


def _inproj_kernel(x_ref, g_ref, w_ref, om_ref, ob_ref, oi_ref, avt_ref, cvt_ref, iwt_ref):
    h = _rms(x_ref[...], g_ref[...]).astype(BF16)
    step = 512
    for c in range(0, N_MAIN, step):
        e = min(c + step, N_MAIN)
        om_ref[:, c:e] = _dot(h, w_ref[:, c:e]).astype(BF16)
    ob_ref[...] = _dot(h, w_ref[:, N_MAIN:N_MAIN + N_B])
    oi_ref[...] = _dot(h, w_ref[:, N_MAIN + N_B:])
    tk = avt_ref.shape[-1]
    a0 = 2 * A_HEADS * LANES
    for hd in range(A_HEADS):
        vt = om_ref[:, a0 + hd * A_V:a0 + (hd + 1) * A_V].astype(F32).T
        for j in range(avt_ref.shape[1]):
            avt_ref[hd, j] = vt[:, j * tk:(j + 1) * tk].astype(BF16)
    ca = cvt_ref.shape[-1]
    cw = C_HEADS * C_DIM
    for half in range(cw // LANES):
        c0 = N_MAIN - cw + half * LANES
        vt = om_ref[:, c0:c0 + LANES].astype(F32).T
        for j in range(cvt_ref.shape[0]):
            cvt_ref[j, half * LANES:(half + 1) * LANES, :] = vt[:, j * ca:(j + 1) * ca].astype(BF16)
    iwt_ref[...] = oi_ref[:, 2 * IDX_DIM:2 * IDX_DIM + LANES].T[:IDX_HEADS]


def _inproj(x, g, w, tm, tk_a, ca_c):
    B, S, D = x.shape
    T = B * S
    nt = S // tm
    n_all = N_MAIN + N_B + N_I
    cw = C_HEADS * C_DIM
    return pl.pallas_call(
        _inproj_kernel,
        grid=(T // tm,),
        in_specs=[pl.BlockSpec((tm, D), lambda i: (i, 0)),
                  pl.BlockSpec((1, D), lambda i: (0, 0)),
                  pl.BlockSpec((D, n_all), lambda i: (0, 0))],
        out_specs=[pl.BlockSpec((tm, N_MAIN), lambda i: (i, 0)),
                   pl.BlockSpec((tm, N_B), lambda i: (i, 0)),
                   pl.BlockSpec((tm, N_I), lambda i: (i, 0)),
                   pl.BlockSpec((None, A_HEADS, tm // tk_a, A_V, tk_a),
                                lambda i: (i // nt, 0, i % nt, 0, 0)),
                   pl.BlockSpec((None, tm // ca_c, cw, ca_c), lambda i: (i // nt, i % nt, 0, 0)),
                   pl.BlockSpec((None, IDX_HEADS, tm), lambda i: (i // nt, 0, i % nt))],
        out_shape=[jax.ShapeDtypeStruct((T, N_MAIN), BF16),
                   jax.ShapeDtypeStruct((T, N_B), F32),
                   jax.ShapeDtypeStruct((T, N_I), F32),
                   jax.ShapeDtypeStruct((B, A_HEADS, S // tk_a, A_V, tk_a), BF16),
                   jax.ShapeDtypeStruct((B, S // ca_c, cw, ca_c), BF16),
                   jax.ShapeDtypeStruct((B, IDX_HEADS, S), F32)],
        compiler_params=_cparams(("parallel",)),
        name="inproj",
    )(x.reshape(T, D), g, w)


def _diff_attn_kernel(lam_ref, q_ref, k_ref, vt_ref, bias_ref, g_ref, o_ref, s_sc, acc_sc,
                      *, tq, tk, d_min, d_const, lam_init):
    qi = pl.program_id(2)
    gran = min(tq, tk)
    last = (qi * tq + tq - 1) // tk
    masked_tile = d_const - d_min + 1
    lv = lam_ref[...]
    lam = (jnp.exp(jnp.sum(lv[0:1] * lv[1:2], axis=-1, keepdims=True))
           - jnp.exp(jnp.sum(lv[2:3] * lv[3:4], axis=-1, keepdims=True)) + lam_init)
    lane = lax.broadcasted_iota(jnp.int32, (tq, LANES), 1)
    q = q_ref[...].astype(F32) * (A_QK ** -0.5 * LOG2E)
    qcat = jnp.concatenate([jnp.where(lane < A_QK, q, 0.0), jnp.where(lane >= A_QK, q, 0.0)],
                           axis=0).astype(BF16)
    acc_sc[...] = jnp.zeros(acc_sc.shape, F32)

    def scores_into(slot, j):
        jc = jnp.minimum(j, last)
        k = k_ref[pl.ds(pl.multiple_of(jc * tk, tk), tk), :]
        d = qi * (tq // gran) - jc * (tk // gran)
        bias = bias_ref[jnp.where(j <= last, jnp.minimum(d, d_const) - d_min, masked_tile)]
        s_sc[slot] = _dot_nt(k, qcat) + jnp.concatenate([bias, bias], axis=1)

    def absorb(slot, j, m_prev):
        m_new = jnp.maximum(m_prev, jnp.max(s_sc[slot], axis=0, keepdims=True))
        alpha = jnp.exp2(m_prev - m_new)
        p = jnp.exp2(s_sc[slot] - m_new)
        vt = jnp.concatenate([vt_ref[jnp.minimum(j, last)], _ones_rows(tk)], axis=0)
        acc_sc[...] = alpha * acc_sc[...] + _dot(vt, p.astype(BF16))
        return m_new

    scores_into(0, 0)

    def pair(a, m):
        scores_into(1, a + 1)
        m = absorb(0, a, m)
        scores_into(0, a + 2)
        return absorb(1, a + 1, m)

    pairs = (last + 2) // 2
    m = lax.fori_loop(0, pairs // 2, lambda jj, m: pair(4 * jj + 2, pair(4 * jj, m)),
                      jnp.full((1, 2 * tq), NEG, F32))
    lax.fori_loop(pairs // 2 * 2, pairs, lambda jj, m: pair(2 * jj, m), m)
    inv = 1.0 / acc_sc[A_V:A_V + 1, :]
    o = acc_sc[:A_V, :tq] * inv[:, :tq] - lam * (acc_sc[:A_V, tq:] * inv[:, tq:])
    o = o * lax.rsqrt(jnp.mean(o * o, axis=0, keepdims=True) + EPS) * g_ref[...] * (1.0 - lam_init)
    o_ref[...] = o.T.astype(o_ref.dtype)


def _diff_attn(proj, vt, lamv, bias_a, subln_g, lam_init, tq, d_min, d_const):
    B, S, _ = proj.shape
    nd, tk = bias_a.shape[1], bias_a.shape[2]
    return pl.pallas_call(
        functools.partial(_diff_attn_kernel, tq=tq, tk=tk, d_min=d_min, d_const=d_const,
                          lam_init=lam_init),
        grid=(A_HEADS, B, S // tq),
        in_specs=[pl.BlockSpec((4, A_QK), lambda h, b, i: (0, 0)),
                  pl.BlockSpec((None, tq, LANES), lambda h, b, i: (b, i, h)),
                  pl.BlockSpec((None, S, LANES), lambda h, b, i: (b, 0, A_HEADS + h)),
                  pl.BlockSpec((None, None, S // tk, A_V, tk), lambda h, b, i: (b, h, 0, 0, 0)),
                  pl.BlockSpec((None, nd, tk, tq), lambda h, b, i: (h, 0, 0, 0)),
                  pl.BlockSpec((A_V, 1), lambda h, b, i: (0, 0))],
        out_specs=pl.BlockSpec((None, tq, LANES), lambda h, b, i: (b, i, h)),
        out_shape=jax.ShapeDtypeStruct((B, S, A_HEADS * A_V), BF16),
        scratch_shapes=[pltpu.VMEM((2, tk, 2 * tq), F32),
                        pltpu.VMEM((A_V + ONES_ROWS, 2 * tq), F32)],
        compiler_params=_cparams(("parallel", "parallel", "arbitrary")),
        name="diff_attn",
    )(lamv, proj, proj, vt, bias_a, subln_g)


class _LaneHalves:
    def __init__(self, *refs):
        self.refs = refs

    def __getitem__(self, idx):
        return jnp.concatenate([r[idx] for r in self.refs], axis=1)

    def __setitem__(self, idx, val):
        for i, r in enumerate(self.refs):
            r[idx] = val[:, i * LANES:(i + 1) * LANES]


def _dil_kernel(q0, q1, k0, k1, v0, v1, bias_ref, o_ref, n0, n1, m0, m1, l0, l1, *, seq, patterns):
    q_ref, k_ref, v_ref = _LaneHalves(q0, q1), _LaneHalves(k0, k1), _LaneHalves(v0, v1)
    n_sc, m_sc, l_sc = _LaneHalves(n0, n1), _LaneHalves(m0, m1), _LaneHalves(l0, l1)
    blk = DIL_BLK
    width = B_HEADS * B_DIM
    head_of_lane = lax.broadcasted_iota(jnp.int32, (blk, width), 1) // B_DIM

    def per_head(x):
        parts = [jnp.broadcast_to(x[h * blk:(h + 1) * blk], (blk, width)) for h in range(B_HEADS)]
        out = parts[-1]
        for h in range(B_HEADS - 2, -1, -1):
            out = jnp.where(head_of_lane == h, parts[h], out)
        return out

    def attend(base, r, with_prev, p_idx):
        q = q_ref[pl.ds(base, blk, stride=r), :] * (B_DIM ** -0.5 * LOG2E)
        qcat = jnp.concatenate([jnp.where(head_of_lane == h, q, 0.0) for h in range(B_HEADS)],
                               axis=0).astype(BF16)
        if with_prev:
            keys = pl.ds(base - blk * r, 2 * blk, stride=r)
            bias = bias_ref[p_idx]
        else:
            keys = pl.ds(base, blk, stride=r)
            bias = bias_ref[p_idx, :, blk:]
        s = _dot_nt(qcat, k_ref[keys, :].astype(BF16)) + bias
        mx = jnp.max(s, axis=-1, keepdims=True)
        p = jnp.exp2(s - mx)
        l = jnp.sum(p, axis=-1, keepdims=True)
        o = _dot(p.astype(BF16), v_ref[keys, :].astype(BF16))
        return per_head(o), per_head(mx), per_head(l)

    def merge(base, r, res, is_first):
        o, m, l = res
        rows = pl.ds(base, blk, stride=r)
        if is_first:
            n_sc[rows, :] = o
            m_sc[rows, :] = m
            l_sc[rows, :] = l
        else:
            m_old = m_sc[rows, :]
            m_new = jnp.maximum(m_old, m)
            a = jnp.exp2(m_old - m_new)
            b = jnp.exp2(m - m_new)
            n_sc[rows, :] = a * n_sc[rows, :] + b * o
            l_sc[rows, :] = a * l_sc[rows, :] + b * l
            m_sc[rows, :] = m_new

    def sweep(count, base_of, r, with_prev, p_idx):
        def trip(width):
            def body(i, carry):
                bases = [base_of(i * width + u) for u in range(width)]
                results = [attend(b, r, with_prev, p_idx) for b in bases]
                for b, res in zip(bases, results):
                    merge(b, r, res, p_idx == len(patterns) - 1)
                return carry
            return body

        full = count // DIL_UNROLL
        lax.fori_loop(0, full, trip(DIL_UNROLL), 0)
        rest = count - full * DIL_UNROLL
        if rest:
            lax.fori_loop(full * DIL_UNROLL, count, trip(1), 0)

    for p_idx, (_, r) in reversed(list(enumerate(patterns))):
        nb = seq // (r * blk)
        sweep(r, lambda c: c, r, False, p_idx)
        sweep(r * (nb - 1), lambda idx, r=r: (idx // r + 1) * blk * r + idx % r, r, True, p_idx)

    everything = (slice(None), slice(None))
    o_ref[...] = (n_sc[everything] / l_sc[everything]).astype(o_ref.dtype)


def _dil_attn(proj_b, bias_b):
    B, S, _ = proj_b.shape
    width = B_HEADS * B_DIM
    return pl.pallas_call(
        functools.partial(_dil_kernel, seq=S, patterns=DILATED_PATTERNS),
        grid=(B,),
        in_specs=[pl.BlockSpec((None, S, LANES), functools.partial(lambda i, b: (b, 0, i), i))
                  for i in range(3 * width // LANES)]
                 + [pl.BlockSpec(bias_b.shape, lambda b: (0, 0, 0))],
        out_specs=pl.BlockSpec((None, S, width), lambda b: (b, 0, 0)),
        out_shape=jax.ShapeDtypeStruct((B, S, width), BF16),
        scratch_shapes=[pltpu.VMEM((S, LANES), F32)] * (3 * width // LANES),
        compiler_params=_cparams(("parallel",)),
        name="dilated_attn",
    )(*([proj_b] * (3 * width // LANES)), bias_b)


SEL16 = jnp.bfloat16


RADIX_LOW_BITS = 5
PEEL_CAP = 6
FLT_MIN_NORMAL = 2.0 ** -126


def _float_key(f):
    b = pltpu.bitcast(f, jnp.int32)
    return jnp.where(b >= 0, b, b ^ jnp.int32(0x7FFFFFFF))


def _key_float(k):
    return pltpu.bitcast(jnp.where(k >= 0, k, k ^ jnp.int32(0x7FFFFFFF)), F32)


def _high_half(f):
    return pltpu.bitcast(pltpu.bitcast(f, jnp.int32) & jnp.int32(-65536), F32)


def _dsa_kernel(iq_ref, ik_ref, iwt_ref, cq_ref, ck_ref, cvt_ref, bias_ref, o_ref, sc_ref, hi_ref, s_sc,
                acc_sc,
                *, tq, ch, ca, topk):
    qi = pl.program_id(1)
    q0 = qi * tq
    nch = (q0 + tq - 1) // ch + 1
    qidx = q0 + lax.broadcasted_iota(jnp.int32, (1, tq), 1)
    lane = lax.broadcasted_iota(jnp.int32, (tq, LANES), 1)
    first = lane < IDX_DIM

    w = iwt_ref[...] * (IDX_HEADS ** -0.5)
    qs = []
    for h in range(IDX_HEADS):
        tile = iq_ref[:, (h // 2) * LANES:(h // 2 + 1) * LANES].astype(F32) * (IDX_DIM ** -0.5)
        qs.append(jnp.where(first if h % 2 == 0 else ~first, tile, 0.0).astype(BF16))

    def score_body(j, carry):
        start = pl.multiple_of(j * ch, ch)
        kk = ik_ref[pl.ds(start, ch), :].astype(BF16)
        sc = jnp.zeros((ch, tq), F32)
        for h in range(IDX_HEADS):
            sc = sc + jnp.maximum(_dot_nt(kk, qs[h]), 0.0) * w[h:h + 1, :]
        kidx = start + lax.broadcasted_iota(jnp.int32, (ch, 1), 0)
        sc = jnp.where(kidx <= qidx, sc, NEG)
        sc_ref[j] = sc
        hi_ref[j] = _high_half(sc).astype(SEL16)
        return carry

    lax.fori_loop(0, nch, score_body, 0)

    rows = 32

    def count_where(pred):
        def body(j, acc):
            hit = jnp.where(pred(sc_ref[j]), 1.0, 0.0)
            return acc + jnp.sum(hit.reshape(ch // rows, rows, tq), axis=0)
        acc = lax.fori_loop(0, nch, body, jnp.zeros((rows, tq), F32))
        return jnp.sum(acc, axis=0, keepdims=True)

    def count_high_ge(v):
        one, zero16 = jnp.ones((), SEL16), jnp.zeros((), SEL16)
        def body(j, acc):
            hit = jnp.where(hi_ref[j] >= v, one, zero16).reshape(ch // rows, rows, tq)
            parts = [hit[i] for i in range(ch // rows)]
            while len(parts) > 1:
                parts = [parts[i] + parts[i + 1] for i in range(0, len(parts), 2)]
            return acc + parts[0]
        acc = lax.fori_loop(0, nch, body, jnp.zeros((rows, tq), SEL16))
        return jnp.sum(acc.astype(F32), axis=0, keepdims=True)

    kf = float(topk)
    n_valid = (qidx + 1).astype(F32)
    half_bits = 16

    def high_body(i, st):
        u, c_lo = st
        cand = u | jnp.left_shift(1, half_bits - 1 - i)
        v = _high_half(_key_float((cand - (1 << (half_bits - 1))) << half_bits)).astype(SEL16)
        c = count_high_ge(v)
        keep = c >= kf
        return jnp.where(keep, cand, u), jnp.where(keep, c, c_lo)

    u, c_lo = lax.fori_loop(0, half_bits, high_body, (jnp.zeros((1, tq), jnp.int32), n_valid))
    key_hi = (u - (1 << (half_bits - 1))) << half_bits

    def low_body(i, st):
        lo_bits, c_lo = st
        cand = lo_bits | jnp.left_shift(1, half_bits - 1 - i)
        c = count_where(lambda sc: sc >= _key_float(key_hi | cand))
        keep = c >= kf
        return jnp.where(keep, cand, lo_bits), jnp.where(keep, c, c_lo)

    lo_bits, c_lo = lax.fori_loop(0, RADIX_LOW_BITS, low_body, (jnp.zeros((1, tq), jnp.int32), c_lo))

    def min_ge(t):
        def body(j, acc):
            sc = sc_ref[j]
            x = jnp.where(sc >= t, sc, -NEG)
            return jnp.minimum(acc, jnp.min(x.reshape(ch // rows, rows, tq), axis=0))
        acc = lax.fori_loop(0, nch, body, jnp.full((rows, tq), -NEG, F32))
        return jnp.min(acc, axis=0, keepdims=True)

    def open_rows(c_lo, done):
        return jnp.logical_and(jnp.logical_and(n_valid > kf, c_lo > kf), done == 0.0)

    def any_row(mask):
        return jnp.max(jnp.where(mask, 1.0, 0.0)) > 0.0

    def peel_cond(st):
        _, c_lo, done, it = st
        return jnp.logical_and(it < PEEL_CAP, any_row(open_rows(c_lo, done)))

    def peel_body(st):
        t, c_lo, done, it = st
        is_open = open_rows(c_lo, done)
        smallest = min_ge(t)
        above = jnp.where(smallest == 0.0, FLT_MIN_NORMAL, _key_float(_float_key(smallest) + 1))
        c = count_where(lambda sc: sc >= above)
        enough = jnp.logical_and(is_open, c >= kf)
        at_tie = jnp.logical_and(is_open, c < kf)
        t = jnp.where(enough, above, jnp.where(at_tie, smallest, t))
        return t, jnp.where(enough, c, c_lo), jnp.where(at_tie, 1.0, done), it + 1

    t_peel, c_peel, done, _ = lax.while_loop(
        peel_cond, peel_body,
        (_key_float(key_hi | lo_bits), c_lo, jnp.zeros((1, tq), F32), jnp.int32(0)))

    def finish_by_radix(_):
        bits, c = lax.fori_loop(RADIX_LOW_BITS, half_bits, low_body, (lo_bits, c_lo))
        return _key_float(key_hi | bits), c

    thr, c_lo = lax.cond(any_row(open_rows(c_peel, done)), finish_by_radix,
                         lambda _: (t_peel, c_peel), 0)
    thr = jnp.where(n_valid > kf, thr, 0.5 * NEG)

    tie_rows = jnp.logical_and(n_valid > kf, c_lo > kf)

    @pl.when(jnp.max(jnp.where(tie_rows, 1.0, 0.0)) > 0.0)
    def _():
        surplus = jnp.where(tie_rows, c_lo - kf, 0.0)
        upper = (lax.broadcasted_iota(jnp.int32, (ch, ch), 1)
                 >= lax.broadcasted_iota(jnp.int32, (ch, ch), 0)).astype(BF16)

        def drop_body(i, seen):
            j = nch - 1 - i
            sc = sc_ref[j]
            eq = sc == thr
            eqf = jnp.where(eq, 1.0, 0.0)
            rank = _dot(upper, eqf.astype(BF16))
            sc_ref[j] = jnp.where(eq, jnp.where(rank <= surplus - seen, NEG, sc), sc)
            return seen + jnp.sum(eqf, axis=0, keepdims=True)

        lax.fori_loop(0, nch, drop_body, jnp.zeros((1, tq), F32))

    cw = C_HEADS * C_DIM
    lane_c = lax.broadcasted_iota(jnp.int32, (tq, cw), 1)
    cq = cq_ref[...].astype(F32) * (C_DIM ** -0.5 * LOG2E)
    qcat = jnp.concatenate([jnp.where(lane_c // C_DIM == h, cq, 0.0) for h in range(C_HEADS)],
                           axis=0).astype(BF16)
    acc_sc[...] = jnp.zeros(acc_sc.shape, F32)
    nd = bias_ref.shape[1]
    last = (q0 + tq - 1) // ca

    def scores_into(slot, j):
        jc = jnp.minimum(j, last)
        kc = ck_ref[pl.ds(pl.multiple_of(jc * ca, ca), ca), :]
        sub = pl.multiple_of((jc % (ch // ca)) * ca, ca)
        sel = sc_ref[jc // (ch // ca), pl.ds(sub, ca), :] >= jnp.where(j <= last, thr, -NEG)
        bias = jnp.concatenate([
            jnp.concatenate([
                bias_ref[h, jnp.clip(q0 // LANES + c - (jc * (ca // LANES) + u), 0, nd - 1)]
                for h in range(C_HEADS) for c in range(tq // LANES)], axis=1)
            for u in range(ca // LANES)], axis=0)
        s_sc[slot] = jnp.where(jnp.concatenate([sel] * C_HEADS, axis=1),
                               _dot_nt(kc, qcat) + bias, NEG)

    def absorb(slot, j, m_prev):
        m_new = jnp.maximum(m_prev, jnp.max(s_sc[slot], axis=0, keepdims=True))
        alpha = jnp.exp2(m_prev - m_new)
        pb = jnp.exp2(s_sc[slot] - m_new).astype(BF16)
        jc = jnp.minimum(j, last)
        for h in range(C_HEADS):
            cols = slice(h * tq, (h + 1) * tq)
            vt = jnp.concatenate([cvt_ref[jc, h * C_DIM:(h + 1) * C_DIM, :], _ones_rows(ca)], axis=0)
            acc_sc[h] = alpha[:, cols] * acc_sc[h] + _dot(vt, pb[:, cols])
        return m_new

    scores_into(0, 0)

    def attn_body(jj, m):
        a = 2 * jj
        scores_into(1, a + 1)
        m = absorb(0, a, m)
        scores_into(0, a + 2)
        return absorb(1, a + 1, m)

    lax.fori_loop(0, (last + 2) // 2, attn_body, jnp.full((1, C_HEADS * tq), NEG, F32))
    o_t = jnp.concatenate([acc_sc[h, :C_DIM] * (1.0 / acc_sc[h, C_DIM:C_DIM + 1])
                           for h in range(C_HEADS)], axis=0)
    o_ref[...] = o_t.T.astype(o_ref.dtype)


def _dsa_attn(proj, proj_i, iwt, cvt, bias_c, tq, ch):
    B, S, _ = proj.shape
    ca = cvt.shape[-1]
    topk = min(TOPK_MAX, S // 4)
    nd = bias_c.shape[1]
    cw = C_HEADS * C_DIM
    return pl.pallas_call(
        functools.partial(_dsa_kernel, tq=tq, ch=ch, ca=ca, topk=topk),
        grid=(B, S // tq),
        in_specs=[pl.BlockSpec((None, tq, IDX_HEADS * IDX_DIM), lambda b, i: (b, i, 3)),
                  pl.BlockSpec((None, S, LANES), lambda b, i: (b, 0, 0)),
                  pl.BlockSpec((None, IDX_HEADS, tq), lambda b, i: (b, 0, i)),
                  pl.BlockSpec((None, tq, cw), lambda b, i: (b, i, 8)),
                  pl.BlockSpec((None, S, cw), lambda b, i: (b, 0, 9)),
                  pl.BlockSpec((None, S // ca, cw, ca), lambda b, i: (b, 0, 0, 0)),
                  pl.BlockSpec((C_HEADS, nd, LANES, LANES), lambda b, i: (0, 0, 0, 0))],
        out_specs=pl.BlockSpec((None, tq, cw), lambda b, i: (b, i, 0)),
        out_shape=jax.ShapeDtypeStruct((B, S, cw), BF16),
        scratch_shapes=[pltpu.VMEM((S // ch, ch, tq), F32),
                        pltpu.VMEM((S // ch, ch, tq), SEL16),
                        pltpu.VMEM((2, ca, C_HEADS * tq), F32),
                        pltpu.VMEM((C_HEADS, C_DIM + ONES_ROWS, tq), F32)],
        compiler_params=_cparams(("parallel", "arbitrary")),
        name="dsa_attn",
    )(proj, proj_i, iwt, proj, proj, cvt, bias_c)


def _memkv_kernel(mem_ref, g_ref, w_ref, o_ref):
    h = _rms(mem_ref[...], g_ref[...]).astype(BF16)
    o_ref[...] = _dot(h, w_ref[...]).astype(o_ref.dtype)


def _memkv(mem, g, w):
    B, M, D = mem.shape
    L = w.shape[0]
    return pl.pallas_call(
        _memkv_kernel,
        grid=(L, B),
        in_specs=[pl.BlockSpec((None, M, D), lambda l, b: (b, 0, 0)),
                  pl.BlockSpec((None, 1, D), lambda l, b: (l, 0, 0)),
                  pl.BlockSpec((None, D, 2 * D), lambda l, b: (l, 0, 0))],
        out_specs=pl.BlockSpec((None, None, M, 2 * D), lambda l, b: (l, b, 0, 0)),
        out_shape=jax.ShapeDtypeStruct((L, B, M, 2 * D), BF16),
        compiler_params=_cparams(("parallel", "arbitrary")),
        name="mem_kv",
    )(mem, g, w)


def _outmem_kernel(x_ref, oa_ref, ob_ref, oc_ref, wo_ref, g_ref, wq_ref, kv_ref, wmo_ref, o_ref):
    D = x_ref.shape[-1]
    na, nb = oa_ref.shape[-1], ob_ref.shape[-1]
    x = (x_ref[...] + _dot(oa_ref[...], wo_ref[0:na]) + _dot(ob_ref[...], wo_ref[na:na + nb])
         + _dot(oc_ref[...], wo_ref[na + nb:]))
    h = _rms(x, g_ref[...]).astype(BF16)
    hd = D // MEM_HEADS
    q = (_dot(h, wq_ref[...]) * (hd ** -0.5)).astype(BF16)
    outs = []
    for hh in range(MEM_HEADS):
        k = kv_ref[:, hh * hd:(hh + 1) * hd]
        v = kv_ref[:, D + hh * hd:D + (hh + 1) * hd]
        s = _dot_nt(q[:, hh * hd:(hh + 1) * hd], k)
        p = jnp.exp(s - jnp.max(s, axis=-1, keepdims=True))
        p = p / jnp.sum(p, axis=-1, keepdims=True)
        outs.append(_dot(p.astype(BF16), v).astype(BF16))
    o = jnp.concatenate(outs, axis=-1)
    o_ref[...] = x + _dot(o, wmo_ref[...])


def _outmem(x, oa, ob, oc, wo, g, wq, kv, layer, wmo, tm):
    B, S, D = x.shape
    M = kv.shape[2]
    const = lambda b, i: (0, 0)
    return pl.pallas_call(
        _outmem_kernel,
        grid=(B, S // tm),
        in_specs=[pl.BlockSpec((None, tm, D), lambda b, i: (b, i, 0)),
                  pl.BlockSpec((None, tm, oa.shape[-1]), lambda b, i: (b, i, 0)),
                  pl.BlockSpec((None, tm, ob.shape[-1]), lambda b, i: (b, i, 0)),
                  pl.BlockSpec((None, tm, oc.shape[-1]), lambda b, i: (b, i, 0)),
                  pl.BlockSpec(wo.shape, const),
                  pl.BlockSpec((1, D), const),
                  pl.BlockSpec(wq.shape, const),
                  pl.BlockSpec((None, None, M, 2 * D), lambda b, i: (layer, b, 0, 0)),
                  pl.BlockSpec(wmo.shape, const)],
        out_specs=pl.BlockSpec((None, tm, D), lambda b, i: (b, i, 0)),
        out_shape=jax.ShapeDtypeStruct((B, S, D), F32),
        compiler_params=_cparams(("parallel", "arbitrary")),
        name="outproj_memattn",
    )(x, oa, ob, oc, wo, g, wq, kv, wmo)


HALO = 8


def _ffn_kernel(x_ref, xp_ref, g_ref, wg_ref, wv_ref, cwg_ref, cwv_ref, cbg_ref, cbv_ref, wd_ref,
                gout_ref, o_ref, *, tm, fc, norm_out):
    i = pl.program_id(1)
    x = x_ref[...]
    g = g_ref[...]
    hp = _rms(xp_ref[...], g) * jnp.where(i > 0, 1.0, 0.0)
    h = jnp.concatenate([hp, _rms(x, g)], axis=0).astype(BF16)
    F = wd_ref.shape[0]

    def conv(u, cw_ref, cb_ref, c, e):
        out = cb_ref[:, c:e]
        for j in range(CONV_WIDTH):
            shift = CONV_WIDTH - 1 - j
            out = out + cw_ref[j:j + 1, c:e] * u[HALO - shift:HALO - shift + tm]
        return out

    acc = x
    for c in range(0, F, fc):
        e = min(c + fc, F)
        gate = conv(_dot(h, wg_ref[:, c:e]), cwg_ref, cbg_ref, c, e)
        val = conv(_dot(h, wv_ref[:, c:e]), cwv_ref, cbv_ref, c, e)
        act = (gate * jax.nn.sigmoid(gate) * val).astype(BF16)
        acc = acc + _dot(act, wd_ref[c:e, :])
    o_ref[...] = _rms(acc, gout_ref[...]) if norm_out else acc


def _ffn(x, g, wg, wv, cwg, cwv, cbg, cbv, wd, gout, norm_out, tm, fc):
    B, S, D = x.shape
    F = wd.shape[0]
    const = lambda b, i: (0, 0)
    hb = tm // HALO
    return pl.pallas_call(
        functools.partial(_ffn_kernel, tm=tm, fc=fc, norm_out=norm_out),
        grid=(B, S // tm),
        in_specs=[pl.BlockSpec((None, tm, D), lambda b, i: (b, i, 0)),
                  pl.BlockSpec((None, HALO, D), lambda b, i: (b, jnp.maximum(i * hb - 1, 0), 0)),
                  pl.BlockSpec((1, D), const),
                  pl.BlockSpec((D, F), const, pipeline_mode=pl.Buffered(1)),
                  pl.BlockSpec((D, F), lambda b, i: (0, 1), pipeline_mode=pl.Buffered(1)),
                  pl.BlockSpec((CONV_WIDTH, F), const), pl.BlockSpec((CONV_WIDTH, F), const),
                  pl.BlockSpec((1, F), const), pl.BlockSpec((1, F), const),
                  pl.BlockSpec((F, D), const, pipeline_mode=pl.Buffered(1)),
                  pl.BlockSpec((1, D), const)],
        out_specs=pl.BlockSpec((None, tm, D), lambda b, i: (b, i, 0)),
        out_shape=jax.ShapeDtypeStruct((B, S, D), F32),
        compiler_params=_cparams(("parallel", "arbitrary")),
        name="conv_ffn",
    )(x, x, g, wg, wv, cwg, cwv, cbg, cbv, wd, gout)


def _tile_params(S, tq_a, tk_a):
    big = 1 << 30
    gran, (d_min, d_const) = min(tq_a, tk_a), _a_offsets(tq_a, tk_a)
    pa = []
    for h in range(A_HEADS):
        pa += [(h, d * gran, 1, 0, big, -1) for d in range(d_min, d_const + 1)]
        pa.append((h, 0, 1, 1, 0, -1))
    pb = []
    for _, r in DILATED_PATTERNS:
        for h in range(B_HEADS):
            pb.append((A_HEADS + h, DIL_BLK, r, 1, DIL_BLK, 1))
            pb.append((A_HEADS + h, 0, r, 0, DIL_BLK, 1))
    pc = [(A_HEADS + B_HEADS + h, d * LANES, 1, -big, big, -1)
          for h in range(C_HEADS) for d in range(_n_offsets(S, LANES))]
    to = lambda p: jnp.asarray(p, jnp.int32)
    return to(pa), to(pb), to(pc)


def _dil_bias_layout(tiles):
    t = tiles.reshape(len(DILATED_PATTERNS), B_HEADS, 2, DIL_BLK, DIL_BLK)
    return jnp.concatenate([t[:, :, 0], t[:, :, 1]], axis=-1).reshape(
        len(DILATED_PATTERNS), B_HEADS * DIL_BLK, 2 * DIL_BLK)


def _n_offsets(S, blk):
    return min(S // blk, REL_MAX_DIST // blk + 2)


def _a_offsets(tq, tk):
    gran = min(tq, tk)
    return -(tq // gran - 1), -(-(REL_MAX_DIST - 1 + tk) // gran)


def _in_weights(w_in_l):
    sizes = (512, 512, 512, 256, 256, 256, 256, 256, 256, 512, 64, 8)
    offs = [0]
    for s in sizes:
        offs.append(offs[-1] + s)
    (aq, ak, av, bq, bk, bv, cq, ck, cv, iq, ik, iw) = [w_in_l[:, offs[i]:offs[i + 1]] for i in range(12)]
    pad = jnp.zeros((w_in_l.shape[0], N_I - 2 * IDX_DIM - IDX_HEADS), w_in_l.dtype)
    return jnp.concatenate([aq, ak, av, iq, cq, ck, cv, bq, bk, bv, ik, ik, iw, pad], axis=1).astype(BF16)


def _forward(x, mem, rel_bias, norm_mix, w_in, lam_q1, lam_k1, lam_q2, lam_k2, subln, w_out,
             norm_mem, norm_memkv, w_mq, w_mkv, w_mo, norm_ffn, w_up, conv_w, conv_b, w_down,
             norm_final, *, tq_a, tk_a, tq_c, ch_c, ca_c, tm_proj, tm_mem, tm_ffn, fc):
    B, S, D = x.shape
    L = w_in.shape[0]
    F = w_down.shape[1]
    pa, pb, pc = _tile_params(S, tq_a, tk_a)
    bias_a = _bias_tiles(pa, rel_bias, tk_a, tq_a, LOG2E).reshape(A_HEADS, -1, tk_a, tq_a)
    d_min, d_const = _a_offsets(tq_a, tk_a)
    bias_b = _dil_bias_layout(_bias_tiles(pb, rel_bias, DIL_BLK, DIL_BLK, LOG2E))
    bias_c = _bias_tiles(pc, rel_bias, LANES, LANES, LOG2E).reshape(C_HEADS, -1, LANES, LANES)
    kv_all = _memkv(mem, norm_memkv.reshape(L, 1, D), w_mkv.astype(BF16))

    for l in range(L):
        lam_init = 0.8 - 0.6 * math.exp(-0.3 * l)
        proj, proj_b, proj_i, avt, cvt, iwt = _inproj(
            x, norm_mix[l].reshape(1, D), _in_weights(w_in[l]), tm_proj, tk_a, ca_c)
        proj = proj.reshape(B, S, N_MAIN)
        proj_i = proj_i.reshape(B, S, N_I)
        lamv = jnp.stack([lam_q1[l], lam_k1[l], lam_q2[l], lam_k2[l]], axis=0)
        o_a = _diff_attn(proj, avt, lamv, bias_a, subln[l].reshape(A_V, 1), lam_init, tq_a,
                         d_min, d_const)
        o_b = _dil_attn(proj_b.reshape(B, S, N_B), bias_b)
        o_c = _dsa_attn(proj, proj_i, iwt, cvt, bias_c, tq_c, ch_c)
        x = _outmem(x, o_a, o_b, o_c, w_out[l].astype(BF16), norm_mem[l].reshape(1, D),
                    w_mq[l].astype(BF16), kv_all, l, w_mo[l].astype(BF16), tm_mem)
        wu = w_up[l].astype(BF16)
        x = _ffn(x, norm_ffn[l].reshape(1, D), wu, wu, conv_w[l][:, :F], conv_w[l][:, F:],
                 conv_b[l][:F].reshape(1, F), conv_b[l][F:].reshape(1, F), w_down[l].astype(BF16),
                 norm_final.reshape(1, D), l == L - 1, tm_ffn, fc)
    return x


def kernel(x, mem, rel_bias, norm_mix, w_in, lam_q1, lam_k1, lam_q2, lam_k2, subln, w_out,
           norm_mem, norm_memkv, w_mq, w_mkv, w_mo, norm_ffn, w_up, conv_w, conv_b, w_down,
           norm_final):
    return _forward(x, mem, rel_bias, norm_mix, w_in, lam_q1, lam_k1, lam_q2, lam_k2, subln, w_out,
                    norm_mem, norm_memkv, w_mq, w_mkv, w_mo, norm_ffn, w_up, conv_w, conv_b, w_down,
                    norm_final, tq_a=512, tk_a=256, tq_c=512, ch_c=512, ca_c=256, tm_proj=1024, tm_mem=1024, tm_ffn=1024,
                    fc=512)
```

```python
import functools
import math

import jax
import jax.numpy as jnp
from jax import lax
from jax.experimental import pallas as pl
from jax.experimental.pallas import tpu as pltpu

F32 = jnp.float32
BF16 = jnp.bfloat16

EPS = 1e-6
NEG = -1e30
LOG2E = math.log2(math.e)
LANES = 128
VMEM_LIMIT = 56 * 1024 * 1024

A_HEADS, A_QK, A_V = 4, 64, 128
B_HEADS, B_DIM = 4, 64
C_HEADS, C_DIM = 4, 64
IDX_HEADS, IDX_DIM = 8, 64
TOPK_MAX = 256
DILATED_PATTERNS = ((128, 1), (512, 4), (2048, 16))
ONES_ROWS = 16
DIL_BLK = 128
DIL_UNROLL = 4
REL_BUCKETS, REL_MAX_DIST = 32, 2048
MEM_HEADS = 4
CONV_WIDTH = 3

N_MAIN = 2816
N_B = 768
N_I = 256


def _cparams(sem):
    return pltpu.CompilerParams(dimension_semantics=sem, vmem_limit_bytes=VMEM_LIMIT)


def _dot(a, b):
    return jnp.dot(a, b, preferred_element_type=F32)


def _dot_nt(a, b):
    return lax.dot_general(a, b, (((1,), (1,)), ((), ())), preferred_element_type=F32)


def _ones_rows(n):
    row = lax.broadcasted_iota(jnp.int32, (ONES_ROWS, n), 0)
    return jnp.where(row == 0, 1.0, 0.0).astype(BF16)


def _rms(x, g):
    return x * lax.rsqrt(jnp.mean(x * x, axis=-1, keepdims=True) + EPS) * g


def _rel_bucket(n):
    max_exact = REL_BUCKETS // 2
    nf = jnp.maximum(n, 1).astype(F32)
    large = max_exact + (jnp.log(nf / max_exact) / math.log(REL_MAX_DIST / max_exact)
                         * (REL_BUCKETS - max_exact)).astype(jnp.int32)
    large = jnp.minimum(large, REL_BUCKETS - 1)
    return jnp.where(n < max_exact, n, large)


def _bias_tiles_kernel(par_ref, tab_ref, o_ref, *, tq, tk, scale):
    t = pl.program_id(0)
    head, off, mult = par_ref[t, 0], par_ref[t, 1], par_ref[t, 2]
    lo, hi, sgn = par_ref[t, 3], par_ref[t, 4], par_ref[t, 5]
    dist = off + sgn * (lax.broadcasted_iota(jnp.int32, (tq, tk), 0)
                        - lax.broadcasted_iota(jnp.int32, (tq, tk), 1))
    bucket = _rel_bucket(jnp.maximum(dist * mult, 0))
    val = jnp.zeros((tq, tk), F32)
    for b in range(REL_BUCKETS):
        val = jnp.where(bucket == b, tab_ref[b, head] * scale, val)
    o_ref[...] = jnp.where((dist >= lo) & (dist <= hi), val, NEG)


def _bias_tiles(params, table, tq, tk, scale=1.0):
    n = params.shape[0]
    return pl.pallas_call(
        functools.partial(_bias_tiles_kernel, tq=tq, tk=tk, scale=scale),
        grid=(n,),
        in_specs=[pl.BlockSpec(memory_space=pltpu.SMEM), pl.BlockSpec(memory_space=pltpu.SMEM)],
        out_specs=pl.BlockSpec((None, tq, tk), lambda t: (t, 0, 0)),
        out_shape=jax.ShapeDtypeStruct((n, tq, tk), F32),
        compiler_params=_cparams(("arbitrary",)),
        name="bias_tiles",
    )(params, table)


def _inproj_kernel(x_ref, g_ref, w_ref, om_ref, ob_ref, oi_ref, avt_ref, cvt_ref, iwt_ref):
    h = _rms(x_ref[...], g_ref[...]).astype(BF16)
    step = 512
    for c in range(0, N_MAIN, step):
        e = min(c + step, N_MAIN)
        om_ref[:, c:e] = _dot(h, w_ref[:, c:e]).astype(BF16)
    ob_ref[...] = _dot(h, w_ref[:, N_MAIN:N_MAIN + N_B])
    oi_ref[...] = _dot(h, w_ref[:, N_MAIN + N_B:])
    tk = avt_ref.shape[-1]
    a0 = 2 * A_HEADS * LANES
    for hd in range(A_HEADS):
        vt = om_ref[:, a0 + hd * A_V:a0 + (hd + 1) * A_V].astype(F32).T
        for j in range(avt_ref.shape[1]):
            avt_ref[hd, j] = vt[:, j * tk:(j + 1) * tk].astype(BF16)
    ca = cvt_ref.shape[-1]
    cw = C_HEADS * C_DIM
    for half in range(cw // LANES):
        c0 = N_MAIN - cw + half * LANES
        vt = om_ref[:, c0:c0 + LANES].astype(F32).T
        for j in range(cvt_ref.shape[0]):
            cvt_ref[j, half * LANES:(half + 1) * LANES, :] = vt[:, j * ca:(j + 1) * ca].astype(BF16)
    iwt_ref[...] = oi_ref[:, 2 * IDX_DIM:2 * IDX_DIM + LANES].T[:IDX_HEADS]


def _inproj(x, g, w, tm, tk_a, ca_c):
    B, S, D = x.shape
    T = B * S
    nt = S // tm
    n_all = N_MAIN + N_B + N_I
    cw = C_HEADS * C_DIM
    return pl.pallas_call(
        _inproj_kernel,
        grid=(T // tm,),
        in_specs=[pl.BlockSpec((tm, D), lambda i: (i, 0)),
                  pl.BlockSpec((1, D), lambda i: (0, 0)),
                  pl.BlockSpec((D, n_all), lambda i: (0, 0))],
        out_specs=[pl.BlockSpec((tm, N_MAIN), lambda i: (i, 0)),
                   pl.BlockSpec((tm, N_B), lambda i: (i, 0)),
                   pl.BlockSpec((tm, N_I), lambda i: (i, 0)),
                   pl.BlockSpec((None, A_HEADS, tm // tk_a, A_V, tk_a),
                                lambda i: (i // nt, 0, i % nt, 0, 0)),
                   pl.BlockSpec((None, tm // ca_c, cw, ca_c), lambda i: (i // nt, i % nt, 0, 0)),
                   pl.BlockSpec((None, IDX_HEADS, tm), lambda i: (i // nt, 0, i % nt))],
        out_shape=[jax.ShapeDtypeStruct((T, N_MAIN), BF16),
                   jax.ShapeDtypeStruct((T, N_B), F32),
                   jax.ShapeDtypeStruct((T, N_I), F32),
                   jax.ShapeDtypeStruct((B, A_HEADS, S // tk_a, A_V, tk_a), BF16),
                   jax.ShapeDtypeStruct((B, S // ca_c, cw, ca_c), BF16),
                   jax.ShapeDtypeStruct((B, IDX_HEADS, S), F32)],
        compiler_params=_cparams(("parallel",)),
        name="inproj",
    )(x.reshape(T, D), g, w)


def _diff_attn_kernel(lam_ref, q_ref, k_ref, vt_ref, bias_ref, g_ref, o_ref, s_sc, acc_sc,
                      *, tq, tk, d_min, d_const, lam_init):
    qi = pl.program_id(2)
    gran = min(tq, tk)
    last = (qi * tq + tq - 1) // tk
    masked_tile = d_const - d_min + 1
    lv = lam_ref[...]
    lam = (jnp.exp(jnp.sum(lv[0:1] * lv[1:2], axis=-1, keepdims=True))
           - jnp.exp(jnp.sum(lv[2:3] * lv[3:4], axis=-1, keepdims=True)) + lam_init)
    lane = lax.broadcasted_iota(jnp.int32, (tq, LANES), 1)
    q = q_ref[...].astype(F32) * (A_QK ** -0.5 * LOG2E)
    qcat = jnp.concatenate([jnp.where(lane < A_QK, q, 0.0), jnp.where(lane >= A_QK, q, 0.0)],
                           axis=0).astype(BF16)
    acc_sc[...] = jnp.zeros(acc_sc.shape, F32)

    def scores_into(slot, j):
        jc = jnp.minimum(j, last)
        k = k_ref[pl.ds(pl.multiple_of(jc * tk, tk), tk), :]
        d = qi * (tq // gran) - jc * (tk // gran)
        bias = bias_ref[jnp.where(j <= last, jnp.minimum(d, d_const) - d_min, masked_tile)]
        s_sc[slot] = _dot_nt(k, qcat) + jnp.concatenate([bias, bias], axis=1)

    def absorb(slot, j, m_prev):
        m_new = jnp.maximum(m_prev, jnp.max(s_sc[slot], axis=0, keepdims=True))
        alpha = jnp.exp2(m_prev - m_new)
        p = jnp.exp2(s_sc[slot] - m_new)
        vt = jnp.concatenate([vt_ref[jnp.minimum(j, last)], _ones_rows(tk)], axis=0)
        acc_sc[...] = alpha * acc_sc[...] + _dot(vt, p.astype(BF16))
        return m_new

    scores_into(0, 0)

    def pair(a, m):
        scores_into(1, a + 1)
        m = absorb(0, a, m)
        scores_into(0, a + 2)
        return absorb(1, a + 1, m)

    pairs = (last + 2) // 2
    m = lax.fori_loop(0, pairs // 2, lambda jj, m: pair(4 * jj + 2, pair(4 * jj, m)),
                      jnp.full((1, 2 * tq), NEG, F32))
    lax.fori_loop(pairs // 2 * 2, pairs, lambda jj, m: pair(2 * jj, m), m)
    inv = 1.0 / acc_sc[A_V:A_V + 1, :]
    o = acc_sc[:A_V, :tq] * inv[:, :tq] - lam * (acc_sc[:A_V, tq:] * inv[:, tq:])
    o = o * lax.rsqrt(jnp.mean(o * o, axis=0, keepdims=True) + EPS) * g_ref[...] * (1.0 - lam_init)
    o_ref[...] = o.T.astype(o_ref.dtype)


def _diff_attn(proj, vt, lamv, bias_a, subln_g, lam_init, tq, d_min, d_const):
    B, S, _ = proj.shape
    nd, tk = bias_a.shape[1], bias_a.shape[2]
    return pl.pallas_call(
        functools.partial(_diff_attn_kernel, tq=tq, tk=tk, d_min=d_min, d_const=d_const,
                          lam_init=lam_init),
        grid=(A_HEADS, B, S // tq),
        in_specs=[pl.BlockSpec((4, A_QK), lambda h, b, i: (0, 0)),
                  pl.BlockSpec((None, tq, LANES), lambda h, b, i: (b, i, h)),
                  pl.BlockSpec((None, S, LANES), lambda h, b, i: (b, 0, A_HEADS + h)),
                  pl.BlockSpec((None, None, S // tk, A_V, tk), lambda h, b, i: (b, h, 0, 0, 0)),
                  pl.BlockSpec((None, nd, tk, tq), lambda h, b, i: (h, 0, 0, 0)),
                  pl.BlockSpec((A_V, 1), lambda h, b, i: (0, 0))],
        out_specs=pl.BlockSpec((None, tq, LANES), lambda h, b, i: (b, i, h)),
        out_shape=jax.ShapeDtypeStruct((B, S, A_HEADS * A_V), BF16),
        scratch_shapes=[pltpu.VMEM((2, tk, 2 * tq), F32),
                        pltpu.VMEM((A_V + ONES_ROWS, 2 * tq), F32)],
        compiler_params=_cparams(("parallel", "parallel", "arbitrary")),
        name="diff_attn",
    )(lamv, proj, proj, vt, bias_a, subln_g)


class _LaneHalves:
    def __init__(self, *refs):
        self.refs = refs

    def __getitem__(self, idx):
        return jnp.concatenate([r[idx] for r in self.refs], axis=1)

    def __setitem__(self, idx, val):
        for i, r in enumerate(self.refs):
            r[idx] = val[:, i * LANES:(i + 1) * LANES]


def _dil_kernel(q0, q1, k0, k1, v0, v1, bias_ref, o_ref, n0, n1, m0, m1, l0, l1, *, seq, patterns):
    q_ref, k_ref, v_ref = _LaneHalves(q0, q1), _LaneHalves(k0, k1), _LaneHalves(v0, v1)
    n_sc, m_sc, l_sc = _LaneHalves(n0, n1), _LaneHalves(m0, m1), _LaneHalves(l0, l1)
    blk = DIL_BLK
    width = B_HEADS * B_DIM
    head_of_lane = lax.broadcasted_iota(jnp.int32, (blk, width), 1) // B_DIM

    def per_head(x):
        parts = [jnp.broadcast_to(x[h * blk:(h + 1) * blk], (blk, width)) for h in range(B_HEADS)]
        out = parts[-1]
        for h in range(B_HEADS - 2, -1, -1):
            out = jnp.where(head_of_lane == h, parts[h], out)
        return out

    def attend(base, r, with_prev, p_idx):
        q = q_ref[pl.ds(base, blk, stride=r), :] * (B_DIM ** -0.5 * LOG2E)
        qcat = jnp.concatenate([jnp.where(head_of_lane == h, q, 0.0) for h in range(B_HEADS)],
                               axis=0).astype(BF16)
        if with_prev:
            keys = pl.ds(base - blk * r, 2 * blk, stride=r)
            bias = bias_ref[p_idx]
        else:
            keys = pl.ds(base, blk, stride=r)
            bias = bias_ref[p_idx, :, blk:]
        s = _dot_nt(qcat, k_ref[keys, :].astype(BF16)) + bias
        mx = jnp.max(s, axis=-1, keepdims=True)
        p = jnp.exp2(s - mx)
        l = jnp.sum(p, axis=-1, keepdims=True)
        o = _dot(p.astype(BF16), v_ref[keys, :].astype(BF16))
        return per_head(o), per_head(mx), per_head(l)

    def merge(base, r, res, is_first):
        o, m, l = res
        rows = pl.ds(base, blk, stride=r)
        if is_first:
            n_sc[rows, :] = o
            m_sc[rows, :] = m
            l_sc[rows, :] = l
        else:
            m_old = m_sc[rows, :]
            m_new = jnp.maximum(m_old, m)
            a = jnp.exp2(m_old - m_new)
            b = jnp.exp2(m - m_new)
            n_sc[rows, :] = a * n_sc[rows, :] + b * o
            l_sc[rows, :] = a * l_sc[rows, :] + b * l
            m_sc[rows, :] = m_new

    def sweep(count, base_of, r, with_prev, p_idx):
        def trip(width):
            def body(i, carry):
                bases = [base_of(i * width + u) for u in range(width)]
                results = [attend(b, r, with_prev, p_idx) for b in bases]
                for b, res in zip(bases, results):
                    merge(b, r, res, p_idx == len(patterns) - 1)
                return carry
            return body

        full = count // DIL_UNROLL
        lax.fori_loop(0, full, trip(DIL_UNROLL), 0)
        rest = count - full * DIL_UNROLL
        if rest:
            lax.fori_loop(full * DIL_UNROLL, count, trip(1), 0)

    for p_idx, (_, r) in reversed(list(enumerate(patterns))):
        nb = seq // (r * blk)
        sweep(r, lambda c: c, r, False, p_idx)
        sweep(r * (nb - 1), lambda idx, r=r: (idx // r + 1) * blk * r + idx % r, r, True, p_idx)

    everything = (slice(None), slice(None))
    o_ref[...] = (n_sc[everything] / l_sc[everything]).astype(o_ref.dtype)


def _dil_attn(proj_b, bias_b):
    B, S, _ = proj_b.shape
    width = B_HEADS * B_DIM
    return pl.pallas_call(
        functools.partial(_dil_kernel, seq=S, patterns=DILATED_PATTERNS),
        grid=(B,),
        in_specs=[pl.BlockSpec((None, S, LANES), functools.partial(lambda i, b: (b, 0, i), i))
                  for i in range(3 * width // LANES)]
                 + [pl.BlockSpec(bias_b.shape, lambda b: (0, 0, 0))],
        out_specs=pl.BlockSpec((None, S, width), lambda b: (b, 0, 0)),
        out_shape=jax.ShapeDtypeStruct((B, S, width), BF16),
        scratch_shapes=[pltpu.VMEM((S, LANES), F32)] * (3 * width // LANES),
        compiler_params=_cparams(("parallel",)),
        name="dilated_attn",
    )(*([proj_b] * (3 * width // LANES)), bias_b)


SEL16 = jnp.bfloat16


RADIX_LOW_BITS = 5
PEEL_CAP = 6
FLT_MIN_NORMAL = 2.0 ** -126


def _float_key(f):
    b = pltpu.bitcast(f, jnp.int32)
    return jnp.where(b >= 0, b, b ^ jnp.int32(0x7FFFFFFF))


def _key_float(k):
    return pltpu.bitcast(jnp.where(k >= 0, k, k ^ jnp.int32(0x7FFFFFFF)), F32)


def _high_half(f):
    return pltpu.bitcast(pltpu.bitcast(f, jnp.int32) & jnp.int32(-65536), F32)


def _dsa_kernel(iq_ref, ik_ref, iwt_ref, cq_ref, ck_ref, cvt_ref, bias_ref, o_ref, sc_ref, hi_ref, s_sc,
                acc_sc,
                *, tq, ch, ca, topk):
    qi = pl.program_id(1)
    q0 = qi * tq
    nch = (q0 + tq - 1) // ch + 1
    qidx = q0 + lax.broadcasted_iota(jnp.int32, (1, tq), 1)
    lane = lax.broadcasted_iota(jnp.int32, (tq, LANES), 1)
    first = lane < IDX_DIM

    w = iwt_ref[...] * (IDX_HEADS ** -0.5)
    qs = []
    for h in range(IDX_HEADS):
        tile = iq_ref[:, (h // 2) * LANES:(h // 2 + 1) * LANES].astype(F32) * (IDX_DIM ** -0.5)
        qs.append(jnp.where(first if h % 2 == 0 else ~first, tile, 0.0).astype(BF16))

    def score_body(j, carry):
        start = pl.multiple_of(j * ch, ch)
        kk = ik_ref[pl.ds(start, ch), :].astype(BF16)
        sc = jnp.zeros((ch, tq), F32)
        for h in range(IDX_HEADS):
            sc = sc + jnp.maximum(_dot_nt(kk, qs[h]), 0.0) * w[h:h + 1, :]
        kidx = start + lax.broadcasted_iota(jnp.int32, (ch, 1), 0)
        sc = jnp.where(kidx <= qidx, sc, NEG)
        sc_ref[j] = sc
        hi_ref[j] = _high_half(sc).astype(SEL16)
        return carry

    lax.fori_loop(0, nch, score_body, 0)

    rows = 32

    def count_where(pred):
        def body(j, acc):
            hit = jnp.where(pred(sc_ref[j]), 1.0, 0.0)
            return acc + jnp.sum(hit.reshape(ch // rows, rows, tq), axis=0)
        acc = lax.fori_loop(0, nch, body, jnp.zeros((rows, tq), F32))
        return jnp.sum(acc, axis=0, keepdims=True)

    def count_high_ge(v):
        one, zero16 = jnp.ones((), SEL16), jnp.zeros((), SEL16)
        def body(j, acc):
            hit = jnp.where(hi_ref[j] >= v, one, zero16).reshape(ch // rows, rows, tq)
            parts = [hit[i] for i in range(ch // rows)]
            while len(parts) > 1:
                parts = [parts[i] + parts[i + 1] for i in range(0, len(parts), 2)]
            return acc + parts[0]
        acc = lax.fori_loop(0, nch, body, jnp.zeros((rows, tq), SEL16))
        return jnp.sum(acc.astype(F32), axis=0, keepdims=True)

    kf = float(topk)
    n_valid = (qidx + 1).astype(F32)
    half_bits = 16

    def high_body(i, st):
        u, c_lo = st
        cand = u | jnp.left_shift(1, half_bits - 1 - i)
        v = _high_half(_key_float((cand - (1 << (half_bits - 1))) << half_bits)).astype(SEL16)
        c = count_high_ge(v)
        keep = c >= kf
        return jnp.where(keep, cand, u), jnp.where(keep, c, c_lo)

    u, c_lo = lax.fori_loop(0, half_bits, high_body, (jnp.zeros((1, tq), jnp.int32), n_valid))
    key_hi = (u - (1 << (half_bits - 1))) << half_bits

    def low_body(i, st):
        lo_bits, c_lo = st
        cand = lo_bits | jnp.left_shift(1, half_bits - 1 - i)
        c = count_where(lambda sc: sc >= _key_float(key_hi | cand))
        keep = c >= kf
        return jnp.where(keep, cand, lo_bits), jnp.where(keep, c, c_lo)

    lo_bits, c_lo = lax.fori_loop(0, RADIX_LOW_BITS, low_body, (jnp.zeros((1, tq), jnp.int32), c_lo))

    def min_ge(t):
        def body(j, acc):
            sc = sc_ref[j]
            x = jnp.where(sc >= t, sc, -NEG)
            return jnp.minimum(acc, jnp.min(x.reshape(ch // rows, rows, tq), axis=0))
        acc = lax.fori_loop(0, nch, body, jnp.full((rows, tq), -NEG, F32))
        return jnp.min(acc, axis=0, keepdims=True)

    def open_rows(c_lo, done):
        return jnp.logical_and(jnp.logical_and(n_valid > kf, c_lo > kf), done == 0.0)

    def any_row(mask):
        return jnp.max(jnp.where(mask, 1.0, 0.0)) > 0.0

    def peel_cond(st):
        _, c_lo, done, it = st
        return jnp.logical_and(it < PEEL_CAP, any_row(open_rows(c_lo, done)))

    def peel_body(st):
        t, c_lo, done, it = st
        is_open = open_rows(c_lo, done)
        smallest = min_ge(t)
        above = jnp.where(smallest == 0.0, FLT_MIN_NORMAL, _key_float(_float_key(smallest) + 1))
        c = count_where(lambda sc: sc >= above)
        enough = jnp.logical_and(is_open, c >= kf)
        at_tie = jnp.logical_and(is_open, c < kf)
        t = jnp.where(enough, above, jnp.where(at_tie, smallest, t))
        return t, jnp.where(enough, c, c_lo), jnp.where(at_tie, 1.0, done), it + 1

    t_peel, c_peel, done, _ = lax.while_loop(
        peel_cond, peel_body,
        (_key_float(key_hi | lo_bits), c_lo, jnp.zeros((1, tq), F32), jnp.int32(0)))

    def finish_by_radix(_):
        bits, c = lax.fori_loop(RADIX_LOW_BITS, half_bits, low_body, (lo_bits, c_lo))
        return _key_float(key_hi | bits), c

    thr, c_lo = lax.cond(any_row(open_rows(c_peel, done)), finish_by_radix,
                         lambda _: (t_peel, c_peel), 0)
    thr = jnp.where(n_valid > kf, thr, 0.5 * NEG)

    tie_rows = jnp.logical_and(n_valid > kf, c_lo > kf)

    @pl.when(jnp.max(jnp.where(tie_rows, 1.0, 0.0)) > 0.0)
    def _():
        surplus = jnp.where(tie_rows, c_lo - kf, 0.0)
        upper = (lax.broadcasted_iota(jnp.int32, (ch, ch), 1)
                 >= lax.broadcasted_iota(jnp.int32, (ch, ch), 0)).astype(BF16)

        def drop_body(st):
            i, seen = st
            j = nch - 1 - i
            sc = sc_ref[j]
            eq = sc == thr
            eqf = jnp.where(eq, 1.0, 0.0)
            rank = _dot(upper, eqf.astype(BF16))
            sc_ref[j] = jnp.where(eq, jnp.where(rank <= surplus - seen, NEG, sc), sc)
            return i + 1, seen + jnp.sum(eqf, axis=0, keepdims=True)

        def more_to_drop(st):
            i, seen = st
            return jnp.logical_and(i < nch, any_row(seen < surplus))

        lax.while_loop(more_to_drop, drop_body, (jnp.int32(0), jnp.zeros((1, tq), F32)))

    cw = C_HEADS * C_DIM
    lane_c = lax.broadcasted_iota(jnp.int32, (tq, cw), 1)
    cq = cq_ref[...].astype(F32) * (C_DIM ** -0.5 * LOG2E)
    qcat = jnp.concatenate([jnp.where(lane_c // C_DIM == h, cq, 0.0) for h in range(C_HEADS)],
                           axis=0).astype(BF16)
    acc_sc[...] = jnp.zeros(acc_sc.shape, F32)
    nd = bias_ref.shape[1]
    last = (q0 + tq - 1) // ca

    def scores_into(slot, j):
        jc = jnp.minimum(j, last)
        kc = ck_ref[pl.ds(pl.multiple_of(jc * ca, ca), ca), :]
        sub = pl.multiple_of((jc % (ch // ca)) * ca, ca)
        sel = sc_ref[jc // (ch // ca), pl.ds(sub, ca), :] >= jnp.where(j <= last, thr, -NEG)
        bias = jnp.concatenate([
            jnp.concatenate([
                bias_ref[h, jnp.clip(q0 // LANES + c - (jc * (ca // LANES) + u), 0, nd - 1)]
                for h in range(C_HEADS) for c in range(tq // LANES)], axis=1)
            for u in range(ca // LANES)], axis=0)
        s_sc[slot] = jnp.where(jnp.concatenate([sel] * C_HEADS, axis=1),
                               _dot_nt(kc, qcat) + bias, NEG)

    def absorb(slot, j, m_prev):
        m_new = jnp.maximum(m_prev, jnp.max(s_sc[slot], axis=0, keepdims=True))
        alpha = jnp.exp2(m_prev - m_new)
        pb = jnp.exp2(s_sc[slot] - m_new).astype(BF16)
        jc = jnp.minimum(j, last)
        for h in range(C_HEADS):
            cols = slice(h * tq, (h + 1) * tq)
            vt = jnp.concatenate([cvt_ref[jc, h * C_DIM:(h + 1) * C_DIM, :], _ones_rows(ca)], axis=0)
            acc_sc[h] = alpha[:, cols] * acc_sc[h] + _dot(vt, pb[:, cols])
        return m_new

    scores_into(0, 0)

    def attn_body(jj, m):
        a = 2 * jj
        scores_into(1, a + 1)
        m = absorb(0, a, m)
        scores_into(0, a + 2)
        return absorb(1, a + 1, m)

    lax.fori_loop(0, (last + 2) // 2, attn_body, jnp.full((1, C_HEADS * tq), NEG, F32))
    o_t = jnp.concatenate([acc_sc[h, :C_DIM] * (1.0 / acc_sc[h, C_DIM:C_DIM + 1])
                           for h in range(C_HEADS)], axis=0)
    o_ref[...] = o_t.T.astype(o_ref.dtype)


def _dsa_attn(proj, proj_i, iwt, cvt, bias_c, tq, ch):
    B, S, _ = proj.shape
    ca = cvt.shape[-1]
    topk = min(TOPK_MAX, S // 4)
    nd = bias_c.shape[1]
    cw = C_HEADS * C_DIM
    return pl.pallas_call(
        functools.partial(_dsa_kernel, tq=tq, ch=ch, ca=ca, topk=topk),
        grid=(B, S // tq),
        in_specs=[pl.BlockSpec((None, tq, IDX_HEADS * IDX_DIM), lambda b, i: (b, i, 3)),
                  pl.BlockSpec((None, S, LANES), lambda b, i: (b, 0, 0)),
                  pl.BlockSpec((None, IDX_HEADS, tq), lambda b, i: (b, 0, i)),
                  pl.BlockSpec((None, tq, cw), lambda b, i: (b, i, 8)),
                  pl.BlockSpec((None, S, cw), lambda b, i: (b, 0, 9)),
                  pl.BlockSpec((None, S // ca, cw, ca), lambda b, i: (b, 0, 0, 0)),
                  pl.BlockSpec((C_HEADS, nd, LANES, LANES), lambda b, i: (0, 0, 0, 0))],
        out_specs=pl.BlockSpec((None, tq, cw), lambda b, i: (b, i, 0)),
        out_shape=jax.ShapeDtypeStruct((B, S, cw), BF16),
        scratch_shapes=[pltpu.VMEM((S // ch, ch, tq), F32),
                        pltpu.VMEM((S // ch, ch, tq), SEL16),
                        pltpu.VMEM((2, ca, C_HEADS * tq), F32),
                        pltpu.VMEM((C_HEADS, C_DIM + ONES_ROWS, tq), F32)],
        compiler_params=_cparams(("parallel", "arbitrary")),
        name="dsa_attn",
    )(proj, proj_i, iwt, proj, proj, cvt, bias_c)


def _memkv_kernel(mem_ref, g_ref, w_ref, o_ref):
    h = _rms(mem_ref[...], g_ref[...]).astype(BF16)
    o_ref[...] = _dot(h, w_ref[...]).astype(o_ref.dtype)


def _memkv(mem, g, w):
    B, M, D = mem.shape
    L = w.shape[0]
    return pl.pallas_call(
        _memkv_kernel,
        grid=(L, B),
        in_specs=[pl.BlockSpec((None, M, D), lambda l, b: (b, 0, 0)),
                  pl.BlockSpec((None, 1, D), lambda l, b: (l, 0, 0)),
                  pl.BlockSpec((None, D, 2 * D), lambda l, b: (l, 0, 0))],
        out_specs=pl.BlockSpec((None, None, M, 2 * D), lambda l, b: (l, b, 0, 0)),
        out_shape=jax.ShapeDtypeStruct((L, B, M, 2 * D), BF16),
        compiler_params=_cparams(("parallel", "arbitrary")),
        name="mem_kv",
    )(mem, g, w)


def _outmem_kernel(x_ref, oa_ref, ob_ref, oc_ref, wo_ref, g_ref, wq_ref, kv_ref, wmo_ref, o_ref):
    D = x_ref.shape[-1]
    na, nb = oa_ref.shape[-1], ob_ref.shape[-1]
    x = (x_ref[...] + _dot(oa_ref[...], wo_ref[0:na]) + _dot(ob_ref[...], wo_ref[na:na + nb])
         + _dot(oc_ref[...], wo_ref[na + nb:]))
    h = _rms(x, g_ref[...]).astype(BF16)
    hd = D // MEM_HEADS
    q = (_dot(h, wq_ref[...]) * (hd ** -0.5)).astype(BF16)
    outs = []
    for hh in range(MEM_HEADS):
        k = kv_ref[:, hh * hd:(hh + 1) * hd]
        v = kv_ref[:, D + hh * hd:D + (hh + 1) * hd]
        s = _dot_nt(q[:, hh * hd:(hh + 1) * hd], k)
        p = jnp.exp(s - jnp.max(s, axis=-1, keepdims=True))
        p = p / jnp.sum(p, axis=-1, keepdims=True)
        outs.append(_dot(p.astype(BF16), v).astype(BF16))
    o = jnp.concatenate(outs, axis=-1)
    o_ref[...] = x + _dot(o, wmo_ref[...])


def _outmem(x, oa, ob, oc, wo, g, wq, kv, layer, wmo, tm):
    B, S, D = x.shape
    M = kv.shape[2]
    const = lambda b, i: (0, 0)
    return pl.pallas_call(
        _outmem_kernel,
        grid=(B, S // tm),
        in_specs=[pl.BlockSpec((None, tm, D), lambda b, i: (b, i, 0)),
                  pl.BlockSpec((None, tm, oa.shape[-1]), lambda b, i: (b, i, 0)),
                  pl.BlockSpec((None, tm, ob.shape[-1]), lambda b, i: (b, i, 0)),
                  pl.BlockSpec((None, tm, oc.shape[-1]), lambda b, i: (b, i, 0)),
                  pl.BlockSpec(wo.shape, const),
                  pl.BlockSpec((1, D), const),
                  pl.BlockSpec(wq.shape, const),
                  pl.BlockSpec((None, None, M, 2 * D), lambda b, i: (layer, b, 0, 0)),
                  pl.BlockSpec(wmo.shape, const)],
        out_specs=pl.BlockSpec((None, tm, D), lambda b, i: (b, i, 0)),
        out_shape=jax.ShapeDtypeStruct((B, S, D), F32),
        compiler_params=_cparams(("parallel", "arbitrary")),
        name="outproj_memattn",
    )(x, oa, ob, oc, wo, g, wq, kv, wmo)


HALO = 8


def _ffn_kernel(x_ref, xp_ref, g_ref, wg_ref, wv_ref, cwg_ref, cwv_ref, cbg_ref, cbv_ref, wd_ref,
                gout_ref, o_ref, *, tm, fc, norm_out):
    i = pl.program_id(1)
    x = x_ref[...]
    g = g_ref[...]
    hp = _rms(xp_ref[...], g) * jnp.where(i > 0, 1.0, 0.0)
    h = jnp.concatenate([hp, _rms(x, g)], axis=0).astype(BF16)
    F = wd_ref.shape[0]

    def conv(u, cw_ref, cb_ref, c, e):
        out = cb_ref[:, c:e]
        for j in range(CONV_WIDTH):
            shift = CONV_WIDTH - 1 - j
            out = out + cw_ref[j:j + 1, c:e] * u[HALO - shift:HALO - shift + tm]
        return out

    acc = x
    for c in range(0, F, fc):
        e = min(c + fc, F)
        gate = conv(_dot(h, wg_ref[:, c:e]), cwg_ref, cbg_ref, c, e)
        val = conv(_dot(h, wv_ref[:, c:e]), cwv_ref, cbv_ref, c, e)
        act = (gate * jax.nn.sigmoid(gate) * val).astype(BF16)
        acc = acc + _dot(act, wd_ref[c:e, :])
    o_ref[...] = _rms(acc, gout_ref[...]) if norm_out else acc


def _ffn(x, g, wg, wv, cwg, cwv, cbg, cbv, wd, gout, norm_out, tm, fc):
    B, S, D = x.shape
    F = wd.shape[0]
    const = lambda b, i: (0, 0)
    hb = tm // HALO
    return pl.pallas_call(
        functools.partial(_ffn_kernel, tm=tm, fc=fc, norm_out=norm_out),
        grid=(B, S // tm),
        in_specs=[pl.BlockSpec((None, tm, D), lambda b, i: (b, i, 0)),
                  pl.BlockSpec((None, HALO, D), lambda b, i: (b, jnp.maximum(i * hb - 1, 0), 0)),
                  pl.BlockSpec((1, D), const),
                  pl.BlockSpec((D, F), const, pipeline_mode=pl.Buffered(1)),
                  pl.BlockSpec((D, F), lambda b, i: (0, 1), pipeline_mode=pl.Buffered(1)),
                  pl.BlockSpec((CONV_WIDTH, F), const), pl.BlockSpec((CONV_WIDTH, F), const),
                  pl.BlockSpec((1, F), const), pl.BlockSpec((1, F), const),
                  pl.BlockSpec((F, D), const, pipeline_mode=pl.Buffered(1)),
                  pl.BlockSpec((1, D), const)],
        out_specs=pl.BlockSpec((None, tm, D), lambda b, i: (b, i, 0)),
        out_shape=jax.ShapeDtypeStruct((B, S, D), F32),
        compiler_params=_cparams(("parallel", "arbitrary")),
        name="conv_ffn",
    )(x, x, g, wg, wv, cwg, cwv, cbg, cbv, wd, gout)


def _tile_params(S, tq_a, tk_a):
    big = 1 << 30
    gran, (d_min, d_const) = min(tq_a, tk_a), _a_offsets(tq_a, tk_a)
    pa = []
    for h in range(A_HEADS):
        pa += [(h, d * gran, 1, 0, big, -1) for d in range(d_min, d_const + 1)]
        pa.append((h, 0, 1, 1, 0, -1))
    pb = []
    for _, r in DILATED_PATTERNS:
        for h in range(B_HEADS):
            pb.append((A_HEADS + h, DIL_BLK, r, 1, DIL_BLK, 1))
            pb.append((A_HEADS + h, 0, r, 0, DIL_BLK, 1))
    pc = [(A_HEADS + B_HEADS + h, d * LANES, 1, -big, big, -1)
          for h in range(C_HEADS) for d in range(_n_offsets(S, LANES))]
    to = lambda p: jnp.asarray(p, jnp.int32)
    return to(pa), to(pb), to(pc)


def _dil_bias_layout(tiles):
    t = tiles.reshape(len(DILATED_PATTERNS), B_HEADS, 2, DIL_BLK, DIL_BLK)
    return jnp.concatenate([t[:, :, 0], t[:, :, 1]], axis=-1).reshape(
        len(DILATED_PATTERNS), B_HEADS * DIL_BLK, 2 * DIL_BLK)


def _n_offsets(S, blk):
    return min(S // blk, REL_MAX_DIST // blk + 2)


def _a_offsets(tq, tk):
    gran = min(tq, tk)
    return -(tq // gran - 1), -(-(REL_MAX_DIST - 1 + tk) // gran)


def _in_weights(w_in_l):
    sizes = (512, 512, 512, 256, 256, 256, 256, 256, 256, 512, 64, 8)
    offs = [0]
    for s in sizes:
        offs.append(offs[-1] + s)
    (aq, ak, av, bq, bk, bv, cq, ck, cv, iq, ik, iw) = [w_in_l[:, offs[i]:offs[i + 1]] for i in range(12)]
    pad = jnp.zeros((w_in_l.shape[0], N_I - 2 * IDX_DIM - IDX_HEADS), w_in_l.dtype)
    return jnp.concatenate([aq, ak, av, iq, cq, ck, cv, bq, bk, bv, ik, ik, iw, pad], axis=1).astype(BF16)


def _forward(x, mem, rel_bias, norm_mix, w_in, lam_q1, lam_k1, lam_q2, lam_k2, subln, w_out,
             norm_mem, norm_memkv, w_mq, w_mkv, w_mo, norm_ffn, w_up, conv_w, conv_b, w_down,
             norm_final, *, tq_a, tk_a, tq_c, ch_c, ca_c, tm_proj, tm_mem, tm_ffn, fc):
    B, S, D = x.shape
    L = w_in.shape[0]
    F = w_down.shape[1]
    pa, pb, pc = _tile_params(S, tq_a, tk_a)
    bias_a = _bias_tiles(pa, rel_bias, tk_a, tq_a, LOG2E).reshape(A_HEADS, -1, tk_a, tq_a)
    d_min, d_const = _a_offsets(tq_a, tk_a)
    bias_b = _dil_bias_layout(_bias_tiles(pb, rel_bias, DIL_BLK, DIL_BLK, LOG2E))
    bias_c = _bias_tiles(pc, rel_bias, LANES, LANES, LOG2E).reshape(C_HEADS, -1, LANES, LANES)
    kv_all = _memkv(mem, norm_memkv.reshape(L, 1, D), w_mkv.astype(BF16))

    for l in range(L):
        lam_init = 0.8 - 0.6 * math.exp(-0.3 * l)
        proj, proj_b, proj_i, avt, cvt, iwt = _inproj(
            x, norm_mix[l].reshape(1, D), _in_weights(w_in[l]), tm_proj, tk_a, ca_c)
        proj = proj.reshape(B, S, N_MAIN)
        proj_i = proj_i.reshape(B, S, N_I)
        lamv = jnp.stack([lam_q1[l], lam_k1[l], lam_q2[l], lam_k2[l]], axis=0)
        o_a = _diff_attn(proj, avt, lamv, bias_a, subln[l].reshape(A_V, 1), lam_init, tq_a,
                         d_min, d_const)
        o_b = _dil_attn(proj_b.reshape(B, S, N_B), bias_b)
        o_c = _dsa_attn(proj, proj_i, iwt, cvt, bias_c, tq_c, ch_c)
        x = _outmem(x, o_a, o_b, o_c, w_out[l].astype(BF16), norm_mem[l].reshape(1, D),
                    w_mq[l].astype(BF16), kv_all, l, w_mo[l].astype(BF16), tm_mem)
        wu = w_up[l].astype(BF16)
        x = _ffn(x, norm_ffn[l].reshape(1, D), wu, wu, conv_w[l][:, :F], conv_w[l][:, F:],
                 conv_b[l][:F].reshape(1, F), conv_b[l][F:].reshape(1, F), w_down[l].astype(BF16),
                 norm_final.reshape(1, D), l == L - 1, tm_ffn, fc)
    return x


def kernel(x, mem, rel_bias, norm_mix, w_in, lam_q1, lam_k1, lam_q2, lam_k2, subln, w_out,
           norm_mem, norm_memkv, w_mq, w_mkv, w_mo, norm_ffn, w_up, conv_w, conv_b, w_down,
           norm_final):
    return _forward(x, mem, rel_bias, norm_mix, w_in, lam_q1, lam_k1, lam_q2, lam_k2, subln, w_out,
                    norm_mem, norm_memkv, w_mq, w_mkv, w_mo, norm_ffn, w_up, conv_w, conv_b, w_down,
                    norm_final, tq_a=512, tk_a=256, tq_c=512, ch_c=512, ca_c=256, tm_proj=1024, tm_mem=1024, tm_ffn=1024,
                    fc=512)
```

```python
import functools
import math

import jax
import jax.numpy as jnp
from jax import lax
from jax.experimental import pallas as pl
from jax.experimental.pallas import tpu as pltpu

F32 = jnp.float32
BF16 = jnp.bfloat16

EPS = 1e-6
NEG = -1e30
LOG2E = math.log2(math.e)
LANES = 128
VMEM_LIMIT = 56 * 1024 * 1024

A_HEADS, A_QK, A_V = 4, 64, 128
B_HEADS, B_DIM = 4, 64
C_HEADS, C_DIM = 4, 64
IDX_HEADS, IDX_DIM = 8, 64
TOPK_MAX = 256
DILATED_PATTERNS = ((128, 1), (512, 4), (2048, 16))
ONES_ROWS = 16
DIL_BLK = 128
DIL_UNROLL = 4
REL_BUCKETS, REL_MAX_DIST = 32, 2048
MEM_HEADS = 4
CONV_WIDTH = 3

N_MAIN = 2816
N_B = 768
N_I = 256


def _cparams(sem):
    return pltpu.CompilerParams(dimension_semantics=sem, vmem_limit_bytes=VMEM_LIMIT)


def _dot(a, b):
    return jnp.dot(a, b, preferred_element_type=F32)


def _dot_nt(a, b):
    return lax.dot_general(a, b, (((1,), (1,)), ((), ())), preferred_element_type=F32)


def _ones_rows(n):
    row = lax.broadcasted_iota(jnp.int32, (ONES_ROWS, n), 0)
    return jnp.where(row == 0, 1.0, 0.0).astype(BF16)


def _rms(x, g):
    return x * lax.rsqrt(jnp.mean(x * x, axis=-1, keepdims=True) + EPS) * g


def _rel_bucket(n):
    max_exact = REL_BUCKETS // 2
    nf = jnp.maximum(n, 1).astype(F32)
    large = max_exact + (jnp.log(nf / max_exact) / math.log(REL_MAX_DIST / max_exact)
                         * (REL_BUCKETS - max_exact)).astype(jnp.int32)
    large = jnp.minimum(large, REL_BUCKETS - 1)
    return jnp.where(n < max_exact, n, large)


def _bias_tiles_kernel(par_ref, tab_ref, o_ref, *, tq, tk, scale):
    t = pl.program_id(0)
    head, off, mult = par_ref[t, 0], par_ref[t, 1], par_ref[t, 2]
    lo, hi, sgn = par_ref[t, 3], par_ref[t, 4], par_ref[t, 5]
    dist = off + sgn * (lax.broadcasted_iota(jnp.int32, (tq, tk), 0)
                        - lax.broadcasted_iota(jnp.int32, (tq, tk), 1))
    bucket = _rel_bucket(jnp.maximum(dist * mult, 0))
    val = jnp.zeros((tq, tk), F32)
    for b in range(REL_BUCKETS):
        val = jnp.where(bucket == b, tab_ref[b, head] * scale, val)
    o_ref[...] = jnp.where((dist >= lo) & (dist <= hi), val, NEG)


def _bias_tiles(params, table, tq, tk, scale=1.0):
    n = params.shape[0]
    return pl.pallas_call(
        functools.partial(_bias_tiles_kernel, tq=tq, tk=tk, scale=scale),
        grid=(n,),
        in_specs=[pl.BlockSpec(memory_space=pltpu.SMEM), pl.BlockSpec(memory_space=pltpu.SMEM)],
        out_specs=pl.BlockSpec((None, tq, tk), lambda t: (t, 0, 0)),
        out_shape=jax.ShapeDtypeStruct((n, tq, tk), F32),
        compiler_params=_cparams(("arbitrary",)),
        name="bias_tiles",
    )(params, table)


def _inproj_kernel(x_ref, g_ref, w_ref, om_ref, ob_ref, oi_ref, avt_ref, cvt_ref, iwt_ref):
    h = _rms(x_ref[...], g_ref[...]).astype(BF16)
    step = 512
    for c in range(0, N_MAIN, step):
        e = min(c + step, N_MAIN)
        om_ref[:, c:e] = _dot(h, w_ref[:, c:e]).astype(BF16)
    ob_ref[...] = _dot(h, w_ref[:, N_MAIN:N_MAIN + N_B])
    oi_ref[...] = _dot(h, w_ref[:, N_MAIN + N_B:])
    tk = avt_ref.shape[-1]
    a0 = 2 * A_HEADS * LANES
    for hd in range(A_HEADS):
        vt = om_ref[:, a0 + hd * A_V:a0 + (hd + 1) * A_V].astype(F32).T
        for j in range(avt_ref.shape[1]):
            avt_ref[hd, j] = vt[:, j * tk:(j + 1) * tk].astype(BF16)
    ca = cvt_ref.shape[-1]
    cw = C_HEADS * C_DIM
    for half in range(cw // LANES):
        c0 = N_MAIN - cw + half * LANES
        vt = om_ref[:, c0:c0 + LANES].astype(F32).T
        for j in range(cvt_ref.shape[0]):
            cvt_ref[j, half * LANES:(half + 1) * LANES, :] = vt[:, j * ca:(j + 1) * ca].astype(BF16)
    iwt_ref[...] = oi_ref[:, 2 * IDX_DIM:2 * IDX_DIM + LANES].T[:IDX_HEADS]


def _inproj(x, g, w, tm, tk_a, ca_c):
    B, S, D = x.shape
    T = B * S
    nt = S // tm
    n_all = N_MAIN + N_B + N_I
    cw = C_HEADS * C_DIM
    return pl.pallas_call(
        _inproj_kernel,
        grid=(T // tm,),
        in_specs=[pl.BlockSpec((tm, D), lambda i: (i, 0)),
                  pl.BlockSpec((1, D), lambda i: (0, 0)),
                  pl.BlockSpec((D, n_all), lambda i: (0, 0))],
        out_specs=[pl.BlockSpec((tm, N_MAIN), lambda i: (i, 0)),
                   pl.BlockSpec((tm, N_B), lambda i: (i, 0)),
                   pl.BlockSpec((tm, N_I), lambda i: (i, 0)),
                   pl.BlockSpec((None, A_HEADS, tm // tk_a, A_V, tk_a),
                                lambda i: (i // nt, 0, i % nt, 0, 0)),
                   pl.BlockSpec((None, tm // ca_c, cw, ca_c), lambda i: (i // nt, i % nt, 0, 0)),
                   pl.BlockSpec((None, IDX_HEADS, tm), lambda i: (i // nt, 0, i % nt))],
        out_shape=[jax.ShapeDtypeStruct((T, N_MAIN), BF16),
                   jax.ShapeDtypeStruct((T, N_B), F32),
                   jax.ShapeDtypeStruct((T, N_I), F32),
                   jax.ShapeDtypeStruct((B, A_HEADS, S // tk_a, A_V, tk_a), BF16),
                   jax.ShapeDtypeStruct((B, S // ca_c, cw, ca_c), BF16),
                   jax.ShapeDtypeStruct((B, IDX_HEADS, S), F32)],
        compiler_params=_cparams(("parallel",)),
        name="inproj",
    )(x.reshape(T, D), g, w)


def _diff_attn_kernel(lam_ref, q_ref, k_ref, vt_ref, bias_ref, g_ref, o_ref, s_sc, acc_sc,
                      *, tq, tk, d_min, d_const, lam_init):
    qi = pl.program_id(2)
    gran = min(tq, tk)
    last = (qi * tq + tq - 1) // tk
    masked_tile = d_const - d_min + 1
    lv = lam_ref[...]
    lam = (jnp.exp(jnp.sum(lv[0:1] * lv[1:2], axis=-1, keepdims=True))
           - jnp.exp(jnp.sum(lv[2:3] * lv[3:4], axis=-1, keepdims=True)) + lam_init)
    lane = lax.broadcasted_iota(jnp.int32, (tq, LANES), 1)
    q = q_ref[...].astype(F32) * (A_QK ** -0.5 * LOG2E)
    qcat = jnp.concatenate([jnp.where(lane < A_QK, q, 0.0), jnp.where(lane >= A_QK, q, 0.0)],
                           axis=0).astype(BF16)
    acc_sc[...] = jnp.zeros(acc_sc.shape, F32)

    def scores_into(slot, j):
        jc = jnp.minimum(j, last)
        k = k_ref[pl.ds(pl.multiple_of(jc * tk, tk), tk), :]
        d = qi * (tq // gran) - jc * (tk // gran)
        bias = bias_ref[jnp.where(j <= last, jnp.minimum(d, d_const) - d_min, masked_tile)]
        s_sc[slot] = _dot_nt(k, qcat) + jnp.concatenate([bias, bias], axis=1)

    def absorb(slot, j, m_prev):
        m_new = jnp.maximum(m_prev, jnp.max(s_sc[slot], axis=0, keepdims=True))
        alpha = jnp.exp2(m_prev - m_new)
        p = jnp.exp2(s_sc[slot] - m_new)
        vt = jnp.concatenate([vt_ref[jnp.minimum(j, last)], _ones_rows(tk)], axis=0)
        acc_sc[...] = alpha * acc_sc[...] + _dot(vt, p.astype(BF16))
        return m_new

    scores_into(0, 0)

    def pair(a, m):
        scores_into(1, a + 1)
        m = absorb(0, a, m)
        scores_into(0, a + 2)
        return absorb(1, a + 1, m)

    pairs = (last + 2) // 2
    m = lax.fori_loop(0, pairs // 2, lambda jj, m: pair(4 * jj + 2, pair(4 * jj, m)),
                      jnp.full((1, 2 * tq), NEG, F32))
    lax.fori_loop(pairs // 2 * 2, pairs, lambda jj, m: pair(2 * jj, m), m)
    inv = 1.0 / acc_sc[A_V:A_V + 1, :]
    o = acc_sc[:A_V, :tq] * inv[:, :tq] - lam * (acc_sc[:A_V, tq:] * inv[:, tq:])
    o = o * lax.rsqrt(jnp.mean(o * o, axis=0, keepdims=True) + EPS) * g_ref[...] * (1.0 - lam_init)
    o_ref[...] = o.T.astype(o_ref.dtype)


def _diff_attn(proj, vt, lamv, bias_a, subln_g, lam_init, tq, d_min, d_const):
    B, S, _ = proj.shape
    nd, tk = bias_a.shape[1], bias_a.shape[2]
    return pl.pallas_call(
        functools.partial(_diff_attn_kernel, tq=tq, tk=tk, d_min=d_min, d_const=d_const,
                          lam_init=lam_init),
        grid=(A_HEADS, B, S // tq),
        in_specs=[pl.BlockSpec((4, A_QK), lambda h, b, i: (0, 0)),
                  pl.BlockSpec((None, tq, LANES), lambda h, b, i: (b, i, h)),
                  pl.BlockSpec((None, S, LANES), lambda h, b, i: (b, 0, A_HEADS + h)),
                  pl.BlockSpec((None, None, S // tk, A_V, tk), lambda h, b, i: (b, h, 0, 0, 0)),
                  pl.BlockSpec((None, nd, tk, tq), lambda h, b, i: (h, 0, 0, 0)),
                  pl.BlockSpec((A_V, 1), lambda h, b, i: (0, 0))],
        out_specs=pl.BlockSpec((None, tq, LANES), lambda h, b, i: (b, i, h)),
        out_shape=jax.ShapeDtypeStruct((B, S, A_HEADS * A_V), BF16),
        scratch_shapes=[pltpu.VMEM((2, tk, 2 * tq), F32),
                        pltpu.VMEM((A_V + ONES_ROWS, 2 * tq), F32)],
        compiler_params=_cparams(("parallel", "parallel", "arbitrary")),
        name="diff_attn",
    )(lamv, proj, proj, vt, bias_a, subln_g)


class _LaneHalves:
    def __init__(self, *refs):
        self.refs = refs

    def __getitem__(self, idx):
        return jnp.concatenate([r[idx] for r in self.refs], axis=1)

    def __setitem__(self, idx, val):
        for i, r in enumerate(self.refs):
            r[idx] = val[:, i * LANES:(i + 1) * LANES]


def _dil_kernel(q0, q1, k0, k1, v0, v1, bias_ref, o_ref, n0, n1, m0, m1, l0, l1, *, seq, patterns):
    q_ref, k_ref, v_ref = _LaneHalves(q0, q1), _LaneHalves(k0, k1), _LaneHalves(v0, v1)
    n_sc, m_sc, l_sc = _LaneHalves(n0, n1), _LaneHalves(m0, m1), _LaneHalves(l0, l1)
    blk = DIL_BLK
    width = B_HEADS * B_DIM
    head_of_lane = lax.broadcasted_iota(jnp.int32, (blk, width), 1) // B_DIM

    def per_head(x):
        parts = [jnp.broadcast_to(x[h * blk:(h + 1) * blk], (blk, width)) for h in range(B_HEADS)]
        out = parts[-1]
        for h in range(B_HEADS - 2, -1, -1):
            out = jnp.where(head_of_lane == h, parts[h], out)
        return out

    def attend(base, r, with_prev, p_idx):
        q = q_ref[pl.ds(base, blk, stride=r), :] * (B_DIM ** -0.5 * LOG2E)
        qcat = jnp.concatenate([jnp.where(head_of_lane == h, q, 0.0) for h in range(B_HEADS)],
                               axis=0).astype(BF16)
        if with_prev:
            keys = pl.ds(base - blk * r, 2 * blk, stride=r)
            bias = bias_ref[p_idx]
        else:
            keys = pl.ds(base, blk, stride=r)
            bias = bias_ref[p_idx, :, blk:]
        s = _dot_nt(qcat, k_ref[keys, :].astype(BF16)) + bias
        mx = jnp.max(s, axis=-1, keepdims=True)
        p = jnp.exp2(s - mx)
        l = jnp.sum(p, axis=-1, keepdims=True)
        o = _dot(p.astype(BF16), v_ref[keys, :].astype(BF16))
        return per_head(o), per_head(mx), per_head(l)

    def merge(base, r, res, is_first):
        o, m, l = res
        rows = pl.ds(base, blk, stride=r)
        if is_first:
            n_sc[rows, :] = o
            m_sc[rows, :] = m
            l_sc[rows, :] = l
        else:
            m_old = m_sc[rows, :]
            m_new = jnp.maximum(m_old, m)
            a = jnp.exp2(m_old - m_new)
            b = jnp.exp2(m - m_new)
            n_sc[rows, :] = a * n_sc[rows, :] + b * o
            l_sc[rows, :] = a * l_sc[rows, :] + b * l
            m_sc[rows, :] = m_new

    def sweep(count, base_of, r, with_prev, p_idx):
        def trip(width):
            def body(i, carry):
                bases = [base_of(i * width + u) for u in range(width)]
                results = [attend(b, r, with_prev, p_idx) for b in bases]
                for b, res in zip(bases, results):
                    merge(b, r, res, p_idx == len(patterns) - 1)
                return carry
            return body

        full = count // DIL_UNROLL
        lax.fori_loop(0, full, trip(DIL_UNROLL), 0)
        rest = count - full * DIL_UNROLL
        if rest:
            lax.fori_loop(full * DIL_UNROLL, count, trip(1), 0)

    for p_idx, (_, r) in reversed(list(enumerate(patterns))):
        nb = seq // (r * blk)
        sweep(r, lambda c: c, r, False, p_idx)
        sweep(r * (nb - 1), lambda idx, r=r: (idx // r + 1) * blk * r + idx % r, r, True, p_idx)

    everything = (slice(None), slice(None))
    o_ref[...] = (n_sc[everything] / l_sc[everything]).astype(o_ref.dtype)


def _dil_attn(proj_b, bias_b):
    B, S, _ = proj_b.shape
    width = B_HEADS * B_DIM
    return pl.pallas_call(
        functools.partial(_dil_kernel, seq=S, patterns=DILATED_PATTERNS),
        grid=(B,),
        in_specs=[pl.BlockSpec((None, S, LANES), functools.partial(lambda i, b: (b, 0, i), i))
                  for i in range(3 * width // LANES)]
                 + [pl.BlockSpec(bias_b.shape, lambda b: (0, 0, 0))],
        out_specs=pl.BlockSpec((None, S, width), lambda b: (b, 0, 0)),
        out_shape=jax.ShapeDtypeStruct((B, S, width), BF16),
        scratch_shapes=[pltpu.VMEM((S, LANES), F32)] * (3 * width // LANES),
        compiler_params=_cparams(("parallel",)),
        name="dilated_attn",
    )(*([proj_b] * (3 * width // LANES)), bias_b)


SEL16 = jnp.bfloat16


RADIX_LOW_BITS = 5
PEEL_CAP = 6
FLT_MIN_NORMAL = 2.0 ** -126


def _float_key(f):
    b = pltpu.bitcast(f, jnp.int32)
    return jnp.where(b >= 0, b, b ^ jnp.int32(0x7FFFFFFF))


def _key_float(k):
    return pltpu.bitcast(jnp.where(k >= 0, k, k ^ jnp.int32(0x7FFFFFFF)), F32)


def _high_half(f):
    return pltpu.bitcast(pltpu.bitcast(f, jnp.int32) & jnp.int32(-65536), F32)


def _dsa_kernel(iq_ref, ik_ref, iwt_ref, cq_ref, ck_ref, cvt_ref, bias_ref, o_ref, sc_ref, hi_ref, s_sc,
                acc_sc,
                *, tq, ch, ca, topk):
    qi = pl.program_id(1)
    q0 = qi * tq
    nch = (q0 + tq - 1) // ch + 1
    qidx = q0 + lax.broadcasted_iota(jnp.int32, (1, tq), 1)
    lane = lax.broadcasted_iota(jnp.int32, (tq, LANES), 1)
    first = lane < IDX_DIM

    w = iwt_ref[...] * (IDX_HEADS ** -0.5)
    qs = []
    for h in range(IDX_HEADS):
        tile = iq_ref[:, (h // 2) * LANES:(h // 2 + 1) * LANES].astype(F32) * (IDX_DIM ** -0.5)
        qs.append(jnp.where(first if h % 2 == 0 else ~first, tile, 0.0).astype(BF16))

    half_k, half_q = ch // 2, tq // 2
    every, late = slice(0, tq), slice(half_q, tq)
    diag = nch - 1

    def scores_of(start, nkeys, cols):
        kk = ik_ref[pl.ds(start, nkeys), :].astype(BF16)
        sc = jnp.zeros((nkeys, cols.stop - cols.start), F32)
        for h in range(IDX_HEADS):
            sc = sc + jnp.maximum(_dot_nt(kk, qs[h][cols]), 0.0) * w[h:h + 1, cols]
        kidx = start + lax.broadcasted_iota(jnp.int32, (nkeys, 1), 0)
        return jnp.where(kidx <= qidx[:, cols], sc, NEG)

    def score_body(j, carry):
        sc = scores_of(pl.multiple_of(j * ch, ch), ch, every)
        sc_ref[j] = sc
        hi_ref[j] = _high_half(sc).astype(SEL16)
        return carry

    lax.fori_loop(0, diag, score_body, 0)
    dstart = pl.multiple_of(diag * ch, ch)
    sc = scores_of(dstart, half_k, every)
    sc_ref[diag, :half_k, :] = sc
    hi_ref[diag, :half_k, :] = _high_half(sc).astype(SEL16)
    sc = scores_of(dstart + half_k, half_k, late)
    sc_ref[diag, half_k:, half_q:] = sc
    hi_ref[diag, half_k:, half_q:] = _high_half(sc).astype(SEL16)
    sc_ref[diag, half_k:, :half_q] = jnp.full((half_k, half_q), NEG, F32)
    hi_ref[diag, half_k:, :half_q] = jnp.full((half_k, half_q), NEG, SEL16)

    rows = 32

    def fold_chunks(ref, step, init):
        acc = lax.fori_loop(0, diag, lambda j, a: step(ref[j], a, every), init)
        acc = step(ref[diag, :half_k, :], acc, every)
        tail = step(ref[diag, half_k:, half_q:], acc[:, half_q:], late)
        return jnp.concatenate([acc[:, :half_q], tail], axis=1)

    def count_ge(t):
        def step(x, acc, cols):
            hit = jnp.where(x >= t[:, cols], 1.0, 0.0)
            return acc + jnp.sum(hit.reshape(x.shape[0] // rows, rows, x.shape[1]), axis=0)
        acc = fold_chunks(sc_ref, step, jnp.zeros((rows, tq), F32))
        return jnp.sum(acc, axis=0, keepdims=True)

    def count_high_ge(v):
        one, zero16 = jnp.ones((), SEL16), jnp.zeros((), SEL16)
        def step(x, acc, cols):
            hit = jnp.where(x >= v[:, cols], one, zero16).reshape(x.shape[0] // rows, rows, x.shape[1])
            parts = [hit[i] for i in range(x.shape[0] // rows)]
            while len(parts) > 1:
                parts = [parts[i] + parts[i + 1] for i in range(0, len(parts), 2)]
            return acc + parts[0]
        acc = fold_chunks(hi_ref, step, jnp.zeros((rows, tq), SEL16))
        return jnp.sum(acc.astype(F32), axis=0, keepdims=True)

    kf = float(topk)
    n_valid = (qidx + 1).astype(F32)
    half_bits = 16

    def high_body(i, st):
        u, c_lo = st
        cand = u | jnp.left_shift(1, half_bits - 1 - i)
        v = _high_half(_key_float((cand - (1 << (half_bits - 1))) << half_bits)).astype(SEL16)
        c = count_high_ge(v)
        keep = c >= kf
        return jnp.where(keep, cand, u), jnp.where(keep, c, c_lo)

    u, c_lo = lax.fori_loop(0, half_bits, high_body, (jnp.zeros((1, tq), jnp.int32), n_valid))
    key_hi = (u - (1 << (half_bits - 1))) << half_bits

    def low_body(i, st):
        lo_bits, c_lo = st
        cand = lo_bits | jnp.left_shift(1, half_bits - 1 - i)
        c = count_ge(_key_float(key_hi | cand))
        keep = c >= kf
        return jnp.where(keep, cand, lo_bits), jnp.where(keep, c, c_lo)

    lo_bits, c_lo = lax.fori_loop(0, RADIX_LOW_BITS, low_body, (jnp.zeros((1, tq), jnp.int32), c_lo))

    def min_ge(t):
        def step(x, acc, cols):
            kept = jnp.where(x >= t[:, cols], x, -NEG)
            return jnp.minimum(acc, jnp.min(kept.reshape(x.shape[0] // rows, rows, x.shape[1]), axis=0))
        acc = fold_chunks(sc_ref, step, jnp.full((rows, tq), -NEG, F32))
        return jnp.min(acc, axis=0, keepdims=True)

    def open_rows(c_lo, done):
        return jnp.logical_and(jnp.logical_and(n_valid > kf, c_lo > kf), done == 0.0)

    def any_row(mask):
        return jnp.max(jnp.where(mask, 1.0, 0.0)) > 0.0

    def peel_cond(st):
        _, c_lo, done, it = st
        return jnp.logical_and(it < PEEL_CAP, any_row(open_rows(c_lo, done)))

    def peel_body(st):
        t, c_lo, done, it = st
        is_open = open_rows(c_lo, done)
        smallest = min_ge(t)
        above = jnp.where(smallest == 0.0, FLT_MIN_NORMAL, _key_float(_float_key(smallest) + 1))
        c = count_ge(above)
        enough = jnp.logical_and(is_open, c >= kf)
        at_tie = jnp.logical_and(is_open, c < kf)
        t = jnp.where(enough, above, jnp.where(at_tie, smallest, t))
        return t, jnp.where(enough, c, c_lo), jnp.where(at_tie, 1.0, done), it + 1

    t_peel, c_peel, done, _ = lax.while_loop(
        peel_cond, peel_body,
        (_key_float(key_hi | lo_bits), c_lo, jnp.zeros((1, tq), F32), jnp.int32(0)))

    def finish_by_radix(_):
        bits, c = lax.fori_loop(RADIX_LOW_BITS, half_bits, low_body, (lo_bits, c_lo))
        return _key_float(key_hi | bits), c

    thr, c_lo = lax.cond(any_row(open_rows(c_peel, done)), finish_by_radix,
                         lambda _: (t_peel, c_peel), 0)
    thr = jnp.where(n_valid > kf, thr, 0.5 * NEG)

    tie_rows = jnp.logical_and(n_valid > kf, c_lo > kf)

    @pl.when(jnp.max(jnp.where(tie_rows, 1.0, 0.0)) > 0.0)
    def _():
        surplus = jnp.where(tie_rows, c_lo - kf, 0.0)
        upper = (lax.broadcasted_iota(jnp.int32, (ch, ch), 1)
                 >= lax.broadcasted_iota(jnp.int32, (ch, ch), 0)).astype(BF16)

        def drop_body(st):
            i, seen = st
            j = nch - 1 - i
            sc = sc_ref[j]
            eq = sc == thr
            eqf = jnp.where(eq, 1.0, 0.0)
            rank = _dot(upper, eqf.astype(BF16))
            sc_ref[j] = jnp.where(eq, jnp.where(rank <= surplus - seen, NEG, sc), sc)
            return i + 1, seen + jnp.sum(eqf, axis=0, keepdims=True)

        def more_to_drop(st):
            i, seen = st
            return jnp.logical_and(i < nch, any_row(seen < surplus))

        lax.while_loop(more_to_drop, drop_body, (jnp.int32(0), jnp.zeros((1, tq), F32)))

    cw = C_HEADS * C_DIM
    lane_c = lax.broadcasted_iota(jnp.int32, (tq, cw), 1)
    cq = cq_ref[...].astype(F32) * (C_DIM ** -0.5 * LOG2E)
    qcat = jnp.concatenate([jnp.where(lane_c // C_DIM == h, cq, 0.0) for h in range(C_HEADS)],
                           axis=0).astype(BF16)
    acc_sc[...] = jnp.zeros(acc_sc.shape, F32)
    nd = bias_ref.shape[1]
    last = (q0 + tq - 1) // ca

    def scores_into(slot, j):
        jc = jnp.minimum(j, last)
        kc = ck_ref[pl.ds(pl.multiple_of(jc * ca, ca), ca), :]
        sub = pl.multiple_of((jc % (ch // ca)) * ca, ca)
        sel = sc_ref[jc // (ch // ca), pl.ds(sub, ca), :] >= jnp.where(j <= last, thr, -NEG)
        bias = jnp.concatenate([
            jnp.concatenate([
                bias_ref[h, jnp.clip(q0 // LANES + c - (jc * (ca // LANES) + u), 0, nd - 1)]
                for h in range(C_HEADS) for c in range(tq // LANES)], axis=1)
            for u in range(ca // LANES)], axis=0)
        s_sc[slot] = jnp.where(jnp.concatenate([sel] * C_HEADS, axis=1),
                               _dot_nt(kc, qcat) + bias, NEG)

    def absorb(slot, j, m_prev):
        m_new = jnp.maximum(m_prev, jnp.max(s_sc[slot], axis=0, keepdims=True))
        alpha = jnp.exp2(m_prev - m_new)
        pb = jnp.exp2(s_sc[slot] - m_new).astype(BF16)
        jc = jnp.minimum(j, last)
        for h in range(C_HEADS):
            cols = slice(h * tq, (h + 1) * tq)
            vt = jnp.concatenate([cvt_ref[jc, h * C_DIM:(h + 1) * C_DIM, :], _ones_rows(ca)], axis=0)
            acc_sc[h] = alpha[:, cols] * acc_sc[h] + _dot(vt, pb[:, cols])
        return m_new

    scores_into(0, 0)

    def attn_body(jj, m):
        a = 2 * jj
        scores_into(1, a + 1)
        m = absorb(0, a, m)
        scores_into(0, a + 2)
        return absorb(1, a + 1, m)

    lax.fori_loop(0, (last + 2) // 2, attn_body, jnp.full((1, C_HEADS * tq), NEG, F32))
    o_t = jnp.concatenate([acc_sc[h, :C_DIM] * (1.0 / acc_sc[h, C_DIM:C_DIM + 1])
                           for h in range(C_HEADS)], axis=0)
    o_ref[...] = o_t.T.astype(o_ref.dtype)


def _dsa_attn(proj, proj_i, iwt, cvt, bias_c, tq, ch):
    B, S, _ = proj.shape
    ca = cvt.shape[-1]
    topk = min(TOPK_MAX, S // 4)
    nd = bias_c.shape[1]
    cw = C_HEADS * C_DIM
    assert tq == ch and ch % ca == 0, (tq, ch, ca)
    return pl.pallas_call(
        functools.partial(_dsa_kernel, tq=tq, ch=ch, ca=ca, topk=topk),
        grid=(B, S // tq),
        in_specs=[pl.BlockSpec((None, tq, IDX_HEADS * IDX_DIM), lambda b, i: (b, i, 3)),
                  pl.BlockSpec((None, S, LANES), lambda b, i: (b, 0, 0)),
                  pl.BlockSpec((None, IDX_HEADS, tq), lambda b, i: (b, 0, i)),
                  pl.BlockSpec((None, tq, cw), lambda b, i: (b, i, 8)),
                  pl.BlockSpec((None, S, cw), lambda b, i: (b, 0, 9)),
                  pl.BlockSpec((None, S // ca, cw, ca), lambda b, i: (b, 0, 0, 0)),
                  pl.BlockSpec((C_HEADS, nd, LANES, LANES), lambda b, i: (0, 0, 0, 0))],
        out_specs=pl.BlockSpec((None, tq, cw), lambda b, i: (b, i, 0)),
        out_shape=jax.ShapeDtypeStruct((B, S, cw), BF16),
        scratch_shapes=[pltpu.VMEM((S // ch, ch, tq), F32),
                        pltpu.VMEM((S // ch, ch, tq), SEL16),
                        pltpu.VMEM((2, ca, C_HEADS * tq), F32),
                        pltpu.VMEM((C_HEADS, C_DIM + ONES_ROWS, tq), F32)],
        compiler_params=_cparams(("parallel", "arbitrary")),
        name="dsa_attn",
    )(proj, proj_i, iwt, proj, proj, cvt, bias_c)


def _memkv_kernel(mem_ref, g_ref, w_ref, o_ref):
    h = _rms(mem_ref[...], g_ref[...]).astype(BF16)
    o_ref[...] = _dot(h, w_ref[...]).astype(o_ref.dtype)


def _memkv(mem, g, w):
    B, M, D = mem.shape
    L = w.shape[0]
    return pl.pallas_call(
        _memkv_kernel,
        grid=(L, B),
        in_specs=[pl.BlockSpec((None, M, D), lambda l, b: (b, 0, 0)),
                  pl.BlockSpec((None, 1, D), lambda l, b: (l, 0, 0)),
                  pl.BlockSpec((None, D, 2 * D), lambda l, b: (l, 0, 0))],
        out_specs=pl.BlockSpec((None, None, M, 2 * D), lambda l, b: (l, b, 0, 0)),
        out_shape=jax.ShapeDtypeStruct((L, B, M, 2 * D), BF16),
        compiler_params=_cparams(("parallel", "arbitrary")),
        name="mem_kv",
    )(mem, g, w)


def _outmem_kernel(x_ref, oa_ref, ob_ref, oc_ref, wo_ref, g_ref, wq_ref, kv_ref, wmo_ref, o_ref):
    D = x_ref.shape[-1]
    na, nb = oa_ref.shape[-1], ob_ref.shape[-1]
    x = (x_ref[...] + _dot(oa_ref[...], wo_ref[0:na]) + _dot(ob_ref[...], wo_ref[na:na + nb])
         + _dot(oc_ref[...], wo_ref[na + nb:]))
    h = _rms(x, g_ref[...]).astype(BF16)
    hd = D // MEM_HEADS
    q = (_dot(h, wq_ref[...]) * (hd ** -0.5)).astype(BF16)
    outs = []
    for hh in range(MEM_HEADS):
        k = kv_ref[:, hh * hd:(hh + 1) * hd]
        v = kv_ref[:, D + hh * hd:D + (hh + 1) * hd]
        s = _dot_nt(q[:, hh * hd:(hh + 1) * hd], k)
        p = jnp.exp(s - jnp.max(s, axis=-1, keepdims=True))
        p = p / jnp.sum(p, axis=-1, keepdims=True)
        outs.append(_dot(p.astype(BF16), v).astype(BF16))
    o = jnp.concatenate(outs, axis=-1)
    o_ref[...] = x + _dot(o, wmo_ref[...])


def _outmem(x, oa, ob, oc, wo, g, wq, kv, layer, wmo, tm):
    B, S, D = x.shape
    M = kv.shape[2]
    const = lambda b, i: (0, 0)
    return pl.pallas_call(
        _outmem_kernel,
        grid=(B, S // tm),
        in_specs=[pl.BlockSpec((None, tm, D), lambda b, i: (b, i, 0)),
                  pl.BlockSpec((None, tm, oa.shape[-1]), lambda b, i: (b, i, 0)),
                  pl.BlockSpec((None, tm, ob.shape[-1]), lambda b, i: (b, i, 0)),
                  pl.BlockSpec((None, tm, oc.shape[-1]), lambda b, i: (b, i, 0)),
                  pl.BlockSpec(wo.shape, const),
                  pl.BlockSpec((1, D), const),
                  pl.BlockSpec(wq.shape, const),
                  pl.BlockSpec((None, None, M, 2 * D), lambda b, i: (layer, b, 0, 0)),
                  pl.BlockSpec(wmo.shape, const)],
        out_specs=pl.BlockSpec((None, tm, D), lambda b, i: (b, i, 0)),
        out_shape=jax.ShapeDtypeStruct((B, S, D), F32),
        compiler_params=_cparams(("parallel", "arbitrary")),
        name="outproj_memattn",
    )(x, oa, ob, oc, wo, g, wq, kv, wmo)


HALO = 8


def _ffn_kernel(x_ref, xp_ref, g_ref, wg_ref, wv_ref, cwg_ref, cwv_ref, cbg_ref, cbv_ref, wd_ref,
                gout_ref, o_ref, *, tm, fc, norm_out):
    i = pl.program_id(1)
    x = x_ref[...]
    g = g_ref[...]
    hp = _rms(xp_ref[...], g) * jnp.where(i > 0, 1.0, 0.0)
    h = jnp.concatenate([hp, _rms(x, g)], axis=0).astype(BF16)
    F = wd_ref.shape[0]

    def conv(u, cw_ref, cb_ref, c, e):
        out = cb_ref[:, c:e]
        for j in range(CONV_WIDTH):
            shift = CONV_WIDTH - 1 - j
            out = out + cw_ref[j:j + 1, c:e] * u[HALO - shift:HALO - shift + tm]
        return out

    acc = x
    for c in range(0, F, fc):
        e = min(c + fc, F)
        gate = conv(_dot(h, wg_ref[:, c:e]), cwg_ref, cbg_ref, c, e)
        val = conv(_dot(h, wv_ref[:, c:e]), cwv_ref, cbv_ref, c, e)
        act = (gate * jax.nn.sigmoid(gate) * val).astype(BF16)
        acc = acc + _dot(act, wd_ref[c:e, :])
    o_ref[...] = _rms(acc, gout_ref[...]) if norm_out else acc


def _ffn(x, g, wg, wv, cwg, cwv, cbg, cbv, wd, gout, norm_out, tm, fc):
    B, S, D = x.shape
    F = wd.shape[0]
    const = lambda b, i: (0, 0)
    hb = tm // HALO
    return pl.pallas_call(
        functools.partial(_ffn_kernel, tm=tm, fc=fc, norm_out=norm_out),
        grid=(B, S // tm),
        in_specs=[pl.BlockSpec((None, tm, D), lambda b, i: (b, i, 0)),
                  pl.BlockSpec((None, HALO, D), lambda b, i: (b, jnp.maximum(i * hb - 1, 0), 0)),
                  pl.BlockSpec((1, D), const),
                  pl.BlockSpec((D, F), const, pipeline_mode=pl.Buffered(1)),
                  pl.BlockSpec((D, F), lambda b, i: (0, 1), pipeline_mode=pl.Buffered(1)),
                  pl.BlockSpec((CONV_WIDTH, F), const), pl.BlockSpec((CONV_WIDTH, F), const),
                  pl.BlockSpec((1, F), const), pl.BlockSpec((1, F), const),
                  pl.BlockSpec((F, D), const, pipeline_mode=pl.Buffered(1)),
                  pl.BlockSpec((1, D), const)],
        out_specs=pl.BlockSpec((None, tm, D), lambda b, i: (b, i, 0)),
        out_shape=jax.ShapeDtypeStruct((B, S, D), F32),
        compiler_params=_cparams(("parallel", "arbitrary")),
        name="conv_ffn",
    )(x, x, g, wg, wv, cwg, cwv, cbg, cbv, wd, gout)


def _tile_params(S, tq_a, tk_a):
    big = 1 << 30
    gran, (d_min, d_const) = min(tq_a, tk_a), _a_offsets(tq_a, tk_a)
    pa = []
    for h in range(A_HEADS):
        pa += [(h, d * gran, 1, 0, big, -1) for d in range(d_min, d_const + 1)]
        pa.append((h, 0, 1, 1, 0, -1))
    pb = []
    for _, r in DILATED_PATTERNS:
        for h in range(B_HEADS):
            pb.append((A_HEADS + h, DIL_BLK, r, 1, DIL_BLK, 1))
            pb.append((A_HEADS + h, 0, r, 0, DIL_BLK, 1))
    pc = [(A_HEADS + B_HEADS + h, d * LANES, 1, -big, big, -1)
          for h in range(C_HEADS) for d in range(_n_offsets(S, LANES))]
    to = lambda p: jnp.asarray(p, jnp.int32)
    return to(pa), to(pb), to(pc)


def _dil_bias_layout(tiles):
    t = tiles.reshape(len(DILATED_PATTERNS), B_HEADS, 2, DIL_BLK, DIL_BLK)
    return jnp.concatenate([t[:, :, 0], t[:, :, 1]], axis=-1).reshape(
        len(DILATED_PATTERNS), B_HEADS * DIL_BLK, 2 * DIL_BLK)


def _n_offsets(S, blk):
    return min(S // blk, REL_MAX_DIST // blk + 2)


def _a_offsets(tq, tk):
    gran = min(tq, tk)
    return -(tq // gran - 1), -(-(REL_MAX_DIST - 1 + tk) // gran)


def _in_weights(w_in_l):
    sizes = (512, 512, 512, 256, 256, 256, 256, 256, 256, 512, 64, 8)
    offs = [0]
    for s in sizes:
        offs.append(offs[-1] + s)
    (aq, ak, av, bq, bk, bv, cq, ck, cv, iq, ik, iw) = [w_in_l[:, offs[i]:offs[i + 1]] for i in range(12)]
    pad = jnp.zeros((w_in_l.shape[0], N_I - 2 * IDX_DIM - IDX_HEADS), w_in_l.dtype)
    return jnp.concatenate([aq, ak, av, iq, cq, ck, cv, bq, bk, bv, ik, ik, iw, pad], axis=1).astype(BF16)


def _forward(x, mem, rel_bias, norm_mix, w_in, lam_q1, lam_k1, lam_q2, lam_k2, subln, w_out,
             norm_mem, norm_memkv, w_mq, w_mkv, w_mo, norm_ffn, w_up, conv_w, conv_b, w_down,
             norm_final, *, tq_a, tk_a, tq_c, ch_c, ca_c, tm_proj, tm_mem, tm_ffn, fc):
    B, S, D = x.shape
    L = w_in.shape[0]
    F = w_down.shape[1]
    pa, pb, pc = _tile_params(S, tq_a, tk_a)
    bias_a = _bias_tiles(pa, rel_bias, tk_a, tq_a, LOG2E).reshape(A_HEADS, -1, tk_a, tq_a)
    d_min, d_const = _a_offsets(tq_a, tk_a)
    bias_b = _dil_bias_layout(_bias_tiles(pb, rel_bias, DIL_BLK, DIL_BLK, LOG2E))
    bias_c = _bias_tiles(pc, rel_bias, LANES, LANES, LOG2E).reshape(C_HEADS, -1, LANES, LANES)
    kv_all = _memkv(mem, norm_memkv.reshape(L, 1, D), w_mkv.astype(BF16))

    for l in range(L):
        lam_init = 0.8 - 0.6 * math.exp(-0.3 * l)
        proj, proj_b, proj_i, avt, cvt, iwt = _inproj(
            x, norm_mix[l].reshape(1, D), _in_weights(w_in[l]), tm_proj, tk_a, ca_c)
        proj = proj.reshape(B, S, N_MAIN)
        proj_i = proj_i.reshape(B, S, N_I)
        lamv = jnp.stack([lam_q1[l], lam_k1[l], lam_q2[l], lam_k2[l]], axis=0)
        o_a = _diff_attn(proj, avt, lamv, bias_a, subln[l].reshape(A_V, 1), lam_init, tq_a,
                         d_min, d_const)
        o_b = _dil_attn(proj_b.reshape(B, S, N_B), bias_b)
        o_c = _dsa_attn(proj, proj_i, iwt, cvt, bias_c, tq_c, ch_c)
        x = _outmem(x, o_a, o_b, o_c, w_out[l].astype(BF16), norm_mem[l].reshape(1, D),
                    w_mq[l].astype(BF16), kv_all, l, w_mo[l].astype(BF16), tm_mem)
        wu = w_up[l].astype(BF16)
        x = _ffn(x, norm_ffn[l].reshape(1, D), wu, wu, conv_w[l][:, :F], conv_w[l][:, F:],
                 conv_b[l][:F].reshape(1, F), conv_b[l][F:].reshape(1, F), w_down[l].astype(BF16),
                 norm_final.reshape(1, D), l == L - 1, tm_ffn, fc)
    return x


def kernel(x, mem, rel_bias, norm_mix, w_in, lam_q1, lam_k1, lam_q2, lam_k2, subln, w_out,
           norm_mem, norm_memkv, w_mq, w_mkv, w_mo, norm_ffn, w_up, conv_w, conv_b, w_down,
           norm_final):
    return _forward(x, mem, rel_bias, norm_mix, w_in, lam_q1, lam_k1, lam_q2, lam_k2, subln, w_out,
                    norm_mem, norm_memkv, w_mq, w_mkv, w_mo, norm_ffn, w_up, conv_w, conv_b, w_down,
                    norm_final, tq_a=512, tk_a=256, tq_c=512, ch_c=512, ca_c=256, tm_proj=1024, tm_mem=1024, tm_ffn=1024,
                    fc=512)
```

```python
import functools
import math

import jax
import jax.numpy as jnp
from jax import lax
from jax.experimental import pallas as pl
from jax.experimental.pallas import tpu as pltpu

F32 = jnp.float32
BF16 = jnp.bfloat16

EPS = 1e-6
NEG = -1e30
LOG2E = math.log2(math.e)
LANES = 128
VMEM_LIMIT = 56 * 1024 * 1024

A_HEADS, A_QK, A_V = 4, 64, 128
B_HEADS, B_DIM = 4, 64
C_HEADS, C_DIM = 4, 64
IDX_HEADS, IDX_DIM = 8, 64
TOPK_MAX = 256
DILATED_PATTERNS = ((128, 1), (512, 4), (2048, 16))
ONES_ROWS = 16
DIL_BLK = 128
DIL_UNROLL = 4
REL_BUCKETS, REL_MAX_DIST = 32, 2048
MEM_HEADS = 4
CONV_WIDTH = 3

N_MAIN = 2816
N_B = 768
N_I = 256


def _cparams(sem):
    return pltpu.CompilerParams(dimension_semantics=sem, vmem_limit_bytes=VMEM_LIMIT)


def _dot(a, b):
    return jnp.dot(a, b, preferred_element_type=F32)


def _dot_nt(a, b):
    return lax.dot_general(a, b, (((1,), (1,)), ((), ())), preferred_element_type=F32)


def _ones_rows(n):
    row = lax.broadcasted_iota(jnp.int32, (ONES_ROWS, n), 0)
    return jnp.where(row == 0, 1.0, 0.0).astype(BF16)


def _rms(x, g):
    return x * lax.rsqrt(jnp.mean(x * x, axis=-1, keepdims=True) + EPS) * g


def _rel_bucket(n):
    max_exact = REL_BUCKETS // 2
    nf = jnp.maximum(n, 1).astype(F32)
    large = max_exact + (jnp.log(nf / max_exact) / math.log(REL_MAX_DIST / max_exact)
                         * (REL_BUCKETS - max_exact)).astype(jnp.int32)
    large = jnp.minimum(large, REL_BUCKETS - 1)
    return jnp.where(n < max_exact, n, large)


def _bias_tiles_kernel(par_ref, tab_ref, o_ref, *, tq, tk, scale):
    t = pl.program_id(0)
    head, off, mult = par_ref[t, 0], par_ref[t, 1], par_ref[t, 2]
    lo, hi, sgn = par_ref[t, 3], par_ref[t, 4], par_ref[t, 5]
    dist = off + sgn * (lax.broadcasted_iota(jnp.int32, (tq, tk), 0)
                        - lax.broadcasted_iota(jnp.int32, (tq, tk), 1))
    bucket = _rel_bucket(jnp.maximum(dist * mult, 0))
    val = jnp.zeros((tq, tk), F32)
    for b in range(REL_BUCKETS):
        val = jnp.where(bucket == b, tab_ref[b, head] * scale, val)
    o_ref[...] = jnp.where((dist >= lo) & (dist <= hi), val, NEG)


def _bias_tiles(params, table, tq, tk, scale=1.0):
    n = params.shape[0]
    return pl.pallas_call(
        functools.partial(_bias_tiles_kernel, tq=tq, tk=tk, scale=scale),
        grid=(n,),
        in_specs=[pl.BlockSpec(memory_space=pltpu.SMEM), pl.BlockSpec(memory_space=pltpu.SMEM)],
        out_specs=pl.BlockSpec((None, tq, tk), lambda t: (t, 0, 0)),
        out_shape=jax.ShapeDtypeStruct((n, tq, tk), F32),
        compiler_params=_cparams(("arbitrary",)),
        name="bias_tiles",
    )(params, table)


def _inproj_kernel(x_ref, g_ref, w_ref, om_ref, ob_ref, oi_ref, avt_ref, cvt_ref, iwt_ref):
    h = _rms(x_ref[...], g_ref[...]).astype(BF16)
    step = 512
    for c in range(0, N_MAIN, step):
        e = min(c + step, N_MAIN)
        om_ref[:, c:e] = _dot(h, w_ref[:, c:e]).astype(BF16)
    ob_ref[...] = _dot(h, w_ref[:, N_MAIN:N_MAIN + N_B])
    oi_ref[...] = _dot(h, w_ref[:, N_MAIN + N_B:])
    tk = avt_ref.shape[-1]
    a0 = 2 * A_HEADS * LANES
    for hd in range(A_HEADS):
        vt = om_ref[:, a0 + hd * A_V:a0 + (hd + 1) * A_V].astype(F32).T
        for j in range(avt_ref.shape[1]):
            avt_ref[hd, j] = vt[:, j * tk:(j + 1) * tk].astype(BF16)
    ca = cvt_ref.shape[-1]
    cw = C_HEADS * C_DIM
    for half in range(cw // LANES):
        c0 = N_MAIN - cw + half * LANES
        vt = om_ref[:, c0:c0 + LANES].astype(F32).T
        for j in range(cvt_ref.shape[0]):
            cvt_ref[j, half * LANES:(half + 1) * LANES, :] = vt[:, j * ca:(j + 1) * ca].astype(BF16)
    iwt_ref[...] = oi_ref[:, 2 * IDX_DIM:2 * IDX_DIM + LANES].T[:IDX_HEADS]


def _inproj(x, g, w, tm, tk_a, ca_c):
    B, S, D = x.shape
    T = B * S
    nt = S // tm
    n_all = N_MAIN + N_B + N_I
    cw = C_HEADS * C_DIM
    return pl.pallas_call(
        _inproj_kernel,
        grid=(T // tm,),
        in_specs=[pl.BlockSpec((tm, D), lambda i: (i, 0)),
                  pl.BlockSpec((1, D), lambda i: (0, 0)),
                  pl.BlockSpec((D, n_all), lambda i: (0, 0))],
        out_specs=[pl.BlockSpec((tm, N_MAIN), lambda i: (i, 0)),
                   pl.BlockSpec((tm, N_B), lambda i: (i, 0)),
                   pl.BlockSpec((tm, N_I), lambda i: (i, 0)),
                   pl.BlockSpec((None, A_HEADS, tm // tk_a, A_V, tk_a),
                                lambda i: (i // nt, 0, i % nt, 0, 0)),
                   pl.BlockSpec((None, tm // ca_c, cw, ca_c), lambda i: (i // nt, i % nt, 0, 0)),
                   pl.BlockSpec((None, IDX_HEADS, tm), lambda i: (i // nt, 0, i % nt))],
        out_shape=[jax.ShapeDtypeStruct((T, N_MAIN), BF16),
                   jax.ShapeDtypeStruct((T, N_B), F32),
                   jax.ShapeDtypeStruct((T, N_I), F32),
                   jax.ShapeDtypeStruct((B, A_HEADS, S // tk_a, A_V, tk_a), BF16),
                   jax.ShapeDtypeStruct((B, S // ca_c, cw, ca_c), BF16),
                   jax.ShapeDtypeStruct((B, IDX_HEADS, S), F32)],
        compiler_params=_cparams(("parallel",)),
        name="inproj",
    )(x.reshape(T, D), g, w)


def _diff_attn_kernel(lam_ref, q_ref, k_ref, vt_ref, bias_ref, g_ref, o_ref, s_sc, acc_sc,
                      *, tq, tk, d_min, d_const, lam_init):
    qi = pl.program_id(2)
    gran = min(tq, tk)
    last = (qi * tq + tq - 1) // tk
    masked_tile = d_const - d_min + 1
    lv = lam_ref[...]
    lam = (jnp.exp(jnp.sum(lv[0:1] * lv[1:2], axis=-1, keepdims=True))
           - jnp.exp(jnp.sum(lv[2:3] * lv[3:4], axis=-1, keepdims=True)) + lam_init)
    lane = lax.broadcasted_iota(jnp.int32, (tq, LANES), 1)
    q = q_ref[...].astype(F32) * (A_QK ** -0.5 * LOG2E)
    qcat = jnp.concatenate([jnp.where(lane < A_QK, q, 0.0), jnp.where(lane >= A_QK, q, 0.0)],
                           axis=0).astype(BF16)
    acc_sc[...] = jnp.zeros(acc_sc.shape, F32)

    def scores_into(slot, j):
        jc = jnp.minimum(j, last)
        k = k_ref[pl.ds(pl.multiple_of(jc * tk, tk), tk), :]
        d = qi * (tq // gran) - jc * (tk // gran)
        bias = bias_ref[jnp.where(j <= last, jnp.minimum(d, d_const) - d_min, masked_tile)]
        s_sc[slot] = _dot_nt(k, qcat) + jnp.concatenate([bias, bias], axis=1)

    def absorb(slot, j, m_prev):
        m_new = jnp.maximum(m_prev, jnp.max(s_sc[slot], axis=0, keepdims=True))
        alpha = jnp.exp2(m_prev - m_new)
        p = jnp.exp2(s_sc[slot] - m_new)
        vt = jnp.concatenate([vt_ref[jnp.minimum(j, last)], _ones_rows(tk)], axis=0)
        acc_sc[...] = alpha * acc_sc[...] + _dot(vt, p.astype(BF16))
        return m_new

    scores_into(0, 0)

    def pair(a, m):
        scores_into(1, a + 1)
        m = absorb(0, a, m)
        scores_into(0, a + 2)
        return absorb(1, a + 1, m)

    pairs = last // 2
    m = lax.fori_loop(0, pairs // 2, lambda jj, m: pair(4 * jj + 2, pair(4 * jj, m)),
                      jnp.full((1, 2 * tq), NEG, F32))
    m = lax.fori_loop(pairs // 2 * 2, pairs, lambda jj, m: pair(2 * jj, m), m)

    half = tq // 2
    late = [slice(mp * tq + half, (mp + 1) * tq) for mp in range(2)]
    pick = lambda x: jnp.concatenate([x[:, c] for c in late], axis=1)
    k_last = k_ref[pl.ds(pl.multiple_of(last * tk, tk), tk), :]
    bias_last = bias_ref[-d_min][:, :half]
    q_late = jnp.concatenate([qcat[c] for c in late], axis=0)
    s_sc[1, :, :tq] = _dot_nt(k_last, q_late) + jnp.concatenate([bias_last, bias_last], axis=1)
    m = absorb(0, last - 1, m)
    m_late = pick(m)
    m_new = jnp.maximum(m_late, jnp.max(s_sc[1, :, :tq], axis=0, keepdims=True))
    alpha = jnp.exp2(m_late - m_new)
    p = jnp.exp2(s_sc[1, :, :tq] - m_new)
    vt = jnp.concatenate([vt_ref[last], _ones_rows(tk)], axis=0)
    upd = alpha * pick(acc_sc[...]) + _dot(vt, p.astype(BF16))
    for i, c in enumerate(late):
        acc_sc[:, c] = upd[:, i * half:(i + 1) * half]
    inv = 1.0 / acc_sc[A_V:A_V + 1, :]
    o = acc_sc[:A_V, :tq] * inv[:, :tq] - lam * (acc_sc[:A_V, tq:] * inv[:, tq:])
    o = o * lax.rsqrt(jnp.mean(o * o, axis=0, keepdims=True) + EPS) * g_ref[...] * (1.0 - lam_init)
    o_ref[...] = o.T.astype(o_ref.dtype)


def _diff_attn(proj, vt, lamv, bias_a, subln_g, lam_init, tq, d_min, d_const):
    B, S, _ = proj.shape
    nd, tk = bias_a.shape[1], bias_a.shape[2]
    assert tq == 2 * tk, (tq, tk)
    return pl.pallas_call(
        functools.partial(_diff_attn_kernel, tq=tq, tk=tk, d_min=d_min, d_const=d_const,
                          lam_init=lam_init),
        grid=(A_HEADS, B, S // tq),
        in_specs=[pl.BlockSpec((4, A_QK), lambda h, b, i: (0, 0)),
                  pl.BlockSpec((None, tq, LANES), lambda h, b, i: (b, i, h)),
                  pl.BlockSpec((None, S, LANES), lambda h, b, i: (b, 0, A_HEADS + h)),
                  pl.BlockSpec((None, None, S // tk, A_V, tk), lambda h, b, i: (b, h, 0, 0, 0)),
                  pl.BlockSpec((None, nd, tk, tq), lambda h, b, i: (h, 0, 0, 0)),
                  pl.BlockSpec((A_V, 1), lambda h, b, i: (0, 0))],
        out_specs=pl.BlockSpec((None, tq, LANES), lambda h, b, i: (b, i, h)),
        out_shape=jax.ShapeDtypeStruct((B, S, A_HEADS * A_V), BF16),
        scratch_shapes=[pltpu.VMEM((2, tk, 2 * tq), F32),
                        pltpu.VMEM((A_V + ONES_ROWS, 2 * tq), F32)],
        compiler_params=_cparams(("parallel", "parallel", "arbitrary")),
        name="diff_attn",
    )(lamv, proj, proj, vt, bias_a, subln_g)


class _LaneHalves:
    def __init__(self, *refs):
        self.refs = refs

    def __getitem__(self, idx):
        return jnp.concatenate([r[idx] for r in self.refs], axis=1)

    def __setitem__(self, idx, val):
        for i, r in enumerate(self.refs):
            r[idx] = val[:, i * LANES:(i + 1) * LANES]


def _dil_kernel(q0, q1, k0, k1, v0, v1, bias_ref, o_ref, n0, n1, m0, m1, l0, l1, *, seq, patterns):
    q_ref, k_ref, v_ref = _LaneHalves(q0, q1), _LaneHalves(k0, k1), _LaneHalves(v0, v1)
    n_sc, m_sc, l_sc = _LaneHalves(n0, n1), _LaneHalves(m0, m1), _LaneHalves(l0, l1)
    blk = DIL_BLK
    width = B_HEADS * B_DIM
    head_of_lane = lax.broadcasted_iota(jnp.int32, (blk, width), 1) // B_DIM

    def per_head(x):
        parts = [jnp.broadcast_to(x[h * blk:(h + 1) * blk], (blk, width)) for h in range(B_HEADS)]
        out = parts[-1]
        for h in range(B_HEADS - 2, -1, -1):
            out = jnp.where(head_of_lane == h, parts[h], out)
        return out

    def attend(base, r, with_prev, p_idx):
        q = q_ref[pl.ds(base, blk, stride=r), :] * (B_DIM ** -0.5 * LOG2E)
        qcat = jnp.concatenate([jnp.where(head_of_lane == h, q, 0.0) for h in range(B_HEADS)],
                               axis=0).astype(BF16)
        if with_prev:
            keys = pl.ds(base - blk * r, 2 * blk, stride=r)
            bias = bias_ref[p_idx]
        else:
            keys = pl.ds(base, blk, stride=r)
            bias = bias_ref[p_idx, :, blk:]
        s = _dot_nt(qcat, k_ref[keys, :].astype(BF16)) + bias
        mx = jnp.max(s, axis=-1, keepdims=True)
        p = jnp.exp2(s - mx)
        l = jnp.sum(p, axis=-1, keepdims=True)
        o = _dot(p.astype(BF16), v_ref[keys, :].astype(BF16))
        return per_head(o), per_head(mx), per_head(l)

    def merge(base, r, res, is_first):
        o, m, l = res
        rows = pl.ds(base, blk, stride=r)
        if is_first:
            n_sc[rows, :] = o
            m_sc[rows, :] = m
            l_sc[rows, :] = l
        else:
            m_old = m_sc[rows, :]
            m_new = jnp.maximum(m_old, m)
            a = jnp.exp2(m_old - m_new)
            b = jnp.exp2(m - m_new)
            n_sc[rows, :] = a * n_sc[rows, :] + b * o
            l_sc[rows, :] = a * l_sc[rows, :] + b * l
            m_sc[rows, :] = m_new

    def sweep(count, base_of, r, with_prev, p_idx):
        def trip(width):
            def body(i, carry):
                bases = [base_of(i * width + u) for u in range(width)]
                results = [attend(b, r, with_prev, p_idx) for b in bases]
                for b, res in zip(bases, results):
                    merge(b, r, res, p_idx == len(patterns) - 1)
                return carry
            return body

        full = count // DIL_UNROLL
        lax.fori_loop(0, full, trip(DIL_UNROLL), 0)
        rest = count - full * DIL_UNROLL
        if rest:
            lax.fori_loop(full * DIL_UNROLL, count, trip(1), 0)

    for p_idx, (_, r) in reversed(list(enumerate(patterns))):
        nb = seq // (r * blk)
        sweep(r, lambda c: c, r, False, p_idx)
        sweep(r * (nb - 1), lambda idx, r=r: (idx // r + 1) * blk * r + idx % r, r, True, p_idx)

    everything = (slice(None), slice(None))
    o_ref[...] = (n_sc[everything] / l_sc[everything]).astype(o_ref.dtype)


def _dil_attn(proj_b, bias_b):
    B, S, _ = proj_b.shape
    width = B_HEADS * B_DIM
    return pl.pallas_call(
        functools.partial(_dil_kernel, seq=S, patterns=DILATED_PATTERNS),
        grid=(B,),
        in_specs=[pl.BlockSpec((None, S, LANES), functools.partial(lambda i, b: (b, 0, i), i))
                  for i in range(3 * width // LANES)]
                 + [pl.BlockSpec(bias_b.shape, lambda b: (0, 0, 0))],
        out_specs=pl.BlockSpec((None, S, width), lambda b: (b, 0, 0)),
        out_shape=jax.ShapeDtypeStruct((B, S, width), BF16),
        scratch_shapes=[pltpu.VMEM((S, LANES), F32)] * (3 * width // LANES),
        compiler_params=_cparams(("parallel",)),
        name="dilated_attn",
    )(*([proj_b] * (3 * width // LANES)), bias_b)


SEL16 = jnp.bfloat16


RADIX_LOW_BITS = 5
PEEL_CAP = 6
FLT_MIN_NORMAL = 2.0 ** -126


def _float_key(f):
    b = pltpu.bitcast(f, jnp.int32)
    return jnp.where(b >= 0, b, b ^ jnp.int32(0x7FFFFFFF))


def _key_float(k):
    return pltpu.bitcast(jnp.where(k >= 0, k, k ^ jnp.int32(0x7FFFFFFF)), F32)


def _high_half(f):
    return pltpu.bitcast(pltpu.bitcast(f, jnp.int32) & jnp.int32(-65536), F32)


def _dsa_kernel(iq_ref, ik_ref, iwt_ref, cq_ref, ck_ref, cvt_ref, bias_ref, o_ref, sc_ref, hi_ref, s_sc,
                acc_sc,
                *, tq, ch, ca, topk):
    qi = pl.program_id(1)
    q0 = qi * tq
    nch = (q0 + tq - 1) // ch + 1
    qidx = q0 + lax.broadcasted_iota(jnp.int32, (1, tq), 1)
    lane = lax.broadcasted_iota(jnp.int32, (tq, LANES), 1)
    first = lane < IDX_DIM

    w = iwt_ref[...] * (IDX_HEADS ** -0.5)
    qs = []
    for h in range(IDX_HEADS):
        tile = iq_ref[:, (h // 2) * LANES:(h // 2 + 1) * LANES].astype(F32) * (IDX_DIM ** -0.5)
        qs.append(jnp.where(first if h % 2 == 0 else ~first, tile, 0.0).astype(BF16))

    half_k, half_q = ch // 2, tq // 2
    every, late = slice(0, tq), slice(half_q, tq)
    diag = nch - 1

    def scores_of(start, nkeys, cols):
        kk = ik_ref[pl.ds(start, nkeys), :].astype(BF16)
        sc = jnp.zeros((nkeys, cols.stop - cols.start), F32)
        for h in range(IDX_HEADS):
            sc = sc + jnp.maximum(_dot_nt(kk, qs[h][cols]), 0.0) * w[h:h + 1, cols]
        kidx = start + lax.broadcasted_iota(jnp.int32, (nkeys, 1), 0)
        return jnp.where(kidx <= qidx[:, cols], sc, NEG)

    def score_body(j, carry):
        sc = scores_of(pl.multiple_of(j * ch, ch), ch, every)
        sc_ref[j] = sc
        hi_ref[j] = _high_half(sc).astype(SEL16)
        return carry

    lax.fori_loop(0, diag, score_body, 0)
    dstart = pl.multiple_of(diag * ch, ch)
    sc = scores_of(dstart, half_k, every)
    sc_ref[diag, :half_k, :] = sc
    hi_ref[diag, :half_k, :] = _high_half(sc).astype(SEL16)
    sc = scores_of(dstart + half_k, half_k, late)
    sc_ref[diag, half_k:, half_q:] = sc
    hi_ref[diag, half_k:, half_q:] = _high_half(sc).astype(SEL16)
    sc_ref[diag, half_k:, :half_q] = jnp.full((half_k, half_q), NEG, F32)
    hi_ref[diag, half_k:, :half_q] = jnp.full((half_k, half_q), NEG, SEL16)

    rows = 32

    def fold_chunks(ref, step, init):
        acc = lax.fori_loop(0, diag, lambda j, a: step(ref[j], a, every), init)
        acc = step(ref[diag, :half_k, :], acc, every)
        tail = step(ref[diag, half_k:, half_q:], acc[:, half_q:], late)
        return jnp.concatenate([acc[:, :half_q], tail], axis=1)

    def count_ge(t):
        def step(x, acc, cols):
            hit = jnp.where(x >= t[:, cols], 1.0, 0.0)
            return acc + jnp.sum(hit.reshape(x.shape[0] // rows, rows, x.shape[1]), axis=0)
        acc = fold_chunks(sc_ref, step, jnp.zeros((rows, tq), F32))
        return jnp.sum(acc, axis=0, keepdims=True)

    def count_high_ge(v):
        one, zero16 = jnp.ones((), SEL16), jnp.zeros((), SEL16)
        def step(x, acc, cols):
            hit = jnp.where(x >= v[:, cols], one, zero16).reshape(x.shape[0] // rows, rows, x.shape[1])
            parts = [hit[i] for i in range(x.shape[0] // rows)]
            while len(parts) > 1:
                parts = [parts[i] + parts[i + 1] for i in range(0, len(parts), 2)]
            return acc + parts[0]
        acc = fold_chunks(hi_ref, step, jnp.zeros((rows, tq), SEL16))
        return jnp.sum(acc.astype(F32), axis=0, keepdims=True)

    kf = float(topk)
    n_valid = (qidx + 1).astype(F32)
    half_bits = 16

    def high_body(i, st):
        u, c_lo = st
        cand = u | jnp.left_shift(1, half_bits - 1 - i)
        v = _high_half(_key_float((cand - (1 << (half_bits - 1))) << half_bits)).astype(SEL16)
        c = count_high_ge(v)
        keep = c >= kf
        return jnp.where(keep, cand, u), jnp.where(keep, c, c_lo)

    u, c_lo = lax.fori_loop(0, half_bits, high_body, (jnp.zeros((1, tq), jnp.int32), n_valid))
    key_hi = (u - (1 << (half_bits - 1))) << half_bits

    def low_body(i, st):
        lo_bits, c_lo = st
        cand = lo_bits | jnp.left_shift(1, half_bits - 1 - i)
        c = count_ge(_key_float(key_hi | cand))
        keep = c >= kf
        return jnp.where(keep, cand, lo_bits), jnp.where(keep, c, c_lo)

    lo_bits, c_lo = lax.fori_loop(0, RADIX_LOW_BITS, low_body, (jnp.zeros((1, tq), jnp.int32), c_lo))

    def min_ge(t):
        def step(x, acc, cols):
            kept = jnp.where(x >= t[:, cols], x, -NEG)
            return jnp.minimum(acc, jnp.min(kept.reshape(x.shape[0] // rows, rows, x.shape[1]), axis=0))
        acc = fold_chunks(sc_ref, step, jnp.full((rows, tq), -NEG, F32))
        return jnp.min(acc, axis=0, keepdims=True)

    def open_rows(c_lo, done):
        return jnp.logical_and(jnp.logical_and(n_valid > kf, c_lo > kf), done == 0.0)

    def any_row(mask):
        return jnp.max(jnp.where(mask, 1.0, 0.0)) > 0.0

    def peel_cond(st):
        _, c_lo, done, it = st
        return jnp.logical_and(it < PEEL_CAP, any_row(open_rows(c_lo, done)))

    def peel_body(st):
        t, c_lo, done, it = st
        is_open = open_rows(c_lo, done)
        smallest = min_ge(t)
        above = jnp.where(smallest == 0.0, FLT_MIN_NORMAL, _key_float(_float_key(smallest) + 1))
        c = count_ge(above)
        enough = jnp.logical_and(is_open, c >= kf)
        at_tie = jnp.logical_and(is_open, c < kf)
        t = jnp.where(enough, above, jnp.where(at_tie, smallest, t))
        return t, jnp.where(enough, c, c_lo), jnp.where(at_tie, 1.0, done), it + 1

    t_peel, c_peel, done, _ = lax.while_loop(
        peel_cond, peel_body,
        (_key_float(key_hi | lo_bits), c_lo, jnp.zeros((1, tq), F32), jnp.int32(0)))

    def finish_by_radix(_):
        bits, c = lax.fori_loop(RADIX_LOW_BITS, half_bits, low_body, (lo_bits, c_lo))
        return _key_float(key_hi | bits), c

    thr, c_lo = lax.cond(any_row(open_rows(c_peel, done)), finish_by_radix,
                         lambda _: (t_peel, c_peel), 0)
    thr = jnp.where(n_valid > kf, thr, 0.5 * NEG)

    tie_rows = jnp.logical_and(n_valid > kf, c_lo > kf)

    @pl.when(jnp.max(jnp.where(tie_rows, 1.0, 0.0)) > 0.0)
    def _():
        surplus = jnp.where(tie_rows, c_lo - kf, 0.0)
        upper = (lax.broadcasted_iota(jnp.int32, (ch, ch), 1)
                 >= lax.broadcasted_iota(jnp.int32, (ch, ch), 0)).astype(BF16)

        def drop_body(st):
            i, seen = st
            j = nch - 1 - i
            sc = sc_ref[j]
            eq = sc == thr
            eqf = jnp.where(eq, 1.0, 0.0)
            rank = _dot(upper, eqf.astype(BF16))
            sc_ref[j] = jnp.where(eq, jnp.where(rank <= surplus - seen, NEG, sc), sc)
            return i + 1, seen + jnp.sum(eqf, axis=0, keepdims=True)

        def more_to_drop(st):
            i, seen = st
            return jnp.logical_and(i < nch, any_row(seen < surplus))

        lax.while_loop(more_to_drop, drop_body, (jnp.int32(0), jnp.zeros((1, tq), F32)))

    cw = C_HEADS * C_DIM
    lane_c = lax.broadcasted_iota(jnp.int32, (tq, cw), 1)
    cq = cq_ref[...].astype(F32) * (C_DIM ** -0.5 * LOG2E)
    qcat = jnp.concatenate([jnp.where(lane_c // C_DIM == h, cq, 0.0) for h in range(C_HEADS)],
                           axis=0).astype(BF16)
    acc_sc[...] = jnp.zeros(acc_sc.shape, F32)
    nd = bias_ref.shape[1]
    last = (q0 + tq - 1) // ca

    def scores_into(slot, j):
        jc = jnp.minimum(j, last)
        kc = ck_ref[pl.ds(pl.multiple_of(jc * ca, ca), ca), :]
        sub = pl.multiple_of((jc % (ch // ca)) * ca, ca)
        sel = sc_ref[jc // (ch // ca), pl.ds(sub, ca), :] >= jnp.where(j <= last, thr, -NEG)
        bias = jnp.concatenate([
            jnp.concatenate([
                bias_ref[h, jnp.clip(q0 // LANES + c - (jc * (ca // LANES) + u), 0, nd - 1)]
                for h in range(C_HEADS) for c in range(tq // LANES)], axis=1)
            for u in range(ca // LANES)], axis=0)
        s_sc[slot] = jnp.where(jnp.concatenate([sel] * C_HEADS, axis=1),
                               _dot_nt(kc, qcat) + bias, NEG)

    def absorb(slot, j, m_prev):
        m_new = jnp.maximum(m_prev, jnp.max(s_sc[slot], axis=0, keepdims=True))
        alpha = jnp.exp2(m_prev - m_new)
        pb = jnp.exp2(s_sc[slot] - m_new).astype(BF16)
        jc = jnp.minimum(j, last)
        for h in range(C_HEADS):
            cols = slice(h * tq, (h + 1) * tq)
            vt = jnp.concatenate([cvt_ref[jc, h * C_DIM:(h + 1) * C_DIM, :], _ones_rows(ca)], axis=0)
            acc_sc[h] = alpha[:, cols] * acc_sc[h] + _dot(vt, pb[:, cols])
        return m_new

    scores_into(0, 0)

    def attn_body(jj, m):
        a = 2 * jj
        scores_into(1, a + 1)
        m = absorb(0, a, m)
        scores_into(0, a + 2)
        return absorb(1, a + 1, m)

    lax.fori_loop(0, (last + 2) // 2, attn_body, jnp.full((1, C_HEADS * tq), NEG, F32))
    o_t = jnp.concatenate([acc_sc[h, :C_DIM] * (1.0 / acc_sc[h, C_DIM:C_DIM + 1])
                           for h in range(C_HEADS)], axis=0)
    o_ref[...] = o_t.T.astype(o_ref.dtype)


def _dsa_attn(proj, proj_i, iwt, cvt, bias_c, tq, ch):
    B, S, _ = proj.shape
    ca = cvt.shape[-1]
    topk = min(TOPK_MAX, S // 4)
    nd = bias_c.shape[1]
    cw = C_HEADS * C_DIM
    assert tq == ch and ch % ca == 0, (tq, ch, ca)
    return pl.pallas_call(
        functools.partial(_dsa_kernel, tq=tq, ch=ch, ca=ca, topk=topk),
        grid=(B, S // tq),
        in_specs=[pl.BlockSpec((None, tq, IDX_HEADS * IDX_DIM), lambda b, i: (b, i, 3)),
                  pl.BlockSpec((None, S, LANES), lambda b, i: (b, 0, 0)),
                  pl.BlockSpec((None, IDX_HEADS, tq), lambda b, i: (b, 0, i)),
                  pl.BlockSpec((None, tq, cw), lambda b, i: (b, i, 8)),
                  pl.BlockSpec((None, S, cw), lambda b, i: (b, 0, 9)),
                  pl.BlockSpec((None, S // ca, cw, ca), lambda b, i: (b, 0, 0, 0)),
                  pl.BlockSpec((C_HEADS, nd, LANES, LANES), lambda b, i: (0, 0, 0, 0))],
        out_specs=pl.BlockSpec((None, tq, cw), lambda b, i: (b, i, 0)),
        out_shape=jax.ShapeDtypeStruct((B, S, cw), BF16),
        scratch_shapes=[pltpu.VMEM((S // ch, ch, tq), F32),
                        pltpu.VMEM((S // ch, ch, tq), SEL16),
                        pltpu.VMEM((2, ca, C_HEADS * tq), F32),
                        pltpu.VMEM((C_HEADS, C_DIM + ONES_ROWS, tq), F32)],
        compiler_params=_cparams(("parallel", "arbitrary")),
        name="dsa_attn",
    )(proj, proj_i, iwt, proj, proj, cvt, bias_c)


def _memkv_kernel(mem_ref, g_ref, w_ref, o_ref):
    h = _rms(mem_ref[...], g_ref[...]).astype(BF16)
    o_ref[...] = _dot(h, w_ref[...]).astype(o_ref.dtype)


def _memkv(mem, g, w):
    B, M, D = mem.shape
    L = w.shape[0]
    return pl.pallas_call(
        _memkv_kernel,
        grid=(L, B),
        in_specs=[pl.BlockSpec((None, M, D), lambda l, b: (b, 0, 0)),
                  pl.BlockSpec((None, 1, D), lambda l, b: (l, 0, 0)),
                  pl.BlockSpec((None, D, 2 * D), lambda l, b: (l, 0, 0))],
        out_specs=pl.BlockSpec((None, None, M, 2 * D), lambda l, b: (l, b, 0, 0)),
        out_shape=jax.ShapeDtypeStruct((L, B, M, 2 * D), BF16),
        compiler_params=_cparams(("parallel", "arbitrary")),
        name="mem_kv",
    )(mem, g, w)


def _outmem_kernel(x_ref, oa_ref, ob_ref, oc_ref, wo_ref, g_ref, wq_ref, kv_ref, wmo_ref, o_ref):
    D = x_ref.shape[-1]
    na, nb = oa_ref.shape[-1], ob_ref.shape[-1]
    x = (x_ref[...] + _dot(oa_ref[...], wo_ref[0:na]) + _dot(ob_ref[...], wo_ref[na:na + nb])
         + _dot(oc_ref[...], wo_ref[na + nb:]))
    h = _rms(x, g_ref[...]).astype(BF16)
    hd = D // MEM_HEADS
    q = (_dot(h, wq_ref[...]) * (hd ** -0.5)).astype(BF16)
    outs = []
    for hh in range(MEM_HEADS):
        k = kv_ref[:, hh * hd:(hh + 1) * hd]
        v = kv_ref[:, D + hh * hd:D + (hh + 1) * hd]
        s = _dot_nt(q[:, hh * hd:(hh + 1) * hd], k)
        p = jnp.exp(s - jnp.max(s, axis=-1, keepdims=True))
        p = p / jnp.sum(p, axis=-1, keepdims=True)
        outs.append(_dot(p.astype(BF16), v).astype(BF16))
    o = jnp.concatenate(outs, axis=-1)
    o_ref[...] = x + _dot(o, wmo_ref[...])


def _outmem(x, oa, ob, oc, wo, g, wq, kv, layer, wmo, tm):
    B, S, D = x.shape
    M = kv.shape[2]
    const = lambda b, i: (0, 0)
    return pl.pallas_call(
        _outmem_kernel,
        grid=(B, S // tm),
        in_specs=[pl.BlockSpec((None, tm, D), lambda b, i: (b, i, 0)),
                  pl.BlockSpec((None, tm, oa.shape[-1]), lambda b, i: (b, i, 0)),
                  pl.BlockSpec((None, tm, ob.shape[-1]), lambda b, i: (b, i, 0)),
                  pl.BlockSpec((None, tm, oc.shape[-1]), lambda b, i: (b, i, 0)),
                  pl.BlockSpec(wo.shape, const),
                  pl.BlockSpec((1, D), const),
                  pl.BlockSpec(wq.shape, const),
                  pl.BlockSpec((None, None, M, 2 * D), lambda b, i: (layer, b, 0, 0)),
                  pl.BlockSpec(wmo.shape, const)],
        out_specs=pl.BlockSpec((None, tm, D), lambda b, i: (b, i, 0)),
        out_shape=jax.ShapeDtypeStruct((B, S, D), F32),
        compiler_params=_cparams(("parallel", "arbitrary")),
        name="outproj_memattn",
    )(x, oa, ob, oc, wo, g, wq, kv, wmo)


HALO = 8


def _ffn_kernel(x_ref, xp_ref, g_ref, wg_ref, wv_ref, cwg_ref, cwv_ref, cbg_ref, cbv_ref, wd_ref,
                gout_ref, o_ref, *, tm, fc, norm_out):
    i = pl.program_id(1)
    x = x_ref[...]
    g = g_ref[...]
    hp = _rms(xp_ref[...], g) * jnp.where(i > 0, 1.0, 0.0)
    h = jnp.concatenate([hp, _rms(x, g)], axis=0).astype(BF16)
    F = wd_ref.shape[0]

    def conv(u, cw_ref, cb_ref, c, e):
        out = cb_ref[:, c:e]
        for j in range(CONV_WIDTH):
            shift = CONV_WIDTH - 1 - j
            out = out + cw_ref[j:j + 1, c:e] * u[HALO - shift:HALO - shift + tm]
        return out

    acc = x
    for c in range(0, F, fc):
        e = min(c + fc, F)
        gate = conv(_dot(h, wg_ref[:, c:e]), cwg_ref, cbg_ref, c, e)
        val = conv(_dot(h, wv_ref[:, c:e]), cwv_ref, cbv_ref, c, e)
        act = (gate * jax.nn.sigmoid(gate) * val).astype(BF16)
        acc = acc + _dot(act, wd_ref[c:e, :])
    o_ref[...] = _rms(acc, gout_ref[...]) if norm_out else acc


def _ffn(x, g, wg, wv, cwg, cwv, cbg, cbv, wd, gout, norm_out, tm, fc):
    B, S, D = x.shape
    F = wd.shape[0]
    const = lambda b, i: (0, 0)
    hb = tm // HALO
    return pl.pallas_call(
        functools.partial(_ffn_kernel, tm=tm, fc=fc, norm_out=norm_out),
        grid=(B, S // tm),
        in_specs=[pl.BlockSpec((None, tm, D), lambda b, i: (b, i, 0)),
                  pl.BlockSpec((None, HALO, D), lambda b, i: (b, jnp.maximum(i * hb - 1, 0), 0)),
                  pl.BlockSpec((1, D), const),
                  pl.BlockSpec((D, F), const, pipeline_mode=pl.Buffered(1)),
                  pl.BlockSpec((D, F), lambda b, i: (0, 1), pipeline_mode=pl.Buffered(1)),
                  pl.BlockSpec((CONV_WIDTH, F), const), pl.BlockSpec((CONV_WIDTH, F), const),
                  pl.BlockSpec((1, F), const), pl.BlockSpec((1, F), const),
                  pl.BlockSpec((F, D), const, pipeline_mode=pl.Buffered(1)),
                  pl.BlockSpec((1, D), const)],
        out_specs=pl.BlockSpec((None, tm, D), lambda b, i: (b, i, 0)),
        out_shape=jax.ShapeDtypeStruct((B, S, D), F32),
        compiler_params=_cparams(("parallel", "arbitrary")),
        name="conv_ffn",
    )(x, x, g, wg, wv, cwg, cwv, cbg, cbv, wd, gout)


def _tile_params(S, tq_a, tk_a):
    big = 1 << 30
    gran, (d_min, d_const) = min(tq_a, tk_a), _a_offsets(tq_a, tk_a)
    pa = []
    for h in range(A_HEADS):
        pa += [(h, d * gran, 1, 0, big, -1) for d in range(d_min, d_const + 1)]
        pa.append((h, 0, 1, 1, 0, -1))
    pb = []
    for _, r in DILATED_PATTERNS:
        for h in range(B_HEADS):
            pb.append((A_HEADS + h, DIL_BLK, r, 1, DIL_BLK, 1))
            pb.append((A_HEADS + h, 0, r, 0, DIL_BLK, 1))
    pc = [(A_HEADS + B_HEADS + h, d * LANES, 1, -big, big, -1)
          for h in range(C_HEADS) for d in range(_n_offsets(S, LANES))]
    to = lambda p: jnp.asarray(p, jnp.int32)
    return to(pa), to(pb), to(pc)


def _dil_bias_layout(tiles):
    t = tiles.reshape(len(DILATED_PATTERNS), B_HEADS, 2, DIL_BLK, DIL_BLK)
    return jnp.concatenate([t[:, :, 0], t[:, :, 1]], axis=-1).reshape(
        len(DILATED_PATTERNS), B_HEADS * DIL_BLK, 2 * DIL_BLK)


def _n_offsets(S, blk):
    return min(S // blk, REL_MAX_DIST // blk + 2)


def _a_offsets(tq, tk):
    gran = min(tq, tk)
    return -(tq // gran - 1), -(-(REL_MAX_DIST - 1 + tk) // gran)


def _in_weights(w_in_l):
    sizes = (512, 512, 512, 256, 256, 256, 256, 256, 256, 512, 64, 8)
    offs = [0]
    for s in sizes:
        offs.append(offs[-1] + s)
    (aq, ak, av, bq, bk, bv, cq, ck, cv, iq, ik, iw) = [w_in_l[:, offs[i]:offs[i + 1]] for i in range(12)]
    pad = jnp.zeros((w_in_l.shape[0], N_I - 2 * IDX_DIM - IDX_HEADS), w_in_l.dtype)
    return jnp.concatenate([aq, ak, av, iq, cq, ck, cv, bq, bk, bv, ik, ik, iw, pad], axis=1).astype(BF16)


def _forward(x, mem, rel_bias, norm_mix, w_in, lam_q1, lam_k1, lam_q2, lam_k2, subln, w_out,
             norm_mem, norm_memkv, w_mq, w_mkv, w_mo, norm_ffn, w_up, conv_w, conv_b, w_down,
             norm_final, *, tq_a, tk_a, tq_c, ch_c, ca_c, tm_proj, tm_mem, tm_ffn, fc):
    B, S, D = x.shape
    L = w_in.shape[0]
    F = w_down.shape[1]
    pa, pb, pc = _tile_params(S, tq_a, tk_a)
    bias_a = _bias_tiles(pa, rel_bias, tk_a, tq_a, LOG2E).reshape(A_HEADS, -1, tk_a, tq_a)
    d_min, d_const = _a_offsets(tq_a, tk_a)
    bias_b = _dil_bias_layout(_bias_tiles(pb, rel_bias, DIL_BLK, DIL_BLK, LOG2E))
    bias_c = _bias_tiles(pc, rel_bias, LANES, LANES, LOG2E).reshape(C_HEADS, -1, LANES, LANES)
    kv_all = _memkv(mem, norm_memkv.reshape(L, 1, D), w_mkv.astype(BF16))

    for l in range(L):
        lam_init = 0.8 - 0.6 * math.exp(-0.3 * l)
        proj, proj_b, proj_i, avt, cvt, iwt = _inproj(
            x, norm_mix[l].reshape(1, D), _in_weights(w_in[l]), tm_proj, tk_a, ca_c)
        proj = proj.reshape(B, S, N_MAIN)
        proj_i = proj_i.reshape(B, S, N_I)
        lamv = jnp.stack([lam_q1[l], lam_k1[l], lam_q2[l], lam_k2[l]], axis=0)
        o_a = _diff_attn(proj, avt, lamv, bias_a, subln[l].reshape(A_V, 1), lam_init, tq_a,
                         d_min, d_const)
        o_b = _dil_attn(proj_b.reshape(B, S, N_B), bias_b)
        o_c = _dsa_attn(proj, proj_i, iwt, cvt, bias_c, tq_c, ch_c)
        x = _outmem(x, o_a, o_b, o_c, w_out[l].astype(BF16), norm_mem[l].reshape(1, D),
                    w_mq[l].astype(BF16), kv_all, l, w_mo[l].astype(BF16), tm_mem)
        wu = w_up[l].astype(BF16)
        x = _ffn(x, norm_ffn[l].reshape(1, D), wu, wu, conv_w[l][:, :F], conv_w[l][:, F:],
                 conv_b[l][:F].reshape(1, F), conv_b[l][F:].reshape(1, F), w_down[l].astype(BF16),
                 norm_final.reshape(1, D), l == L - 1, tm_ffn, fc)
    return x


def kernel(x, mem, rel_bias, norm_mix, w_in, lam_q1, lam_k1, lam_q2, lam_k2, subln, w_out,
           norm_mem, norm_memkv, w_mq, w_mkv, w_mo, norm_ffn, w_up, conv_w, conv_b, w_down,
           norm_final):
    return _forward(x, mem, rel_bias, norm_mix, w_in, lam_q1, lam_k1, lam_q2, lam_k2, subln, w_out,
                    norm_mem, norm_memkv, w_mq, w_mkv, w_mo, norm_ffn, w_up, conv_w, conv_b, w_down,
                    norm_final, tq_a=512, tk_a=256, tq_c=512, ch_c=512, ca_c=256, tm_proj=1024, tm_mem=1024, tm_ffn=1024,
                    fc=512)
```

```python
import functools
import math

import jax
import jax.numpy as jnp
from jax import lax
from jax.experimental import pallas as pl
from jax.experimental.pallas import tpu as pltpu

F32 = jnp.float32
BF16 = jnp.bfloat16

EPS = 1e-6
NEG = -1e30
LOG2E = math.log2(math.e)
LANES = 128
VMEM_LIMIT = 56 * 1024 * 1024

A_HEADS, A_QK, A_V = 4, 64, 128
B_HEADS, B_DIM = 4, 64
C_HEADS, C_DIM = 4, 64
IDX_HEADS, IDX_DIM = 8, 64
TOPK_MAX = 256
DILATED_PATTERNS = ((128, 1), (512, 4), (2048, 16))
ONES_ROWS = 16
DIL_BLK = 128
DIL_UNROLL = 4
REL_BUCKETS, REL_MAX_DIST = 32, 2048
MEM_HEADS = 4
CONV_WIDTH = 3

N_MAIN = 2816
N_B = 768
N_I = 256


def _cparams(sem):
    return pltpu.CompilerParams(dimension_semantics=sem, vmem_limit_bytes=VMEM_LIMIT)


def _dot(a, b):
    return jnp.dot(a, b, preferred_element_type=F32)


def _dot_nt(a, b):
    return lax.dot_general(a, b, (((1,), (1,)), ((), ())), preferred_element_type=F32)


def _ones_rows(n):
    row = lax.broadcasted_iota(jnp.int32, (ONES_ROWS, n), 0)
    return jnp.where(row == 0, 1.0, 0.0).astype(BF16)


def _rms(x, g):
    return x * lax.rsqrt(jnp.mean(x * x, axis=-1, keepdims=True) + EPS) * g


def _rel_bucket(n):
    max_exact = REL_BUCKETS // 2
    nf = jnp.maximum(n, 1).astype(F32)
    large = max_exact + (jnp.log(nf / max_exact) / math.log(REL_MAX_DIST / max_exact)
                         * (REL_BUCKETS - max_exact)).astype(jnp.int32)
    large = jnp.minimum(large, REL_BUCKETS - 1)
    return jnp.where(n < max_exact, n, large)


def _bias_tiles_kernel(par_ref, tab_ref, o_ref, *, tq, tk, scale):
    t = pl.program_id(0)
    head, off, mult = par_ref[t, 0], par_ref[t, 1], par_ref[t, 2]
    lo, hi, sgn = par_ref[t, 3], par_ref[t, 4], par_ref[t, 5]
    dist = off + sgn * (lax.broadcasted_iota(jnp.int32, (tq, tk), 0)
                        - lax.broadcasted_iota(jnp.int32, (tq, tk), 1))
    bucket = _rel_bucket(jnp.maximum(dist * mult, 0))
    val = jnp.zeros((tq, tk), F32)
    for b in range(REL_BUCKETS):
        val = jnp.where(bucket == b, tab_ref[b, head] * scale, val)
    o_ref[...] = jnp.where((dist >= lo) & (dist <= hi), val, NEG)


def _bias_tiles(params, table, tq, tk, scale=1.0):
    n = params.shape[0]
    return pl.pallas_call(
        functools.partial(_bias_tiles_kernel, tq=tq, tk=tk, scale=scale),
        grid=(n,),
        in_specs=[pl.BlockSpec(memory_space=pltpu.SMEM), pl.BlockSpec(memory_space=pltpu.SMEM)],
        out_specs=pl.BlockSpec((None, tq, tk), lambda t: (t, 0, 0)),
        out_shape=jax.ShapeDtypeStruct((n, tq, tk), F32),
        compiler_params=_cparams(("arbitrary",)),
        name="bias_tiles",
    )(params, table)


def _inproj_kernel(x_ref, g_ref, w_ref, om_ref, ob_ref, oi_ref, avt_ref, cvt_ref, iwt_ref):
    h = _rms(x_ref[...], g_ref[...]).astype(BF16)
    step = 512
    for c in range(0, N_MAIN, step):
        e = min(c + step, N_MAIN)
        om_ref[:, c:e] = _dot(h, w_ref[:, c:e]).astype(BF16)
    ob_ref[...] = _dot(h, w_ref[:, N_MAIN:N_MAIN + N_B])
    oi_ref[...] = _dot(h, w_ref[:, N_MAIN + N_B:])
    tk = avt_ref.shape[-1]
    a0 = 2 * A_HEADS * LANES
    for hd in range(A_HEADS):
        vt = om_ref[:, a0 + hd * A_V:a0 + (hd + 1) * A_V].astype(F32).T
        for j in range(avt_ref.shape[1]):
            avt_ref[hd, j] = vt[:, j * tk:(j + 1) * tk].astype(BF16)
    ca = cvt_ref.shape[-1]
    cw = C_HEADS * C_DIM
    for half in range(cw // LANES):
        c0 = N_MAIN - cw + half * LANES
        vt = om_ref[:, c0:c0 + LANES].astype(F32).T
        for j in range(cvt_ref.shape[0]):
            cvt_ref[j, half * LANES:(half + 1) * LANES, :] = vt[:, j * ca:(j + 1) * ca].astype(BF16)
    iwt_ref[...] = oi_ref[:, 2 * IDX_DIM:2 * IDX_DIM + LANES].T[:IDX_HEADS]


def _inproj(x, g, w, tm, tk_a, ca_c):
    B, S, D = x.shape
    T = B * S
    nt = S // tm
    n_all = N_MAIN + N_B + N_I
    cw = C_HEADS * C_DIM
    return pl.pallas_call(
        _inproj_kernel,
        grid=(T // tm,),
        in_specs=[pl.BlockSpec((tm, D), lambda i: (i, 0)),
                  pl.BlockSpec((1, D), lambda i: (0, 0)),
                  pl.BlockSpec((D, n_all), lambda i: (0, 0))],
        out_specs=[pl.BlockSpec((tm, N_MAIN), lambda i: (i, 0)),
                   pl.BlockSpec((tm, N_B), lambda i: (i, 0)),
                   pl.BlockSpec((tm, N_I), lambda i: (i, 0)),
                   pl.BlockSpec((None, A_HEADS, tm // tk_a, A_V, tk_a),
                                lambda i: (i // nt, 0, i % nt, 0, 0)),
                   pl.BlockSpec((None, tm // ca_c, cw, ca_c), lambda i: (i // nt, i % nt, 0, 0)),
                   pl.BlockSpec((None, IDX_HEADS, tm), lambda i: (i // nt, 0, i % nt))],
        out_shape=[jax.ShapeDtypeStruct((T, N_MAIN), BF16),
                   jax.ShapeDtypeStruct((T, N_B), F32),
                   jax.ShapeDtypeStruct((T, N_I), F32),
                   jax.ShapeDtypeStruct((B, A_HEADS, S // tk_a, A_V, tk_a), BF16),
                   jax.ShapeDtypeStruct((B, S // ca_c, cw, ca_c), BF16),
                   jax.ShapeDtypeStruct((B, IDX_HEADS, S), F32)],
        compiler_params=_cparams(("parallel",)),
        name="inproj",
    )(x.reshape(T, D), g, w)


def _diff_attn_kernel(lam_ref, q_ref, k_ref, vt_ref, bias_ref, g_ref, o_ref, s_sc, acc_sc,
                      *, tq, tk, d_min, d_const, lam_init):
    qi = pl.program_id(2)
    gran = min(tq, tk)
    last = (qi * tq + tq - 1) // tk
    masked_tile = d_const - d_min + 1
    lv = lam_ref[...]
    lam = (jnp.exp(jnp.sum(lv[0:1] * lv[1:2], axis=-1, keepdims=True))
           - jnp.exp(jnp.sum(lv[2:3] * lv[3:4], axis=-1, keepdims=True)) + lam_init)
    lane = lax.broadcasted_iota(jnp.int32, (tq, LANES), 1)
    q = q_ref[...].astype(F32) * (A_QK ** -0.5 * LOG2E)
    qcat = jnp.concatenate([jnp.where(lane < A_QK, q, 0.0), jnp.where(lane >= A_QK, q, 0.0)],
                           axis=0).astype(BF16)
    acc_sc[...] = jnp.zeros(acc_sc.shape, F32)

    def scores_into(slot, j):
        jc = jnp.minimum(j, last)
        k = k_ref[pl.ds(pl.multiple_of(jc * tk, tk), tk), :]
        d = qi * (tq // gran) - jc * (tk // gran)
        bias = bias_ref[jnp.where(j <= last, jnp.minimum(d, d_const) - d_min, masked_tile)]
        s_sc[slot] = _dot_nt(k, qcat) + jnp.concatenate([bias, bias], axis=1)

    def absorb(slot, j, m_prev):
        m_new = jnp.maximum(m_prev, jnp.max(s_sc[slot], axis=0, keepdims=True))
        alpha = jnp.exp2(m_prev - m_new)
        p = jnp.exp2(s_sc[slot] - m_new)
        vt = jnp.concatenate([vt_ref[jnp.minimum(j, last)], _ones_rows(tk)], axis=0)
        acc_sc[...] = alpha * acc_sc[...] + _dot(vt, p.astype(BF16))
        return m_new

    scores_into(0, 0)

    def pair(a, m):
        scores_into(1, a + 1)
        m = absorb(0, a, m)
        scores_into(0, a + 2)
        return absorb(1, a + 1, m)

    pairs = last // 2
    m = lax.fori_loop(0, pairs // 2, lambda jj, m: pair(4 * jj + 2, pair(4 * jj, m)),
                      jnp.full((1, 2 * tq), NEG, F32))
    m = lax.fori_loop(pairs // 2 * 2, pairs, lambda jj, m: pair(2 * jj, m), m)

    half = tq // 2
    late = [slice(mp * tq + half, (mp + 1) * tq) for mp in range(2)]
    pick = lambda x: jnp.concatenate([x[:, c] for c in late], axis=1)
    k_last = k_ref[pl.ds(pl.multiple_of(last * tk, tk), tk), :]
    bias_last = bias_ref[-d_min][:, :half]
    q_late = jnp.concatenate([qcat[c] for c in late], axis=0)
    s_sc[1, :, :tq] = _dot_nt(k_last, q_late) + jnp.concatenate([bias_last, bias_last], axis=1)
    m = absorb(0, last - 1, m)
    m_late = pick(m)
    m_new = jnp.maximum(m_late, jnp.max(s_sc[1, :, :tq], axis=0, keepdims=True))
    alpha = jnp.exp2(m_late - m_new)
    p = jnp.exp2(s_sc[1, :, :tq] - m_new)
    vt = jnp.concatenate([vt_ref[last], _ones_rows(tk)], axis=0)
    upd = alpha * pick(acc_sc[...]) + _dot(vt, p.astype(BF16))
    for i, c in enumerate(late):
        acc_sc[:, c] = upd[:, i * half:(i + 1) * half]
    inv = 1.0 / acc_sc[A_V:A_V + 1, :]
    o = acc_sc[:A_V, :tq] * inv[:, :tq] - lam * (acc_sc[:A_V, tq:] * inv[:, tq:])
    o = o * lax.rsqrt(jnp.mean(o * o, axis=0, keepdims=True) + EPS) * g_ref[...] * (1.0 - lam_init)
    o_ref[...] = o.T.astype(o_ref.dtype)


def _diff_attn(proj, vt, lamv, bias_a, subln_g, lam_init, tq, d_min, d_const):
    B, S, _ = proj.shape
    nd, tk = bias_a.shape[1], bias_a.shape[2]
    assert tq == 2 * tk, (tq, tk)
    return pl.pallas_call(
        functools.partial(_diff_attn_kernel, tq=tq, tk=tk, d_min=d_min, d_const=d_const,
                          lam_init=lam_init),
        grid=(A_HEADS, B, S // tq),
        in_specs=[pl.BlockSpec((4, A_QK), lambda h, b, i: (0, 0)),
                  pl.BlockSpec((None, tq, LANES), lambda h, b, i: (b, i, h)),
                  pl.BlockSpec((None, S, LANES), lambda h, b, i: (b, 0, A_HEADS + h)),
                  pl.BlockSpec((None, None, S // tk, A_V, tk), lambda h, b, i: (b, h, 0, 0, 0)),
                  pl.BlockSpec((None, nd, tk, tq), lambda h, b, i: (h, 0, 0, 0)),
                  pl.BlockSpec((A_V, 1), lambda h, b, i: (0, 0))],
        out_specs=pl.BlockSpec((None, tq, LANES), lambda h, b, i: (b, i, h)),
        out_shape=jax.ShapeDtypeStruct((B, S, A_HEADS * A_V), BF16),
        scratch_shapes=[pltpu.VMEM((2, tk, 2 * tq), F32),
                        pltpu.VMEM((A_V + ONES_ROWS, 2 * tq), F32)],
        compiler_params=_cparams(("parallel", "parallel", "arbitrary")),
        name="diff_attn",
    )(lamv, proj, proj, vt, bias_a, subln_g)


class _LaneHalves:
    def __init__(self, *refs):
        self.refs = refs

    def __getitem__(self, idx):
        return jnp.concatenate([r[idx] for r in self.refs], axis=1)

    def __setitem__(self, idx, val):
        for i, r in enumerate(self.refs):
            r[idx] = val[:, i * LANES:(i + 1) * LANES]


def _dil_kernel(q0, q1, k0, k1, v0, v1, bias_ref, o_ref, n0, n1, m0, m1, l0, l1, *, seq, patterns):
    q_ref, k_ref, v_ref = _LaneHalves(q0, q1), _LaneHalves(k0, k1), _LaneHalves(v0, v1)
    n_sc, m_sc, l_sc = _LaneHalves(n0, n1), _LaneHalves(m0, m1), _LaneHalves(l0, l1)
    blk = DIL_BLK
    width = B_HEADS * B_DIM
    head_of_lane = lax.broadcasted_iota(jnp.int32, (blk, width), 1) // B_DIM

    def per_head(x):
        parts = [jnp.broadcast_to(x[h * blk:(h + 1) * blk], (blk, width)) for h in range(B_HEADS)]
        out = parts[-1]
        for h in range(B_HEADS - 2, -1, -1):
            out = jnp.where(head_of_lane == h, parts[h], out)
        return out

    def attend(base, r, with_prev, p_idx):
        q = q_ref[pl.ds(base, blk, stride=r), :] * (B_DIM ** -0.5 * LOG2E)
        qcat = jnp.concatenate([jnp.where(head_of_lane == h, q, 0.0) for h in range(B_HEADS)],
                               axis=0).astype(BF16)
        if with_prev:
            keys = pl.ds(base - blk * r, 2 * blk, stride=r)
            bias = bias_ref[p_idx]
        else:
            keys = pl.ds(base, blk, stride=r)
            bias = bias_ref[p_idx, :, blk:]
        s = _dot_nt(qcat, k_ref[keys, :].astype(BF16)) + bias
        mx = jnp.max(s, axis=-1, keepdims=True)
        p = jnp.exp2(s - mx)
        l = jnp.sum(p, axis=-1, keepdims=True)
        o = _dot(p.astype(BF16), v_ref[keys, :].astype(BF16))
        return per_head(o), per_head(mx), per_head(l)

    def merge(base, r, res, is_first):
        o, m, l = res
        rows = pl.ds(base, blk, stride=r)
        if is_first:
            n_sc[rows, :] = o
            m_sc[rows, :] = m
            l_sc[rows, :] = l
        else:
            m_old = m_sc[rows, :]
            m_new = jnp.maximum(m_old, m)
            a = jnp.exp2(m_old - m_new)
            b = jnp.exp2(m - m_new)
            n_sc[rows, :] = a * n_sc[rows, :] + b * o
            l_sc[rows, :] = a * l_sc[rows, :] + b * l
            m_sc[rows, :] = m_new

    def sweep(count, base_of, r, with_prev, p_idx):
        def trip(width):
            def body(i, carry):
                bases = [base_of(i * width + u) for u in range(width)]
                results = [attend(b, r, with_prev, p_idx) for b in bases]
                for b, res in zip(bases, results):
                    merge(b, r, res, p_idx == len(patterns) - 1)
                return carry
            return body

        full = count // DIL_UNROLL
        lax.fori_loop(0, full, trip(DIL_UNROLL), 0)
        rest = count - full * DIL_UNROLL
        if rest:
            lax.fori_loop(full * DIL_UNROLL, count, trip(1), 0)

    for p_idx, (_, r) in reversed(list(enumerate(patterns))):
        nb = seq // (r * blk)
        sweep(r, lambda c: c, r, False, p_idx)
        sweep(r * (nb - 1), lambda idx, r=r: (idx // r + 1) * blk * r + idx % r, r, True, p_idx)

    everything = (slice(None), slice(None))
    o_ref[...] = (n_sc[everything] / l_sc[everything]).astype(o_ref.dtype)


def _dil_attn(proj_b, bias_b):
    B, S, _ = proj_b.shape
    width = B_HEADS * B_DIM
    return pl.pallas_call(
        functools.partial(_dil_kernel, seq=S, patterns=DILATED_PATTERNS),
        grid=(B,),
        in_specs=[pl.BlockSpec((None, S, LANES), functools.partial(lambda i, b: (b, 0, i), i))
                  for i in range(3 * width // LANES)]
                 + [pl.BlockSpec(bias_b.shape, lambda b: (0, 0, 0))],
        out_specs=pl.BlockSpec((None, S, width), lambda b: (b, 0, 0)),
        out_shape=jax.ShapeDtypeStruct((B, S, width), BF16),
        scratch_shapes=[pltpu.VMEM((S, LANES), F32)] * (3 * width // LANES),
        compiler_params=_cparams(("parallel",)),
        name="dilated_attn",
    )(*([proj_b] * (3 * width // LANES)), bias_b)


SEL16 = jnp.bfloat16


RADIX_LOW_BITS = 5
PEEL_CAP = 6
FLT_MIN_NORMAL = 2.0 ** -126


def _float_key(f):
    b = pltpu.bitcast(f, jnp.int32)
    return jnp.where(b >= 0, b, b ^ jnp.int32(0x7FFFFFFF))


def _key_float(k):
    return pltpu.bitcast(jnp.where(k >= 0, k, k ^ jnp.int32(0x7FFFFFFF)), F32)


def _high_half(f):
    return pltpu.bitcast(pltpu.bitcast(f, jnp.int32) & jnp.int32(-65536), F32)


def _dsa_kernel(iq_ref, ik_ref, iwt_ref, cq_ref, ck_ref, cvt_ref, bias_ref, o_ref, sc_ref, hi_ref, s_sc,
                acc_sc,
                *, tq, ch, ca, topk):
    qi = pl.program_id(1)
    q0 = qi * tq
    nch = (q0 + tq - 1) // ch + 1
    qidx = q0 + lax.broadcasted_iota(jnp.int32, (1, tq), 1)
    lane = lax.broadcasted_iota(jnp.int32, (tq, LANES), 1)
    first = lane < IDX_DIM

    w = iwt_ref[...] * (IDX_HEADS ** -0.5)
    qs = []
    for h in range(IDX_HEADS):
        tile = iq_ref[:, (h // 2) * LANES:(h // 2 + 1) * LANES].astype(F32) * (IDX_DIM ** -0.5)
        qs.append(jnp.where(first if h % 2 == 0 else ~first, tile, 0.0).astype(BF16))

    half_k, half_q = ch // 2, tq // 2
    every, late = slice(0, tq), slice(half_q, tq)
    diag = nch - 1

    def scores_of(start, nkeys, cols):
        kk = ik_ref[pl.ds(start, nkeys), :].astype(BF16)
        sc = jnp.zeros((nkeys, cols.stop - cols.start), F32)
        for h in range(IDX_HEADS):
            sc = sc + jnp.maximum(_dot_nt(kk, qs[h][cols]), 0.0) * w[h:h + 1, cols]
        kidx = start + lax.broadcasted_iota(jnp.int32, (nkeys, 1), 0)
        return jnp.where(kidx <= qidx[:, cols], sc, NEG)

    def score_body(j, carry):
        sc = scores_of(pl.multiple_of(j * ch, ch), ch, every)
        sc_ref[j] = sc
        hi_ref[j] = _high_half(sc).astype(SEL16)
        return carry

    lax.fori_loop(0, diag, score_body, 0)
    dstart = pl.multiple_of(diag * ch, ch)
    sc = scores_of(dstart, half_k, every)
    sc_ref[diag, :half_k, :] = sc
    hi_ref[diag, :half_k, :] = _high_half(sc).astype(SEL16)
    sc = scores_of(dstart + half_k, half_k, late)
    sc_ref[diag, half_k:, half_q:] = sc
    hi_ref[diag, half_k:, half_q:] = _high_half(sc).astype(SEL16)
    sc_ref[diag, half_k:, :half_q] = jnp.full((half_k, half_q), NEG, F32)
    hi_ref[diag, half_k:, :half_q] = jnp.full((half_k, half_q), NEG, SEL16)

    rows = 32

    def fold_chunks(ref, step, init):
        acc = lax.fori_loop(0, diag, lambda j, a: step(ref[j], a, every), init)
        acc = step(ref[diag, :half_k, :], acc, every)
        tail = step(ref[diag, half_k:, half_q:], acc[:, half_q:], late)
        return jnp.concatenate([acc[:, :half_q], tail], axis=1)

    def count_ge(t):
        def step(x, acc, cols):
            hit = jnp.where(x >= t[:, cols], 1.0, 0.0)
            return acc + jnp.sum(hit.reshape(x.shape[0] // rows, rows, x.shape[1]), axis=0)
        acc = fold_chunks(sc_ref, step, jnp.zeros((rows, tq), F32))
        return jnp.sum(acc, axis=0, keepdims=True)

    def count_high_ge(v):
        one, zero16 = jnp.ones((), SEL16), jnp.zeros((), SEL16)
        def step(x, acc, cols):
            hit = jnp.where(x >= v[:, cols], one, zero16).reshape(x.shape[0] // rows, rows, x.shape[1])
            parts = [hit[i] for i in range(x.shape[0] // rows)]
            while len(parts) > 1:
                parts = [parts[i] + parts[i + 1] for i in range(0, len(parts), 2)]
            return acc + parts[0]
        acc = fold_chunks(hi_ref, step, jnp.zeros((rows, tq), SEL16))
        return jnp.sum(acc.astype(F32), axis=0, keepdims=True)

    kf = float(topk)
    n_valid = (qidx + 1).astype(F32)
    half_bits = 16

    def high_body(i, st):
        u, c_lo = st
        cand = u | jnp.left_shift(1, half_bits - 1 - i)
        v = _high_half(_key_float((cand - (1 << (half_bits - 1))) << half_bits)).astype(SEL16)
        c = count_high_ge(v)
        keep = c >= kf
        return jnp.where(keep, cand, u), jnp.where(keep, c, c_lo)

    u, c_lo = lax.fori_loop(0, half_bits, high_body, (jnp.zeros((1, tq), jnp.int32), n_valid))
    key_hi = (u - (1 << (half_bits - 1))) << half_bits

    def low_body(i, st):
        lo_bits, c_lo = st
        cand = lo_bits | jnp.left_shift(1, half_bits - 1 - i)
        c = count_ge(_key_float(key_hi | cand))
        keep = c >= kf
        return jnp.where(keep, cand, lo_bits), jnp.where(keep, c, c_lo)

    lo_bits, c_lo = lax.fori_loop(0, RADIX_LOW_BITS, low_body, (jnp.zeros((1, tq), jnp.int32), c_lo))

    def min_ge(t):
        def step(x, acc, cols):
            kept = jnp.where(x >= t[:, cols], x, -NEG)
            return jnp.minimum(acc, jnp.min(kept.reshape(x.shape[0] // rows, rows, x.shape[1]), axis=0))
        acc = fold_chunks(sc_ref, step, jnp.full((rows, tq), -NEG, F32))
        return jnp.min(acc, axis=0, keepdims=True)

    def open_rows(c_lo, done):
        return jnp.logical_and(jnp.logical_and(n_valid > kf, c_lo > kf), done == 0.0)

    def any_row(mask):
        return jnp.max(jnp.where(mask, 1.0, 0.0)) > 0.0

    def peel_cond(st):
        _, c_lo, done, it = st
        return jnp.logical_and(it < PEEL_CAP, any_row(open_rows(c_lo, done)))

    def peel_body(st):
        t, c_lo, done, it = st
        is_open = open_rows(c_lo, done)
        smallest = min_ge(t)
        above = jnp.where(smallest == 0.0, FLT_MIN_NORMAL, _key_float(_float_key(smallest) + 1))
        c = count_ge(above)
        enough = jnp.logical_and(is_open, c >= kf)
        at_tie = jnp.logical_and(is_open, c < kf)
        t = jnp.where(enough, above, jnp.where(at_tie, smallest, t))
        return t, jnp.where(enough, c, c_lo), jnp.where(at_tie, 1.0, done), it + 1

    t_peel, c_peel, done, _ = lax.while_loop(
        peel_cond, peel_body,
        (_key_float(key_hi | lo_bits), c_lo, jnp.zeros((1, tq), F32), jnp.int32(0)))

    def finish_by_radix(_):
        bits, c = lax.fori_loop(RADIX_LOW_BITS, half_bits, low_body, (lo_bits, c_lo))
        return _key_float(key_hi | bits), c

    thr, c_lo = lax.cond(any_row(open_rows(c_peel, done)), finish_by_radix,
                         lambda _: (t_peel, c_peel), 0)
    thr = jnp.where(n_valid > kf, thr, 0.5 * NEG)

    tie_rows = jnp.logical_and(n_valid > kf, c_lo > kf)

    @pl.when(jnp.max(jnp.where(tie_rows, 1.0, 0.0)) > 0.0)
    def _():
        surplus = jnp.where(tie_rows, c_lo - kf, 0.0)
        upper = (lax.broadcasted_iota(jnp.int32, (ch, ch), 1)
                 >= lax.broadcasted_iota(jnp.int32, (ch, ch), 0)).astype(BF16)

        def drop_body(st):
            i, seen = st
            j = nch - 1 - i
            sc = sc_ref[j]
            eq = sc == thr
            eqf = jnp.where(eq, 1.0, 0.0)
            rank = _dot(upper, eqf.astype(BF16))
            sc_ref[j] = jnp.where(eq, jnp.where(rank <= surplus - seen, NEG, sc), sc)
            return i + 1, seen + jnp.sum(eqf, axis=0, keepdims=True)

        def more_to_drop(st):
            i, seen = st
            return jnp.logical_and(i < nch, any_row(seen < surplus))

        lax.while_loop(more_to_drop, drop_body, (jnp.int32(0), jnp.zeros((1, tq), F32)))

    cw = C_HEADS * C_DIM
    lane_c = lax.broadcasted_iota(jnp.int32, (tq, cw), 1)
    cq = cq_ref[...].astype(F32) * (C_DIM ** -0.5 * LOG2E)
    qcat = jnp.concatenate([jnp.where(lane_c // C_DIM == h, cq, 0.0) for h in range(C_HEADS)],
                           axis=0).astype(BF16)
    acc_sc[...] = jnp.zeros(acc_sc.shape, F32)
    nd = bias_ref.shape[1]
    last = (q0 + tq - 1) // ca

    def chunk_scores(j, c0):
        jc = jnp.minimum(j, last)
        kc = ck_ref[pl.ds(pl.multiple_of(jc * ca, ca), ca), :]
        sub = pl.multiple_of((jc % (ch // ca)) * ca, ca)
        sel = (sc_ref[jc // (ch // ca), pl.ds(sub, ca), c0:]
               >= jnp.where(j <= last, thr[:, c0:], -NEG))
        bias = jnp.concatenate([
            jnp.concatenate([
                bias_ref[h, jnp.clip(q0 // LANES + c - (jc * (ca // LANES) + u), 0, nd - 1)]
                for h in range(C_HEADS) for c in range(c0 // LANES, tq // LANES)], axis=1)
            for u in range(ca // LANES)], axis=0)
        q_rows = qcat if c0 == 0 else jnp.concatenate(
            [qcat[h * tq + c0:(h + 1) * tq] for h in range(C_HEADS)], axis=0)
        return jnp.where(jnp.concatenate([sel] * C_HEADS, axis=1), _dot_nt(kc, q_rows) + bias, NEG)

    def scores_into(slot, j):
        s_sc[slot] = chunk_scores(j, 0)

    def absorb(read, j, m_prev, c0):
        width = tq - c0
        m_new = jnp.maximum(m_prev, jnp.max(read(), axis=0, keepdims=True))
        alpha = jnp.exp2(m_prev - m_new)
        pb = jnp.exp2(read() - m_new).astype(BF16)
        jc = jnp.minimum(j, last)
        for h in range(C_HEADS):
            cols = slice(h * width, (h + 1) * width)
            vt = jnp.concatenate([cvt_ref[jc, h * C_DIM:(h + 1) * C_DIM, :], _ones_rows(ca)], axis=0)
            acc_sc[h, :, c0:] = alpha[:, cols] * acc_sc[h, :, c0:] + _dot(vt, pb[:, cols])
        return m_new

    scores_into(0, 0)

    def attn_body(jj, m):
        a = 2 * jj
        scores_into(1, a + 1)
        m = absorb(lambda: s_sc[0], a, m, 0)
        scores_into(0, a + 2)
        return absorb(lambda: s_sc[1], a + 1, m, 0)

    m = lax.fori_loop(0, last // 2, attn_body, jnp.full((1, C_HEADS * tq), NEG, F32))
    half = tq // 2
    s_sc[1, :, :C_HEADS * half] = chunk_scores(last, half)
    m = absorb(lambda: s_sc[0], last - 1, m, 0)
    m_late = jnp.concatenate([m[:, h * tq + half:(h + 1) * tq] for h in range(C_HEADS)], axis=1)
    absorb(lambda: s_sc[1, :, :C_HEADS * half], last, m_late, half)
    o_t = jnp.concatenate([acc_sc[h, :C_DIM] * (1.0 / acc_sc[h, C_DIM:C_DIM + 1])
                           for h in range(C_HEADS)], axis=0)
    o_ref[...] = o_t.T.astype(o_ref.dtype)


def _dsa_attn(proj, proj_i, iwt, cvt, bias_c, tq, ch):
    B, S, _ = proj.shape
    ca = cvt.shape[-1]
    topk = min(TOPK_MAX, S // 4)
    nd = bias_c.shape[1]
    cw = C_HEADS * C_DIM
    assert tq == ch == 2 * ca, (tq, ch, ca)
    return pl.pallas_call(
        functools.partial(_dsa_kernel, tq=tq, ch=ch, ca=ca, topk=topk),
        grid=(B, S // tq),
        in_specs=[pl.BlockSpec((None, tq, IDX_HEADS * IDX_DIM), lambda b, i: (b, i, 3)),
                  pl.BlockSpec((None, S, LANES), lambda b, i: (b, 0, 0)),
                  pl.BlockSpec((None, IDX_HEADS, tq), lambda b, i: (b, 0, i)),
                  pl.BlockSpec((None, tq, cw), lambda b, i: (b, i, 8)),
                  pl.BlockSpec((None, S, cw), lambda b, i: (b, 0, 9)),
                  pl.BlockSpec((None, S // ca, cw, ca), lambda b, i: (b, 0, 0, 0)),
                  pl.BlockSpec((C_HEADS, nd, LANES, LANES), lambda b, i: (0, 0, 0, 0))],
        out_specs=pl.BlockSpec((None, tq, cw), lambda b, i: (b, i, 0)),
        out_shape=jax.ShapeDtypeStruct((B, S, cw), BF16),
        scratch_shapes=[pltpu.VMEM((S // ch, ch, tq), F32),
                        pltpu.VMEM((S // ch, ch, tq), SEL16),
                        pltpu.VMEM((2, ca, C_HEADS * tq), F32),
                        pltpu.VMEM((C_HEADS, C_DIM + ONES_ROWS, tq), F32)],
        compiler_params=_cparams(("parallel", "arbitrary")),
        name="dsa_attn",
    )(proj, proj_i, iwt, proj, proj, cvt, bias_c)


def _memkv_kernel(mem_ref, g_ref, w_ref, o_ref):
    h = _rms(mem_ref[...], g_ref[...]).astype(BF16)
    o_ref[...] = _dot(h, w_ref[...]).astype(o_ref.dtype)


def _memkv(mem, g, w):
    B, M, D = mem.shape
    L = w.shape[0]
    return pl.pallas_call(
        _memkv_kernel,
        grid=(L, B),
        in_specs=[pl.BlockSpec((None, M, D), lambda l, b: (b, 0, 0)),
                  pl.BlockSpec((None, 1, D), lambda l, b: (l, 0, 0)),
                  pl.BlockSpec((None, D, 2 * D), lambda l, b: (l, 0, 0))],
        out_specs=pl.BlockSpec((None, None, M, 2 * D), lambda l, b: (l, b, 0, 0)),
        out_shape=jax.ShapeDtypeStruct((L, B, M, 2 * D), BF16),
        compiler_params=_cparams(("parallel", "arbitrary")),
        name="mem_kv",
    )(mem, g, w)


def _outmem_kernel(x_ref, oa_ref, ob_ref, oc_ref, wo_ref, g_ref, wq_ref, kv_ref, wmo_ref, o_ref):
    D = x_ref.shape[-1]
    na, nb = oa_ref.shape[-1], ob_ref.shape[-1]
    x = (x_ref[...] + _dot(oa_ref[...], wo_ref[0:na]) + _dot(ob_ref[...], wo_ref[na:na + nb])
         + _dot(oc_ref[...], wo_ref[na + nb:]))
    h = _rms(x, g_ref[...]).astype(BF16)
    hd = D // MEM_HEADS
    q = (_dot(h, wq_ref[...]) * (hd ** -0.5)).astype(BF16)
    outs = []
    for hh in range(MEM_HEADS):
        k = kv_ref[:, hh * hd:(hh + 1) * hd]
        v = kv_ref[:, D + hh * hd:D + (hh + 1) * hd]
        s = _dot_nt(q[:, hh * hd:(hh + 1) * hd], k)
        p = jnp.exp(s - jnp.max(s, axis=-1, keepdims=True))
        p = p / jnp.sum(p, axis=-1, keepdims=True)
        outs.append(_dot(p.astype(BF16), v).astype(BF16))
    o = jnp.concatenate(outs, axis=-1)
    o_ref[...] = x + _dot(o, wmo_ref[...])


def _outmem(x, oa, ob, oc, wo, g, wq, kv, layer, wmo, tm):
    B, S, D = x.shape
    M = kv.shape[2]
    const = lambda b, i: (0, 0)
    return pl.pallas_call(
        _outmem_kernel,
        grid=(B, S // tm),
        in_specs=[pl.BlockSpec((None, tm, D), lambda b, i: (b, i, 0)),
                  pl.BlockSpec((None, tm, oa.shape[-1]), lambda b, i: (b, i, 0)),
                  pl.BlockSpec((None, tm, ob.shape[-1]), lambda b, i: (b, i, 0)),
                  pl.BlockSpec((None, tm, oc.shape[-1]), lambda b, i: (b, i, 0)),
                  pl.BlockSpec(wo.shape, const),
                  pl.BlockSpec((1, D), const),
                  pl.BlockSpec(wq.shape, const),
                  pl.BlockSpec((None, None, M, 2 * D), lambda b, i: (layer, b, 0, 0)),
                  pl.BlockSpec(wmo.shape, const)],
        out_specs=pl.BlockSpec((None, tm, D), lambda b, i: (b, i, 0)),
        out_shape=jax.ShapeDtypeStruct((B, S, D), F32),
        compiler_params=_cparams(("parallel", "arbitrary")),
        name="outproj_memattn",
    )(x, oa, ob, oc, wo, g, wq, kv, wmo)


HALO = 8


def _ffn_kernel(x_ref, xp_ref, g_ref, wg_ref, wv_ref, cwg_ref, cwv_ref, cbg_ref, cbv_ref, wd_ref,
                gout_ref, o_ref, *, tm, fc, norm_out):
    i = pl.program_id(1)
    x = x_ref[...]
    g = g_ref[...]
    hp = _rms(xp_ref[...], g) * jnp.where(i > 0, 1.0, 0.0)
    h = jnp.concatenate([hp, _rms(x, g)], axis=0).astype(BF16)
    F = wd_ref.shape[0]

    def conv(u, cw_ref, cb_ref, c, e):
        out = cb_ref[:, c:e]
        for j in range(CONV_WIDTH):
            shift = CONV_WIDTH - 1 - j
            out = out + cw_ref[j:j + 1, c:e] * u[HALO - shift:HALO - shift + tm]
        return out

    acc = x
    for c in range(0, F, fc):
        e = min(c + fc, F)
        gate = conv(_dot(h, wg_ref[:, c:e]), cwg_ref, cbg_ref, c, e)
        val = conv(_dot(h, wv_ref[:, c:e]), cwv_ref, cbv_ref, c, e)
        act = (gate * jax.nn.sigmoid(gate) * val).astype(BF16)
        acc = acc + _dot(act, wd_ref[c:e, :])
    o_ref[...] = _rms(acc, gout_ref[...]) if norm_out else acc


def _ffn(x, g, wg, wv, cwg, cwv, cbg, cbv, wd, gout, norm_out, tm, fc):
    B, S, D = x.shape
    F = wd.shape[0]
    const = lambda b, i: (0, 0)
    hb = tm // HALO
    return pl.pallas_call(
        functools.partial(_ffn_kernel, tm=tm, fc=fc, norm_out=norm_out),
        grid=(B, S // tm),
        in_specs=[pl.BlockSpec((None, tm, D), lambda b, i: (b, i, 0)),
                  pl.BlockSpec((None, HALO, D), lambda b, i: (b, jnp.maximum(i * hb - 1, 0), 0)),
                  pl.BlockSpec((1, D), const),
                  pl.BlockSpec((D, F), const, pipeline_mode=pl.Buffered(1)),
                  pl.BlockSpec((D, F), lambda b, i: (0, 1), pipeline_mode=pl.Buffered(1)),
                  pl.BlockSpec((CONV_WIDTH, F), const), pl.BlockSpec((CONV_WIDTH, F), const),
                  pl.BlockSpec((1, F), const), pl.BlockSpec((1, F), const),
                  pl.BlockSpec((F, D), const, pipeline_mode=pl.Buffered(1)),
                  pl.BlockSpec((1, D), const)],
        out_specs=pl.BlockSpec((None, tm, D), lambda b, i: (b, i, 0)),
        out_shape=jax.ShapeDtypeStruct((B, S, D), F32),
        compiler_params=_cparams(("parallel", "arbitrary")),
        name="conv_ffn",
    )(x, x, g, wg, wv, cwg, cwv, cbg, cbv, wd, gout)


def _tile_params(S, tq_a, tk_a):
    big = 1 << 30
    gran, (d_min, d_const) = min(tq_a, tk_a), _a_offsets(tq_a, tk_a)
    pa = []
    for h in range(A_HEADS):
        pa += [(h, d * gran, 1, 0, big, -1) for d in range(d_min, d_const + 1)]
        pa.append((h, 0, 1, 1, 0, -1))
    pb = []
    for _, r in DILATED_PATTERNS:
        for h in range(B_HEADS):
            pb.append((A_HEADS + h, DIL_BLK, r, 1, DIL_BLK, 1))
            pb.append((A_HEADS + h, 0, r, 0, DIL_BLK, 1))
    pc = [(A_HEADS + B_HEADS + h, d * LANES, 1, -big, big, -1)
          for h in range(C_HEADS) for d in range(_n_offsets(S, LANES))]
    to = lambda p: jnp.asarray(p, jnp.int32)
    return to(pa), to(pb), to(pc)


def _dil_bias_layout(tiles):
    t = tiles.reshape(len(DILATED_PATTERNS), B_HEADS, 2, DIL_BLK, DIL_BLK)
    return jnp.concatenate([t[:, :, 0], t[:, :, 1]], axis=-1).reshape(
        len(DILATED_PATTERNS), B_HEADS * DIL_BLK, 2 * DIL_BLK)


def _n_offsets(S, blk):
    return min(S // blk, REL_MAX_DIST // blk + 2)


def _a_offsets(tq, tk):
    gran = min(tq, tk)
    return -(tq // gran - 1), -(-(REL_MAX_DIST - 1 + tk) // gran)


def _in_weights(w_in_l):
    sizes = (512, 512, 512, 256, 256, 256, 256, 256, 256, 512, 64, 8)
    offs = [0]
    for s in sizes:
        offs.append(offs[-1] + s)
    (aq, ak, av, bq, bk, bv, cq, ck, cv, iq, ik, iw) = [w_in_l[:, offs[i]:offs[i + 1]] for i in range(12)]
    pad = jnp.zeros((w_in_l.shape[0], N_I - 2 * IDX_DIM - IDX_HEADS), w_in_l.dtype)
    return jnp.concatenate([aq, ak, av, iq, cq, ck, cv, bq, bk, bv, ik, ik, iw, pad], axis=1).astype(BF16)


def _forward(x, mem, rel_bias, norm_mix, w_in, lam_q1, lam_k1, lam_q2, lam_k2, subln, w_out,
             norm_mem, norm_memkv, w_mq, w_mkv, w_mo, norm_ffn, w_up, conv_w, conv_b, w_down,
             norm_final, *, tq_a, tk_a, tq_c, ch_c, ca_c, tm_proj, tm_mem, tm_ffn, fc):
    B, S, D = x.shape
    L = w_in.shape[0]
    F = w_down.shape[1]
    pa, pb, pc = _tile_params(S, tq_a, tk_a)
    bias_a = _bias_tiles(pa, rel_bias, tk_a, tq_a, LOG2E).reshape(A_HEADS, -1, tk_a, tq_a)
    d_min, d_const = _a_offsets(tq_a, tk_a)
    bias_b = _dil_bias_layout(_bias_tiles(pb, rel_bias, DIL_BLK, DIL_BLK, LOG2E))
    bias_c = _bias_tiles(pc, rel_bias, LANES, LANES, LOG2E).reshape(C_HEADS, -1, LANES, LANES)
    kv_all = _memkv(mem, norm_memkv.reshape(L, 1, D), w_mkv.astype(BF16))

    for l in range(L):
        lam_init = 0.8 - 0.6 * math.exp(-0.3 * l)
        proj, proj_b, proj_i, avt, cvt, iwt = _inproj(
            x, norm_mix[l].reshape(1, D), _in_weights(w_in[l]), tm_proj, tk_a, ca_c)
        proj = proj.reshape(B, S, N_MAIN)
        proj_i = proj_i.reshape(B, S, N_I)
        lamv = jnp.stack([lam_q1[l], lam_k1[l], lam_q2[l], lam_k2[l]], axis=0)
        o_a = _diff_attn(proj, avt, lamv, bias_a, subln[l].reshape(A_V, 1), lam_init, tq_a,
                         d_min, d_const)
        o_b = _dil_attn(proj_b.reshape(B, S, N_B), bias_b)
        o_c = _dsa_attn(proj, proj_i, iwt, cvt, bias_c, tq_c, ch_c)
        x = _outmem(x, o_a, o_b, o_c, w_out[l].astype(BF16), norm_mem[l].reshape(1, D),
                    w_mq[l].astype(BF16), kv_all, l, w_mo[l].astype(BF16), tm_mem)
        wu = w_up[l].astype(BF16)
        x = _ffn(x, norm_ffn[l].reshape(1, D), wu, wu, conv_w[l][:, :F], conv_w[l][:, F:],
                 conv_b[l][:F].reshape(1, F), conv_b[l][F:].reshape(1, F), w_down[l].astype(BF16),
                 norm_final.reshape(1, D), l == L - 1, tm_ffn, fc)
    return x


def kernel(x, mem, rel_bias, norm_mix, w_in, lam_q1, lam_k1, lam_q2, lam_k2, subln, w_out,
           norm_mem, norm_memkv, w_mq, w_mkv, w_mo, norm_ffn, w_up, conv_w, conv_b, w_down,
           norm_final):
    return _forward(x, mem, rel_bias, norm_mix, w_in, lam_q1, lam_k1, lam_q2, lam_k2, subln, w_out,
                    norm_mem, norm_memkv, w_mq, w_mkv, w_mo, norm_ffn, w_up, conv_w, conv_b, w_down,
                    norm_final, tq_a=512, tk_a=256, tq_c=512, ch_c=512, ca_c=256, tm_proj=1024, tm_mem=1024, tm_ffn=1024,
                    fc=512)
```

```python
import functools
import math

import jax
import jax.numpy as jnp
from jax import lax
from jax.experimental import pallas as pl
from jax.experimental.pallas import tpu as pltpu

F32 = jnp.float32
BF16 = jnp.bfloat16

EPS = 1e-6
NEG = -1e30
LOG2E = math.log2(math.e)
LANES = 128
VMEM_LIMIT = 56 * 1024 * 1024

A_HEADS, A_QK, A_V = 4, 64, 128
B_HEADS, B_DIM = 4, 64
C_HEADS, C_DIM = 4, 64
IDX_HEADS, IDX_DIM = 8, 64
TOPK_MAX = 256
DILATED_PATTERNS = ((128, 1), (512, 4), (2048, 16))
ONES_ROWS = 16
DIL_BLK = 128
DIL_UNROLL = 4
REL_BUCKETS, REL_MAX_DIST = 32, 2048
MEM_HEADS = 4
CONV_WIDTH = 3

N_MAIN = 2816
N_B = 768
N_I = 256


def _cparams(sem):
    return pltpu.CompilerParams(dimension_semantics=sem, vmem_limit_bytes=VMEM_LIMIT)


def _dot(a, b):
    return jnp.dot(a, b, preferred_element_type=F32)


def _dot_nt(a, b):
    return lax.dot_general(a, b, (((1,), (1,)), ((), ())), preferred_element_type=F32)


def _ones_rows(n):
    row = lax.broadcasted_iota(jnp.int32, (ONES_ROWS, n), 0)
    return jnp.where(row == 0, 1.0, 0.0).astype(BF16)


def _rms(x, g):
    return x * lax.rsqrt(jnp.mean(x * x, axis=-1, keepdims=True) + EPS) * g


def _rel_bucket(n):
    max_exact = REL_BUCKETS // 2
    nf = jnp.maximum(n, 1).astype(F32)
    large = max_exact + (jnp.log(nf / max_exact) / math.log(REL_MAX_DIST / max_exact)
                         * (REL_BUCKETS - max_exact)).astype(jnp.int32)
    large = jnp.minimum(large, REL_BUCKETS - 1)
    return jnp.where(n < max_exact, n, large)


def _bias_tiles_kernel(par_ref, tab_ref, o_ref, *, tq, tk, scale):
    t = pl.program_id(0)
    head, off, mult = par_ref[t, 0], par_ref[t, 1], par_ref[t, 2]
    lo, hi, sgn = par_ref[t, 3], par_ref[t, 4], par_ref[t, 5]
    dist = off + sgn * (lax.broadcasted_iota(jnp.int32, (tq, tk), 0)
                        - lax.broadcasted_iota(jnp.int32, (tq, tk), 1))
    bucket = _rel_bucket(jnp.maximum(dist * mult, 0))
    val = jnp.zeros((tq, tk), F32)
    for b in range(REL_BUCKETS):
        val = jnp.where(bucket == b, tab_ref[b, head] * scale, val)
    o_ref[...] = jnp.where((dist >= lo) & (dist <= hi), val, NEG)


def _bias_tiles(params, table, tq, tk, scale):
    n = params.shape[0]
    return pl.pallas_call(
        functools.partial(_bias_tiles_kernel, tq=tq, tk=tk, scale=scale),
        grid=(n,),
        in_specs=[pl.BlockSpec(memory_space=pltpu.SMEM), pl.BlockSpec(memory_space=pltpu.SMEM)],
        out_specs=pl.BlockSpec((None, tq, tk), lambda t: (t, 0, 0)),
        out_shape=jax.ShapeDtypeStruct((n, tq, tk), F32),
        compiler_params=_cparams(("arbitrary",)),
        name="bias_tiles",
    )(params, table)


def _inproj_kernel(x_ref, g_ref, w_ref, om_ref, ob_ref, oi_ref, avt_ref, cvt_ref, iwt_ref):
    h = _rms(x_ref[...], g_ref[...]).astype(BF16)
    step = 512
    for c in range(0, N_MAIN, step):
        e = min(c + step, N_MAIN)
        om_ref[:, c:e] = _dot(h, w_ref[:, c:e]).astype(BF16)
    ob_ref[...] = _dot(h, w_ref[:, N_MAIN:N_MAIN + N_B])
    oi_ref[...] = _dot(h, w_ref[:, N_MAIN + N_B:])
    tk = avt_ref.shape[-1]
    a0 = 2 * A_HEADS * LANES
    for hd in range(A_HEADS):
        vt = om_ref[:, a0 + hd * A_V:a0 + (hd + 1) * A_V].astype(F32).T
        for j in range(avt_ref.shape[1]):
            avt_ref[hd, j] = vt[:, j * tk:(j + 1) * tk].astype(BF16)
    ca = cvt_ref.shape[-1]
    cw = C_HEADS * C_DIM
    for half in range(cw // LANES):
        c0 = N_MAIN - cw + half * LANES
        vt = om_ref[:, c0:c0 + LANES].astype(F32).T
        for j in range(cvt_ref.shape[0]):
            cvt_ref[j, half * LANES:(half + 1) * LANES, :] = vt[:, j * ca:(j + 1) * ca].astype(BF16)
    iwt_ref[...] = oi_ref[:, 2 * IDX_DIM:2 * IDX_DIM + LANES].T[:IDX_HEADS]


def _inproj(x, g, w, tm, tk_a, ca_c):
    B, S, D = x.shape
    T = B * S
    nt = S // tm
    n_all = N_MAIN + N_B + N_I
    cw = C_HEADS * C_DIM
    return pl.pallas_call(
        _inproj_kernel,
        grid=(T // tm,),
        in_specs=[pl.BlockSpec((tm, D), lambda i: (i, 0)),
                  pl.BlockSpec((1, D), lambda i: (0, 0)),
                  pl.BlockSpec((D, n_all), lambda i: (0, 0))],
        out_specs=[pl.BlockSpec((tm, N_MAIN), lambda i: (i, 0)),
                   pl.BlockSpec((tm, N_B), lambda i: (i, 0)),
                   pl.BlockSpec((tm, N_I), lambda i: (i, 0)),
                   pl.BlockSpec((None, A_HEADS, tm // tk_a, A_V, tk_a),
                                lambda i: (i // nt, 0, i % nt, 0, 0)),
                   pl.BlockSpec((None, tm // ca_c, cw, ca_c), lambda i: (i // nt, i % nt, 0, 0)),
                   pl.BlockSpec((None, IDX_HEADS, tm), lambda i: (i // nt, 0, i % nt))],
        out_shape=[jax.ShapeDtypeStruct((T, N_MAIN), BF16),
                   jax.ShapeDtypeStruct((T, N_B), F32),
                   jax.ShapeDtypeStruct((T, N_I), F32),
                   jax.ShapeDtypeStruct((B, A_HEADS, S // tk_a, A_V, tk_a), BF16),
                   jax.ShapeDtypeStruct((B, S // ca_c, cw, ca_c), BF16),
                   jax.ShapeDtypeStruct((B, IDX_HEADS, S), F32)],
        compiler_params=_cparams(("parallel",)),
        name="inproj",
    )(x.reshape(T, D), g, w)


def _diff_attn_kernel(lam_ref, q_ref, k_ref, vt_ref, bias_ref, g_ref, o_ref, s_sc, acc_sc,
                      *, tq, tk, d_min, d_const, lam_init):
    qi = pl.program_id(2)
    gran = min(tq, tk)
    last = (qi * tq + tq - 1) // tk
    lv = lam_ref[...]
    lam = (jnp.exp(jnp.sum(lv[0:1] * lv[1:2], axis=-1, keepdims=True))
           - jnp.exp(jnp.sum(lv[2:3] * lv[3:4], axis=-1, keepdims=True)) + lam_init)
    lane = lax.broadcasted_iota(jnp.int32, (tq, LANES), 1)
    q = q_ref[...].astype(F32) * (A_QK ** -0.5 * LOG2E)
    qcat = jnp.concatenate([jnp.where(lane < A_QK, q, 0.0), jnp.where(lane >= A_QK, q, 0.0)],
                           axis=0).astype(BF16)
    acc_sc[...] = jnp.zeros(acc_sc.shape, F32)

    def scores_into(slot, j):
        k = k_ref[pl.ds(pl.multiple_of(j * tk, tk), tk), :]
        d = qi * (tq // gran) - j * (tk // gran)
        bias = bias_ref[jnp.minimum(d, d_const) - d_min]
        s_sc[slot] = _dot_nt(k, qcat) + jnp.concatenate([bias, bias], axis=1)

    def absorb(slot, j, m_prev):
        m_new = jnp.maximum(m_prev, jnp.max(s_sc[slot], axis=0, keepdims=True))
        alpha = jnp.exp2(m_prev - m_new)
        p = jnp.exp2(s_sc[slot] - m_new)
        vt = jnp.concatenate([vt_ref[j], _ones_rows(tk)], axis=0)
        acc_sc[...] = alpha * acc_sc[...] + _dot(vt, p.astype(BF16))
        return m_new

    scores_into(0, 0)

    def pair(a, m):
        scores_into(1, a + 1)
        m = absorb(0, a, m)
        scores_into(0, a + 2)
        return absorb(1, a + 1, m)

    pairs = last // 2
    m = lax.fori_loop(0, pairs // 2, lambda jj, m: pair(4 * jj + 2, pair(4 * jj, m)),
                      jnp.full((1, 2 * tq), NEG, F32))
    m = lax.fori_loop(pairs // 2 * 2, pairs, lambda jj, m: pair(2 * jj, m), m)

    half = tq // 2
    late = [slice(mp * tq + half, (mp + 1) * tq) for mp in range(2)]
    pick = lambda x: jnp.concatenate([x[:, c] for c in late], axis=1)
    k_last = k_ref[pl.ds(pl.multiple_of(last * tk, tk), tk), :]
    bias_last = bias_ref[-d_min][:, :half]
    q_late = jnp.concatenate([qcat[c] for c in late], axis=0)
    s_sc[1, :, :tq] = _dot_nt(k_last, q_late) + jnp.concatenate([bias_last, bias_last], axis=1)
    m = absorb(0, last - 1, m)
    m_late = pick(m)
    m_new = jnp.maximum(m_late, jnp.max(s_sc[1, :, :tq], axis=0, keepdims=True))
    alpha = jnp.exp2(m_late - m_new)
    p = jnp.exp2(s_sc[1, :, :tq] - m_new)
    vt = jnp.concatenate([vt_ref[last], _ones_rows(tk)], axis=0)
    upd = alpha * pick(acc_sc[...]) + _dot(vt, p.astype(BF16))
    for i, c in enumerate(late):
        acc_sc[:, c] = upd[:, i * half:(i + 1) * half]
    inv = 1.0 / acc_sc[A_V:A_V + 1, :]
    o = acc_sc[:A_V, :tq] * inv[:, :tq] - lam * (acc_sc[:A_V, tq:] * inv[:, tq:])
    o = o * lax.rsqrt(jnp.mean(o * o, axis=0, keepdims=True) + EPS) * g_ref[...] * (1.0 - lam_init)
    o_ref[...] = o.T.astype(o_ref.dtype)


def _diff_attn(proj, vt, lamv, bias_a, subln_g, lam_init, tq, d_min, d_const):
    B, S, _ = proj.shape
    nd, tk = bias_a.shape[1], bias_a.shape[2]
    assert tq == 2 * tk, (tq, tk)
    return pl.pallas_call(
        functools.partial(_diff_attn_kernel, tq=tq, tk=tk, d_min=d_min, d_const=d_const,
                          lam_init=lam_init),
        grid=(A_HEADS, B, S // tq),
        in_specs=[pl.BlockSpec((4, A_QK), lambda h, b, i: (0, 0)),
                  pl.BlockSpec((None, tq, LANES), lambda h, b, i: (b, i, h)),
                  pl.BlockSpec((None, S, LANES), lambda h, b, i: (b, 0, A_HEADS + h)),
                  pl.BlockSpec((None, None, S // tk, A_V, tk), lambda h, b, i: (b, h, 0, 0, 0)),
                  pl.BlockSpec((None, nd, tk, tq), lambda h, b, i: (h, 0, 0, 0)),
                  pl.BlockSpec((A_V, 1), lambda h, b, i: (0, 0))],
        out_specs=pl.BlockSpec((None, tq, LANES), lambda h, b, i: (b, i, h)),
        out_shape=jax.ShapeDtypeStruct((B, S, A_HEADS * A_V), BF16),
        scratch_shapes=[pltpu.VMEM((2, tk, 2 * tq), F32),
                        pltpu.VMEM((A_V + ONES_ROWS, 2 * tq), F32)],
        compiler_params=_cparams(("parallel", "parallel", "arbitrary")),
        name="diff_attn",
    )(lamv, proj, proj, vt, bias_a, subln_g)


class _LaneHalves:
    def __init__(self, *refs):
        self.refs = refs

    def __getitem__(self, idx):
        return jnp.concatenate([r[idx] for r in self.refs], axis=1)

    def __setitem__(self, idx, val):
        for i, r in enumerate(self.refs):
            r[idx] = val[:, i * LANES:(i + 1) * LANES]


def _dil_kernel(q0, q1, k0, k1, v0, v1, bias_ref, o_ref, n0, n1, m0, m1, l0, l1, *, seq, patterns):
    q_ref, k_ref, v_ref = _LaneHalves(q0, q1), _LaneHalves(k0, k1), _LaneHalves(v0, v1)
    n_sc, m_sc, l_sc = _LaneHalves(n0, n1), _LaneHalves(m0, m1), _LaneHalves(l0, l1)
    blk = DIL_BLK
    width = B_HEADS * B_DIM
    head_of_lane = lax.broadcasted_iota(jnp.int32, (blk, width), 1) // B_DIM

    def per_head(x):
        parts = [jnp.broadcast_to(x[h * blk:(h + 1) * blk], (blk, width)) for h in range(B_HEADS)]
        out = parts[-1]
        for h in range(B_HEADS - 2, -1, -1):
            out = jnp.where(head_of_lane == h, parts[h], out)
        return out

    def attend(base, r, with_prev, p_idx):
        q = q_ref[pl.ds(base, blk, stride=r), :] * (B_DIM ** -0.5 * LOG2E)
        qcat = jnp.concatenate([jnp.where(head_of_lane == h, q, 0.0) for h in range(B_HEADS)],
                               axis=0).astype(BF16)
        if with_prev:
            keys = pl.ds(base - blk * r, 2 * blk, stride=r)
            bias = bias_ref[p_idx]
        else:
            keys = pl.ds(base, blk, stride=r)
            bias = bias_ref[p_idx, :, blk:]
        s = _dot_nt(qcat, k_ref[keys, :].astype(BF16)) + bias
        mx = jnp.max(s, axis=-1, keepdims=True)
        p = jnp.exp2(s - mx)
        l = jnp.sum(p, axis=-1, keepdims=True)
        o = _dot(p.astype(BF16), v_ref[keys, :].astype(BF16))
        return per_head(o), per_head(mx), per_head(l)

    def merge(base, r, res, is_first):
        o, m, l = res
        rows = pl.ds(base, blk, stride=r)
        if is_first:
            n_sc[rows, :] = o
            m_sc[rows, :] = m
            l_sc[rows, :] = l
        else:
            m_old = m_sc[rows, :]
            m_new = jnp.maximum(m_old, m)
            a = jnp.exp2(m_old - m_new)
            b = jnp.exp2(m - m_new)
            n_sc[rows, :] = a * n_sc[rows, :] + b * o
            l_sc[rows, :] = a * l_sc[rows, :] + b * l
            m_sc[rows, :] = m_new

    def sweep(count, base_of, r, with_prev, p_idx):
        def trip(width):
            def body(i, carry):
                bases = [base_of(i * width + u) for u in range(width)]
                results = [attend(b, r, with_prev, p_idx) for b in bases]
                for b, res in zip(bases, results):
                    merge(b, r, res, p_idx == len(patterns) - 1)
                return carry
            return body

        full = count // DIL_UNROLL
        lax.fori_loop(0, full, trip(DIL_UNROLL), 0)
        rest = count - full * DIL_UNROLL
        if rest:
            lax.fori_loop(full * DIL_UNROLL, count, trip(1), 0)

    for p_idx, (_, r) in reversed(list(enumerate(patterns))):
        nb = seq // (r * blk)
        sweep(r, lambda c: c, r, False, p_idx)
        sweep(r * (nb - 1), lambda idx, r=r: (idx // r + 1) * blk * r + idx % r, r, True, p_idx)

    everything = (slice(None), slice(None))
    o_ref[...] = (n_sc[everything] / l_sc[everything]).astype(o_ref.dtype)


def _dil_attn(proj_b, bias_b):
    B, S, _ = proj_b.shape
    width = B_HEADS * B_DIM
    return pl.pallas_call(
        functools.partial(_dil_kernel, seq=S, patterns=DILATED_PATTERNS),
        grid=(B,),
        in_specs=[pl.BlockSpec((None, S, LANES), functools.partial(lambda i, b: (b, 0, i), i))
                  for i in range(3 * width // LANES)]
                 + [pl.BlockSpec(bias_b.shape, lambda b: (0, 0, 0))],
        out_specs=pl.BlockSpec((None, S, width), lambda b: (b, 0, 0)),
        out_shape=jax.ShapeDtypeStruct((B, S, width), BF16),
        scratch_shapes=[pltpu.VMEM((S, LANES), F32)] * (3 * width // LANES),
        compiler_params=_cparams(("parallel",)),
        name="dilated_attn",
    )(*([proj_b] * (3 * width // LANES)), bias_b)


SEL16 = jnp.bfloat16


RADIX_LOW_BITS = 5
PEEL_CAP = 6
FLT_MIN_NORMAL = 2.0 ** -126


def _float_key(f):
    b = pltpu.bitcast(f, jnp.int32)
    return jnp.where(b >= 0, b, b ^ jnp.int32(0x7FFFFFFF))


def _key_float(k):
    return pltpu.bitcast(jnp.where(k >= 0, k, k ^ jnp.int32(0x7FFFFFFF)), F32)


def _high_half(f):
    return pltpu.bitcast(pltpu.bitcast(f, jnp.int32) & jnp.int32(-65536), F32)


def _dsa_kernel(iq_ref, ik_ref, iwt_ref, cq_ref, ck_ref, cvt_ref, bias_ref, o_ref, sc_ref, hi_ref, s_sc,
                acc_sc,
                *, tq, ch, ca, topk):
    qi = pl.program_id(1)
    q0 = qi * tq
    nch = (q0 + tq - 1) // ch + 1
    qidx = q0 + lax.broadcasted_iota(jnp.int32, (1, tq), 1)
    lane = lax.broadcasted_iota(jnp.int32, (tq, LANES), 1)
    first = lane < IDX_DIM

    w = iwt_ref[...] * (IDX_HEADS ** -0.5)
    qs = []
    for h in range(IDX_HEADS):
        tile = iq_ref[:, (h // 2) * LANES:(h // 2 + 1) * LANES].astype(F32) * (IDX_DIM ** -0.5)
        qs.append(jnp.where(first if h % 2 == 0 else ~first, tile, 0.0).astype(BF16))

    half_k, half_q = ch // 2, tq // 2
    every, late = slice(0, tq), slice(half_q, tq)
    diag = nch - 1

    def scores_of(start, nkeys, cols):
        kk = ik_ref[pl.ds(start, nkeys), :].astype(BF16)
        sc = jnp.zeros((nkeys, cols.stop - cols.start), F32)
        for h in range(IDX_HEADS):
            sc = sc + jnp.maximum(_dot_nt(kk, qs[h][cols]), 0.0) * w[h:h + 1, cols]
        kidx = start + lax.broadcasted_iota(jnp.int32, (nkeys, 1), 0)
        return jnp.where(kidx <= qidx[:, cols], sc, NEG)

    def score_body(j, carry):
        sc = scores_of(pl.multiple_of(j * ch, ch), ch, every)
        sc_ref[j] = sc
        hi_ref[j] = _high_half(sc).astype(SEL16)
        return carry

    lax.fori_loop(0, diag, score_body, 0)
    dstart = pl.multiple_of(diag * ch, ch)
    sc = scores_of(dstart, half_k, every)
    sc_ref[diag, :half_k, :] = sc
    hi_ref[diag, :half_k, :] = _high_half(sc).astype(SEL16)
    sc = scores_of(dstart + half_k, half_k, late)
    sc_ref[diag, half_k:, half_q:] = sc
    hi_ref[diag, half_k:, half_q:] = _high_half(sc).astype(SEL16)
    sc_ref[diag, half_k:, :half_q] = jnp.full((half_k, half_q), NEG, F32)
    hi_ref[diag, half_k:, :half_q] = jnp.full((half_k, half_q), NEG, SEL16)

    rows = 32

    def fold_chunks(ref, step, init):
        acc = lax.fori_loop(0, diag, lambda j, a: step(ref[j], a, every), init)
        acc = step(ref[diag, :half_k, :], acc, every)
        tail = step(ref[diag, half_k:, half_q:], acc[:, half_q:], late)
        return jnp.concatenate([acc[:, :half_q], tail], axis=1)

    def count_ge(t):
        def step(x, acc, cols):
            hit = jnp.where(x >= t[:, cols], 1.0, 0.0)
            return acc + jnp.sum(hit.reshape(x.shape[0] // rows, rows, x.shape[1]), axis=0)
        acc = fold_chunks(sc_ref, step, jnp.zeros((rows, tq), F32))
        return jnp.sum(acc, axis=0, keepdims=True)

    def count_high_ge(v):
        one, zero16 = jnp.ones((), SEL16), jnp.zeros((), SEL16)
        def step(x, acc, cols):
            hit = jnp.where(x >= v[:, cols], one, zero16).reshape(x.shape[0] // rows, rows, x.shape[1])
            parts = [hit[i] for i in range(x.shape[0] // rows)]
            while len(parts) > 1:
                parts = [parts[i] + parts[i + 1] for i in range(0, len(parts), 2)]
            return acc + parts[0]
        acc = fold_chunks(hi_ref, step, jnp.zeros((rows, tq), SEL16))
        return jnp.sum(acc.astype(F32), axis=0, keepdims=True)

    kf = float(topk)
    n_valid = (qidx + 1).astype(F32)
    half_bits = 16

    def high_body(i, st):
        u, c_lo = st
        cand = u | jnp.left_shift(1, half_bits - 1 - i)
        v = _high_half(_key_float((cand - (1 << (half_bits - 1))) << half_bits)).astype(SEL16)
        c = count_high_ge(v)
        keep = c >= kf
        return jnp.where(keep, cand, u), jnp.where(keep, c, c_lo)

    u, c_lo = lax.fori_loop(0, half_bits, high_body, (jnp.zeros((1, tq), jnp.int32), n_valid))
    key_hi = (u - (1 << (half_bits - 1))) << half_bits

    def low_body(i, st):
        lo_bits, c_lo = st
        cand = lo_bits | jnp.left_shift(1, half_bits - 1 - i)
        c = count_ge(_key_float(key_hi | cand))
        keep = c >= kf
        return jnp.where(keep, cand, lo_bits), jnp.where(keep, c, c_lo)

    lo_bits, c_lo = lax.fori_loop(0, RADIX_LOW_BITS, low_body, (jnp.zeros((1, tq), jnp.int32), c_lo))

    def min_ge(t):
        def step(x, acc, cols):
            kept = jnp.where(x >= t[:, cols], x, -NEG)
            return jnp.minimum(acc, jnp.min(kept.reshape(x.shape[0] // rows, rows, x.shape[1]), axis=0))
        acc = fold_chunks(sc_ref, step, jnp.full((rows, tq), -NEG, F32))
        return jnp.min(acc, axis=0, keepdims=True)

    def open_rows(c_lo, done):
        return jnp.logical_and(jnp.logical_and(n_valid > kf, c_lo > kf), done == 0.0)

    def any_row(mask):
        return jnp.max(jnp.where(mask, 1.0, 0.0)) > 0.0

    def peel_cond(st):
        _, c_lo, done, it = st
        return jnp.logical_and(it < PEEL_CAP, any_row(open_rows(c_lo, done)))

    def peel_body(st):
        t, c_lo, done, it = st
        is_open = open_rows(c_lo, done)
        smallest = min_ge(t)
        above = jnp.where(smallest == 0.0, FLT_MIN_NORMAL, _key_float(_float_key(smallest) + 1))
        c = count_ge(above)
        enough = jnp.logical_and(is_open, c >= kf)
        at_tie = jnp.logical_and(is_open, c < kf)
        t = jnp.where(enough, above, jnp.where(at_tie, smallest, t))
        return t, jnp.where(enough, c, c_lo), jnp.where(at_tie, 1.0, done), it + 1

    t_peel, c_peel, done, _ = lax.while_loop(
        peel_cond, peel_body,
        (_key_float(key_hi | lo_bits), c_lo, jnp.zeros((1, tq), F32), jnp.int32(0)))

    def finish_by_radix(_):
        bits, c = lax.fori_loop(RADIX_LOW_BITS, half_bits, low_body, (lo_bits, c_lo))
        return _key_float(key_hi | bits), c

    thr, c_lo = lax.cond(any_row(open_rows(c_peel, done)), finish_by_radix,
                         lambda _: (t_peel, c_peel), 0)
    thr = jnp.where(n_valid > kf, thr, 0.5 * NEG)

    tie_rows = jnp.logical_and(n_valid > kf, c_lo > kf)

    @pl.when(jnp.max(jnp.where(tie_rows, 1.0, 0.0)) > 0.0)
    def _():
        surplus = jnp.where(tie_rows, c_lo - kf, 0.0)
        upper = (lax.broadcasted_iota(jnp.int32, (ch, ch), 1)
                 >= lax.broadcasted_iota(jnp.int32, (ch, ch), 0)).astype(BF16)

        def drop_body(st):
            i, seen = st
            j = nch - 1 - i
            sc = sc_ref[j]
            eq = sc == thr
            eqf = jnp.where(eq, 1.0, 0.0)
            rank = _dot(upper, eqf.astype(BF16))
            sc_ref[j] = jnp.where(eq, jnp.where(rank <= surplus - seen, NEG, sc), sc)
            return i + 1, seen + jnp.sum(eqf, axis=0, keepdims=True)

        def more_to_drop(st):
            i, seen = st
            return jnp.logical_and(i < nch, any_row(seen < surplus))

        lax.while_loop(more_to_drop, drop_body, (jnp.int32(0), jnp.zeros((1, tq), F32)))

    cw = C_HEADS * C_DIM
    lane_c = lax.broadcasted_iota(jnp.int32, (tq, cw), 1)
    cq = cq_ref[...].astype(F32) * (C_DIM ** -0.5 * LOG2E)
    qcat = jnp.concatenate([jnp.where(lane_c // C_DIM == h, cq, 0.0) for h in range(C_HEADS)],
                           axis=0).astype(BF16)
    acc_sc[...] = jnp.zeros(acc_sc.shape, F32)
    nd = bias_ref.shape[1]
    last = (q0 + tq - 1) // ca

    def chunk_scores(j, c0):
        kc = ck_ref[pl.ds(pl.multiple_of(j * ca, ca), ca), :]
        sub = pl.multiple_of((j % (ch // ca)) * ca, ca)
        sel = sc_ref[j // (ch // ca), pl.ds(sub, ca), c0:] >= thr[:, c0:]
        bias = jnp.concatenate([
            jnp.concatenate([
                bias_ref[h, jnp.clip(q0 // LANES + c - (j * (ca // LANES) + u), 0, nd - 1)]
                for h in range(C_HEADS) for c in range(c0 // LANES, tq // LANES)], axis=1)
            for u in range(ca // LANES)], axis=0)
        q_rows = qcat if c0 == 0 else jnp.concatenate(
            [qcat[h * tq + c0:(h + 1) * tq] for h in range(C_HEADS)], axis=0)
        return jnp.where(jnp.concatenate([sel] * C_HEADS, axis=1), _dot_nt(kc, q_rows) + bias, NEG)

    def scores_into(slot, j):
        s_sc[slot] = chunk_scores(j, 0)

    def absorb(read, j, m_prev, c0):
        width = tq - c0
        m_new = jnp.maximum(m_prev, jnp.max(read(), axis=0, keepdims=True))
        alpha = jnp.exp2(m_prev - m_new)
        pb = jnp.exp2(read() - m_new).astype(BF16)
        for h in range(C_HEADS):
            cols = slice(h * width, (h + 1) * width)
            vt = jnp.concatenate([cvt_ref[j, h * C_DIM:(h + 1) * C_DIM, :], _ones_rows(ca)], axis=0)
            acc_sc[h, :, c0:] = alpha[:, cols] * acc_sc[h, :, c0:] + _dot(vt, pb[:, cols])
        return m_new

    scores_into(0, 0)

    def attn_body(jj, m):
        a = 2 * jj
        scores_into(1, a + 1)
        m = absorb(lambda: s_sc[0], a, m, 0)
        scores_into(0, a + 2)
        return absorb(lambda: s_sc[1], a + 1, m, 0)

    m = lax.fori_loop(0, last // 2, attn_body, jnp.full((1, C_HEADS * tq), NEG, F32))
    half = tq // 2
    s_sc[1, :, :C_HEADS * half] = chunk_scores(last, half)
    m = absorb(lambda: s_sc[0], last - 1, m, 0)
    m_late = jnp.concatenate([m[:, h * tq + half:(h + 1) * tq] for h in range(C_HEADS)], axis=1)
    absorb(lambda: s_sc[1, :, :C_HEADS * half], last, m_late, half)
    o_t = jnp.concatenate([acc_sc[h, :C_DIM] * (1.0 / acc_sc[h, C_DIM:C_DIM + 1])
                           for h in range(C_HEADS)], axis=0)
    o_ref[...] = o_t.T.astype(o_ref.dtype)


def _dsa_attn(proj, proj_i, iwt, cvt, bias_c, tq, ch):
    B, S, _ = proj.shape
    ca = cvt.shape[-1]
    topk = min(TOPK_MAX, S // 4)
    nd = bias_c.shape[1]
    cw = C_HEADS * C_DIM
    assert tq == ch == 2 * ca, (tq, ch, ca)
    return pl.pallas_call(
        functools.partial(_dsa_kernel, tq=tq, ch=ch, ca=ca, topk=topk),
        grid=(B, S // tq),
        in_specs=[pl.BlockSpec((None, tq, IDX_HEADS * IDX_DIM), lambda b, i: (b, i, 3)),
                  pl.BlockSpec((None, S, LANES), lambda b, i: (b, 0, 0)),
                  pl.BlockSpec((None, IDX_HEADS, tq), lambda b, i: (b, 0, i)),
                  pl.BlockSpec((None, tq, cw), lambda b, i: (b, i, 8)),
                  pl.BlockSpec((None, S, cw), lambda b, i: (b, 0, 9)),
                  pl.BlockSpec((None, S // ca, cw, ca), lambda b, i: (b, 0, 0, 0)),
                  pl.BlockSpec((C_HEADS, nd, LANES, LANES), lambda b, i: (0, 0, 0, 0))],
        out_specs=pl.BlockSpec((None, tq, cw), lambda b, i: (b, i, 0)),
        out_shape=jax.ShapeDtypeStruct((B, S, cw), BF16),
        scratch_shapes=[pltpu.VMEM((S // ch, ch, tq), F32),
                        pltpu.VMEM((S // ch, ch, tq), SEL16),
                        pltpu.VMEM((2, ca, C_HEADS * tq), F32),
                        pltpu.VMEM((C_HEADS, C_DIM + ONES_ROWS, tq), F32)],
        compiler_params=_cparams(("parallel", "arbitrary")),
        name="dsa_attn",
    )(proj, proj_i, iwt, proj, proj, cvt, bias_c)


def _memkv_kernel(mem_ref, g_ref, w_ref, o_ref):
    h = _rms(mem_ref[...], g_ref[...]).astype(BF16)
    o_ref[...] = _dot(h, w_ref[...]).astype(o_ref.dtype)


def _memkv(mem, g, w):
    B, M, D = mem.shape
    L = w.shape[0]
    return pl.pallas_call(
        _memkv_kernel,
        grid=(L, B),
        in_specs=[pl.BlockSpec((None, M, D), lambda l, b: (b, 0, 0)),
                  pl.BlockSpec((None, 1, D), lambda l, b: (l, 0, 0)),
                  pl.BlockSpec((None, D, 2 * D), lambda l, b: (l, 0, 0))],
        out_specs=pl.BlockSpec((None, None, M, 2 * D), lambda l, b: (l, b, 0, 0)),
        out_shape=jax.ShapeDtypeStruct((L, B, M, 2 * D), BF16),
        compiler_params=_cparams(("parallel", "arbitrary")),
        name="mem_kv",
    )(mem, g, w)


def _outmem_kernel(x_ref, oa_ref, ob_ref, oc_ref, wo_ref, g_ref, wq_ref, kv_ref, wmo_ref, o_ref):
    D = x_ref.shape[-1]
    na, nb = oa_ref.shape[-1], ob_ref.shape[-1]
    x = (x_ref[...] + _dot(oa_ref[...], wo_ref[0:na]) + _dot(ob_ref[...], wo_ref[na:na + nb])
         + _dot(oc_ref[...], wo_ref[na + nb:]))
    h = _rms(x, g_ref[...]).astype(BF16)
    hd = D // MEM_HEADS
    q = (_dot(h, wq_ref[...]) * (hd ** -0.5)).astype(BF16)
    outs = []
    for hh in range(MEM_HEADS):
        k = kv_ref[:, hh * hd:(hh + 1) * hd]
        v = kv_ref[:, D + hh * hd:D + (hh + 1) * hd]
        s = _dot_nt(q[:, hh * hd:(hh + 1) * hd], k)
        p = jnp.exp(s - jnp.max(s, axis=-1, keepdims=True))
        p = p / jnp.sum(p, axis=-1, keepdims=True)
        outs.append(_dot(p.astype(BF16), v).astype(BF16))
    o = jnp.concatenate(outs, axis=-1)
    o_ref[...] = x + _dot(o, wmo_ref[...])


def _outmem(x, oa, ob, oc, wo, g, wq, kv, layer, wmo, tm):
    B, S, D = x.shape
    M = kv.shape[2]
    const = lambda b, i: (0, 0)
    return pl.pallas_call(
        _outmem_kernel,
        grid=(B, S // tm),
        in_specs=[pl.BlockSpec((None, tm, D), lambda b, i: (b, i, 0)),
                  pl.BlockSpec((None, tm, oa.shape[-1]), lambda b, i: (b, i, 0)),
                  pl.BlockSpec((None, tm, ob.shape[-1]), lambda b, i: (b, i, 0)),
                  pl.BlockSpec((None, tm, oc.shape[-1]), lambda b, i: (b, i, 0)),
                  pl.BlockSpec(wo.shape, const),
                  pl.BlockSpec((1, D), const),
                  pl.BlockSpec(wq.shape, const),
                  pl.BlockSpec((None, None, M, 2 * D), lambda b, i: (layer, b, 0, 0)),
                  pl.BlockSpec(wmo.shape, const)],
        out_specs=pl.BlockSpec((None, tm, D), lambda b, i: (b, i, 0)),
        out_shape=jax.ShapeDtypeStruct((B, S, D), F32),
        compiler_params=_cparams(("parallel", "arbitrary")),
        name="outproj_memattn",
    )(x, oa, ob, oc, wo, g, wq, kv, wmo)


HALO = 8


def _ffn_kernel(x_ref, xp_ref, g_ref, wg_ref, wv_ref, cwg_ref, cwv_ref, cbg_ref, cbv_ref, wd_ref,
                gout_ref, o_ref, *, tm, fc, norm_out):
    i = pl.program_id(1)
    x = x_ref[...]
    g = g_ref[...]
    hp = _rms(xp_ref[...], g) * jnp.where(i > 0, 1.0, 0.0)
    h = jnp.concatenate([hp, _rms(x, g)], axis=0).astype(BF16)
    F = wd_ref.shape[0]

    def conv(u, cw_ref, cb_ref, c, e):
        out = cb_ref[:, c:e]
        for j in range(CONV_WIDTH):
            shift = CONV_WIDTH - 1 - j
            out = out + cw_ref[j:j + 1, c:e] * u[HALO - shift:HALO - shift + tm]
        return out

    acc = x
    for c in range(0, F, fc):
        e = min(c + fc, F)
        gate = conv(_dot(h, wg_ref[:, c:e]), cwg_ref, cbg_ref, c, e)
        val = conv(_dot(h, wv_ref[:, c:e]), cwv_ref, cbv_ref, c, e)
        act = (gate * jax.nn.sigmoid(gate) * val).astype(BF16)
        acc = acc + _dot(act, wd_ref[c:e, :])
    o_ref[...] = _rms(acc, gout_ref[...]) if norm_out else acc


def _ffn(x, g, wg, wv, cwg, cwv, cbg, cbv, wd, gout, norm_out, tm, fc):
    B, S, D = x.shape
    F = wd.shape[0]
    const = lambda b, i: (0, 0)
    hb = tm // HALO
    return pl.pallas_call(
        functools.partial(_ffn_kernel, tm=tm, fc=fc, norm_out=norm_out),
        grid=(B, S // tm),
        in_specs=[pl.BlockSpec((None, tm, D), lambda b, i: (b, i, 0)),
                  pl.BlockSpec((None, HALO, D), lambda b, i: (b, jnp.maximum(i * hb - 1, 0), 0)),
                  pl.BlockSpec((1, D), const),
                  pl.BlockSpec((D, F), const, pipeline_mode=pl.Buffered(1)),
                  pl.BlockSpec((D, F), lambda b, i: (0, 1), pipeline_mode=pl.Buffered(1)),
                  pl.BlockSpec((CONV_WIDTH, F), const), pl.BlockSpec((CONV_WIDTH, F), const),
                  pl.BlockSpec((1, F), const), pl.BlockSpec((1, F), const),
                  pl.BlockSpec((F, D), const, pipeline_mode=pl.Buffered(1)),
                  pl.BlockSpec((1, D), const)],
        out_specs=pl.BlockSpec((None, tm, D), lambda b, i: (b, i, 0)),
        out_shape=jax.ShapeDtypeStruct((B, S, D), F32),
        compiler_params=_cparams(("parallel", "arbitrary")),
        name="conv_ffn",
    )(x, x, g, wg, wv, cwg, cwv, cbg, cbv, wd, gout)


def _tile_params(S, tq_a, tk_a):
    big = 1 << 30
    gran, (d_min, d_const) = min(tq_a, tk_a), _a_offsets(tq_a, tk_a)
    pa = []
    for h in range(A_HEADS):
        pa += [(h, d * gran, 1, 0, big, -1) for d in range(d_min, d_const + 1)]
    pb = []
    for _, r in DILATED_PATTERNS:
        for h in range(B_HEADS):
            pb.append((A_HEADS + h, DIL_BLK, r, 1, DIL_BLK, 1))
            pb.append((A_HEADS + h, 0, r, 0, DIL_BLK, 1))
    pc = [(A_HEADS + B_HEADS + h, d * LANES, 1, -big, big, -1)
          for h in range(C_HEADS) for d in range(_n_offsets(S, LANES))]
    to = lambda p: jnp.asarray(p, jnp.int32)
    return to(pa), to(pb), to(pc)


def _dil_bias_layout(tiles):
    t = tiles.reshape(len(DILATED_PATTERNS), B_HEADS, 2, DIL_BLK, DIL_BLK)
    return jnp.concatenate([t[:, :, 0], t[:, :, 1]], axis=-1).reshape(
        len(DILATED_PATTERNS), B_HEADS * DIL_BLK, 2 * DIL_BLK)


def _n_offsets(S, blk):
    return min(S // blk, REL_MAX_DIST // blk + 2)


def _a_offsets(tq, tk):
    gran = min(tq, tk)
    return -(tq // gran - 1), -(-(REL_MAX_DIST - 1 + tk) // gran)


def _in_weights(w_in_l):
    sizes = (512, 512, 512, 256, 256, 256, 256, 256, 256, 512, 64, 8)
    offs = [0]
    for s in sizes:
        offs.append(offs[-1] + s)
    (aq, ak, av, bq, bk, bv, cq, ck, cv, iq, ik, iw) = [w_in_l[:, offs[i]:offs[i + 1]] for i in range(12)]
    pad = jnp.zeros((w_in_l.shape[0], N_I - 2 * IDX_DIM - IDX_HEADS), w_in_l.dtype)
    return jnp.concatenate([aq, ak, av, iq, cq, ck, cv, bq, bk, bv, ik, ik, iw, pad], axis=1).astype(BF16)


def _forward(x, mem, rel_bias, norm_mix, w_in, lam_q1, lam_k1, lam_q2, lam_k2, subln, w_out,
             norm_mem, norm_memkv, w_mq, w_mkv, w_mo, norm_ffn, w_up, conv_w, conv_b, w_down,
             norm_final, *, tq_a, tk_a, tq_c, ch_c, ca_c, tm_proj, tm_mem, tm_ffn, fc):
    B, S, D = x.shape
    L = w_in.shape[0]
    F = w_down.shape[1]
    pa, pb, pc = _tile_params(S, tq_a, tk_a)
    bias_a = _bias_tiles(pa, rel_bias, tk_a, tq_a, LOG2E).reshape(A_HEADS, -1, tk_a, tq_a)
    d_min, d_const = _a_offsets(tq_a, tk_a)
    bias_b = _dil_bias_layout(_bias_tiles(pb, rel_bias, DIL_BLK, DIL_BLK, LOG2E))
    bias_c = _bias_tiles(pc, rel_bias, LANES, LANES, LOG2E).reshape(C_HEADS, -1, LANES, LANES)
    kv_all = _memkv(mem, norm_memkv.reshape(L, 1, D), w_mkv.astype(BF16))

    for l in range(L):
        lam_init = 0.8 - 0.6 * math.exp(-0.3 * l)
        proj, proj_b, proj_i, avt, cvt, iwt = _inproj(
            x, norm_mix[l].reshape(1, D), _in_weights(w_in[l]), tm_proj, tk_a, ca_c)
        proj = proj.reshape(B, S, N_MAIN)
        proj_i = proj_i.reshape(B, S, N_I)
        lamv = jnp.stack([lam_q1[l], lam_k1[l], lam_q2[l], lam_k2[l]], axis=0)
        o_a = _diff_attn(proj, avt, lamv, bias_a, subln[l].reshape(A_V, 1), lam_init, tq_a,
                         d_min, d_const)
        o_b = _dil_attn(proj_b.reshape(B, S, N_B), bias_b)
        o_c = _dsa_attn(proj, proj_i, iwt, cvt, bias_c, tq_c, ch_c)
        x = _outmem(x, o_a, o_b, o_c, w_out[l].astype(BF16), norm_mem[l].reshape(1, D),
                    w_mq[l].astype(BF16), kv_all, l, w_mo[l].astype(BF16), tm_mem)
        wu = w_up[l].astype(BF16)
        x = _ffn(x, norm_ffn[l].reshape(1, D), wu, wu, conv_w[l][:, :F], conv_w[l][:, F:],
                 conv_b[l][:F].reshape(1, F), conv_b[l][F:].reshape(1, F), w_down[l].astype(BF16),
                 norm_final.reshape(1, D), l == L - 1, tm_ffn, fc)
    return x


def kernel(x, mem, rel_bias, norm_mix, w_in, lam_q1, lam_k1, lam_q2, lam_k2, subln, w_out,
           norm_mem, norm_memkv, w_mq, w_mkv, w_mo, norm_ffn, w_up, conv_w, conv_b, w_down,
           norm_final):
    return _forward(x, mem, rel_bias, norm_mix, w_in, lam_q1, lam_k1, lam_q2, lam_k2, subln, w_out,
                    norm_mem, norm_memkv, w_mq, w_mkv, w_mo, norm_ffn, w_up, conv_w, conv_b, w_down,
                    norm_final, tq_a=512, tk_a=256, tq_c=512, ch_c=512, ca_c=256, tm_proj=1024, tm_mem=1024, tm_ffn=1024,
                    fc=512)
```

```python
import functools
import math

import jax
import jax.numpy as jnp
from jax import lax
from jax.experimental import pallas as pl
from jax.experimental.pallas import tpu as pltpu

F32 = jnp.float32
BF16 = jnp.bfloat16

EPS = 1e-6
NEG = -1e30
LOG2E = math.log2(math.e)
LANES = 128
VMEM_LIMIT = 56 * 1024 * 1024

A_HEADS, A_QK, A_V = 4, 64, 128
B_HEADS, B_DIM = 4, 64
C_HEADS, C_DIM = 4, 64
IDX_HEADS, IDX_DIM = 8, 64
TOPK_MAX = 256
DILATED_PATTERNS = ((128, 1), (512, 4), (2048, 16))
ONES_ROWS = 16
DIL_BLK = 128
DIL_UNROLL = 4
REL_BUCKETS, REL_MAX_DIST = 32, 2048
MEM_HEADS = 4
CONV_WIDTH = 3

N_MAIN = 2816
N_B = 768
N_I = 256


def _cparams(sem):
    return pltpu.CompilerParams(dimension_semantics=sem, vmem_limit_bytes=VMEM_LIMIT)


def _dot(a, b):
    return jnp.dot(a, b, preferred_element_type=F32)


def _dot_nt(a, b):
    return lax.dot_general(a, b, (((1,), (1,)), ((), ())), preferred_element_type=F32)


def _ones_rows(n):
    row = lax.broadcasted_iota(jnp.int32, (ONES_ROWS, n), 0)
    return jnp.where(row == 0, 1.0, 0.0).astype(BF16)


def _rms(x, g):
    return x * lax.rsqrt(jnp.mean(x * x, axis=-1, keepdims=True) + EPS) * g


def _rel_bucket(n):
    max_exact = REL_BUCKETS // 2
    nf = jnp.maximum(n, 1).astype(F32)
    large = max_exact + (jnp.log(nf / max_exact) / math.log(REL_MAX_DIST / max_exact)
                         * (REL_BUCKETS - max_exact)).astype(jnp.int32)
    large = jnp.minimum(large, REL_BUCKETS - 1)
    return jnp.where(n < max_exact, n, large)


def _bias_tiles_kernel(par_ref, tab_ref, o_ref, *, tq, tk, scale):
    t = pl.program_id(0)
    head, off, mult = par_ref[t, 0], par_ref[t, 1], par_ref[t, 2]
    lo, hi, sgn = par_ref[t, 3], par_ref[t, 4], par_ref[t, 5]
    dist = off + sgn * (lax.broadcasted_iota(jnp.int32, (tq, tk), 0)
                        - lax.broadcasted_iota(jnp.int32, (tq, tk), 1))
    bucket = _rel_bucket(jnp.maximum(dist * mult, 0))
    val = jnp.zeros((tq, tk), F32)
    for b in range(REL_BUCKETS):
        val = jnp.where(bucket == b, tab_ref[b, head] * scale, val)
    o_ref[...] = jnp.where((dist >= lo) & (dist <= hi), val, NEG)


def _bias_tiles(params, table, tq, tk, scale):
    n = params.shape[0]
    return pl.pallas_call(
        functools.partial(_bias_tiles_kernel, tq=tq, tk=tk, scale=scale),
        grid=(n,),
        in_specs=[pl.BlockSpec(memory_space=pltpu.SMEM), pl.BlockSpec(memory_space=pltpu.SMEM)],
        out_specs=pl.BlockSpec((None, tq, tk), lambda t: (t, 0, 0)),
        out_shape=jax.ShapeDtypeStruct((n, tq, tk), F32),
        compiler_params=_cparams(("arbitrary",)),
        name="bias_tiles",
    )(params, table)


def _inproj_kernel(x_ref, g_ref, w_ref, om_ref, ob_ref, oi_ref, avt_ref, cvt_ref, iwt_ref):
    h = _rms(x_ref[...], g_ref[...]).astype(BF16)
    step = 512
    for c in range(0, N_MAIN, step):
        e = min(c + step, N_MAIN)
        om_ref[:, c:e] = _dot(h, w_ref[:, c:e]).astype(BF16)
    ob_ref[...] = _dot(h, w_ref[:, N_MAIN:N_MAIN + N_B])
    oi_ref[...] = _dot(h, w_ref[:, N_MAIN + N_B:])
    tk = avt_ref.shape[-1]
    a0 = 2 * A_HEADS * LANES
    for hd in range(A_HEADS):
        vt = om_ref[:, a0 + hd * A_V:a0 + (hd + 1) * A_V].astype(F32).T
        for j in range(avt_ref.shape[1]):
            avt_ref[hd, j] = vt[:, j * tk:(j + 1) * tk].astype(BF16)
    ca = cvt_ref.shape[-1]
    cw = C_HEADS * C_DIM
    for half in range(cw // LANES):
        c0 = N_MAIN - cw + half * LANES
        vt = om_ref[:, c0:c0 + LANES].astype(F32).T
        for j in range(cvt_ref.shape[0]):
            cvt_ref[j, half * LANES:(half + 1) * LANES, :] = vt[:, j * ca:(j + 1) * ca].astype(BF16)
    iwt_ref[...] = oi_ref[:, 2 * IDX_DIM:2 * IDX_DIM + LANES].T[:IDX_HEADS]


def _inproj(x, g, w, tm, tk_a, ca_c):
    B, S, D = x.shape
    T = B * S
    nt = S // tm
    n_all = N_MAIN + N_B + N_I
    cw = C_HEADS * C_DIM
    return pl.pallas_call(
        _inproj_kernel,
        grid=(T // tm,),
        in_specs=[pl.BlockSpec((tm, D), lambda i: (i, 0)),
                  pl.BlockSpec((1, D), lambda i: (0, 0)),
                  pl.BlockSpec((D, n_all), lambda i: (0, 0))],
        out_specs=[pl.BlockSpec((tm, N_MAIN), lambda i: (i, 0)),
                   pl.BlockSpec((tm, N_B), lambda i: (i, 0)),
                   pl.BlockSpec((tm, N_I), lambda i: (i, 0)),
                   pl.BlockSpec((None, A_HEADS, tm // tk_a, A_V, tk_a),
                                lambda i: (i // nt, 0, i % nt, 0, 0)),
                   pl.BlockSpec((None, tm // ca_c, cw, ca_c), lambda i: (i // nt, i % nt, 0, 0)),
                   pl.BlockSpec((None, IDX_HEADS, tm), lambda i: (i // nt, 0, i % nt))],
        out_shape=[jax.ShapeDtypeStruct((T, N_MAIN), BF16),
                   jax.ShapeDtypeStruct((T, N_B), F32),
                   jax.ShapeDtypeStruct((T, N_I), F32),
                   jax.ShapeDtypeStruct((B, A_HEADS, S // tk_a, A_V, tk_a), BF16),
                   jax.ShapeDtypeStruct((B, S // ca_c, cw, ca_c), BF16),
                   jax.ShapeDtypeStruct((B, IDX_HEADS, S), F32)],
        compiler_params=_cparams(("parallel",)),
        name="inproj",
    )(x.reshape(T, D), g, w)


def _diff_attn_kernel(lam_ref, q_ref, k_ref, vt_ref, bias_ref, g_ref, o_ref, s_sc, acc_sc,
                      *, tq, tk, d_min, d_const, lam_init):
    qi = pl.program_id(2)
    gran = min(tq, tk)
    last = (qi * tq + tq - 1) // tk
    lv = lam_ref[...]
    lam = (jnp.exp(jnp.sum(lv[0:1] * lv[1:2], axis=-1, keepdims=True))
           - jnp.exp(jnp.sum(lv[2:3] * lv[3:4], axis=-1, keepdims=True)) + lam_init)
    lane = lax.broadcasted_iota(jnp.int32, (tq, LANES), 1)
    q = q_ref[...].astype(F32) * (A_QK ** -0.5 * LOG2E)
    qcat = jnp.concatenate([jnp.where(lane < A_QK, q, 0.0), jnp.where(lane >= A_QK, q, 0.0)],
                           axis=0).astype(BF16)
    acc_sc[...] = jnp.zeros(acc_sc.shape, F32)

    def scores_into(slot, j):
        k = k_ref[pl.ds(pl.multiple_of(j * tk, tk), tk), :]
        d = qi * (tq // gran) - j * (tk // gran)
        bias = bias_ref[jnp.minimum(d, d_const) - d_min]
        s_sc[slot] = _dot_nt(k, qcat) + jnp.concatenate([bias, bias], axis=1)

    def absorb(slot, j, m_prev):
        m_new = jnp.maximum(m_prev, jnp.max(s_sc[slot], axis=0, keepdims=True))
        alpha = jnp.exp2(m_prev - m_new)
        p = jnp.exp2(s_sc[slot] - m_new)
        vt = jnp.concatenate([vt_ref[j], _ones_rows(tk)], axis=0)
        acc_sc[...] = alpha * acc_sc[...] + _dot(vt, p.astype(BF16))
        return m_new

    scores_into(0, 0)

    def pair(a, m):
        scores_into(1, a + 1)
        m = absorb(0, a, m)
        scores_into(0, a + 2)
        return absorb(1, a + 1, m)

    pairs = last // 2
    m = lax.fori_loop(0, pairs // 2, lambda jj, m: pair(4 * jj + 2, pair(4 * jj, m)),
                      jnp.full((1, 2 * tq), NEG, F32))
    m = lax.fori_loop(pairs // 2 * 2, pairs, lambda jj, m: pair(2 * jj, m), m)

    half = tq // 2
    late = [slice(mp * tq + half, (mp + 1) * tq) for mp in range(2)]
    pick = lambda x: jnp.concatenate([x[:, c] for c in late], axis=1)
    k_last = k_ref[pl.ds(pl.multiple_of(last * tk, tk), tk), :]
    bias_last = bias_ref[-d_min][:, :half]
    q_late = jnp.concatenate([qcat[c] for c in late], axis=0)
    s_sc[1, :, :tq] = _dot_nt(k_last, q_late) + jnp.concatenate([bias_last, bias_last], axis=1)
    m = absorb(0, last - 1, m)
    m_late = pick(m)
    m_new = jnp.maximum(m_late, jnp.max(s_sc[1, :, :tq], axis=0, keepdims=True))
    alpha = jnp.exp2(m_late - m_new)
    p = jnp.exp2(s_sc[1, :, :tq] - m_new)
    vt = jnp.concatenate([vt_ref[last], _ones_rows(tk)], axis=0)
    upd = alpha * pick(acc_sc[...]) + _dot(vt, p.astype(BF16))
    for i, c in enumerate(late):
        acc_sc[:, c] = upd[:, i * half:(i + 1) * half]
    inv = 1.0 / acc_sc[A_V:A_V + 1, :]
    o = acc_sc[:A_V, :tq] * inv[:, :tq] - lam * (acc_sc[:A_V, tq:] * inv[:, tq:])
    o = o * lax.rsqrt(jnp.mean(o * o, axis=0, keepdims=True) + EPS) * g_ref[...] * (1.0 - lam_init)
    o_ref[...] = o.T.astype(o_ref.dtype)


def _diff_attn(proj, vt, lamv, bias_a, subln_g, lam_init, tq, d_min, d_const):
    B, S, _ = proj.shape
    nd, tk = bias_a.shape[1], bias_a.shape[2]
    assert tq == 2 * tk, (tq, tk)
    return pl.pallas_call(
        functools.partial(_diff_attn_kernel, tq=tq, tk=tk, d_min=d_min, d_const=d_const,
                          lam_init=lam_init),
        grid=(A_HEADS, B, S // tq),
        in_specs=[pl.BlockSpec((4, A_QK), lambda h, b, i: (0, 0)),
                  pl.BlockSpec((None, tq, LANES), lambda h, b, i: (b, i, h)),
                  pl.BlockSpec((None, S, LANES), lambda h, b, i: (b, 0, A_HEADS + h)),
                  pl.BlockSpec((None, None, S // tk, A_V, tk), lambda h, b, i: (b, h, 0, 0, 0)),
                  pl.BlockSpec((None, nd, tk, tq), lambda h, b, i: (h, 0, 0, 0)),
                  pl.BlockSpec((A_V, 1), lambda h, b, i: (0, 0))],
        out_specs=pl.BlockSpec((None, tq, LANES), lambda h, b, i: (b, i, h)),
        out_shape=jax.ShapeDtypeStruct((B, S, A_HEADS * A_V), BF16),
        scratch_shapes=[pltpu.VMEM((2, tk, 2 * tq), F32),
                        pltpu.VMEM((A_V + ONES_ROWS, 2 * tq), F32)],
        compiler_params=_cparams(("parallel", "parallel", "arbitrary")),
        name="diff_attn",
    )(lamv, proj, proj, vt, bias_a, subln_g)


class _LaneHalves:
    def __init__(self, *refs):
        self.refs = refs

    def __getitem__(self, idx):
        return jnp.concatenate([r[idx] for r in self.refs], axis=1)

    def __setitem__(self, idx, val):
        for i, r in enumerate(self.refs):
            r[idx] = val[:, i * LANES:(i + 1) * LANES]


def _dil_kernel(q0, q1, k0, k1, v0, v1, bias_ref, o_ref, n0, n1, m0, m1, l0, l1, *, seq, patterns):
    q_ref, k_ref, v_ref = _LaneHalves(q0, q1), _LaneHalves(k0, k1), _LaneHalves(v0, v1)
    n_sc, m_sc, l_sc = _LaneHalves(n0, n1), _LaneHalves(m0, m1), _LaneHalves(l0, l1)
    blk = DIL_BLK
    width = B_HEADS * B_DIM
    head_of_lane = lax.broadcasted_iota(jnp.int32, (blk, width), 1) // B_DIM

    def per_head(x):
        parts = [jnp.broadcast_to(x[h * blk:(h + 1) * blk], (blk, width)) for h in range(B_HEADS)]
        out = parts[-1]
        for h in range(B_HEADS - 2, -1, -1):
            out = jnp.where(head_of_lane == h, parts[h], out)
        return out

    def attend(base, r, with_prev, p_idx):
        q = q_ref[pl.ds(base, blk, stride=r), :] * (B_DIM ** -0.5 * LOG2E)
        qcat = jnp.concatenate([jnp.where(head_of_lane == h, q, 0.0) for h in range(B_HEADS)],
                               axis=0).astype(BF16)
        if with_prev:
            keys = pl.ds(base - blk * r, 2 * blk, stride=r)
            bias = bias_ref[p_idx]
        else:
            keys = pl.ds(base, blk, stride=r)
            bias = bias_ref[p_idx, :, blk:]
        s = _dot_nt(qcat, k_ref[keys, :].astype(BF16)) + bias
        mx = jnp.max(s, axis=-1, keepdims=True)
        p = jnp.exp2(s - mx)
        l = jnp.sum(p, axis=-1, keepdims=True)
        o = _dot(p.astype(BF16), v_ref[keys, :].astype(BF16))
        return per_head(o), per_head(mx), per_head(l)

    def merge(base, r, res, is_first):
        o, m, l = res
        rows = pl.ds(base, blk, stride=r)
        if is_first:
            n_sc[rows, :] = o
            m_sc[rows, :] = m
            l_sc[rows, :] = l
        else:
            m_old = m_sc[rows, :]
            m_new = jnp.maximum(m_old, m)
            a = jnp.exp2(m_old - m_new)
            b = jnp.exp2(m - m_new)
            n_sc[rows, :] = a * n_sc[rows, :] + b * o
            l_sc[rows, :] = a * l_sc[rows, :] + b * l
            m_sc[rows, :] = m_new

    def sweep(count, base_of, r, with_prev, p_idx):
        def trip(width):
            def body(i, carry):
                bases = [base_of(i * width + u) for u in range(width)]
                results = [attend(b, r, with_prev, p_idx) for b in bases]
                for b, res in zip(bases, results):
                    merge(b, r, res, p_idx == len(patterns) - 1)
                return carry
            return body

        full = count // DIL_UNROLL
        lax.fori_loop(0, full, trip(DIL_UNROLL), 0)
        rest = count - full * DIL_UNROLL
        if rest:
            lax.fori_loop(full * DIL_UNROLL, count, trip(1), 0)

    for p_idx, (_, r) in reversed(list(enumerate(patterns))):
        nb = seq // (r * blk)
        sweep(r, lambda c: c, r, False, p_idx)
        sweep(r * (nb - 1), lambda idx, r=r: (idx // r + 1) * blk * r + idx % r, r, True, p_idx)

    everything = (slice(None), slice(None))
    o_ref[...] = (n_sc[everything] / l_sc[everything]).astype(o_ref.dtype)


def _dil_attn(proj_b, bias_b):
    B, S, _ = proj_b.shape
    width = B_HEADS * B_DIM
    return pl.pallas_call(
        functools.partial(_dil_kernel, seq=S, patterns=DILATED_PATTERNS),
        grid=(B,),
        in_specs=[pl.BlockSpec((None, S, LANES), functools.partial(lambda i, b: (b, 0, i), i))
                  for i in range(3 * width // LANES)]
                 + [pl.BlockSpec(bias_b.shape, lambda b: (0, 0, 0))],
        out_specs=pl.BlockSpec((None, S, width), lambda b: (b, 0, 0)),
        out_shape=jax.ShapeDtypeStruct((B, S, width), BF16),
        scratch_shapes=[pltpu.VMEM((S, LANES), F32)] * (3 * width // LANES),
        compiler_params=_cparams(("parallel",)),
        name="dilated_attn",
    )(*([proj_b] * (3 * width // LANES)), bias_b)


SEL16 = jnp.bfloat16


RADIX_LOW_BITS = 3
PEEL_CAP = 6
FLT_MIN_NORMAL = 2.0 ** -126


def _float_key(f):
    b = pltpu.bitcast(f, jnp.int32)
    return jnp.where(b >= 0, b, b ^ jnp.int32(0x7FFFFFFF))


def _key_float(k):
    return pltpu.bitcast(jnp.where(k >= 0, k, k ^ jnp.int32(0x7FFFFFFF)), F32)


def _high_half(f):
    return pltpu.bitcast(pltpu.bitcast(f, jnp.int32) & jnp.int32(-65536), F32)


def _dsa_kernel(iq_ref, ik_ref, iwt_ref, cq_ref, ck_ref, cvt_ref, bias_ref, o_ref, sc_ref, hi_ref, s_sc,
                acc_sc,
                *, tq, ch, ca, topk):
    qi = pl.program_id(1)
    q0 = qi * tq
    nch = (q0 + tq - 1) // ch + 1
    qidx = q0 + lax.broadcasted_iota(jnp.int32, (1, tq), 1)
    lane = lax.broadcasted_iota(jnp.int32, (tq, LANES), 1)
    first = lane < IDX_DIM

    w = iwt_ref[...] * (IDX_HEADS ** -0.5)
    qs = []
    for h in range(IDX_HEADS):
        tile = iq_ref[:, (h // 2) * LANES:(h // 2 + 1) * LANES].astype(F32) * (IDX_DIM ** -0.5)
        qs.append(jnp.where(first if h % 2 == 0 else ~first, tile, 0.0).astype(BF16))

    half_k, half_q = ch // 2, tq // 2
    every, late = slice(0, tq), slice(half_q, tq)
    diag = nch - 1

    def scores_of(start, nkeys, cols):
        kk = ik_ref[pl.ds(start, nkeys), :].astype(BF16)
        sc = jnp.zeros((nkeys, cols.stop - cols.start), F32)
        for h in range(IDX_HEADS):
            sc = sc + jnp.maximum(_dot_nt(kk, qs[h][cols]), 0.0) * w[h:h + 1, cols]
        kidx = start + lax.broadcasted_iota(jnp.int32, (nkeys, 1), 0)
        return jnp.where(kidx <= qidx[:, cols], sc, NEG)

    def score_body(j, carry):
        sc = scores_of(pl.multiple_of(j * ch, ch), ch, every)
        sc_ref[j] = sc
        hi_ref[j] = _high_half(sc).astype(SEL16)
        return carry

    lax.fori_loop(0, diag, score_body, 0)
    dstart = pl.multiple_of(diag * ch, ch)
    sc = scores_of(dstart, half_k, every)
    sc_ref[diag, :half_k, :] = sc
    hi_ref[diag, :half_k, :] = _high_half(sc).astype(SEL16)
    sc = scores_of(dstart + half_k, half_k, late)
    sc_ref[diag, half_k:, half_q:] = sc
    hi_ref[diag, half_k:, half_q:] = _high_half(sc).astype(SEL16)
    sc_ref[diag, half_k:, :half_q] = jnp.full((half_k, half_q), NEG, F32)
    hi_ref[diag, half_k:, :half_q] = jnp.full((half_k, half_q), NEG, SEL16)

    rows = 32

    def fold_chunks(ref, step, init):
        acc = lax.fori_loop(0, diag, lambda j, a: step(ref[j], a, every), init)
        acc = step(ref[diag, :half_k, :], acc, every)
        tail = step(ref[diag, half_k:, half_q:], acc[:, half_q:], late)
        return jnp.concatenate([acc[:, :half_q], tail], axis=1)

    def count_ge(t):
        def step(x, acc, cols):
            hit = jnp.where(x >= t[:, cols], 1.0, 0.0)
            return acc + jnp.sum(hit.reshape(x.shape[0] // rows, rows, x.shape[1]), axis=0)
        acc = fold_chunks(sc_ref, step, jnp.zeros((rows, tq), F32))
        return jnp.sum(acc, axis=0, keepdims=True)

    def count_high_ge(v):
        one, zero16 = jnp.ones((), SEL16), jnp.zeros((), SEL16)
        def step(x, acc, cols):
            hit = jnp.where(x >= v[:, cols], one, zero16).reshape(x.shape[0] // rows, rows, x.shape[1])
            parts = [hit[i] for i in range(x.shape[0] // rows)]
            while len(parts) > 1:
                parts = [parts[i] + parts[i + 1] for i in range(0, len(parts), 2)]
            return acc + parts[0]
        acc = fold_chunks(hi_ref, step, jnp.zeros((rows, tq), SEL16))
        return jnp.sum(acc.astype(F32), axis=0, keepdims=True)

    kf = float(topk)
    n_valid = (qidx + 1).astype(F32)
    half_bits = 16

    def high_body(i, st):
        u, c_lo = st
        cand = u | jnp.left_shift(1, half_bits - 1 - i)
        v = _high_half(_key_float((cand - (1 << (half_bits - 1))) << half_bits)).astype(SEL16)
        c = count_high_ge(v)
        keep = c >= kf
        return jnp.where(keep, cand, u), jnp.where(keep, c, c_lo)

    u, c_lo = lax.fori_loop(0, half_bits, high_body, (jnp.zeros((1, tq), jnp.int32), n_valid))
    key_hi = (u - (1 << (half_bits - 1))) << half_bits

    def low_body(i, st):
        lo_bits, c_lo = st
        cand = lo_bits | jnp.left_shift(1, half_bits - 1 - i)
        c = count_ge(_key_float(key_hi | cand))
        keep = c >= kf
        return jnp.where(keep, cand, lo_bits), jnp.where(keep, c, c_lo)

    lo_bits, c_lo = lax.fori_loop(0, RADIX_LOW_BITS, low_body, (jnp.zeros((1, tq), jnp.int32), c_lo))

    def min_ge(t):
        def step(x, acc, cols):
            kept = jnp.where(x >= t[:, cols], x, -NEG)
            return jnp.minimum(acc, jnp.min(kept.reshape(x.shape[0] // rows, rows, x.shape[1]), axis=0))
        acc = fold_chunks(sc_ref, step, jnp.full((rows, tq), -NEG, F32))
        return jnp.min(acc, axis=0, keepdims=True)

    def open_rows(c_lo, done):
        return jnp.logical_and(jnp.logical_and(n_valid > kf, c_lo > kf), done == 0.0)

    def any_row(mask):
        return jnp.max(jnp.where(mask, 1.0, 0.0)) > 0.0

    def peel_cond(st):
        _, c_lo, done, it = st
        return jnp.logical_and(it < PEEL_CAP, any_row(open_rows(c_lo, done)))

    def peel_body(st):
        t, c_lo, done, it = st
        is_open = open_rows(c_lo, done)
        smallest = min_ge(t)
        above = jnp.where(smallest == 0.0, FLT_MIN_NORMAL, _key_float(_float_key(smallest) + 1))
        c = count_ge(above)
        enough = jnp.logical_and(is_open, c >= kf)
        at_tie = jnp.logical_and(is_open, c < kf)
        t = jnp.where(enough, above, jnp.where(at_tie, smallest, t))
        return t, jnp.where(enough, c, c_lo), jnp.where(at_tie, 1.0, done), it + 1

    t_peel, c_peel, done, _ = lax.while_loop(
        peel_cond, peel_body,
        (_key_float(key_hi | lo_bits), c_lo, jnp.zeros((1, tq), F32), jnp.int32(0)))

    def finish_by_radix(_):
        bits, c = lax.fori_loop(RADIX_LOW_BITS, half_bits, low_body, (lo_bits, c_lo))
        return _key_float(key_hi | bits), c

    thr, c_lo = lax.cond(any_row(open_rows(c_peel, done)), finish_by_radix,
                         lambda _: (t_peel, c_peel), 0)
    thr = jnp.where(n_valid > kf, thr, 0.5 * NEG)

    tie_rows = jnp.logical_and(n_valid > kf, c_lo > kf)

    @pl.when(jnp.max(jnp.where(tie_rows, 1.0, 0.0)) > 0.0)
    def _():
        surplus = jnp.where(tie_rows, c_lo - kf, 0.0)
        upper = (lax.broadcasted_iota(jnp.int32, (ch, ch), 1)
                 >= lax.broadcasted_iota(jnp.int32, (ch, ch), 0)).astype(BF16)

        def drop_body(st):
            i, seen = st
            j = nch - 1 - i
            sc = sc_ref[j]
            eq = sc == thr
            eqf = jnp.where(eq, 1.0, 0.0)
            rank = _dot(upper, eqf.astype(BF16))
            sc_ref[j] = jnp.where(eq, jnp.where(rank <= surplus - seen, NEG, sc), sc)
            return i + 1, seen + jnp.sum(eqf, axis=0, keepdims=True)

        def more_to_drop(st):
            i, seen = st
            return jnp.logical_and(i < nch, any_row(seen < surplus))

        lax.while_loop(more_to_drop, drop_body, (jnp.int32(0), jnp.zeros((1, tq), F32)))

    cw = C_HEADS * C_DIM
    lane_c = lax.broadcasted_iota(jnp.int32, (tq, cw), 1)
    cq = cq_ref[...].astype(F32) * (C_DIM ** -0.5 * LOG2E)
    qcat = jnp.concatenate([jnp.where(lane_c // C_DIM == h, cq, 0.0) for h in range(C_HEADS)],
                           axis=0).astype(BF16)
    acc_sc[...] = jnp.zeros(acc_sc.shape, F32)
    nd = bias_ref.shape[1]
    last = (q0 + tq - 1) // ca

    def chunk_scores(j, c0):
        kc = ck_ref[pl.ds(pl.multiple_of(j * ca, ca), ca), :]
        sub = pl.multiple_of((j % (ch // ca)) * ca, ca)
        sel = sc_ref[j // (ch // ca), pl.ds(sub, ca), c0:] >= thr[:, c0:]
        bias = jnp.concatenate([
            jnp.concatenate([
                bias_ref[h, jnp.clip(q0 // LANES + c - (j * (ca // LANES) + u), 0, nd - 1)]
                for h in range(C_HEADS) for c in range(c0 // LANES, tq // LANES)], axis=1)
            for u in range(ca // LANES)], axis=0)
        q_rows = qcat if c0 == 0 else jnp.concatenate(
            [qcat[h * tq + c0:(h + 1) * tq] for h in range(C_HEADS)], axis=0)
        return jnp.where(jnp.concatenate([sel] * C_HEADS, axis=1), _dot_nt(kc, q_rows) + bias, NEG)

    def scores_into(slot, j):
        s_sc[slot] = chunk_scores(j, 0)

    def absorb(read, j, m_prev, c0):
        width = tq - c0
        m_new = jnp.maximum(m_prev, jnp.max(read(), axis=0, keepdims=True))
        alpha = jnp.exp2(m_prev - m_new)
        pb = jnp.exp2(read() - m_new).astype(BF16)
        for h in range(C_HEADS):
            cols = slice(h * width, (h + 1) * width)
            vt = jnp.concatenate([cvt_ref[j, h * C_DIM:(h + 1) * C_DIM, :], _ones_rows(ca)], axis=0)
            acc_sc[h, :, c0:] = alpha[:, cols] * acc_sc[h, :, c0:] + _dot(vt, pb[:, cols])
        return m_new

    scores_into(0, 0)

    def attn_body(jj, m):
        a = 2 * jj
        scores_into(1, a + 1)
        m = absorb(lambda: s_sc[0], a, m, 0)
        scores_into(0, a + 2)
        return absorb(lambda: s_sc[1], a + 1, m, 0)

    m = lax.fori_loop(0, last // 2, attn_body, jnp.full((1, C_HEADS * tq), NEG, F32))
    half = tq // 2
    s_sc[1, :, :C_HEADS * half] = chunk_scores(last, half)
    m = absorb(lambda: s_sc[0], last - 1, m, 0)
    m_late = jnp.concatenate([m[:, h * tq + half:(h + 1) * tq] for h in range(C_HEADS)], axis=1)
    absorb(lambda: s_sc[1, :, :C_HEADS * half], last, m_late, half)
    o_t = jnp.concatenate([acc_sc[h, :C_DIM] * (1.0 / acc_sc[h, C_DIM:C_DIM + 1])
                           for h in range(C_HEADS)], axis=0)
    o_ref[...] = o_t.T.astype(o_ref.dtype)


def _dsa_attn(proj, proj_i, iwt, cvt, bias_c, tq, ch):
    B, S, _ = proj.shape
    ca = cvt.shape[-1]
    topk = min(TOPK_MAX, S // 4)
    nd = bias_c.shape[1]
    cw = C_HEADS * C_DIM
    assert tq == ch == 2 * ca, (tq, ch, ca)
    return pl.pallas_call(
        functools.partial(_dsa_kernel, tq=tq, ch=ch, ca=ca, topk=topk),
        grid=(B, S // tq),
        in_specs=[pl.BlockSpec((None, tq, IDX_HEADS * IDX_DIM), lambda b, i: (b, i, 3)),
                  pl.BlockSpec((None, S, LANES), lambda b, i: (b, 0, 0)),
                  pl.BlockSpec((None, IDX_HEADS, tq), lambda b, i: (b, 0, i)),
                  pl.BlockSpec((None, tq, cw), lambda b, i: (b, i, 8)),
                  pl.BlockSpec((None, S, cw), lambda b, i: (b, 0, 9)),
                  pl.BlockSpec((None, S // ca, cw, ca), lambda b, i: (b, 0, 0, 0)),
                  pl.BlockSpec((C_HEADS, nd, LANES, LANES), lambda b, i: (0, 0, 0, 0))],
        out_specs=pl.BlockSpec((None, tq, cw), lambda b, i: (b, i, 0)),
        out_shape=jax.ShapeDtypeStruct((B, S, cw), BF16),
        scratch_shapes=[pltpu.VMEM((S // ch, ch, tq), F32),
                        pltpu.VMEM((S // ch, ch, tq), SEL16),
                        pltpu.VMEM((2, ca, C_HEADS * tq), F32),
                        pltpu.VMEM((C_HEADS, C_DIM + ONES_ROWS, tq), F32)],
        compiler_params=_cparams(("parallel", "arbitrary")),
        name="dsa_attn",
    )(proj, proj_i, iwt, proj, proj, cvt, bias_c)


def _memkv_kernel(mem_ref, g_ref, w_ref, o_ref):
    h = _rms(mem_ref[...], g_ref[...]).astype(BF16)
    o_ref[...] = _dot(h, w_ref[...]).astype(o_ref.dtype)


def _memkv(mem, g, w):
    B, M, D = mem.shape
    L = w.shape[0]
    return pl.pallas_call(
        _memkv_kernel,
        grid=(L, B),
        in_specs=[pl.BlockSpec((None, M, D), lambda l, b: (b, 0, 0)),
                  pl.BlockSpec((None, 1, D), lambda l, b: (l, 0, 0)),
                  pl.BlockSpec((None, D, 2 * D), lambda l, b: (l, 0, 0))],
        out_specs=pl.BlockSpec((None, None, M, 2 * D), lambda l, b: (l, b, 0, 0)),
        out_shape=jax.ShapeDtypeStruct((L, B, M, 2 * D), BF16),
        compiler_params=_cparams(("parallel", "arbitrary")),
        name="mem_kv",
    )(mem, g, w)


def _outmem_kernel(x_ref, oa_ref, ob_ref, oc_ref, wo_ref, g_ref, wq_ref, kv_ref, wmo_ref, o_ref):
    D = x_ref.shape[-1]
    na, nb = oa_ref.shape[-1], ob_ref.shape[-1]
    x = (x_ref[...] + _dot(oa_ref[...], wo_ref[0:na]) + _dot(ob_ref[...], wo_ref[na:na + nb])
         + _dot(oc_ref[...], wo_ref[na + nb:]))
    h = _rms(x, g_ref[...]).astype(BF16)
    hd = D // MEM_HEADS
    q = (_dot(h, wq_ref[...]) * (hd ** -0.5)).astype(BF16)
    outs = []
    for hh in range(MEM_HEADS):
        k = kv_ref[:, hh * hd:(hh + 1) * hd]
        v = kv_ref[:, D + hh * hd:D + (hh + 1) * hd]
        s = _dot_nt(q[:, hh * hd:(hh + 1) * hd], k)
        p = jnp.exp(s - jnp.max(s, axis=-1, keepdims=True))
        p = p / jnp.sum(p, axis=-1, keepdims=True)
        outs.append(_dot(p.astype(BF16), v).astype(BF16))
    o = jnp.concatenate(outs, axis=-1)
    o_ref[...] = x + _dot(o, wmo_ref[...])


def _outmem(x, oa, ob, oc, wo, g, wq, kv, layer, wmo, tm):
    B, S, D = x.shape
    M = kv.shape[2]
    const = lambda b, i: (0, 0)
    return pl.pallas_call(
        _outmem_kernel,
        grid=(B, S // tm),
        in_specs=[pl.BlockSpec((None, tm, D), lambda b, i: (b, i, 0)),
                  pl.BlockSpec((None, tm, oa.shape[-1]), lambda b, i: (b, i, 0)),
                  pl.BlockSpec((None, tm, ob.shape[-1]), lambda b, i: (b, i, 0)),
                  pl.BlockSpec((None, tm, oc.shape[-1]), lambda b, i: (b, i, 0)),
                  pl.BlockSpec(wo.shape, const),
                  pl.BlockSpec((1, D), const),
                  pl.BlockSpec(wq.shape, const),
                  pl.BlockSpec((None, None, M, 2 * D), lambda b, i: (layer, b, 0, 0)),
                  pl.BlockSpec(wmo.shape, const)],
        out_specs=pl.BlockSpec((None, tm, D), lambda b, i: (b, i, 0)),
        out_shape=jax.ShapeDtypeStruct((B, S, D), F32),
        compiler_params=_cparams(("parallel", "arbitrary")),
        name="outproj_memattn",
    )(x, oa, ob, oc, wo, g, wq, kv, wmo)


HALO = 8


def _ffn_kernel(x_ref, xp_ref, g_ref, wg_ref, wv_ref, cwg_ref, cwv_ref, cbg_ref, cbv_ref, wd_ref,
                gout_ref, o_ref, *, tm, fc, norm_out):
    i = pl.program_id(1)
    x = x_ref[...]
    g = g_ref[...]
    hp = _rms(xp_ref[...], g) * jnp.where(i > 0, 1.0, 0.0)
    h = jnp.concatenate([hp, _rms(x, g)], axis=0).astype(BF16)
    F = wd_ref.shape[0]

    def conv(u, cw_ref, cb_ref, c, e):
        out = cb_ref[:, c:e]
        for j in range(CONV_WIDTH):
            shift = CONV_WIDTH - 1 - j
            out = out + cw_ref[j:j + 1, c:e] * u[HALO - shift:HALO - shift + tm]
        return out

    acc = x
    for c in range(0, F, fc):
        e = min(c + fc, F)
        gate = conv(_dot(h, wg_ref[:, c:e]), cwg_ref, cbg_ref, c, e)
        val = conv(_dot(h, wv_ref[:, c:e]), cwv_ref, cbv_ref, c, e)
        act = (gate * jax.nn.sigmoid(gate) * val).astype(BF16)
        acc = acc + _dot(act, wd_ref[c:e, :])
    o_ref[...] = _rms(acc, gout_ref[...]) if norm_out else acc


def _ffn(x, g, wg, wv, cwg, cwv, cbg, cbv, wd, gout, norm_out, tm, fc):
    B, S, D = x.shape
    F = wd.shape[0]
    const = lambda b, i: (0, 0)
    hb = tm // HALO
    return pl.pallas_call(
        functools.partial(_ffn_kernel, tm=tm, fc=fc, norm_out=norm_out),
        grid=(B, S // tm),
        in_specs=[pl.BlockSpec((None, tm, D), lambda b, i: (b, i, 0)),
                  pl.BlockSpec((None, HALO, D), lambda b, i: (b, jnp.maximum(i * hb - 1, 0), 0)),
                  pl.BlockSpec((1, D), const),
                  pl.BlockSpec((D, F), const, pipeline_mode=pl.Buffered(1)),
                  pl.BlockSpec((D, F), lambda b, i: (0, 1), pipeline_mode=pl.Buffered(1)),
                  pl.BlockSpec((CONV_WIDTH, F), const), pl.BlockSpec((CONV_WIDTH, F), const),
                  pl.BlockSpec((1, F), const), pl.BlockSpec((1, F), const),
                  pl.BlockSpec((F, D), const, pipeline_mode=pl.Buffered(1)),
                  pl.BlockSpec((1, D), const)],
        out_specs=pl.BlockSpec((None, tm, D), lambda b, i: (b, i, 0)),
        out_shape=jax.ShapeDtypeStruct((B, S, D), F32),
        compiler_params=_cparams(("parallel", "arbitrary")),
        name="conv_ffn",
    )(x, x, g, wg, wv, cwg, cwv, cbg, cbv, wd, gout)


def _tile_params(S, tq_a, tk_a):
    big = 1 << 30
    gran, (d_min, d_const) = min(tq_a, tk_a), _a_offsets(tq_a, tk_a)
    pa = []
    for h in range(A_HEADS):
        pa += [(h, d * gran, 1, 0, big, -1) for d in range(d_min, d_const + 1)]
    pb = []
    for _, r in DILATED_PATTERNS:
        for h in range(B_HEADS):
            pb.append((A_HEADS + h, DIL_BLK, r, 1, DIL_BLK, 1))
            pb.append((A_HEADS + h, 0, r, 0, DIL_BLK, 1))
    pc = [(A_HEADS + B_HEADS + h, d * LANES, 1, -big, big, -1)
          for h in range(C_HEADS) for d in range(_n_offsets(S, LANES))]
    to = lambda p: jnp.asarray(p, jnp.int32)
    return to(pa), to(pb), to(pc)


def _dil_bias_layout(tiles):
    t = tiles.reshape(len(DILATED_PATTERNS), B_HEADS, 2, DIL_BLK, DIL_BLK)
    return jnp.concatenate([t[:, :, 0], t[:, :, 1]], axis=-1).reshape(
        len(DILATED_PATTERNS), B_HEADS * DIL_BLK, 2 * DIL_BLK)


def _n_offsets(S, blk):
    return min(S // blk, REL_MAX_DIST // blk + 2)


def _a_offsets(tq, tk):
    gran = min(tq, tk)
    return -(tq // gran - 1), -(-(REL_MAX_DIST - 1 + tk) // gran)


def _in_weights(w_in_l):
    sizes = (512, 512, 512, 256, 256, 256, 256, 256, 256, 512, 64, 8)
    offs = [0]
    for s in sizes:
        offs.append(offs[-1] + s)
    (aq, ak, av, bq, bk, bv, cq, ck, cv, iq, ik, iw) = [w_in_l[:, offs[i]:offs[i + 1]] for i in range(12)]
    pad = jnp.zeros((w_in_l.shape[0], N_I - 2 * IDX_DIM - IDX_HEADS), w_in_l.dtype)
    return jnp.concatenate([aq, ak, av, iq, cq, ck, cv, bq, bk, bv, ik, ik, iw, pad], axis=1).astype(BF16)


def _forward(x, mem, rel_bias, norm_mix, w_in, lam_q1, lam_k1, lam_q2, lam_k2, subln, w_out,
             norm_mem, norm_memkv, w_mq, w_mkv, w_mo, norm_ffn, w_up, conv_w, conv_b, w_down,
             norm_final, *, tq_a, tk_a, tq_c, ch_c, ca_c, tm_proj, tm_mem, tm_ffn, fc):
    B, S, D = x.shape
    L = w_in.shape[0]
    F = w_down.shape[1]
    pa, pb, pc = _tile_params(S, tq_a, tk_a)
    bias_a = _bias_tiles(pa, rel_bias, tk_a, tq_a, LOG2E).reshape(A_HEADS, -1, tk_a, tq_a)
    d_min, d_const = _a_offsets(tq_a, tk_a)
    bias_b = _dil_bias_layout(_bias_tiles(pb, rel_bias, DIL_BLK, DIL_BLK, LOG2E))
    bias_c = _bias_tiles(pc, rel_bias, LANES, LANES, LOG2E).reshape(C_HEADS, -1, LANES, LANES)
    kv_all = _memkv(mem, norm_memkv.reshape(L, 1, D), w_mkv.astype(BF16))

    for l in range(L):
        lam_init = 0.8 - 0.6 * math.exp(-0.3 * l)
        proj, proj_b, proj_i, avt, cvt, iwt = _inproj(
            x, norm_mix[l].reshape(1, D), _in_weights(w_in[l]), tm_proj, tk_a, ca_c)
        proj = proj.reshape(B, S, N_MAIN)
        proj_i = proj_i.reshape(B, S, N_I)
        lamv = jnp.stack([lam_q1[l], lam_k1[l], lam_q2[l], lam_k2[l]], axis=0)
        o_a = _diff_attn(proj, avt, lamv, bias_a, subln[l].reshape(A_V, 1), lam_init, tq_a,
                         d_min, d_const)
        o_b = _dil_attn(proj_b.reshape(B, S, N_B), bias_b)
        o_c = _dsa_attn(proj, proj_i, iwt, cvt, bias_c, tq_c, ch_c)
        x = _outmem(x, o_a, o_b, o_c, w_out[l].astype(BF16), norm_mem[l].reshape(1, D),
                    w_mq[l].astype(BF16), kv_all, l, w_mo[l].astype(BF16), tm_mem)
        wu = w_up[l].astype(BF16)
        x = _ffn(x, norm_ffn[l].reshape(1, D), wu, wu, conv_w[l][:, :F], conv_w[l][:, F:],
                 conv_b[l][:F].reshape(1, F), conv_b[l][F:].reshape(1, F), w_down[l].astype(BF16),
                 norm_final.reshape(1, D), l == L - 1, tm_ffn, fc)
    return x


def kernel(x, mem, rel_bias, norm_mix, w_in, lam_q1, lam_k1, lam_q2, lam_k2, subln, w_out,
           norm_mem, norm_memkv, w_mq, w_mkv, w_mo, norm_ffn, w_up, conv_w, conv_b, w_down,
           norm_final):
    return _forward(x, mem, rel_bias, norm_mix, w_in, lam_q1, lam_k1, lam_q2, lam_k2, subln, w_out,
                    norm_mem, norm_memkv, w_mq, w_mkv, w_mo, norm_ffn, w_up, conv_w, conv_b, w_down,
                    norm_final, tq_a=512, tk_a=256, tq_c=512, ch_c=512, ca_c=256, tm_proj=1024, tm_mem=1024, tm_ffn=1024,
                    fc=512)
```

```python
import functools
import math

import jax
import jax.numpy as jnp
from jax import lax
from jax.experimental import pallas as pl
from jax.experimental.pallas import tpu as pltpu

F32 = jnp.float32
BF16 = jnp.bfloat16

EPS = 1e-6
NEG = -1e30
LOG2E = math.log2(math.e)
LANES = 128
VMEM_LIMIT = 56 * 1024 * 1024

A_HEADS, A_QK, A_V = 4, 64, 128
B_HEADS, B_DIM = 4, 64
C_HEADS, C_DIM = 4, 64
IDX_HEADS, IDX_DIM = 8, 64
TOPK_MAX = 256
DILATED_PATTERNS = ((128, 1), (512, 4), (2048, 16))
ONES_ROWS = 16
DIL_BLK = 128
DIL_UNROLL = 4
REL_BUCKETS, REL_MAX_DIST = 32, 2048
MEM_HEADS = 4
CONV_WIDTH = 3

N_MAIN = 2816
N_B = 768
N_I = 256


def _cparams(sem):
    return pltpu.CompilerParams(dimension_semantics=sem, vmem_limit_bytes=VMEM_LIMIT)


def _dot(a, b):
    return jnp.dot(a, b, preferred_element_type=F32)


def _dot_nt(a, b):
    return lax.dot_general(a, b, (((1,), (1,)), ((), ())), preferred_element_type=F32)


def _ones_rows(n):
    row = lax.broadcasted_iota(jnp.int32, (ONES_ROWS, n), 0)
    return jnp.where(row == 0, 1.0, 0.0).astype(BF16)


def _rms(x, g):
    return x * lax.rsqrt(jnp.mean(x * x, axis=-1, keepdims=True) + EPS) * g


def _rel_bucket(n):
    max_exact = REL_BUCKETS // 2
    nf = jnp.maximum(n, 1).astype(F32)
    large = max_exact + (jnp.log(nf / max_exact) / math.log(REL_MAX_DIST / max_exact)
                         * (REL_BUCKETS - max_exact)).astype(jnp.int32)
    large = jnp.minimum(large, REL_BUCKETS - 1)
    return jnp.where(n < max_exact, n, large)


def _bias_tiles_kernel(par_ref, tab_ref, o_ref, *, tq, tk, scale):
    t = pl.program_id(0)
    head, off, mult = par_ref[t, 0], par_ref[t, 1], par_ref[t, 2]
    lo, hi, sgn = par_ref[t, 3], par_ref[t, 4], par_ref[t, 5]
    dist = off + sgn * (lax.broadcasted_iota(jnp.int32, (tq, tk), 0)
                        - lax.broadcasted_iota(jnp.int32, (tq, tk), 1))
    bucket = _rel_bucket(jnp.maximum(dist * mult, 0))
    val = jnp.zeros((tq, tk), F32)
    for b in range(REL_BUCKETS):
        val = jnp.where(bucket == b, tab_ref[b, head] * scale, val)
    o_ref[...] = jnp.where((dist >= lo) & (dist <= hi), val, NEG)


def _bias_tiles(params, table, tq, tk, scale):
    n = params.shape[0]
    return pl.pallas_call(
        functools.partial(_bias_tiles_kernel, tq=tq, tk=tk, scale=scale),
        grid=(n,),
        in_specs=[pl.BlockSpec(memory_space=pltpu.SMEM), pl.BlockSpec(memory_space=pltpu.SMEM)],
        out_specs=pl.BlockSpec((None, tq, tk), lambda t: (t, 0, 0)),
        out_shape=jax.ShapeDtypeStruct((n, tq, tk), F32),
        compiler_params=_cparams(("arbitrary",)),
        name="bias_tiles",
    )(params, table)


def _inproj_kernel(x_ref, g_ref, w_ref, om_ref, ob_ref, oi_ref, avt_ref, cvt_ref, iwt_ref):
    h = _rms(x_ref[...], g_ref[...]).astype(BF16)
    step = 512
    for c in range(0, N_MAIN, step):
        e = min(c + step, N_MAIN)
        om_ref[:, c:e] = _dot(h, w_ref[:, c:e]).astype(BF16)
    ob_ref[...] = _dot(h, w_ref[:, N_MAIN:N_MAIN + N_B])
    oi = _dot(h, w_ref[:, N_MAIN + N_B:])
    oi_ref[...] = oi[:, :2 * IDX_DIM].astype(BF16)
    tk = avt_ref.shape[-1]
    a0 = 2 * A_HEADS * LANES
    for hd in range(A_HEADS):
        vt = om_ref[:, a0 + hd * A_V:a0 + (hd + 1) * A_V].astype(F32).T
        for j in range(avt_ref.shape[1]):
            avt_ref[hd, j] = vt[:, j * tk:(j + 1) * tk].astype(BF16)
    ca = cvt_ref.shape[-1]
    cw = C_HEADS * C_DIM
    for half in range(cw // LANES):
        c0 = N_MAIN - cw + half * LANES
        vt = om_ref[:, c0:c0 + LANES].astype(F32).T
        for j in range(cvt_ref.shape[0]):
            cvt_ref[j, half * LANES:(half + 1) * LANES, :] = vt[:, j * ca:(j + 1) * ca].astype(BF16)
    iwt_ref[...] = oi[:, 2 * IDX_DIM:2 * IDX_DIM + LANES].T[:IDX_HEADS]


def _inproj(x, g, w, tm, tk_a, ca_c):
    B, S, D = x.shape
    T = B * S
    nt = S // tm
    n_all = N_MAIN + N_B + N_I
    cw = C_HEADS * C_DIM
    return pl.pallas_call(
        _inproj_kernel,
        grid=(T // tm,),
        in_specs=[pl.BlockSpec((tm, D), lambda i: (i, 0)),
                  pl.BlockSpec((1, D), lambda i: (0, 0)),
                  pl.BlockSpec((D, n_all), lambda i: (0, 0))],
        out_specs=[pl.BlockSpec((tm, N_MAIN), lambda i: (i, 0)),
                   pl.BlockSpec((tm, N_B), lambda i: (i, 0)),
                   pl.BlockSpec((tm, 2 * IDX_DIM), lambda i: (i, 0)),
                   pl.BlockSpec((None, A_HEADS, tm // tk_a, A_V, tk_a),
                                lambda i: (i // nt, 0, i % nt, 0, 0)),
                   pl.BlockSpec((None, tm // ca_c, cw, ca_c), lambda i: (i // nt, i % nt, 0, 0)),
                   pl.BlockSpec((None, IDX_HEADS, tm), lambda i: (i // nt, 0, i % nt))],
        out_shape=[jax.ShapeDtypeStruct((T, N_MAIN), BF16),
                   jax.ShapeDtypeStruct((T, N_B), F32),
                   jax.ShapeDtypeStruct((T, 2 * IDX_DIM), BF16),
                   jax.ShapeDtypeStruct((B, A_HEADS, S // tk_a, A_V, tk_a), BF16),
                   jax.ShapeDtypeStruct((B, S // ca_c, cw, ca_c), BF16),
                   jax.ShapeDtypeStruct((B, IDX_HEADS, S), F32)],
        compiler_params=_cparams(("parallel",)),
        name="inproj",
    )(x.reshape(T, D), g, w)


def _diff_attn_kernel(lam_ref, q_ref, k_ref, vt_ref, bias_ref, g_ref, o_ref, s_sc, acc_sc,
                      *, tq, tk, d_min, d_const, lam_init):
    qi = pl.program_id(2)
    gran = min(tq, tk)
    last = (qi * tq + tq - 1) // tk
    lv = lam_ref[...]
    lam = (jnp.exp(jnp.sum(lv[0:1] * lv[1:2], axis=-1, keepdims=True))
           - jnp.exp(jnp.sum(lv[2:3] * lv[3:4], axis=-1, keepdims=True)) + lam_init)
    lane = lax.broadcasted_iota(jnp.int32, (tq, LANES), 1)
    q = q_ref[...].astype(F32) * (A_QK ** -0.5 * LOG2E)
    qcat = jnp.concatenate([jnp.where(lane < A_QK, q, 0.0), jnp.where(lane >= A_QK, q, 0.0)],
                           axis=0).astype(BF16)
    acc_sc[...] = jnp.zeros(acc_sc.shape, F32)

    def scores_into(slot, j):
        k = k_ref[pl.ds(pl.multiple_of(j * tk, tk), tk), :]
        d = qi * (tq // gran) - j * (tk // gran)
        bias = bias_ref[jnp.minimum(d, d_const) - d_min]
        s_sc[slot] = _dot_nt(k, qcat) + jnp.concatenate([bias, bias], axis=1)

    def absorb(slot, j, m_prev):
        m_new = jnp.maximum(m_prev, jnp.max(s_sc[slot], axis=0, keepdims=True))
        alpha = jnp.exp2(m_prev - m_new)
        p = jnp.exp2(s_sc[slot] - m_new)
        vt = jnp.concatenate([vt_ref[j], _ones_rows(tk)], axis=0)
        acc_sc[...] = alpha * acc_sc[...] + _dot(vt, p.astype(BF16))
        return m_new

    scores_into(0, 0)

    def pair(a, m):
        scores_into(1, a + 1)
        m = absorb(0, a, m)
        scores_into(0, a + 2)
        return absorb(1, a + 1, m)

    pairs = last // 2
    m = lax.fori_loop(0, pairs // 2, lambda jj, m: pair(4 * jj + 2, pair(4 * jj, m)),
                      jnp.full((1, 2 * tq), NEG, F32))
    m = lax.fori_loop(pairs // 2 * 2, pairs, lambda jj, m: pair(2 * jj, m), m)

    half = tq // 2
    late = [slice(mp * tq + half, (mp + 1) * tq) for mp in range(2)]
    pick = lambda x: jnp.concatenate([x[:, c] for c in late], axis=1)
    k_last = k_ref[pl.ds(pl.multiple_of(last * tk, tk), tk), :]
    bias_last = bias_ref[-d_min][:, :half]
    q_late = jnp.concatenate([qcat[c] for c in late], axis=0)
    s_sc[1, :, :tq] = _dot_nt(k_last, q_late) + jnp.concatenate([bias_last, bias_last], axis=1)
    m = absorb(0, last - 1, m)
    m_late = pick(m)
    m_new = jnp.maximum(m_late, jnp.max(s_sc[1, :, :tq], axis=0, keepdims=True))
    alpha = jnp.exp2(m_late - m_new)
    p = jnp.exp2(s_sc[1, :, :tq] - m_new)
    vt = jnp.concatenate([vt_ref[last], _ones_rows(tk)], axis=0)
    upd = alpha * pick(acc_sc[...]) + _dot(vt, p.astype(BF16))
    for i, c in enumerate(late):
        acc_sc[:, c] = upd[:, i * half:(i + 1) * half]
    inv = 1.0 / acc_sc[A_V:A_V + 1, :]
    o = acc_sc[:A_V, :tq] * inv[:, :tq] - lam * (acc_sc[:A_V, tq:] * inv[:, tq:])
    o = o * lax.rsqrt(jnp.mean(o * o, axis=0, keepdims=True) + EPS) * g_ref[...] * (1.0 - lam_init)
    o_ref[...] = o.T.astype(o_ref.dtype)


def _diff_attn(proj, vt, lamv, bias_a, subln_g, lam_init, tq, d_min, d_const):
    B, S, _ = proj.shape
    nd, tk = bias_a.shape[1], bias_a.shape[2]
    assert tq == 2 * tk, (tq, tk)
    return pl.pallas_call(
        functools.partial(_diff_attn_kernel, tq=tq, tk=tk, d_min=d_min, d_const=d_const,
                          lam_init=lam_init),
        grid=(A_HEADS, B, S // tq),
        in_specs=[pl.BlockSpec((4, A_QK), lambda h, b, i: (0, 0)),
                  pl.BlockSpec((None, tq, LANES), lambda h, b, i: (b, i, h)),
                  pl.BlockSpec((None, S, LANES), lambda h, b, i: (b, 0, A_HEADS + h)),
                  pl.BlockSpec((None, None, S // tk, A_V, tk), lambda h, b, i: (b, h, 0, 0, 0)),
                  pl.BlockSpec((None, nd, tk, tq), lambda h, b, i: (h, 0, 0, 0)),
                  pl.BlockSpec((A_V, 1), lambda h, b, i: (0, 0))],
        out_specs=pl.BlockSpec((None, tq, LANES), lambda h, b, i: (b, i, h)),
        out_shape=jax.ShapeDtypeStruct((B, S, A_HEADS * A_V), BF16),
        scratch_shapes=[pltpu.VMEM((2, tk, 2 * tq), F32),
                        pltpu.VMEM((A_V + ONES_ROWS, 2 * tq), F32)],
        compiler_params=_cparams(("parallel", "parallel", "arbitrary")),
        name="diff_attn",
    )(lamv, proj, proj, vt, bias_a, subln_g)


class _LaneHalves:
    def __init__(self, *refs):
        self.refs = refs

    def __getitem__(self, idx):
        return jnp.concatenate([r[idx] for r in self.refs], axis=1)

    def __setitem__(self, idx, val):
        for i, r in enumerate(self.refs):
            r[idx] = val[:, i * LANES:(i + 1) * LANES]


def _dil_kernel(q0, q1, k0, k1, v0, v1, bias_ref, o_ref, n0, n1, m0, m1, l0, l1, *, seq, patterns):
    q_ref, k_ref, v_ref = _LaneHalves(q0, q1), _LaneHalves(k0, k1), _LaneHalves(v0, v1)
    n_sc, m_sc, l_sc = _LaneHalves(n0, n1), _LaneHalves(m0, m1), _LaneHalves(l0, l1)
    blk = DIL_BLK
    width = B_HEADS * B_DIM
    head_of_lane = lax.broadcasted_iota(jnp.int32, (blk, width), 1) // B_DIM

    def per_head(x):
        parts = [jnp.broadcast_to(x[h * blk:(h + 1) * blk], (blk, width)) for h in range(B_HEADS)]
        out = parts[-1]
        for h in range(B_HEADS - 2, -1, -1):
            out = jnp.where(head_of_lane == h, parts[h], out)
        return out

    def attend(base, r, with_prev, p_idx):
        q = q_ref[pl.ds(base, blk, stride=r), :] * (B_DIM ** -0.5 * LOG2E)
        qcat = jnp.concatenate([jnp.where(head_of_lane == h, q, 0.0) for h in range(B_HEADS)],
                               axis=0).astype(BF16)
        if with_prev:
            keys = pl.ds(base - blk * r, 2 * blk, stride=r)
            bias = bias_ref[p_idx]
        else:
            keys = pl.ds(base, blk, stride=r)
            bias = bias_ref[p_idx, :, blk:]
        s = _dot_nt(qcat, k_ref[keys, :].astype(BF16)) + bias
        mx = jnp.max(s, axis=-1, keepdims=True)
        p = jnp.exp2(s - mx)
        l = jnp.sum(p, axis=-1, keepdims=True)
        o = _dot(p.astype(BF16), v_ref[keys, :].astype(BF16))
        return per_head(o), per_head(mx), per_head(l)

    def merge(base, r, res, is_first):
        o, m, l = res
        rows = pl.ds(base, blk, stride=r)
        if is_first:
            n_sc[rows, :] = o
            m_sc[rows, :] = m
            l_sc[rows, :] = l
        else:
            m_old = m_sc[rows, :]
            m_new = jnp.maximum(m_old, m)
            a = jnp.exp2(m_old - m_new)
            b = jnp.exp2(m - m_new)
            n_sc[rows, :] = a * n_sc[rows, :] + b * o
            l_sc[rows, :] = a * l_sc[rows, :] + b * l
            m_sc[rows, :] = m_new

    def sweep(count, base_of, r, with_prev, p_idx):
        def trip(width):
            def body(i, carry):
                bases = [base_of(i * width + u) for u in range(width)]
                results = [attend(b, r, with_prev, p_idx) for b in bases]
                for b, res in zip(bases, results):
                    merge(b, r, res, p_idx == len(patterns) - 1)
                return carry
            return body

        full = count // DIL_UNROLL
        lax.fori_loop(0, full, trip(DIL_UNROLL), 0)
        rest = count - full * DIL_UNROLL
        if rest:
            lax.fori_loop(full * DIL_UNROLL, count, trip(1), 0)

    for p_idx, (_, r) in reversed(list(enumerate(patterns))):
        nb = seq // (r * blk)
        sweep(r, lambda c: c, r, False, p_idx)
        sweep(r * (nb - 1), lambda idx, r=r: (idx // r + 1) * blk * r + idx % r, r, True, p_idx)

    everything = (slice(None), slice(None))
    o_ref[...] = (n_sc[everything] / l_sc[everything]).astype(o_ref.dtype)


def _dil_attn(proj_b, bias_b):
    B, S, _ = proj_b.shape
    width = B_HEADS * B_DIM
    return pl.pallas_call(
        functools.partial(_dil_kernel, seq=S, patterns=DILATED_PATTERNS),
        grid=(B,),
        in_specs=[pl.BlockSpec((None, S, LANES), functools.partial(lambda i, b: (b, 0, i), i))
                  for i in range(3 * width // LANES)]
                 + [pl.BlockSpec(bias_b.shape, lambda b: (0, 0, 0))],
        out_specs=pl.BlockSpec((None, S, width), lambda b: (b, 0, 0)),
        out_shape=jax.ShapeDtypeStruct((B, S, width), BF16),
        scratch_shapes=[pltpu.VMEM((S, LANES), F32)] * (3 * width // LANES),
        compiler_params=_cparams(("parallel",)),
        name="dilated_attn",
    )(*([proj_b] * (3 * width // LANES)), bias_b)


SEL16 = jnp.bfloat16


RADIX_LOW_BITS = 5
PEEL_CAP = 6
FLT_MIN_NORMAL = 2.0 ** -126


def _float_key(f):
    b = pltpu.bitcast(f, jnp.int32)
    return jnp.where(b >= 0, b, b ^ jnp.int32(0x7FFFFFFF))


def _key_float(k):
    return pltpu.bitcast(jnp.where(k >= 0, k, k ^ jnp.int32(0x7FFFFFFF)), F32)


def _high_half(f):
    return pltpu.bitcast(pltpu.bitcast(f, jnp.int32) & jnp.int32(-65536), F32)


def _dsa_kernel(iq_ref, ik_ref, iwt_ref, cq_ref, ck_ref, cvt_ref, bias_ref, o_ref, sc_ref, hi_ref, s_sc,
                acc_sc,
                *, tq, ch, ca, topk):
    qi = pl.program_id(1)
    q0 = qi * tq
    nch = (q0 + tq - 1) // ch + 1
    qidx = q0 + lax.broadcasted_iota(jnp.int32, (1, tq), 1)
    lane = lax.broadcasted_iota(jnp.int32, (tq, LANES), 1)
    first = lane < IDX_DIM

    w = iwt_ref[...] * (IDX_HEADS ** -0.5)
    qs = []
    for h in range(IDX_HEADS):
        tile = iq_ref[:, (h // 2) * LANES:(h // 2 + 1) * LANES].astype(F32) * (IDX_DIM ** -0.5)
        qs.append(jnp.where(first if h % 2 == 0 else ~first, tile, 0.0).astype(BF16))

    half_k, half_q = ch // 2, tq // 2
    every, late = slice(0, tq), slice(half_q, tq)
    diag = nch - 1

    def scores_of(start, nkeys, cols):
        kk = ik_ref[pl.ds(start, nkeys), :]
        sc = jnp.zeros((nkeys, cols.stop - cols.start), F32)
        for h in range(IDX_HEADS):
            sc = sc + jnp.maximum(_dot_nt(kk, qs[h][cols]), 0.0) * w[h:h + 1, cols]
        kidx = start + lax.broadcasted_iota(jnp.int32, (nkeys, 1), 0)
        return jnp.where(kidx <= qidx[:, cols], sc, NEG)

    def score_body(j, carry):
        sc = scores_of(pl.multiple_of(j * ch, ch), ch, every)
        sc_ref[j] = sc
        hi_ref[j] = _high_half(sc).astype(SEL16)
        return carry

    lax.fori_loop(0, diag, score_body, 0)
    dstart = pl.multiple_of(diag * ch, ch)
    sc = scores_of(dstart, half_k, every)
    sc_ref[diag, :half_k, :] = sc
    hi_ref[diag, :half_k, :] = _high_half(sc).astype(SEL16)
    sc = scores_of(dstart + half_k, half_k, late)
    sc_ref[diag, half_k:, half_q:] = sc
    hi_ref[diag, half_k:, half_q:] = _high_half(sc).astype(SEL16)
    sc_ref[diag, half_k:, :half_q] = jnp.full((half_k, half_q), NEG, F32)
    hi_ref[diag, half_k:, :half_q] = jnp.full((half_k, half_q), NEG, SEL16)

    rows = 32

    def fold_chunks(ref, step, init):
        acc = lax.fori_loop(0, diag, lambda j, a: step(ref[j], a, every), init)
        acc = step(ref[diag, :half_k, :], acc, every)
        tail = step(ref[diag, half_k:, half_q:], acc[:, half_q:], late)
        return jnp.concatenate([acc[:, :half_q], tail], axis=1)

    def count_ge(t):
        def step(x, acc, cols):
            hit = jnp.where(x >= t[:, cols], 1.0, 0.0)
            return acc + jnp.sum(hit.reshape(x.shape[0] // rows, rows, x.shape[1]), axis=0)
        acc = fold_chunks(sc_ref, step, jnp.zeros((rows, tq), F32))
        return jnp.sum(acc, axis=0, keepdims=True)

    def count_high_ge(v):
        one, zero16 = jnp.ones((), SEL16), jnp.zeros((), SEL16)
        def step(x, acc, cols):
            hit = jnp.where(x >= v[:, cols], one, zero16).reshape(x.shape[0] // rows, rows, x.shape[1])
            parts = [hit[i] for i in range(x.shape[0] // rows)]
            while len(parts) > 1:
                parts = [parts[i] + parts[i + 1] for i in range(0, len(parts), 2)]
            return acc + parts[0]
        acc = fold_chunks(hi_ref, step, jnp.zeros((rows, tq), SEL16))
        return jnp.sum(acc.astype(F32), axis=0, keepdims=True)

    kf = float(topk)
    n_valid = (qidx + 1).astype(F32)
    half_bits = 16

    def high_body(i, st):
        u, c_lo = st
        cand = u | jnp.left_shift(1, half_bits - 1 - i)
        v = _high_half(_key_float((cand - (1 << (half_bits - 1))) << half_bits)).astype(SEL16)
        c = count_high_ge(v)
        keep = c >= kf
        return jnp.where(keep, cand, u), jnp.where(keep, c, c_lo)

    u, c_lo = lax.fori_loop(0, half_bits, high_body, (jnp.zeros((1, tq), jnp.int32), n_valid))
    key_hi = (u - (1 << (half_bits - 1))) << half_bits

    def low_body(i, st):
        lo_bits, c_lo = st
        cand = lo_bits | jnp.left_shift(1, half_bits - 1 - i)
        c = count_ge(_key_float(key_hi | cand))
        keep = c >= kf
        return jnp.where(keep, cand, lo_bits), jnp.where(keep, c, c_lo)

    lo_bits, c_lo = lax.fori_loop(0, RADIX_LOW_BITS, low_body, (jnp.zeros((1, tq), jnp.int32), c_lo))

    def min_ge(t):
        def step(x, acc, cols):
            kept = jnp.where(x >= t[:, cols], x, -NEG)
            return jnp.minimum(acc, jnp.min(kept.reshape(x.shape[0] // rows, rows, x.shape[1]), axis=0))
        acc = fold_chunks(sc_ref, step, jnp.full((rows, tq), -NEG, F32))
        return jnp.min(acc, axis=0, keepdims=True)

    def open_rows(c_lo, done):
        return jnp.logical_and(jnp.logical_and(n_valid > kf, c_lo > kf), done == 0.0)

    def any_row(mask):
        return jnp.max(jnp.where(mask, 1.0, 0.0)) > 0.0

    def peel_cond(st):
        _, c_lo, done, it = st
        return jnp.logical_and(it < PEEL_CAP, any_row(open_rows(c_lo, done)))

    def peel_body(st):
        t, c_lo, done, it = st
        is_open = open_rows(c_lo, done)
        smallest = min_ge(t)
        above = jnp.where(smallest == 0.0, FLT_MIN_NORMAL, _key_float(_float_key(smallest) + 1))
        c = count_ge(above)
        enough = jnp.logical_and(is_open, c >= kf)
        at_tie = jnp.logical_and(is_open, c < kf)
        t = jnp.where(enough, above, jnp.where(at_tie, smallest, t))
        return t, jnp.where(enough, c, c_lo), jnp.where(at_tie, 1.0, done), it + 1

    t_peel, c_peel, done, _ = lax.while_loop(
        peel_cond, peel_body,
        (_key_float(key_hi | lo_bits), c_lo, jnp.zeros((1, tq), F32), jnp.int32(0)))

    def finish_by_radix(_):
        bits, c = lax.fori_loop(RADIX_LOW_BITS, half_bits, low_body, (lo_bits, c_lo))
        return _key_float(key_hi | bits), c

    thr, c_lo = lax.cond(any_row(open_rows(c_peel, done)), finish_by_radix,
                         lambda _: (t_peel, c_peel), 0)
    thr = jnp.where(n_valid > kf, thr, 0.5 * NEG)

    tie_rows = jnp.logical_and(n_valid > kf, c_lo > kf)

    @pl.when(jnp.max(jnp.where(tie_rows, 1.0, 0.0)) > 0.0)
    def _():
        surplus = jnp.where(tie_rows, c_lo - kf, 0.0)
        upper = (lax.broadcasted_iota(jnp.int32, (ch, ch), 1)
                 >= lax.broadcasted_iota(jnp.int32, (ch, ch), 0)).astype(BF16)

        def drop_body(st):
            i, seen = st
            j = nch - 1 - i
            sc = sc_ref[j]
            eq = sc == thr
            eqf = jnp.where(eq, 1.0, 0.0)
            rank = _dot(upper, eqf.astype(BF16))
            sc_ref[j] = jnp.where(eq, jnp.where(rank <= surplus - seen, NEG, sc), sc)
            return i + 1, seen + jnp.sum(eqf, axis=0, keepdims=True)

        def more_to_drop(st):
            i, seen = st
            return jnp.logical_and(i < nch, any_row(seen < surplus))

        lax.while_loop(more_to_drop, drop_body, (jnp.int32(0), jnp.zeros((1, tq), F32)))

    cw = C_HEADS * C_DIM
    lane_c = lax.broadcasted_iota(jnp.int32, (tq, cw), 1)
    cq = cq_ref[...].astype(F32) * (C_DIM ** -0.5 * LOG2E)
    qcat = jnp.concatenate([jnp.where(lane_c // C_DIM == h, cq, 0.0) for h in range(C_HEADS)],
                           axis=0).astype(BF16)
    acc_sc[...] = jnp.zeros(acc_sc.shape, F32)
    nd = bias_ref.shape[1]
    last = (q0 + tq - 1) // ca

    def chunk_scores(j, c0):
        kc = ck_ref[pl.ds(pl.multiple_of(j * ca, ca), ca), :]
        sub = pl.multiple_of((j % (ch // ca)) * ca, ca)
        sel = sc_ref[j // (ch // ca), pl.ds(sub, ca), c0:] >= thr[:, c0:]
        bias = jnp.concatenate([
            jnp.concatenate([
                bias_ref[h, jnp.clip(q0 // LANES + c - (j * (ca // LANES) + u), 0, nd - 1)]
                for h in range(C_HEADS) for c in range(c0 // LANES, tq // LANES)], axis=1)
            for u in range(ca // LANES)], axis=0)
        q_rows = qcat if c0 == 0 else jnp.concatenate(
            [qcat[h * tq + c0:(h + 1) * tq] for h in range(C_HEADS)], axis=0)
        return jnp.where(jnp.concatenate([sel] * C_HEADS, axis=1), _dot_nt(kc, q_rows) + bias, NEG)

    def scores_into(slot, j):
        s_sc[slot] = chunk_scores(j, 0)

    def absorb(read, j, m_prev, c0):
        width = tq - c0
        m_new = jnp.maximum(m_prev, jnp.max(read(), axis=0, keepdims=True))
        alpha = jnp.exp2(m_prev - m_new)
        pb = jnp.exp2(read() - m_new).astype(BF16)
        for h in range(C_HEADS):
            cols = slice(h * width, (h + 1) * width)
            vt = jnp.concatenate([cvt_ref[j, h * C_DIM:(h + 1) * C_DIM, :], _ones_rows(ca)], axis=0)
            acc_sc[h, :, c0:] = alpha[:, cols] * acc_sc[h, :, c0:] + _dot(vt, pb[:, cols])
        return m_new

    scores_into(0, 0)

    def attn_body(jj, m):
        a = 2 * jj
        scores_into(1, a + 1)
        m = absorb(lambda: s_sc[0], a, m, 0)
        scores_into(0, a + 2)
        return absorb(lambda: s_sc[1], a + 1, m, 0)

    m = lax.fori_loop(0, last // 2, attn_body, jnp.full((1, C_HEADS * tq), NEG, F32))
    half = tq // 2
    s_sc[1, :, :C_HEADS * half] = chunk_scores(last, half)
    m = absorb(lambda: s_sc[0], last - 1, m, 0)
    m_late = jnp.concatenate([m[:, h * tq + half:(h + 1) * tq] for h in range(C_HEADS)], axis=1)
    absorb(lambda: s_sc[1, :, :C_HEADS * half], last, m_late, half)
    o_t = jnp.concatenate([acc_sc[h, :C_DIM] * (1.0 / acc_sc[h, C_DIM:C_DIM + 1])
                           for h in range(C_HEADS)], axis=0)
    o_ref[...] = o_t.T.astype(o_ref.dtype)


def _dsa_attn(proj, proj_i, iwt, cvt, bias_c, tq, ch):
    B, S, _ = proj.shape
    ca = cvt.shape[-1]
    topk = min(TOPK_MAX, S // 4)
    nd = bias_c.shape[1]
    cw = C_HEADS * C_DIM
    assert tq == ch == 2 * ca, (tq, ch, ca)
    return pl.pallas_call(
        functools.partial(_dsa_kernel, tq=tq, ch=ch, ca=ca, topk=topk),
        grid=(B, S // tq),
        in_specs=[pl.BlockSpec((None, tq, IDX_HEADS * IDX_DIM), lambda b, i: (b, i, 3)),
                  pl.BlockSpec((None, S, LANES), lambda b, i: (b, 0, 0)),
                  pl.BlockSpec((None, IDX_HEADS, tq), lambda b, i: (b, 0, i)),
                  pl.BlockSpec((None, tq, cw), lambda b, i: (b, i, 8)),
                  pl.BlockSpec((None, S, cw), lambda b, i: (b, 0, 9)),
                  pl.BlockSpec((None, S // ca, cw, ca), lambda b, i: (b, 0, 0, 0)),
                  pl.BlockSpec((C_HEADS, nd, LANES, LANES), lambda b, i: (0, 0, 0, 0))],
        out_specs=pl.BlockSpec((None, tq, cw), lambda b, i: (b, i, 0)),
        out_shape=jax.ShapeDtypeStruct((B, S, cw), BF16),
        scratch_shapes=[pltpu.VMEM((S // ch, ch, tq), F32),
                        pltpu.VMEM((S // ch, ch, tq), SEL16),
                        pltpu.VMEM((2, ca, C_HEADS * tq), F32),
                        pltpu.VMEM((C_HEADS, C_DIM + ONES_ROWS, tq), F32)],
        compiler_params=_cparams(("parallel", "arbitrary")),
        name="dsa_attn",
    )(proj, proj_i, iwt, proj, proj, cvt, bias_c)


def _memkv_kernel(mem_ref, g_ref, w_ref, o_ref):
    h = _rms(mem_ref[...], g_ref[...]).astype(BF16)
    o_ref[...] = _dot(h, w_ref[...]).astype(o_ref.dtype)


def _memkv(mem, g, w):
    B, M, D = mem.shape
    L = w.shape[0]
    return pl.pallas_call(
        _memkv_kernel,
        grid=(L, B),
        in_specs=[pl.BlockSpec((None, M, D), lambda l, b: (b, 0, 0)),
                  pl.BlockSpec((None, 1, D), lambda l, b: (l, 0, 0)),
                  pl.BlockSpec((None, D, 2 * D), lambda l, b: (l, 0, 0))],
        out_specs=pl.BlockSpec((None, None, M, 2 * D), lambda l, b: (l, b, 0, 0)),
        out_shape=jax.ShapeDtypeStruct((L, B, M, 2 * D), BF16),
        compiler_params=_cparams(("parallel", "arbitrary")),
        name="mem_kv",
    )(mem, g, w)


def _outmem_kernel(x_ref, oa_ref, ob_ref, oc_ref, wo_ref, g_ref, wq_ref, kv_ref, wmo_ref, o_ref):
    D = x_ref.shape[-1]
    na, nb = oa_ref.shape[-1], ob_ref.shape[-1]
    x = (x_ref[...] + _dot(oa_ref[...], wo_ref[0:na]) + _dot(ob_ref[...], wo_ref[na:na + nb])
         + _dot(oc_ref[...], wo_ref[na + nb:]))
    h = _rms(x, g_ref[...]).astype(BF16)
    hd = D // MEM_HEADS
    q = (_dot(h, wq_ref[...]) * (hd ** -0.5)).astype(BF16)
    outs = []
    for hh in range(MEM_HEADS):
        k = kv_ref[:, hh * hd:(hh + 1) * hd]
        v = kv_ref[:, D + hh * hd:D + (hh + 1) * hd]
        s = _dot_nt(q[:, hh * hd:(hh + 1) * hd], k)
        p = jnp.exp(s - jnp.max(s, axis=-1, keepdims=True))
        p = p / jnp.sum(p, axis=-1, keepdims=True)
        outs.append(_dot(p.astype(BF16), v).astype(BF16))
    o = jnp.concatenate(outs, axis=-1)
    o_ref[...] = x + _dot(o, wmo_ref[...])


def _outmem(x, oa, ob, oc, wo, g, wq, kv, layer, wmo, tm):
    B, S, D = x.shape
    M = kv.shape[2]
    const = lambda b, i: (0, 0)
    return pl.pallas_call(
        _outmem_kernel,
        grid=(B, S // tm),
        in_specs=[pl.BlockSpec((None, tm, D), lambda b, i: (b, i, 0)),
                  pl.BlockSpec((None, tm, oa.shape[-1]), lambda b, i: (b, i, 0)),
                  pl.BlockSpec((None, tm, ob.shape[-1]), lambda b, i: (b, i, 0)),
                  pl.BlockSpec((None, tm, oc.shape[-1]), lambda b, i: (b, i, 0)),
                  pl.BlockSpec(wo.shape, const),
                  pl.BlockSpec((1, D), const),
                  pl.BlockSpec(wq.shape, const),
                  pl.BlockSpec((None, None, M, 2 * D), lambda b, i: (layer, b, 0, 0)),
                  pl.BlockSpec(wmo.shape, const)],
        out_specs=pl.BlockSpec((None, tm, D), lambda b, i: (b, i, 0)),
        out_shape=jax.ShapeDtypeStruct((B, S, D), F32),
        compiler_params=_cparams(("parallel", "arbitrary")),
        name="outproj_memattn",
    )(x, oa, ob, oc, wo, g, wq, kv, wmo)


HALO = 8


def _ffn_kernel(x_ref, xp_ref, g_ref, wg_ref, wv_ref, cwg_ref, cwv_ref, cbg_ref, cbv_ref, wd_ref,
                gout_ref, o_ref, *, tm, fc, norm_out):
    i = pl.program_id(1)
    x = x_ref[...]
    g = g_ref[...]
    hp = _rms(xp_ref[...], g) * jnp.where(i > 0, 1.0, 0.0)
    h = jnp.concatenate([hp, _rms(x, g)], axis=0).astype(BF16)
    F = wd_ref.shape[0]

    def conv(u, cw_ref, cb_ref, c, e):
        out = cb_ref[:, c:e]
        for j in range(CONV_WIDTH):
            shift = CONV_WIDTH - 1 - j
            out = out + cw_ref[j:j + 1, c:e] * u[HALO - shift:HALO - shift + tm]
        return out

    acc = x
    for c in range(0, F, fc):
        e = min(c + fc, F)
        gate = conv(_dot(h, wg_ref[:, c:e]), cwg_ref, cbg_ref, c, e)
        val = conv(_dot(h, wv_ref[:, c:e]), cwv_ref, cbv_ref, c, e)
        act = (gate * jax.nn.sigmoid(gate) * val).astype(BF16)
        acc = acc + _dot(act, wd_ref[c:e, :])
    o_ref[...] = _rms(acc, gout_ref[...]) if norm_out else acc


def _ffn(x, g, wg, wv, cwg, cwv, cbg, cbv, wd, gout, norm_out, tm, fc):
    B, S, D = x.shape
    F = wd.shape[0]
    const = lambda b, i: (0, 0)
    hb = tm // HALO
    return pl.pallas_call(
        functools.partial(_ffn_kernel, tm=tm, fc=fc, norm_out=norm_out),
        grid=(B, S // tm),
        in_specs=[pl.BlockSpec((None, tm, D), lambda b, i: (b, i, 0)),
                  pl.BlockSpec((None, HALO, D), lambda b, i: (b, jnp.maximum(i * hb - 1, 0), 0)),
                  pl.BlockSpec((1, D), const),
                  pl.BlockSpec((D, F), const, pipeline_mode=pl.Buffered(1)),
                  pl.BlockSpec((D, F), lambda b, i: (0, 1), pipeline_mode=pl.Buffered(1)),
                  pl.BlockSpec((CONV_WIDTH, F), const), pl.BlockSpec((CONV_WIDTH, F), const),
                  pl.BlockSpec((1, F), const), pl.BlockSpec((1, F), const),
                  pl.BlockSpec((F, D), const, pipeline_mode=pl.Buffered(1)),
                  pl.BlockSpec((1, D), const)],
        out_specs=pl.BlockSpec((None, tm, D), lambda b, i: (b, i, 0)),
        out_shape=jax.ShapeDtypeStruct((B, S, D), F32),
        compiler_params=_cparams(("parallel", "arbitrary")),
        name="conv_ffn",
    )(x, x, g, wg, wv, cwg, cwv, cbg, cbv, wd, gout)


def _tile_params(S, tq_a, tk_a):
    big = 1 << 30
    gran, (d_min, d_const) = min(tq_a, tk_a), _a_offsets(tq_a, tk_a)
    pa = []
    for h in range(A_HEADS):
        pa += [(h, d * gran, 1, 0, big, -1) for d in range(d_min, d_const + 1)]
    pb = []
    for _, r in DILATED_PATTERNS:
        for h in range(B_HEADS):
            pb.append((A_HEADS + h, DIL_BLK, r, 1, DIL_BLK, 1))
            pb.append((A_HEADS + h, 0, r, 0, DIL_BLK, 1))
    pc = [(A_HEADS + B_HEADS + h, d * LANES, 1, -big, big, -1)
          for h in range(C_HEADS) for d in range(_n_offsets(S, LANES))]
    to = lambda p: jnp.asarray(p, jnp.int32)
    return to(pa), to(pb), to(pc)


def _dil_bias_layout(tiles):
    t = tiles.reshape(len(DILATED_PATTERNS), B_HEADS, 2, DIL_BLK, DIL_BLK)
    return jnp.concatenate([t[:, :, 0], t[:, :, 1]], axis=-1).reshape(
        len(DILATED_PATTERNS), B_HEADS * DIL_BLK, 2 * DIL_BLK)


def _n_offsets(S, blk):
    return min(S // blk, REL_MAX_DIST // blk + 2)


def _a_offsets(tq, tk):
    gran = min(tq, tk)
    return -(tq // gran - 1), -(-(REL_MAX_DIST - 1 + tk) // gran)


def _in_weights(w_in_l):
    sizes = (512, 512, 512, 256, 256, 256, 256, 256, 256, 512, 64, 8)
    offs = [0]
    for s in sizes:
        offs.append(offs[-1] + s)
    (aq, ak, av, bq, bk, bv, cq, ck, cv, iq, ik, iw) = [w_in_l[:, offs[i]:offs[i + 1]] for i in range(12)]
    pad = jnp.zeros((w_in_l.shape[0], N_I - 2 * IDX_DIM - IDX_HEADS), w_in_l.dtype)
    return jnp.concatenate([aq, ak, av, iq, cq, ck, cv, bq, bk, bv, ik, ik, iw, pad], axis=1).astype(BF16)


def _forward(x, mem, rel_bias, norm_mix, w_in, lam_q1, lam_k1, lam_q2, lam_k2, subln, w_out,
             norm_mem, norm_memkv, w_mq, w_mkv, w_mo, norm_ffn, w_up, conv_w, conv_b, w_down,
             norm_final, *, tq_a, tk_a, tq_c, ch_c, ca_c, tm_proj, tm_mem, tm_ffn, fc):
    B, S, D = x.shape
    L = w_in.shape[0]
    F = w_down.shape[1]
    pa, pb, pc = _tile_params(S, tq_a, tk_a)
    bias_a = _bias_tiles(pa, rel_bias, tk_a, tq_a, LOG2E).reshape(A_HEADS, -1, tk_a, tq_a)
    d_min, d_const = _a_offsets(tq_a, tk_a)
    bias_b = _dil_bias_layout(_bias_tiles(pb, rel_bias, DIL_BLK, DIL_BLK, LOG2E))
    bias_c = _bias_tiles(pc, rel_bias, LANES, LANES, LOG2E).reshape(C_HEADS, -1, LANES, LANES)
    kv_all = _memkv(mem, norm_memkv.reshape(L, 1, D), w_mkv.astype(BF16))

    for l in range(L):
        lam_init = 0.8 - 0.6 * math.exp(-0.3 * l)
        proj, proj_b, proj_i, avt, cvt, iwt = _inproj(
            x, norm_mix[l].reshape(1, D), _in_weights(w_in[l]), tm_proj, tk_a, ca_c)
        proj = proj.reshape(B, S, N_MAIN)
        proj_i = proj_i.reshape(B, S, 2 * IDX_DIM)
        lamv = jnp.stack([lam_q1[l], lam_k1[l], lam_q2[l], lam_k2[l]], axis=0)
        o_a = _diff_attn(proj, avt, lamv, bias_a, subln[l].reshape(A_V, 1), lam_init, tq_a,
                         d_min, d_const)
        o_b = _dil_attn(proj_b.reshape(B, S, N_B), bias_b)
        o_c = _dsa_attn(proj, proj_i, iwt, cvt, bias_c, tq_c, ch_c)
        x = _outmem(x, o_a, o_b, o_c, w_out[l].astype(BF16), norm_mem[l].reshape(1, D),
                    w_mq[l].astype(BF16), kv_all, l, w_mo[l].astype(BF16), tm_mem)
        wu = w_up[l].astype(BF16)
        x = _ffn(x, norm_ffn[l].reshape(1, D), wu, wu, conv_w[l][:, :F], conv_w[l][:, F:],
                 conv_b[l][:F].reshape(1, F), conv_b[l][F:].reshape(1, F), w_down[l].astype(BF16),
                 norm_final.reshape(1, D), l == L - 1, tm_ffn, fc)
    return x


def kernel(x, mem, rel_bias, norm_mix, w_in, lam_q1, lam_k1, lam_q2, lam_k2, subln, w_out,
           norm_mem, norm_memkv, w_mq, w_mkv, w_mo, norm_ffn, w_up, conv_w, conv_b, w_down,
           norm_final):
    return _forward(x, mem, rel_bias, norm_mix, w_in, lam_q1, lam_k1, lam_q2, lam_k2, subln, w_out,
                    norm_mem, norm_memkv, w_mq, w_mkv, w_mo, norm_ffn, w_up, conv_w, conv_b, w_down,
                    norm_final, tq_a=512, tk_a=256, tq_c=512, ch_c=512, ca_c=256, tm_proj=1024, tm_mem=1024, tm_ffn=1024,
                    fc=512)
```
